```python
import math
import jax, jax.numpy as jnp
from jax import lax
import numpy as np

D_MODEL = 2048
BATCH = 2
SEQ = 8192
DEPTH = 1

D_SG = D_MODEL // 2
SG_GROUPS = 8
SG_GROUP_DIM = D_SG // SG_GROUPS
SG_CHUNK = 128
ATT_HEADS = 8
HEAD_DIM = 128
D_ATT = ATT_HEADS * HEAD_DIM
MOBA_BLOCK = 256
MOBA_TOPK = 3
Q_CHUNK = 32
REL_BUCKETS = 32
REL_MAX_DIST = 128
N_GROUPS = 4
EXPERTS_PER_GROUP = 4
N_EXPERTS = N_GROUPS * EXPERTS_PER_GROUP
TOP_K_IN_GROUP = 2
D_EXPERT = 512
EPS = 1e-6
IN_SIZES = (D_SG, D_SG, D_ATT, D_ATT, D_ATT, D_MODEL, D_MODEL)
IN_COLS = sum(IN_SIZES)
IN_SPLITS = tuple(int(s) for s in np.cumsum(IN_SIZES)[:-1])

kernel_name = "gated_sgu_moba_hmoe_adaln_block"


def rmsnorm(x, w):
    xf = x.astype(jnp.float32)
    y = xf * lax.rsqrt(jnp.mean(xf * xf, axis=-1, keepdims=True) + EPS)
    return (y * w.astype(jnp.float32)).astype(x.dtype)


def layernorm(x, w, b):
    xf = x.astype(jnp.float32)
    mu = jnp.mean(xf, axis=-1, keepdims=True)
    var = jnp.mean(jnp.square(xf - mu), axis=-1, keepdims=True)
    y = (xf - mu) * lax.rsqrt(var + EPS)
    return (y * w.astype(jnp.float32) + b.astype(jnp.float32)).astype(x.dtype)


def t5_bucket(n):
    max_exact = REL_BUCKETS // 2
    nf = jnp.maximum(n, max_exact).astype(jnp.float32)
    large = max_exact + (jnp.log(nf / max_exact) / math.log(REL_MAX_DIST / max_exact)
                         * (REL_BUCKETS - max_exact)).astype(jnp.int32)
    large = jnp.minimum(large, REL_BUCKETS - 1)
    return jnp.where(n < max_exact, n, large)


def spatial_gating(u, v, ln_w, ln_b, w_s, b_s):
    B, S, _ = u.shape
    vn = layernorm(v, ln_w, ln_b).reshape(B, S // SG_CHUNK, SG_CHUNK, SG_GROUPS, SG_GROUP_DIM)
    causal = jnp.tril(jnp.ones((SG_CHUNK, SG_CHUNK), dtype=bool))
    w_m = jnp.where(causal[None], w_s, jnp.zeros_like(w_s))
    z = jnp.einsum('gts,bcsgd->bctgd', w_m, vn) + b_s.T[None, None, :, :, None]
    return u * z.reshape(B, S, D_SG)


def moba_attention(q, k, v, rel_bias):
    B, S, H, dh = q.shape
    nb = -(-S // MOBA_BLOCK)
    pad = nb * MOBA_BLOCK - S
    q = q.transpose(0, 2, 1, 3)
    kb = jnp.pad(k, ((0, 0), (0, pad), (0, 0), (0, 0))).transpose(0, 2, 1, 3).reshape(B, H, nb, MOBA_BLOCK, dh)
    vb = jnp.pad(v, ((0, 0), (0, pad), (0, 0), (0, 0))).transpose(0, 2, 1, 3).reshape(B, H, nb, MOBA_BLOCK, dh)
    qpos = jnp.arange(S, dtype=jnp.int32)
    own = qpos // MOBA_BLOCK
    kmean = jnp.mean(kb.astype(jnp.float32), axis=3)
    score = jnp.einsum('bhsd,bhnd->bhsn', q.astype(jnp.float32), kmean)
    past = jnp.arange(nb, dtype=jnp.int32)[None, :] < own[:, None]
    score = jnp.where(past[None, None], score, -jnp.inf)
    k_sel = min(MOBA_TOPK, nb)
    _, sel = lax.top_k(score, k_sel)
    sel = sel.astype(jnp.int32)
    sel_ok = sel < own[None, None, :, None]
    own_b = jnp.broadcast_to(own[None, None, :, None], (B, H, S, 1))
    blocks = jnp.concatenate([sel, own_b], axis=-1)
    ok = jnp.concatenate([sel_ok, jnp.ones_like(own_b, dtype=bool)], axis=-1)
    nqc = S // Q_CHUNK
    q_c = jnp.moveaxis(q.reshape(B, H, nqc, Q_CHUNK, dh), 2, 0)
    blk_c = jnp.moveaxis(blocks.reshape(B, H, nqc, Q_CHUNK, k_sel + 1), 2, 0)
    ok_c = jnp.moveaxis(ok.reshape(B, H, nqc, Q_CHUNK, k_sel + 1), 2, 0)
    qpos_c = qpos.reshape(nqc, Q_CHUNK)
    bi = jnp.arange(B)[:, None, None, None]
    hi = jnp.arange(H)[None, :, None, None]
    bias_t = rel_bias.T
    scale = HEAD_DIM ** -0.5
    key_off = jnp.arange(MOBA_BLOCK, dtype=jnp.int32)

    def one_chunk(args):
        qc, blk, okc, qp = args
        kg = kb[bi, hi, blk]
        vg = vb[bi, hi, blk]
        kpos = blk[..., None] * MOBA_BLOCK + key_off
        rel = qp[None, None, :, None, None] - kpos
        mask = okc[..., None] & (rel >= 0)
        bias = bias_t[hi[..., None], t5_bucket(jnp.maximum(rel, 0))]
        logits = jnp.einsum('bhqd,bhqnkd->bhqnk', qc, kg).astype(jnp.float32) * scale + bias.astype(jnp.float32)
        logits = jnp.where(mask, logits, -jnp.inf)
        p = jax.nn.softmax(logits, axis=(-2, -1))
        return jnp.einsum('bhqnk,bhqnkd->bhqd', p.astype(vg.dtype), vg)

    out = lax.map(one_chunk, (q_c, blk_c, ok_c, qpos_c))
    return out.transpose(1, 0, 3, 2, 4).reshape(B, S, H * dh)


def hierarchical_moe(h, w_rg, w_re, w1, w3, w2):
    B, S, D = h.shape
    t = h.reshape(B * S, D)
    g_prob = jax.nn.softmax((t @ w_rg).astype(jnp.float32), axis=-1)
    g_p, g_idx = lax.top_k(g_prob, 1)
    e_logits = (t @ w_re).astype(jnp.float32).reshape(-1, N_GROUPS, EXPERTS_PER_GROUP)
    e_logits = jnp.take_along_axis(e_logits, g_idx[:, :, None], axis=1)[:, 0]
    e_p, e_idx = lax.top_k(jax.nn.softmax(e_logits, axis=-1), TOP_K_IN_GROUP)
    e_p = e_p / jnp.sum(e_p, axis=-1, keepdims=True)
    weights = g_p * e_p
    expert_id = g_idx * EXPERTS_PER_GROUP + e_idx
    gates = jnp.sum(jax.nn.one_hot(expert_id, N_EXPERTS, dtype=jnp.float32) * weights[..., None], axis=1)

    def expert_step(acc, xs):
        w1e, w3e, w2e, ge = xs
        y = (jax.nn.silu(t @ w1e) * (t @ w3e)) @ w2e
        return acc + ge[:, None].astype(t.dtype) * y, None

    y, _ = lax.scan(expert_step, jnp.zeros_like(t), (w1, w3, w2, gates.T))
    return y.reshape(B, S, D)


def setup_inputs(seed: int = 0) -> dict:
    key = jax.random.key(seed)
    ks = jax.random.split(key, 24)
    f32 = jnp.float32
    nrm = lambda k, shape, s: jax.random.normal(k, shape, f32) * s
    L, D = DEPTH, D_MODEL
    return {
        "x": nrm(ks[0], (BATCH, SEQ, D), 1.0),
        "c": nrm(ks[1], (BATCH, D), 1.0),
        "w_ada": nrm(ks[2], (L, D, 6 * D), 0.5 * D ** -0.5),
        "b_ada": nrm(ks[3], (L, 6 * D), 0.02),
        "norm1_w": 1.0 + nrm(ks[4], (L, D), 0.05),
        "norm2_w": 1.0 + nrm(ks[5], (L, D), 0.05),
        "final_norm_w": 1.0 + nrm(ks[6], (D,), 0.05),
        "w_in": nrm(ks[7], (L, D, IN_COLS), D ** -0.5),
        "sg_ln_w": 1.0 + nrm(ks[8], (L, D_SG), 0.05),
        "sg_ln_b": nrm(ks[9], (L, D_SG), 0.02),
        "w_spatial": nrm(ks[10], (L, SG_GROUPS, SG_CHUNK, SG_CHUNK), SG_CHUNK ** -0.5),
        "b_spatial": 1.0 + nrm(ks[11], (L, SG_GROUPS, SG_CHUNK), 0.05),
        "rel_bias": nrm(ks[12], (REL_BUCKETS, ATT_HEADS), 0.5),
        "w_out_sg": nrm(ks[13], (L, D_SG, D), D_SG ** -0.5),
        "w_out_att": nrm(ks[14], (L, D_ATT, D), D_ATT ** -0.5),
        "w_o": nrm(ks[15], (L, D, D), D ** -0.5),
        "w_router_group": nrm(ks[16], (L, D, N_GROUPS), D ** -0.5),
        "w_router_expert": nrm(ks[17], (L, D, N_EXPERTS), D ** -0.5),
        "w_exp_gate": nrm(ks[18], (L, N_EXPERTS, D, D_EXPERT), D ** -0.5),
        "w_exp_up": nrm(ks[19], (L, N_EXPERTS, D, D_EXPERT), D ** -0.5),
        "w_exp_down": nrm(ks[20], (L, N_EXPERTS, D_EXPERT, D), D_EXPERT ** -0.5),
    }


def reference(x, c, w_ada, b_ada, norm1_w, norm2_w, final_norm_w, w_in, sg_ln_w, sg_ln_b,
              w_spatial, b_spatial, rel_bias, w_out_sg, w_out_att, w_o, w_router_group,
              w_router_expert, w_exp_gate, w_exp_up, w_exp_down):
    B, S, D = x.shape
    c_act = jax.nn.silu(c)
    for l in range(DEPTH):
        mod = c_act @ w_ada[l] + b_ada[l]
        sh1, sc1, g1, sh2, sc2, g2 = [m[:, None, :] for m in jnp.split(mod, 6, axis=-1)]
        h = rmsnorm(x, norm1_w[l]) * (1.0 + sc1) + sh1
        proj = h @ w_in[l]
        u_a, v_a, q, k, v, gate_sg, gate_att = jnp.split(proj, IN_SPLITS, axis=-1)
        y_sg = spatial_gating(jax.nn.gelu(u_a, approximate=False), jax.nn.gelu(v_a, approximate=False),
                              sg_ln_w[l], sg_ln_b[l], w_spatial[l], b_spatial[l])
        y_att = moba_attention(q.reshape(B, S, ATT_HEADS, HEAD_DIM), k.reshape(B, S, ATT_HEADS, HEAD_DIM),
                               v.reshape(B, S, ATT_HEADS, HEAD_DIM), rel_bias)
        merged = jax.nn.sigmoid(gate_sg) * (y_sg @ w_out_sg[l]) + jax.nn.sigmoid(gate_att) * (y_att @ w_out_att[l])
        x = x + g1 * (merged @ w_o[l])
        h2 = rmsnorm(x, norm2_w[l]) * (1.0 + sc2) + sh2
        x = x + g2 * hierarchical_moe(h2, w_router_group[l], w_router_expert[l],
                                      w_exp_gate[l], w_exp_up[l], w_exp_down[l])
    return rmsnorm(x, final_norm_w)
```

```python
import functools
import math

import jax
import jax.numpy as jnp
from jax import lax
from jax.experimental import pallas as pl
from jax.experimental.pallas import tpu as pltpu

F32 = jnp.float32
BF16 = jnp.bfloat16

LANES = 128
D_MODEL = 2048
D_SG = D_MODEL // 2
SG_GROUPS = 8
SG_CHUNK = 128
ATT_HEADS = 8
HEAD_DIM = 128
D_ATT = ATT_HEADS * HEAD_DIM
MOBA_BLOCK = 256
MOBA_TOPK = 3
REL_BUCKETS = 32
REL_MAX_DIST = 128
N_GROUPS = 4
EXPERTS_PER_GROUP = 4
N_EXPERTS = N_GROUPS * EXPERTS_PER_GROUP
D_EXPERT = 512
EPS = 1e-6
IN_COLS = 2 * D_SG + 3 * D_ATT + 2 * D_MODEL

N_GATE_COLS = 2 * D_MODEL
BLK_GATE_SG = 0
BLK_GATE_ATT = 1
BLK_U = N_GATE_COLS // D_SG
BLK_V = BLK_U + 1
CB_Q = (N_GATE_COLS + 2 * D_SG) // LANES
CB_K = CB_Q + ATT_HEADS
CB_VAL = CB_K + ATT_HEADS

MASK_NEG = -1e9
ROUTER_COL0 = N_GROUPS
VMEM_LIMIT = 56 * 1024 * 1024


def _cparams(sem):
    return pltpu.CompilerParams(dimension_semantics=sem, vmem_limit_bytes=VMEM_LIMIT)


def _nt_dot(a, b):
    return lax.dot_general(a, b, (((1,), (1,)), ((), ())), preferred_element_type=F32)


def _split_bf16(a):
    hi = a.astype(BF16)
    lo = (a - hi.astype(F32)).astype(BF16)
    return hi, lo


def _adaln_kernel(c_ref, w_ref, b_ref, o_ref):
    c = c_ref[...]
    ca = c * jax.nn.sigmoid(c)
    o_ref[...] = jnp.dot(ca, w_ref[...], preferred_element_type=F32,
                         precision=lax.Precision.HIGHEST) + b_ref[...]


def _adaln(c_pad, w, b):
    rows, d = c_pad.shape
    cols = w.shape[1]
    tn = 1024
    return pl.pallas_call(
        _adaln_kernel,
        grid=(cols // tn,),
        in_specs=[pl.BlockSpec((rows, d), lambda j: (0, 0)),
                  pl.BlockSpec((d, tn), lambda j: (0, j)),
                  pl.BlockSpec((1, tn), lambda j: (0, j))],
        out_specs=pl.BlockSpec((rows, tn), lambda j: (0, j)),
        out_shape=jax.ShapeDtypeStruct((rows, cols), F32),
        compiler_params=_cparams(("arbitrary",)),
        name="adaln",
    )(c_pad, w, b)


def _proj_kernel(x_ref, mod_ref, nw_ref, w_ref, cs_ref, o_ref, h_scr):
    @pl.when(pl.program_id(1) == 0)
    def _():
        x = x_ref[...]
        ms = jnp.mean(x * x, axis=-1, keepdims=True)
        y = x * lax.rsqrt(ms + EPS) * nw_ref[...]
        h = y * (1.0 + mod_ref[0, 1:2, :]) + mod_ref[0, 0:1, :]
        h_scr[...] = h.astype(BF16)

    acc = jnp.dot(h_scr[...], w_ref[...], preferred_element_type=F32)
    o_ref[...] = (acc * cs_ref[...]).astype(BF16)


def _proj(x2, mod, norm_w, w_in, colscale, seq):
    n, d = x2.shape
    cols = w_in.shape[1]
    tm = min(1024, seq)
    tn = 1024
    tpb = seq // tm
    return pl.pallas_call(
        _proj_kernel,
        grid=(n // tm, cols // tn),
        in_specs=[pl.BlockSpec((tm, d), lambda i, j: (i, 0)),
                  pl.BlockSpec((1, 8, d), lambda i, j: (i // tpb, 0, 0)),
                  pl.BlockSpec((1, d), lambda i, j: (0, 0)),
                  pl.BlockSpec((d, tn), lambda i, j: (0, j)),
                  pl.BlockSpec((1, tn), lambda i, j: (0, j))],
        out_specs=pl.BlockSpec((tm, tn), lambda i, j: (i, j)),
        out_shape=jax.ShapeDtypeStruct((n, cols), BF16),
        scratch_shapes=[pltpu.VMEM((tm, d), BF16)],
        compiler_params=_cparams(("parallel", "arbitrary")),
        name="proj",
    )(x2, mod, norm_w, w_in, colscale)


def _gelu(a):
    return 0.5 * a * (1.0 + lax.erf(a * (1.0 / math.sqrt(2.0))))


def _sgu_kernel(u_ref, v_ref, lnw_ref, lnb_ref, ws_ref, bst_ref, o_ref, *, chunks):
    u = _gelu(u_ref[...].astype(F32))
    v = _gelu(v_ref[...].astype(F32))
    mu = jnp.mean(v, axis=-1, keepdims=True)
    vc = v - mu
    var = jnp.mean(vc * vc, axis=-1, keepdims=True)
    vn = (vc * lax.rsqrt(var + EPS) * lnw_ref[...] + lnb_ref[...]).astype(BF16)
    row = lax.broadcasted_iota(jnp.int32, (SG_CHUNK, SG_CHUNK), 0)
    col = lax.broadcasted_iota(jnp.int32, (SG_CHUNK, SG_CHUNK), 1)
    causal = col <= row
    for g in range(SG_GROUPS):
        wm = jnp.where(causal, ws_ref[g], 0.0).astype(BF16)
        bcol = bst_ref[:, g:g + 1]
        gs = slice(g * LANES, (g + 1) * LANES)
        for c in range(chunks):
            rs = slice(c * SG_CHUNK, (c + 1) * SG_CHUNK)
            z = jnp.dot(wm, vn[rs, gs], preferred_element_type=F32) + bcol
            o_ref[rs, gs] = (u[rs, gs] * z).astype(BF16)


def _sgu(proj, ln_w, ln_b, w_s, b_s_t):
    n = proj.shape[0]
    chunks = 4
    tm = chunks * SG_CHUNK
    return pl.pallas_call(
        functools.partial(_sgu_kernel, chunks=chunks),
        grid=(n // tm,),
        in_specs=[pl.BlockSpec((tm, D_SG), lambda i: (i, BLK_U)),
                  pl.BlockSpec((tm, D_SG), lambda i: (i, BLK_V)),
                  pl.BlockSpec((1, D_SG), lambda i: (0, 0)),
                  pl.BlockSpec((1, D_SG), lambda i: (0, 0)),
                  pl.BlockSpec((SG_GROUPS, SG_CHUNK, SG_CHUNK), lambda i: (0, 0, 0)),
                  pl.BlockSpec((SG_CHUNK, SG_GROUPS), lambda i: (0, 0))],
        out_specs=pl.BlockSpec((tm, D_SG), lambda i: (i, 0)),
        out_shape=jax.ShapeDtypeStruct((n, D_SG), BF16),
        compiler_params=_cparams(("parallel",)),
        name="sgu",
    )(proj, proj, ln_w, ln_b, w_s, b_s_t)


def _kmean_kernel(k_ref, o_ref, *, nb):
    k = k_ref[...].astype(F32).reshape(nb, MOBA_BLOCK, HEAD_DIM)
    o_ref[...] = jnp.zeros(o_ref.shape, F32)
    o_ref[0:nb, :] = jnp.mean(k, axis=1)


def _kmean(proj, batch, seq):
    nb = seq // MOBA_BLOCK
    return pl.pallas_call(
        functools.partial(_kmean_kernel, nb=nb),
        grid=(batch, ATT_HEADS),
        in_specs=[pl.BlockSpec((seq, HEAD_DIM), lambda b, h: (b, CB_K + h))],
        out_specs=pl.BlockSpec((None, None, LANES, HEAD_DIM), lambda b, h: (b, h, 0, 0)),
        out_shape=jax.ShapeDtypeStruct((batch, ATT_HEADS, LANES, HEAD_DIM), F32),
        compiler_params=_cparams(("parallel", "parallel")),
        name="kmean",
    )(proj)


def _bias_tiles_kernel(rb_ref, o_ref):
    h = pl.program_id(0)
    qi = lax.broadcasted_iota(jnp.int32, (MOBA_BLOCK, MOBA_BLOCK), 0)
    kj = lax.broadcasted_iota(jnp.int32, (MOBA_BLOCK, MOBA_BLOCK), 1)
    max_exact = REL_BUCKETS // 2
    for t in range(3):
        rel = qi - kj + MOBA_BLOCK * t
        n = jnp.maximum(rel, 0)
        nf = jnp.maximum(n, max_exact).astype(F32)
        large = max_exact + (jnp.log(nf / max_exact) / math.log(REL_MAX_DIST / max_exact)
                             * (REL_BUCKETS - max_exact)).astype(jnp.int32)
        large = jnp.minimum(large, REL_BUCKETS - 1)
        bucket = jnp.where(n < max_exact, n, large)
        bias = jnp.zeros((MOBA_BLOCK, MOBA_BLOCK), F32)
        for r in range(REL_BUCKETS):
            bias = jnp.where(bucket == r, rb_ref[r, h], bias)
        o_ref[t] = jnp.where(rel >= 0, bias, MASK_NEG)


def _bias_tiles(rel_bias):
    return pl.pallas_call(
        _bias_tiles_kernel,
        grid=(ATT_HEADS,),
        in_specs=[pl.BlockSpec(memory_space=pltpu.SMEM)],
        out_specs=pl.BlockSpec((None, 3, MOBA_BLOCK, MOBA_BLOCK), lambda h: (h, 0, 0, 0)),
        out_shape=jax.ShapeDtypeStruct((ATT_HEADS, 3, MOBA_BLOCK, MOBA_BLOCK), F32),
        compiler_params=_cparams(("arbitrary",)),
        name="bias_tiles",
    )(rel_bias)


def _moba_kernel(q_ref, k_ref, v_ref, km_ref, bias_ref, o_ref):
    i = pl.program_id(2)
    q = q_ref[...]
    lane = lax.broadcasted_iota(jnp.int32, (MOBA_BLOCK, LANES), 1)

    km_hi, km_lo = _split_bf16(km_ref[...])
    score = _nt_dot(q, km_hi) + _nt_dot(q, km_lo)
    neg_inf = jnp.float32(-jnp.inf)
    score = jnp.where(lane < i, score, neg_inf)
    chosen = lane == i
    for _ in range(MOBA_TOPK):
        mx = jnp.max(score, axis=1, keepdims=True)
        hit = (score == mx) & (mx > neg_inf)
        idx = jnp.min(jnp.where(hit, lane, LANES), axis=1, keepdims=True)
        pick = lane == idx
        chosen = chosen | pick
        score = jnp.where(pick, neg_inf, score)
    ext = jnp.where(chosen, 0.0, MASK_NEG).astype(BF16)
    qx = jnp.concatenate([q, ext], axis=1)

    def tile(j, bias):
        start = pl.multiple_of(j * MOBA_BLOCK, MOBA_BLOCK)
        kj = k_ref[pl.ds(start, MOBA_BLOCK), :]
        vj = v_ref[pl.ds(start, MOBA_BLOCK), :]
        onehot = jnp.where(lane == j, 1.0, 0.0).astype(BF16)
        kx = jnp.concatenate([kj, onehot], axis=1)
        return _nt_dot(qx, kx) + bias, vj

    s, vj = tile(i, bias_ref[0])
    m = jnp.max(s, axis=1, keepdims=True)
    p = jnp.exp(s - m)
    l = jnp.sum(p, axis=1, keepdims=True)
    acc = jnp.dot(p.astype(BF16), vj, preferred_element_type=F32)

    def body(j, carry):
        m, l, acc = carry
        t = jnp.minimum(i - j, 2)
        s, vj = tile(j, bias_ref[t])
        m_new = jnp.maximum(m, jnp.max(s, axis=1, keepdims=True))
        alpha = jnp.exp(m - m_new)
        p = jnp.exp(s - m_new)
        l = alpha * l + jnp.sum(p, axis=1, keepdims=True)
        acc = alpha * acc + jnp.dot(p.astype(BF16), vj, preferred_element_type=F32)
        return m_new, l, acc

    m, l, acc = lax.fori_loop(0, i, body, (m, l, acc))
    o_ref[...] = (acc / l).astype(BF16)


def _moba(proj, kmean, bias_tiles, batch, seq):
    n = proj.shape[0]
    nq = seq // MOBA_BLOCK
    return pl.pallas_call(
        _moba_kernel,
        grid=(batch, ATT_HEADS, nq),
        in_specs=[pl.BlockSpec((MOBA_BLOCK, HEAD_DIM), lambda b, h, i: (b * nq + i, CB_Q + h)),
                  pl.BlockSpec((seq, HEAD_DIM), lambda b, h, i: (b, CB_K + h)),
                  pl.BlockSpec((seq, HEAD_DIM), lambda b, h, i: (b, CB_VAL + h)),
                  pl.BlockSpec((None, None, LANES, HEAD_DIM), lambda b, h, i: (b, h, 0, 0)),
                  pl.BlockSpec((None, 3, MOBA_BLOCK, MOBA_BLOCK), lambda b, h, i: (h, 0, 0, 0))],
        out_specs=pl.BlockSpec((MOBA_BLOCK, HEAD_DIM), lambda b, h, i: (b * nq + i, h)),
        out_shape=jax.ShapeDtypeStruct((n, D_ATT), BF16),
        compiler_params=_cparams(("parallel", "parallel", "arbitrary")),
        name="moba",
    )(proj, proj, proj, kmean, bias_tiles)


def _merge_kernel(x_ref, ysg_ref, yatt_ref, gsg_ref, gatt_ref, mod_ref, nw_ref,
                  wsg_ref, watt_ref, wo_ref, wr_ref, x1_ref, h2_ref, gates_ref):
    a_sg = jnp.dot(ysg_ref[...], wsg_ref[...], preferred_element_type=F32)
    a_att = jnp.dot(yatt_ref[...], watt_ref[...], preferred_element_type=F32)
    merged = (jax.nn.sigmoid(gsg_ref[...].astype(F32)) * a_sg
              + jax.nn.sigmoid(gatt_ref[...].astype(F32)) * a_att)
    mixed = jnp.dot(merged.astype(BF16), wo_ref[...], preferred_element_type=F32)
    x1 = x_ref[...] + mod_ref[0, 2:3, :] * mixed
    x1_ref[...] = x1
    ms = jnp.mean(x1 * x1, axis=-1, keepdims=True)
    y = x1 * lax.rsqrt(ms + EPS) * nw_ref[...]
    h2 = y * (1.0 + mod_ref[0, 4:5, :]) + mod_ref[0, 3:4, :]
    h2_ref[...] = h2.astype(BF16)

    h_hi, h_lo = _split_bf16(h2)
    w_hi, w_lo = _split_bf16(wr_ref[...])
    logits = (jnp.dot(h_hi, w_hi, preferred_element_type=F32)
              + jnp.dot(h_lo, w_hi, preferred_element_type=F32)
              + jnp.dot(h_hi, w_lo, preferred_element_type=F32))

    lane = lax.broadcasted_iota(jnp.int32, logits.shape, 1)
    neg_inf = jnp.float32(-jnp.inf)
    in_g = lane < N_GROUPS
    lg = jnp.where(in_g, logits, neg_inf)
    mg = jnp.max(lg, axis=1, keepdims=True)
    eg = jnp.exp(lg - mg)
    g_prob = eg / jnp.sum(eg, axis=1, keepdims=True)
    g_p = jnp.max(g_prob, axis=1, keepdims=True)
    g_idx = jnp.min(jnp.where(g_prob == g_p, lane, LANES), axis=1, keepdims=True)
    lo_col = ROUTER_COL0 + EXPERTS_PER_GROUP * g_idx
    in_e = (lane >= lo_col) & (lane < lo_col + EXPERTS_PER_GROUP)
    le = jnp.where(in_e, logits, neg_inf)
    me = jnp.max(le, axis=1, keepdims=True)
    ee = jnp.exp(le - me)
    e_prob = jnp.where(in_e, ee / jnp.sum(ee, axis=1, keepdims=True), -1.0)
    p1 = jnp.max(e_prob, axis=1, keepdims=True)
    i1 = jnp.min(jnp.where(e_prob == p1, lane, LANES), axis=1, keepdims=True)
    rest = jnp.where(lane == i1, -1.0, e_prob)
    p2 = jnp.max(rest, axis=1, keepdims=True)
    i2 = jnp.min(jnp.where(rest == p2, lane, LANES), axis=1, keepdims=True)
    denom = p1 + p2
    gates_ref[...] = (jnp.where(lane == i1, g_p * (p1 / denom), 0.0)
                      + jnp.where(lane == i2, g_p * (p2 / denom), 0.0))


def _merge(x2, y_sg, y_att, proj, mod, norm2_w, w_sg, w_att, w_o, w_router, seq):
    n, d = x2.shape
    tm = 256
    tpb = seq // tm
    resident = lambda shape: pl.BlockSpec(shape, lambda i: (0, 0), pipeline_mode=pl.Buffered(1))
    return pl.pallas_call(
        _merge_kernel,
        grid=(n // tm,),
        in_specs=[pl.BlockSpec((tm, d), lambda i: (i, 0)),
                  pl.BlockSpec((tm, D_SG), lambda i: (i, 0)),
                  pl.BlockSpec((tm, D_ATT), lambda i: (i, 0)),
                  pl.BlockSpec((tm, d), lambda i: (i, BLK_GATE_SG)),
                  pl.BlockSpec((tm, d), lambda i: (i, BLK_GATE_ATT)),
                  pl.BlockSpec((1, 8, d), lambda i: (i // tpb, 0, 0)),
                  pl.BlockSpec((1, d), lambda i: (0, 0)),
                  resident((D_SG, d)), resident((D_ATT, d)), resident((d, d)),
                  resident((d, LANES))],
        out_specs=[pl.BlockSpec((tm, d), lambda i: (i, 0)),
                   pl.BlockSpec((tm, d), lambda i: (i, 0)),
                   pl.BlockSpec((tm, LANES), lambda i: (i, 0))],
        out_shape=[jax.ShapeDtypeStruct((n, d), F32),
                   jax.ShapeDtypeStruct((n, d), BF16),
                   jax.ShapeDtypeStruct((n, LANES), F32)],
        compiler_params=_cparams(("parallel",)),
        name="merge",
    )(x2, y_sg, y_att, proj, proj, mod, norm2_w, w_sg, w_att, w_o, w_router)


def _moe_kernel(t_ref, gates_ref, x1_ref, mod_ref, fw_ref, w1_ref, w3_ref, w2_ref, o_ref, acc_ref):
    e = pl.program_id(1)

    @pl.when(e == 0)
    def _():
        acc_ref[...] = jnp.zeros(acc_ref.shape, F32)

    t = t_ref[...]
    a = jnp.dot(t, w1_ref[...], preferred_element_type=F32)
    b = jnp.dot(t, w3_ref[...], preferred_element_type=F32)
    lane = lax.broadcasted_iota(jnp.int32, gates_ref.shape, 1)
    ge = jnp.sum(jnp.where(lane == ROUTER_COL0 + e, gates_ref[...], 0.0), axis=1, keepdims=True)
    hmid = (a * jax.nn.sigmoid(a)) * b * ge
    acc_ref[...] += jnp.dot(hmid.astype(BF16), w2_ref[...], preferred_element_type=F32)

    @pl.when(e == pl.num_programs(1) - 1)
    def _():
        x2 = x1_ref[...] + mod_ref[0, 5:6, :] * acc_ref[...]
        ms = jnp.mean(x2 * x2, axis=-1, keepdims=True)
        o_ref[...] = x2 * lax.rsqrt(ms + EPS) * fw_ref[...]


def _moe(h2, gates, x1, mod, final_w, w1, w3, w2, seq):
    n, d = h2.shape
    tm = 512
    tpb = seq // tm
    return pl.pallas_call(
        _moe_kernel,
        grid=(n // tm, N_EXPERTS),
        in_specs=[pl.BlockSpec((tm, d), lambda i, e: (i, 0)),
                  pl.BlockSpec((tm, LANES), lambda i, e: (i, 0)),
                  pl.BlockSpec((tm, d), lambda i, e: (i, 0)),
                  pl.BlockSpec((1, 8, d), lambda i, e: (i // tpb, 0, 0)),
                  pl.BlockSpec((1, d), lambda i, e: (0, 0)),
                  pl.BlockSpec((None, d, D_EXPERT), lambda i, e: (e, 0, 0)),
                  pl.BlockSpec((None, d, D_EXPERT), lambda i, e: (e, 0, 0)),
                  pl.BlockSpec((None, D_EXPERT, d), lambda i, e: (e, 0, 0))],
        out_specs=pl.BlockSpec((tm, d), lambda i, e: (i, 0)),
        out_shape=jax.ShapeDtypeStruct((n, d), F32),
        scratch_shapes=[pltpu.VMEM((tm, d), F32)],
        compiler_params=_cparams(("parallel", "arbitrary")),
        name="moe",
    )(h2, gates, x1, mod, final_w, w1, w3, w2)


def kernel(x, c, w_ada, b_ada, norm1_w, norm2_w, final_norm_w, w_in, sg_ln_w, sg_ln_b, w_spatial,
           b_spatial, rel_bias, w_out_sg, w_out_att, w_o, w_router_group, w_router_expert,
           w_exp_gate, w_exp_up, w_exp_down):
    batch, seq, d = x.shape
    assert d == D_MODEL and w_ada.shape[0] == 1
    assert seq % 1024 == 0 and seq // MOBA_BLOCK <= LANES
    n = batch * seq
    x2 = x.reshape(n, d)

    c_pad = jnp.zeros((8, d), F32).at[:batch].set(c)
    mod = _adaln(c_pad, w_ada[0], b_ada[0].reshape(1, 6 * d))
    mod = mod[:batch].reshape(batch, 6, d)
    mod = jnp.concatenate([mod, jnp.zeros((batch, 2, d), F32)], axis=1)

    colscale = jnp.ones((1, IN_COLS), F32).at[:, CB_Q * LANES:CB_K * LANES].set(HEAD_DIM ** -0.5)
    w_in_r = jnp.concatenate([w_in[0][:, IN_COLS - N_GATE_COLS:], w_in[0][:, :IN_COLS - N_GATE_COLS]],
                             axis=1).astype(BF16)
    proj = _proj(x2, mod, norm1_w[0].reshape(1, d), w_in_r, colscale, seq)

    y_sg = _sgu(proj, sg_ln_w[0].reshape(1, D_SG), sg_ln_b[0].reshape(1, D_SG),
                w_spatial[0], b_spatial[0].T)

    kmean = _kmean(proj, batch, seq)
    bias_tiles = _bias_tiles(rel_bias)
    y_att = _moba(proj, kmean, bias_tiles, batch, seq)

    w_router = jnp.zeros((d, LANES), F32)
    w_router = w_router.at[:, :N_GROUPS].set(w_router_group[0])
    w_router = w_router.at[:, ROUTER_COL0:ROUTER_COL0 + N_EXPERTS].set(w_router_expert[0])
    x1, h2, gates = _merge(x2, y_sg, y_att, proj, mod, norm2_w[0].reshape(1, d),
                           w_out_sg[0].astype(BF16), w_out_att[0].astype(BF16),
                           w_o[0].astype(BF16), w_router, seq)

    out = _moe(h2, gates, x1, mod, final_norm_w.reshape(1, d), w_exp_gate[0].astype(BF16),
               w_exp_up[0].astype(BF16), w_exp_down[0].astype(BF16), seq)
    return out.reshape(batch, seq, d)
```

```python
import functools
import math

import jax
import jax.numpy as jnp
from jax import lax
from jax.experimental import pallas as pl
from jax.experimental.pallas import tpu as pltpu

F32 = jnp.float32
BF16 = jnp.bfloat16

LANES = 128
D_MODEL = 2048
D_SG = D_MODEL // 2
SG_GROUPS = 8
SG_CHUNK = 128
ATT_HEADS = 8
HEAD_DIM = 128
D_ATT = ATT_HEADS * HEAD_DIM
MOBA_BLOCK = 256
MOBA_TOPK = 3
REL_BUCKETS = 32
REL_MAX_DIST = 128
N_GROUPS = 4
EXPERTS_PER_GROUP = 4
N_EXPERTS = N_GROUPS * EXPERTS_PER_GROUP
D_EXPERT = 512
EPS = 1e-6
IN_COLS = 2 * D_SG + 3 * D_ATT + 2 * D_MODEL

N_GATE_COLS = 2 * D_MODEL
BLK_GATE_SG = 0
BLK_GATE_ATT = 1
BLK_U = N_GATE_COLS // D_SG
BLK_V = BLK_U + 1
CB_Q = (N_GATE_COLS + 2 * D_SG) // LANES
CB_K = CB_Q + ATT_HEADS
CB_VAL = CB_K + ATT_HEADS

MASK_NEG = -1e9
ROUTER_COL0 = N_GROUPS
VMEM_LIMIT = 56 * 1024 * 1024


def _cparams(sem):
    return pltpu.CompilerParams(dimension_semantics=sem, vmem_limit_bytes=VMEM_LIMIT)


def _nt_dot(a, b):
    return lax.dot_general(a, b, (((1,), (1,)), ((), ())), preferred_element_type=F32)


def _split_bf16(a):
    hi = a.astype(BF16)
    lo = (a - hi.astype(F32)).astype(BF16)
    return hi, lo


def _adaln_kernel(c_ref, w_ref, b_ref, o_ref):
    c = c_ref[...]
    ca = c * jax.nn.sigmoid(c)
    o_ref[...] = jnp.dot(ca, w_ref[...], preferred_element_type=F32,
                         precision=lax.Precision.HIGHEST) + b_ref[...]


def _adaln(c_pad, w, b):
    rows, d = c_pad.shape
    cols = w.shape[1]
    tn = 1024
    return pl.pallas_call(
        _adaln_kernel,
        grid=(cols // tn,),
        in_specs=[pl.BlockSpec((rows, d), lambda j: (0, 0)),
                  pl.BlockSpec((d, tn), lambda j: (0, j)),
                  pl.BlockSpec((1, tn), lambda j: (0, j))],
        out_specs=pl.BlockSpec((rows, tn), lambda j: (0, j)),
        out_shape=jax.ShapeDtypeStruct((rows, cols), F32),
        compiler_params=_cparams(("arbitrary",)),
        name="adaln",
    )(c_pad, w, b)


def _proj_kernel(x_ref, mod_ref, nw_ref, w_ref, cs_ref, o_ref, h_scr):
    @pl.when(pl.program_id(1) == 0)
    def _():
        x = x_ref[...]
        ms = jnp.mean(x * x, axis=-1, keepdims=True)
        y = x * lax.rsqrt(ms + EPS) * nw_ref[...]
        h = y * (1.0 + mod_ref[0, 1:2, :]) + mod_ref[0, 0:1, :]
        h_scr[...] = h.astype(BF16)

    acc = jnp.dot(h_scr[...], w_ref[...], preferred_element_type=F32)
    o_ref[...] = (acc * cs_ref[...]).astype(BF16)


def _proj(x2, mod, norm_w, w_in, colscale, seq):
    n, d = x2.shape
    cols = w_in.shape[1]
    tm = min(1024, seq)
    tn = 1024
    tpb = seq // tm
    return pl.pallas_call(
        _proj_kernel,
        grid=(n // tm, cols // tn),
        in_specs=[pl.BlockSpec((tm, d), lambda i, j: (i, 0)),
                  pl.BlockSpec((1, 8, d), lambda i, j: (i // tpb, 0, 0)),
                  pl.BlockSpec((1, d), lambda i, j: (0, 0)),
                  pl.BlockSpec((d, tn), lambda i, j: (0, j)),
                  pl.BlockSpec((1, tn), lambda i, j: (0, j))],
        out_specs=pl.BlockSpec((tm, tn), lambda i, j: (i, j)),
        out_shape=jax.ShapeDtypeStruct((n, cols), BF16),
        scratch_shapes=[pltpu.VMEM((tm, d), BF16)],
        compiler_params=_cparams(("parallel", "arbitrary")),
        name="proj",
    )(x2, mod, norm_w, w_in, colscale)


def _gelu(a):
    return 0.5 * a * (1.0 + lax.erf(a * (1.0 / math.sqrt(2.0))))


def _sgu_kernel(u_ref, v_ref, lnw_ref, lnb_ref, ws_ref, bst_ref, o_ref, *, chunks):
    u = _gelu(u_ref[...].astype(F32))
    v = _gelu(v_ref[...].astype(F32))
    mu = jnp.mean(v, axis=-1, keepdims=True)
    vc = v - mu
    var = jnp.mean(vc * vc, axis=-1, keepdims=True)
    vn = (vc * lax.rsqrt(var + EPS) * lnw_ref[...] + lnb_ref[...]).astype(BF16)
    row = lax.broadcasted_iota(jnp.int32, (SG_CHUNK, SG_CHUNK), 0)
    col = lax.broadcasted_iota(jnp.int32, (SG_CHUNK, SG_CHUNK), 1)
    causal = col <= row
    for g in range(SG_GROUPS):
        wm = jnp.where(causal, ws_ref[g], 0.0).astype(BF16)
        bcol = bst_ref[:, g:g + 1]
        gs = slice(g * LANES, (g + 1) * LANES)
        for c in range(chunks):
            rs = slice(c * SG_CHUNK, (c + 1) * SG_CHUNK)
            z = jnp.dot(wm, vn[rs, gs], preferred_element_type=F32) + bcol
            o_ref[rs, gs] = (u[rs, gs] * z).astype(BF16)


def _sgu(proj, ln_w, ln_b, w_s, b_s_t):
    n = proj.shape[0]
    chunks = 4
    tm = chunks * SG_CHUNK
    return pl.pallas_call(
        functools.partial(_sgu_kernel, chunks=chunks),
        grid=(n // tm,),
        in_specs=[pl.BlockSpec((tm, D_SG), lambda i: (i, BLK_U)),
                  pl.BlockSpec((tm, D_SG), lambda i: (i, BLK_V)),
                  pl.BlockSpec((1, D_SG), lambda i: (0, 0)),
                  pl.BlockSpec((1, D_SG), lambda i: (0, 0)),
                  pl.BlockSpec((SG_GROUPS, SG_CHUNK, SG_CHUNK), lambda i: (0, 0, 0)),
                  pl.BlockSpec((SG_CHUNK, SG_GROUPS), lambda i: (0, 0))],
        out_specs=pl.BlockSpec((tm, D_SG), lambda i: (i, 0)),
        out_shape=jax.ShapeDtypeStruct((n, D_SG), BF16),
        compiler_params=_cparams(("parallel",)),
        name="sgu",
    )(proj, proj, ln_w, ln_b, w_s, b_s_t)


VT_ROWS = HEAD_DIM + 16
EXT_LO = REL_BUCKETS


def _moba_prep_kernel(q_ref, k_ref, v_ref, qt_ref, kx_ref, vt_ref, km_ref, *, nb):
    lane = lax.broadcasted_iota(jnp.int32, (MOBA_BLOCK, LANES), 1)
    pad = jnp.concatenate([jnp.ones((1, MOBA_BLOCK), F32),
                           jnp.zeros((VT_ROWS - HEAD_DIM - 1, MOBA_BLOCK), F32)], axis=0)
    for j in range(nb):
        rows = slice(j * MOBA_BLOCK, (j + 1) * MOBA_BLOCK)
        kj = k_ref[rows, :]
        onehot = jnp.where((lane == j) | (lane == j + EXT_LO), 1.0, 0.0).astype(BF16)
        kx_ref[rows, :] = jnp.concatenate([kj, onehot], axis=1)
        km_ref[j:j + 1, :] = jnp.mean(kj.astype(F32), axis=0, keepdims=True)
        vt = v_ref[rows, :].astype(F32).T
        vt_ref[j] = jnp.concatenate([vt, pad], axis=0).astype(BF16)
        qt_ref[j] = q_ref[rows, :].astype(F32).T.astype(BF16)


def _moba_prep(proj, batch, seq):
    nb = seq // MOBA_BLOCK
    bh = lambda shape: pl.BlockSpec((None, None) + shape, lambda b, h: (b, h) + (0,) * len(shape))
    return pl.pallas_call(
        functools.partial(_moba_prep_kernel, nb=nb),
        grid=(batch, ATT_HEADS),
        in_specs=[pl.BlockSpec((seq, HEAD_DIM), lambda b, h: (b, CB_Q + h)),
                  pl.BlockSpec((seq, HEAD_DIM), lambda b, h: (b, CB_K + h)),
                  pl.BlockSpec((seq, HEAD_DIM), lambda b, h: (b, CB_VAL + h))],
        out_specs=[bh((nb, HEAD_DIM, MOBA_BLOCK)), bh((seq, 2 * HEAD_DIM)),
                   bh((nb, VT_ROWS, MOBA_BLOCK)), bh((nb, HEAD_DIM))],
        out_shape=[jax.ShapeDtypeStruct((batch, ATT_HEADS, nb, HEAD_DIM, MOBA_BLOCK), BF16),
                   jax.ShapeDtypeStruct((batch, ATT_HEADS, seq, 2 * HEAD_DIM), BF16),
                   jax.ShapeDtypeStruct((batch, ATT_HEADS, nb, VT_ROWS, MOBA_BLOCK), BF16),
                   jax.ShapeDtypeStruct((batch, ATT_HEADS, nb, HEAD_DIM), F32)],
        compiler_params=_cparams(("parallel", "parallel")),
        name="moba_prep",
    )(proj, proj, proj)


def _bias_tiles_kernel(rb_ref, o_ref):
    h = pl.program_id(0)
    kj = lax.broadcasted_iota(jnp.int32, (MOBA_BLOCK, MOBA_BLOCK), 0)
    qi = lax.broadcasted_iota(jnp.int32, (MOBA_BLOCK, MOBA_BLOCK), 1)
    max_exact = REL_BUCKETS // 2
    for t in range(2):
        rel = qi - kj + MOBA_BLOCK * t
        n = jnp.maximum(rel, 0)
        nf = jnp.maximum(n, max_exact).astype(F32)
        large = max_exact + (jnp.log(nf / max_exact) / math.log(REL_MAX_DIST / max_exact)
                             * (REL_BUCKETS - max_exact)).astype(jnp.int32)
        large = jnp.minimum(large, REL_BUCKETS - 1)
        bucket = jnp.where(n < max_exact, n, large)
        bias = jnp.zeros((MOBA_BLOCK, MOBA_BLOCK), F32)
        for r in range(REL_BUCKETS):
            bias = jnp.where(bucket == r, rb_ref[r, h], bias)
        o_ref[t] = jnp.where(rel >= 0, bias, MASK_NEG)


def _bias_tiles(rel_bias):
    return pl.pallas_call(
        _bias_tiles_kernel,
        grid=(ATT_HEADS,),
        in_specs=[pl.BlockSpec(memory_space=pltpu.SMEM)],
        out_specs=pl.BlockSpec((None, 2, MOBA_BLOCK, MOBA_BLOCK), lambda h: (h, 0, 0, 0)),
        out_shape=jax.ShapeDtypeStruct((ATT_HEADS, 2, MOBA_BLOCK, MOBA_BLOCK), F32),
        compiler_params=_cparams(("arbitrary",)),
        name="bias_tiles",
    )(rel_bias)


FAR_BLOCKS = 4


def _moba_kernel(rb_ref, qt_ref, kx_ref, vt_ref, km_ref, bias_ref, o_ref, *, nb):
    h = pl.program_id(1)
    i = pl.program_id(2)
    qt = qt_ref[...]

    km_hi, km_lo = _split_bf16(km_ref[...])
    score = (jnp.dot(km_hi, qt, preferred_element_type=F32)
             + jnp.dot(km_lo, qt, preferred_element_type=F32))
    bid = lax.broadcasted_iota(jnp.int32, score.shape, 0)
    neg_inf = jnp.float32(-jnp.inf)
    score = jnp.where(bid < i, score, neg_inf)
    chosen = bid < 0
    for _ in range(MOBA_TOPK):
        mx = jnp.max(score, axis=0, keepdims=True)
        hit = (score == mx) & (mx > neg_inf)
        idx = jnp.min(jnp.where(hit, bid, nb), axis=0, keepdims=True)
        pick = bid == idx
        chosen = chosen | pick
        score = jnp.where(pick, neg_inf, score)

    def with_mask(val):
        hi, lo = _split_bf16(val)
        fill = jnp.zeros((HEAD_DIM - 2 * EXT_LO, MOBA_BLOCK), BF16)
        if nb < EXT_LO:
            gap = jnp.zeros((EXT_LO - nb, MOBA_BLOCK), BF16)
            return jnp.concatenate([qt, hi, gap, lo, gap, fill], axis=0)
        return jnp.concatenate([qt, hi, lo, fill], axis=0)

    far_bias = rb_ref[REL_BUCKETS - 1, h]
    qx_far = with_mask(jnp.where(chosen & (bid <= i - 2), far_bias, MASK_NEG))
    qx_near = with_mask(jnp.where((bid == i) | (chosen & (bid == i - 1)), 0.0, MASK_NEG))

    jp = jnp.maximum(i - 1, 0)
    prev0 = pl.multiple_of(jp * MOBA_BLOCK, MOBA_BLOCK)
    own0 = pl.multiple_of(i * MOBA_BLOCK, MOBA_BLOCK)
    kx = jnp.concatenate([kx_ref[pl.ds(prev0, MOBA_BLOCK), :], kx_ref[pl.ds(own0, MOBA_BLOCK), :]], axis=0)
    no_prev = jnp.where(i > 0, 0.0, MASK_NEG)
    bias = jnp.concatenate([bias_ref[1] + no_prev, bias_ref[0]], axis=0)
    s = jnp.dot(kx, qx_near, preferred_element_type=F32) + bias
    m = jnp.max(s, axis=0, keepdims=True)
    p = jnp.exp(s - m).astype(BF16)
    vt = jnp.concatenate([vt_ref[jp], vt_ref[i]], axis=1)
    acc = jnp.dot(vt, p, preferred_element_type=F32)

    def far_step(c, carry):
        m, acc = carry
        row0 = pl.multiple_of(c * (FAR_BLOCKS * MOBA_BLOCK), FAR_BLOCKS * MOBA_BLOCK)
        s = jnp.dot(kx_ref[pl.ds(row0, FAR_BLOCKS * MOBA_BLOCK), :], qx_far, preferred_element_type=F32)
        m_new = jnp.maximum(m, jnp.max(s, axis=0, keepdims=True))
        alpha = jnp.exp(m - m_new)
        p = jnp.exp(s - m_new).astype(BF16)
        vt = jnp.concatenate([vt_ref[c * FAR_BLOCKS + k] for k in range(FAR_BLOCKS)], axis=1)
        return m_new, alpha * acc + jnp.dot(vt, p, preferred_element_type=F32)

    n_far = lax.shift_right_logical(jnp.maximum(i - 1, 0) + FAR_BLOCKS - 1, 2)
    m, acc = lax.fori_loop(0, n_far, far_step, (m, acc))
    out = acc[0:HEAD_DIM] / acc[HEAD_DIM:HEAD_DIM + 1]
    o_ref[...] = out.T.astype(BF16)


def _moba(rel_bias, qt, kx, vt, kmean, bias_tiles, batch, seq):
    nb = seq // MOBA_BLOCK
    assert nb % FAR_BLOCKS == 0 and FAR_BLOCKS == 4
    bh = lambda shape: pl.BlockSpec((None, None) + shape, lambda b, h, i: (b, h) + (0,) * len(shape))
    return pl.pallas_call(
        functools.partial(_moba_kernel, nb=nb),
        grid=(batch, ATT_HEADS, nb),
        in_specs=[pl.BlockSpec(memory_space=pltpu.SMEM),
                  pl.BlockSpec((None, None, None, HEAD_DIM, MOBA_BLOCK), lambda b, h, i: (b, h, i, 0, 0)),
                  bh((seq, 2 * HEAD_DIM)), bh((nb, VT_ROWS, MOBA_BLOCK)), bh((nb, HEAD_DIM)),
                  pl.BlockSpec((None, 2, MOBA_BLOCK, MOBA_BLOCK), lambda b, h, i: (h, 0, 0, 0))],
        out_specs=pl.BlockSpec((MOBA_BLOCK, HEAD_DIM), lambda b, h, i: (b * nb + i, h)),
        out_shape=jax.ShapeDtypeStruct((batch * seq, D_ATT), BF16),
        compiler_params=_cparams(("parallel", "parallel", "arbitrary")),
        name="moba",
    )(rel_bias, qt, kx, vt, kmean, bias_tiles)


def _merge_kernel(x_ref, ysg_ref, yatt_ref, gsg_ref, gatt_ref, mod_ref, nw_ref,
                  wsg_ref, watt_ref, wo_ref, wr_ref, x1_ref, h2_ref, gates_ref):
    a_sg = jnp.dot(ysg_ref[...], wsg_ref[...], preferred_element_type=F32)
    a_att = jnp.dot(yatt_ref[...], watt_ref[...], preferred_element_type=F32)
    merged = (jax.nn.sigmoid(gsg_ref[...].astype(F32)) * a_sg
              + jax.nn.sigmoid(gatt_ref[...].astype(F32)) * a_att)
    mixed = jnp.dot(merged.astype(BF16), wo_ref[...], preferred_element_type=F32)
    x1 = x_ref[...] + mod_ref[0, 2:3, :] * mixed
    x1_ref[...] = x1
    ms = jnp.mean(x1 * x1, axis=-1, keepdims=True)
    y = x1 * lax.rsqrt(ms + EPS) * nw_ref[...]
    h2 = y * (1.0 + mod_ref[0, 4:5, :]) + mod_ref[0, 3:4, :]
    h2_ref[...] = h2.astype(BF16)

    h_hi, h_lo = _split_bf16(h2)
    w_hi, w_lo = _split_bf16(wr_ref[...])
    logits = (jnp.dot(h_hi, w_hi, preferred_element_type=F32)
              + jnp.dot(h_lo, w_hi, preferred_element_type=F32)
              + jnp.dot(h_hi, w_lo, preferred_element_type=F32))

    lane = lax.broadcasted_iota(jnp.int32, logits.shape, 1)
    neg_inf = jnp.float32(-jnp.inf)
    in_g = lane < N_GROUPS
    lg = jnp.where(in_g, logits, neg_inf)
    mg = jnp.max(lg, axis=1, keepdims=True)
    eg = jnp.exp(lg - mg)
    g_prob = eg / jnp.sum(eg, axis=1, keepdims=True)
    g_p = jnp.max(g_prob, axis=1, keepdims=True)
    g_idx = jnp.min(jnp.where(g_prob == g_p, lane, LANES), axis=1, keepdims=True)
    lo_col = ROUTER_COL0 + EXPERTS_PER_GROUP * g_idx
    in_e = (lane >= lo_col) & (lane < lo_col + EXPERTS_PER_GROUP)
    le = jnp.where(in_e, logits, neg_inf)
    me = jnp.max(le, axis=1, keepdims=True)
    ee = jnp.exp(le - me)
    e_prob = jnp.where(in_e, ee / jnp.sum(ee, axis=1, keepdims=True), -1.0)
    p1 = jnp.max(e_prob, axis=1, keepdims=True)
    i1 = jnp.min(jnp.where(e_prob == p1, lane, LANES), axis=1, keepdims=True)
    rest = jnp.where(lane == i1, -1.0, e_prob)
    p2 = jnp.max(rest, axis=1, keepdims=True)
    i2 = jnp.min(jnp.where(rest == p2, lane, LANES), axis=1, keepdims=True)
    denom = p1 + p2
    gates_ref[...] = (jnp.where(lane == i1, g_p * (p1 / denom), 0.0)
                      + jnp.where(lane == i2, g_p * (p2 / denom), 0.0))


def _merge(x2, y_sg, y_att, proj, mod, norm2_w, w_sg, w_att, w_o, w_router, seq):
    n, d = x2.shape
    tm = 256
    tpb = seq // tm
    resident = lambda shape: pl.BlockSpec(shape, lambda i: (0, 0), pipeline_mode=pl.Buffered(1))
    return pl.pallas_call(
        _merge_kernel,
        grid=(n // tm,),
        in_specs=[pl.BlockSpec((tm, d), lambda i: (i, 0)),
                  pl.BlockSpec((tm, D_SG), lambda i: (i, 0)),
                  pl.BlockSpec((tm, D_ATT), lambda i: (i, 0)),
                  pl.BlockSpec((tm, d), lambda i: (i, BLK_GATE_SG)),
                  pl.BlockSpec((tm, d), lambda i: (i, BLK_GATE_ATT)),
                  pl.BlockSpec((1, 8, d), lambda i: (i // tpb, 0, 0)),
                  pl.BlockSpec((1, d), lambda i: (0, 0)),
                  resident((D_SG, d)), resident((D_ATT, d)), resident((d, d)),
                  resident((d, LANES))],
        out_specs=[pl.BlockSpec((tm, d), lambda i: (i, 0)),
                   pl.BlockSpec((tm, d), lambda i: (i, 0)),
                   pl.BlockSpec((tm, LANES), lambda i: (i, 0))],
        out_shape=[jax.ShapeDtypeStruct((n, d), F32),
                   jax.ShapeDtypeStruct((n, d), BF16),
                   jax.ShapeDtypeStruct((n, LANES), F32)],
        compiler_params=_cparams(("parallel",)),
        name="merge",
    )(x2, y_sg, y_att, proj, proj, mod, norm2_w, w_sg, w_att, w_o, w_router)


def _moe_kernel(t_ref, gates_ref, x1_ref, mod_ref, fw_ref, w1_ref, w3_ref, w2_ref, o_ref, acc_ref):
    e = pl.program_id(1)

    @pl.when(e == 0)
    def _():
        acc_ref[...] = jnp.zeros(acc_ref.shape, F32)

    t = t_ref[...]
    a = jnp.dot(t, w1_ref[...], preferred_element_type=F32)
    b = jnp.dot(t, w3_ref[...], preferred_element_type=F32)
    lane = lax.broadcasted_iota(jnp.int32, gates_ref.shape, 1)
    ge = jnp.sum(jnp.where(lane == ROUTER_COL0 + e, gates_ref[...], 0.0), axis=1, keepdims=True)
    hmid = (a * jax.nn.sigmoid(a)) * b * ge
    acc_ref[...] += jnp.dot(hmid.astype(BF16), w2_ref[...], preferred_element_type=F32)

    @pl.when(e == pl.num_programs(1) - 1)
    def _():
        x2 = x1_ref[...] + mod_ref[0, 5:6, :] * acc_ref[...]
        ms = jnp.mean(x2 * x2, axis=-1, keepdims=True)
        o_ref[...] = x2 * lax.rsqrt(ms + EPS) * fw_ref[...]


def _moe(h2, gates, x1, mod, final_w, w1, w3, w2, seq):
    n, d = h2.shape
    tm = 512
    tpb = seq // tm
    return pl.pallas_call(
        _moe_kernel,
        grid=(n // tm, N_EXPERTS),
        in_specs=[pl.BlockSpec((tm, d), lambda i, e: (i, 0)),
                  pl.BlockSpec((tm, LANES), lambda i, e: (i, 0)),
                  pl.BlockSpec((tm, d), lambda i, e: (i, 0)),
                  pl.BlockSpec((1, 8, d), lambda i, e: (i // tpb, 0, 0)),
                  pl.BlockSpec((1, d), lambda i, e: (0, 0)),
                  pl.BlockSpec((None, d, D_EXPERT), lambda i, e: (e, 0, 0)),
                  pl.BlockSpec((None, d, D_EXPERT), lambda i, e: (e, 0, 0)),
                  pl.BlockSpec((None, D_EXPERT, d), lambda i, e: (e, 0, 0))],
        out_specs=pl.BlockSpec((tm, d), lambda i, e: (i, 0)),
        out_shape=jax.ShapeDtypeStruct((n, d), F32),
        scratch_shapes=[pltpu.VMEM((tm, d), F32)],
        compiler_params=_cparams(("parallel", "arbitrary")),
        name="moe",
    )(h2, gates, x1, mod, final_w, w1, w3, w2)


def kernel(x, c, w_ada, b_ada, norm1_w, norm2_w, final_norm_w, w_in, sg_ln_w, sg_ln_b, w_spatial,
           b_spatial, rel_bias, w_out_sg, w_out_att, w_o, w_router_group, w_router_expert,
           w_exp_gate, w_exp_up, w_exp_down):
    batch, seq, d = x.shape
    assert d == D_MODEL and w_ada.shape[0] == 1
    assert seq % 1024 == 0 and seq // MOBA_BLOCK <= LANES
    n = batch * seq
    x2 = x.reshape(n, d)

    c_pad = jnp.zeros((8, d), F32).at[:batch].set(c)
    mod = _adaln(c_pad, w_ada[0], b_ada[0].reshape(1, 6 * d))
    mod = mod[:batch].reshape(batch, 6, d)
    mod = jnp.concatenate([mod, jnp.zeros((batch, 2, d), F32)], axis=1)

    colscale = jnp.ones((1, IN_COLS), F32).at[:, CB_Q * LANES:CB_K * LANES].set(HEAD_DIM ** -0.5)
    w_in_r = jnp.concatenate([w_in[0][:, IN_COLS - N_GATE_COLS:], w_in[0][:, :IN_COLS - N_GATE_COLS]],
                             axis=1).astype(BF16)
    proj = _proj(x2, mod, norm1_w[0].reshape(1, d), w_in_r, colscale, seq)

    y_sg = _sgu(proj, sg_ln_w[0].reshape(1, D_SG), sg_ln_b[0].reshape(1, D_SG),
                w_spatial[0], b_spatial[0].T)

    qt, kx, vt, kmean = _moba_prep(proj, batch, seq)
    bias_tiles = _bias_tiles(rel_bias)
    y_att = _moba(rel_bias, qt, kx, vt, kmean, bias_tiles, batch, seq)

    w_router = jnp.zeros((d, LANES), F32)
    w_router = w_router.at[:, :N_GROUPS].set(w_router_group[0])
    w_router = w_router.at[:, ROUTER_COL0:ROUTER_COL0 + N_EXPERTS].set(w_router_expert[0])
    x1, h2, gates = _merge(x2, y_sg, y_att, proj, mod, norm2_w[0].reshape(1, d),
                           w_out_sg[0].astype(BF16), w_out_att[0].astype(BF16),
                           w_o[0].astype(BF16), w_router, seq)

    out = _moe(h2, gates, x1, mod, final_norm_w.reshape(1, d), w_exp_gate[0].astype(BF16),
               w_exp_up[0].astype(BF16), w_exp_down[0].astype(BF16), seq)
    return out.reshape(batch, seq, d)
```

```python
import functools
import math

import jax
import jax.numpy as jnp
from jax import lax
from jax.experimental import pallas as pl
from jax.experimental.pallas import tpu as pltpu

F32 = jnp.float32
BF16 = jnp.bfloat16

LANES = 128
D_MODEL = 2048
D_SG = D_MODEL // 2
SG_GROUPS = 8
SG_CHUNK = 128
ATT_HEADS = 8
HEAD_DIM = 128
D_ATT = ATT_HEADS * HEAD_DIM
MOBA_BLOCK = 256
MOBA_TOPK = 3
REL_BUCKETS = 32
REL_MAX_DIST = 128
N_GROUPS = 4
EXPERTS_PER_GROUP = 4
N_EXPERTS = N_GROUPS * EXPERTS_PER_GROUP
D_EXPERT = 512
EPS = 1e-6
IN_COLS = 2 * D_SG + 3 * D_ATT + 2 * D_MODEL

N_GATE_COLS = 2 * D_MODEL
BLK_GATE_SG = 0
BLK_GATE_ATT = 1
BLK_U = N_GATE_COLS // D_SG
BLK_V = BLK_U + 1
CB_Q = (N_GATE_COLS + 2 * D_SG) // LANES
CB_K = CB_Q + ATT_HEADS
CB_VAL = CB_K + ATT_HEADS

MASK_NEG = -1e9
ROUTER_COL0 = N_GROUPS
VMEM_LIMIT = 56 * 1024 * 1024


def _cparams(sem):
    return pltpu.CompilerParams(dimension_semantics=sem, vmem_limit_bytes=VMEM_LIMIT)


def _nt_dot(a, b):
    return lax.dot_general(a, b, (((1,), (1,)), ((), ())), preferred_element_type=F32)


def _split_bf16(a):
    hi = a.astype(BF16)
    lo = (a - hi.astype(F32)).astype(BF16)
    return hi, lo


def _adaln_kernel(c_ref, w_ref, b_ref, o_ref):
    c = c_ref[...]
    ca = c * jax.nn.sigmoid(c)
    o_ref[...] = jnp.dot(ca, w_ref[...], preferred_element_type=F32,
                         precision=lax.Precision.HIGHEST) + b_ref[...]


def _adaln(c_pad, w, b):
    rows, d = c_pad.shape
    cols = w.shape[1]
    tn = 1024
    return pl.pallas_call(
        _adaln_kernel,
        grid=(cols // tn,),
        in_specs=[pl.BlockSpec((rows, d), lambda j: (0, 0)),
                  pl.BlockSpec((d, tn), lambda j: (0, j)),
                  pl.BlockSpec((1, tn), lambda j: (0, j))],
        out_specs=pl.BlockSpec((rows, tn), lambda j: (0, j)),
        out_shape=jax.ShapeDtypeStruct((rows, cols), F32),
        compiler_params=_cparams(("arbitrary",)),
        name="adaln",
    )(c_pad, w, b)


def _proj_kernel(x_ref, mod_ref, nw_ref, w_ref, cs_ref, o_ref, h_scr):
    @pl.when(pl.program_id(1) == 0)
    def _():
        x = x_ref[...]
        ms = jnp.mean(x * x, axis=-1, keepdims=True)
        y = x * lax.rsqrt(ms + EPS) * nw_ref[...]
        h = y * (1.0 + mod_ref[0, 1:2, :]) + mod_ref[0, 0:1, :]
        h_scr[...] = h.astype(BF16)

    acc = jnp.dot(h_scr[...], w_ref[...], preferred_element_type=F32)
    o_ref[...] = (acc * cs_ref[...]).astype(BF16)


def _proj(x2, mod, norm_w, w_in, colscale, seq):
    n, d = x2.shape
    cols = w_in.shape[1]
    tm = min(1024, seq)
    tn = 1024
    tpb = seq // tm
    return pl.pallas_call(
        _proj_kernel,
        grid=(n // tm, cols // tn),
        in_specs=[pl.BlockSpec((tm, d), lambda i, j: (i, 0)),
                  pl.BlockSpec((1, 8, d), lambda i, j: (i // tpb, 0, 0)),
                  pl.BlockSpec((1, d), lambda i, j: (0, 0)),
                  pl.BlockSpec((d, tn), lambda i, j: (0, j)),
                  pl.BlockSpec((1, tn), lambda i, j: (0, j))],
        out_specs=pl.BlockSpec((tm, tn), lambda i, j: (i, j)),
        out_shape=jax.ShapeDtypeStruct((n, cols), BF16),
        scratch_shapes=[pltpu.VMEM((tm, d), BF16)],
        compiler_params=_cparams(("parallel", "arbitrary")),
        name="proj",
    )(x2, mod, norm_w, w_in, colscale)


def _gelu(a):
    return 0.5 * a * (1.0 + lax.erf(a * (1.0 / math.sqrt(2.0))))


def _sgu_kernel(u_ref, v_ref, lnw_ref, lnb_ref, ws_ref, bst_ref, o_ref, *, chunks):
    u = _gelu(u_ref[...].astype(F32))
    v = _gelu(v_ref[...].astype(F32))
    mu = jnp.mean(v, axis=-1, keepdims=True)
    vc = v - mu
    var = jnp.mean(vc * vc, axis=-1, keepdims=True)
    vn = (vc * lax.rsqrt(var + EPS) * lnw_ref[...] + lnb_ref[...]).astype(BF16)
    row = lax.broadcasted_iota(jnp.int32, (SG_CHUNK, SG_CHUNK), 0)
    col = lax.broadcasted_iota(jnp.int32, (SG_CHUNK, SG_CHUNK), 1)
    causal = col <= row
    for g in range(SG_GROUPS):
        wm = jnp.where(causal, ws_ref[g], 0.0).astype(BF16)
        bcol = bst_ref[:, g:g + 1]
        gs = slice(g * LANES, (g + 1) * LANES)
        for c in range(chunks):
            rs = slice(c * SG_CHUNK, (c + 1) * SG_CHUNK)
            z = jnp.dot(wm, vn[rs, gs], preferred_element_type=F32) + bcol
            o_ref[rs, gs] = (u[rs, gs] * z).astype(BF16)


def _sgu(proj, ln_w, ln_b, w_s, b_s_t):
    n = proj.shape[0]
    chunks = 4
    tm = chunks * SG_CHUNK
    return pl.pallas_call(
        functools.partial(_sgu_kernel, chunks=chunks),
        grid=(n // tm,),
        in_specs=[pl.BlockSpec((tm, D_SG), lambda i: (i, BLK_U)),
                  pl.BlockSpec((tm, D_SG), lambda i: (i, BLK_V)),
                  pl.BlockSpec((1, D_SG), lambda i: (0, 0)),
                  pl.BlockSpec((1, D_SG), lambda i: (0, 0)),
                  pl.BlockSpec((SG_GROUPS, SG_CHUNK, SG_CHUNK), lambda i: (0, 0, 0)),
                  pl.BlockSpec((SG_CHUNK, SG_GROUPS), lambda i: (0, 0))],
        out_specs=pl.BlockSpec((tm, D_SG), lambda i: (i, 0)),
        out_shape=jax.ShapeDtypeStruct((n, D_SG), BF16),
        compiler_params=_cparams(("parallel",)),
        name="sgu",
    )(proj, proj, ln_w, ln_b, w_s, b_s_t)


VT_ROWS = HEAD_DIM + 16
EXT_LO = REL_BUCKETS


def _moba_prep_kernel(q_ref, k_ref, v_ref, qt_ref, kx_ref, vt_ref, km_ref, *, nb):
    lane = lax.broadcasted_iota(jnp.int32, (MOBA_BLOCK, LANES), 1)
    pad = jnp.concatenate([jnp.ones((1, MOBA_BLOCK), F32),
                           jnp.zeros((VT_ROWS - HEAD_DIM - 1, MOBA_BLOCK), F32)], axis=0)
    for j in range(nb):
        rows = slice(j * MOBA_BLOCK, (j + 1) * MOBA_BLOCK)
        kj = k_ref[rows, :]
        onehot = jnp.where((lane == j) | (lane == j + EXT_LO), 1.0, 0.0).astype(BF16)
        kx_ref[rows, :] = jnp.concatenate([kj, onehot], axis=1)
        km_ref[j:j + 1, :] = jnp.mean(kj.astype(F32), axis=0, keepdims=True)
        vt = v_ref[rows, :].astype(F32).T
        vt_ref[j] = jnp.concatenate([vt, pad], axis=0).astype(BF16)
        qt_ref[j] = q_ref[rows, :].astype(F32).T.astype(BF16)


def _moba_prep(proj, batch, seq):
    nb = seq // MOBA_BLOCK
    bh = lambda shape: pl.BlockSpec((None, None) + shape, lambda b, h: (b, h) + (0,) * len(shape))
    return pl.pallas_call(
        functools.partial(_moba_prep_kernel, nb=nb),
        grid=(batch, ATT_HEADS),
        in_specs=[pl.BlockSpec((seq, HEAD_DIM), lambda b, h: (b, CB_Q + h)),
                  pl.BlockSpec((seq, HEAD_DIM), lambda b, h: (b, CB_K + h)),
                  pl.BlockSpec((seq, HEAD_DIM), lambda b, h: (b, CB_VAL + h))],
        out_specs=[bh((nb, HEAD_DIM, MOBA_BLOCK)), bh((seq, 2 * HEAD_DIM)),
                   bh((nb, VT_ROWS, MOBA_BLOCK)), bh((nb, HEAD_DIM))],
        out_shape=[jax.ShapeDtypeStruct((batch, ATT_HEADS, nb, HEAD_DIM, MOBA_BLOCK), BF16),
                   jax.ShapeDtypeStruct((batch, ATT_HEADS, seq, 2 * HEAD_DIM), BF16),
                   jax.ShapeDtypeStruct((batch, ATT_HEADS, nb, VT_ROWS, MOBA_BLOCK), BF16),
                   jax.ShapeDtypeStruct((batch, ATT_HEADS, nb, HEAD_DIM), F32)],
        compiler_params=_cparams(("parallel", "parallel")),
        name="moba_prep",
    )(proj, proj, proj)


def _bias_tiles_kernel(rb_ref, o_ref):
    h = pl.program_id(0)
    kj = lax.broadcasted_iota(jnp.int32, (MOBA_BLOCK, MOBA_BLOCK), 0)
    qi = lax.broadcasted_iota(jnp.int32, (MOBA_BLOCK, MOBA_BLOCK), 1)
    max_exact = REL_BUCKETS // 2
    for t in range(2):
        rel = qi - kj + MOBA_BLOCK * t
        n = jnp.maximum(rel, 0)
        nf = jnp.maximum(n, max_exact).astype(F32)
        large = max_exact + (jnp.log(nf / max_exact) / math.log(REL_MAX_DIST / max_exact)
                             * (REL_BUCKETS - max_exact)).astype(jnp.int32)
        large = jnp.minimum(large, REL_BUCKETS - 1)
        bucket = jnp.where(n < max_exact, n, large)
        bias = jnp.zeros((MOBA_BLOCK, MOBA_BLOCK), F32)
        for r in range(REL_BUCKETS):
            bias = jnp.where(bucket == r, rb_ref[r, h], bias)
        o_ref[t] = jnp.where(rel >= 0, bias, MASK_NEG)


def _bias_tiles(rel_bias):
    return pl.pallas_call(
        _bias_tiles_kernel,
        grid=(ATT_HEADS,),
        in_specs=[pl.BlockSpec(memory_space=pltpu.SMEM)],
        out_specs=pl.BlockSpec((None, 2, MOBA_BLOCK, MOBA_BLOCK), lambda h: (h, 0, 0, 0)),
        out_shape=jax.ShapeDtypeStruct((ATT_HEADS, 2, MOBA_BLOCK, MOBA_BLOCK), F32),
        compiler_params=_cparams(("arbitrary",)),
        name="bias_tiles",
    )(rel_bias)


FAR_BLOCKS = 4
HEADS_PER_STEP = 2


def _moba_kernel(rb_ref, qt_ref, kx_ref, vt_ref, km_ref, bias_ref, o_ref, *, nb):
    i = pl.program_id(2)
    jp = jnp.maximum(i - 1, 0)
    prev0 = pl.multiple_of(jp * MOBA_BLOCK, MOBA_BLOCK)
    own0 = pl.multiple_of(i * MOBA_BLOCK, MOBA_BLOCK)
    no_prev = jnp.where(i > 0, 0.0, MASK_NEG)
    neg_inf = jnp.float32(-jnp.inf)

    def head_start(hh):
        h = pl.program_id(1) * HEADS_PER_STEP + hh
        qt = qt_ref[hh]
        km_hi, km_lo = _split_bf16(km_ref[hh])
        score = (jnp.dot(km_hi, qt, preferred_element_type=F32)
                 + jnp.dot(km_lo, qt, preferred_element_type=F32))
        bid = lax.broadcasted_iota(jnp.int32, score.shape, 0)
        score = jnp.where(bid < i, score, neg_inf)
        chosen = bid < 0
        for _ in range(MOBA_TOPK):
            mx = jnp.max(score, axis=0, keepdims=True)
            hit = (score == mx) & (mx > neg_inf)
            idx = jnp.min(jnp.where(hit, bid, nb), axis=0, keepdims=True)
            pick = bid == idx
            chosen = chosen | pick
            score = jnp.where(pick, neg_inf, score)

        def with_mask(val):
            hi, lo = _split_bf16(val)
            fill = jnp.zeros((HEAD_DIM - 2 * EXT_LO, MOBA_BLOCK), BF16)
            if nb < EXT_LO:
                gap = jnp.zeros((EXT_LO - nb, MOBA_BLOCK), BF16)
                return jnp.concatenate([qt, hi, gap, lo, gap, fill], axis=0)
            return jnp.concatenate([qt, hi, lo, fill], axis=0)

        far_bias = rb_ref[REL_BUCKETS - 1, h]
        qx_far = with_mask(jnp.where(chosen & (bid <= i - 2), far_bias, MASK_NEG))
        qx_near = with_mask(jnp.where((bid == i) | (chosen & (bid == i - 1)), 0.0, MASK_NEG))

        kx = jnp.concatenate([kx_ref[hh, pl.ds(prev0, MOBA_BLOCK), :],
                              kx_ref[hh, pl.ds(own0, MOBA_BLOCK), :]], axis=0)
        bias = jnp.concatenate([bias_ref[hh, 1] + no_prev, bias_ref[hh, 0]], axis=0)
        s = jnp.dot(kx, qx_near, preferred_element_type=F32) + bias
        m = jnp.max(s, axis=0, keepdims=True)
        p = jnp.exp(s - m).astype(BF16)
        vt = jnp.concatenate([vt_ref[hh, jp], vt_ref[hh, i]], axis=1)
        acc = jnp.dot(vt, p, preferred_element_type=F32)
        return qx_far, m, acc

    starts = [head_start(hh) for hh in range(HEADS_PER_STEP)]
    qx_fars = [st[0] for st in starts]

    def far_step(c, carry):
        row0 = pl.multiple_of(c * (FAR_BLOCKS * MOBA_BLOCK), FAR_BLOCKS * MOBA_BLOCK)
        heads = range(HEADS_PER_STEP)
        ss = [jnp.dot(kx_ref[hh, pl.ds(row0, FAR_BLOCKS * MOBA_BLOCK), :], qx_fars[hh],
                      preferred_element_type=F32) for hh in heads]
        m_news = [jnp.maximum(carry[hh][0], jnp.max(ss[hh], axis=0, keepdims=True)) for hh in heads]
        ps = [jnp.exp(ss[hh] - m_news[hh]).astype(BF16) for hh in heads]
        out = []
        for hh in heads:
            m, acc = carry[hh]
            alpha = jnp.exp(m - m_news[hh])
            vt = jnp.concatenate([vt_ref[hh, c * FAR_BLOCKS + k] for k in range(FAR_BLOCKS)], axis=1)
            out.append((m_news[hh], alpha * acc + jnp.dot(vt, ps[hh], preferred_element_type=F32)))
        return tuple(out)

    n_far = lax.shift_right_logical(jnp.maximum(i - 1, 0) + FAR_BLOCKS - 1, 2)
    final = lax.fori_loop(0, n_far, far_step, tuple((st[1], st[2]) for st in starts))
    for hh in range(HEADS_PER_STEP):
        acc = final[hh][1]
        out = acc[0:HEAD_DIM] / acc[HEAD_DIM:HEAD_DIM + 1]
        o_ref[:, hh * HEAD_DIM:(hh + 1) * HEAD_DIM] = out.T.astype(BF16)


def _moba(rel_bias, qt, kx, vt, kmean, bias_tiles, batch, seq):
    nb = seq // MOBA_BLOCK
    hps = HEADS_PER_STEP
    assert nb % FAR_BLOCKS == 0 and FAR_BLOCKS == 4 and ATT_HEADS % hps == 0
    bh = lambda shape: pl.BlockSpec((None, hps) + shape, lambda b, h, i: (b, h) + (0,) * len(shape))
    return pl.pallas_call(
        functools.partial(_moba_kernel, nb=nb),
        grid=(batch, ATT_HEADS // hps, nb),
        in_specs=[pl.BlockSpec(memory_space=pltpu.SMEM),
                  pl.BlockSpec((None, hps, None, HEAD_DIM, MOBA_BLOCK), lambda b, h, i: (b, h, i, 0, 0)),
                  bh((seq, 2 * HEAD_DIM)), bh((nb, VT_ROWS, MOBA_BLOCK)), bh((nb, HEAD_DIM)),
                  pl.BlockSpec((hps, 2, MOBA_BLOCK, MOBA_BLOCK), lambda b, h, i: (h, 0, 0, 0))],
        out_specs=pl.BlockSpec((MOBA_BLOCK, hps * HEAD_DIM), lambda b, h, i: (b * nb + i, h)),
        out_shape=jax.ShapeDtypeStruct((batch * seq, D_ATT), BF16),
        compiler_params=_cparams(("parallel", "parallel", "arbitrary")),
        name="moba",
    )(rel_bias, qt, kx, vt, kmean, bias_tiles)


def _merge_kernel(x_ref, ysg_ref, yatt_ref, gsg_ref, gatt_ref, mod_ref, nw_ref,
                  wsg_ref, watt_ref, wo_ref, wr_ref, x1_ref, h2_ref, gates_ref):
    a_sg = jnp.dot(ysg_ref[...], wsg_ref[...], preferred_element_type=F32)
    a_att = jnp.dot(yatt_ref[...], watt_ref[...], preferred_element_type=F32)
    merged = (jax.nn.sigmoid(gsg_ref[...].astype(F32)) * a_sg
              + jax.nn.sigmoid(gatt_ref[...].astype(F32)) * a_att)
    mixed = jnp.dot(merged.astype(BF16), wo_ref[...], preferred_element_type=F32)
    x1 = x_ref[...] + mod_ref[0, 2:3, :] * mixed
    x1_ref[...] = x1
    ms = jnp.mean(x1 * x1, axis=-1, keepdims=True)
    y = x1 * lax.rsqrt(ms + EPS) * nw_ref[...]
    h2 = y * (1.0 + mod_ref[0, 4:5, :]) + mod_ref[0, 3:4, :]
    h2_ref[...] = h2.astype(BF16)

    h_hi, h_lo = _split_bf16(h2)
    w_hi, w_lo = _split_bf16(wr_ref[...])
    logits = (jnp.dot(h_hi, w_hi, preferred_element_type=F32)
              + jnp.dot(h_lo, w_hi, preferred_element_type=F32)
              + jnp.dot(h_hi, w_lo, preferred_element_type=F32))

    lane = lax.broadcasted_iota(jnp.int32, logits.shape, 1)
    neg_inf = jnp.float32(-jnp.inf)
    in_g = lane < N_GROUPS
    lg = jnp.where(in_g, logits, neg_inf)
    mg = jnp.max(lg, axis=1, keepdims=True)
    eg = jnp.exp(lg - mg)
    g_prob = eg / jnp.sum(eg, axis=1, keepdims=True)
    g_p = jnp.max(g_prob, axis=1, keepdims=True)
    g_idx = jnp.min(jnp.where(g_prob == g_p, lane, LANES), axis=1, keepdims=True)
    lo_col = ROUTER_COL0 + EXPERTS_PER_GROUP * g_idx
    in_e = (lane >= lo_col) & (lane < lo_col + EXPERTS_PER_GROUP)
    le = jnp.where(in_e, logits, neg_inf)
    me = jnp.max(le, axis=1, keepdims=True)
    ee = jnp.exp(le - me)
    e_prob = jnp.where(in_e, ee / jnp.sum(ee, axis=1, keepdims=True), -1.0)
    p1 = jnp.max(e_prob, axis=1, keepdims=True)
    i1 = jnp.min(jnp.where(e_prob == p1, lane, LANES), axis=1, keepdims=True)
    rest = jnp.where(lane == i1, -1.0, e_prob)
    p2 = jnp.max(rest, axis=1, keepdims=True)
    i2 = jnp.min(jnp.where(rest == p2, lane, LANES), axis=1, keepdims=True)
    denom = p1 + p2
    gates_ref[...] = (jnp.where(lane == i1, g_p * (p1 / denom), 0.0)
                      + jnp.where(lane == i2, g_p * (p2 / denom), 0.0))


def _merge(x2, y_sg, y_att, proj, mod, norm2_w, w_sg, w_att, w_o, w_router, seq):
    n, d = x2.shape
    tm = 256
    tpb = seq // tm
    resident = lambda shape: pl.BlockSpec(shape, lambda i: (0, 0), pipeline_mode=pl.Buffered(1))
    return pl.pallas_call(
        _merge_kernel,
        grid=(n // tm,),
        in_specs=[pl.BlockSpec((tm, d), lambda i: (i, 0)),
                  pl.BlockSpec((tm, D_SG), lambda i: (i, 0)),
                  pl.BlockSpec((tm, D_ATT), lambda i: (i, 0)),
                  pl.BlockSpec((tm, d), lambda i: (i, BLK_GATE_SG)),
                  pl.BlockSpec((tm, d), lambda i: (i, BLK_GATE_ATT)),
                  pl.BlockSpec((1, 8, d), lambda i: (i // tpb, 0, 0)),
                  pl.BlockSpec((1, d), lambda i: (0, 0)),
                  resident((D_SG, d)), resident((D_ATT, d)), resident((d, d)),
                  resident((d, LANES))],
        out_specs=[pl.BlockSpec((tm, d), lambda i: (i, 0)),
                   pl.BlockSpec((tm, d), lambda i: (i, 0)),
                   pl.BlockSpec((tm, LANES), lambda i: (i, 0))],
        out_shape=[jax.ShapeDtypeStruct((n, d), F32),
                   jax.ShapeDtypeStruct((n, d), BF16),
                   jax.ShapeDtypeStruct((n, LANES), F32)],
        compiler_params=_cparams(("parallel",)),
        name="merge",
    )(x2, y_sg, y_att, proj, proj, mod, norm2_w, w_sg, w_att, w_o, w_router)


def _moe_kernel(t_ref, gates_ref, x1_ref, mod_ref, fw_ref, w1_ref, w3_ref, w2_ref, o_ref, acc_ref):
    e = pl.program_id(1)

    @pl.when(e == 0)
    def _():
        acc_ref[...] = jnp.zeros(acc_ref.shape, F32)

    t = t_ref[...]
    a = jnp.dot(t, w1_ref[...], preferred_element_type=F32)
    b = jnp.dot(t, w3_ref[...], preferred_element_type=F32)
    lane = lax.broadcasted_iota(jnp.int32, gates_ref.shape, 1)
    ge = jnp.sum(jnp.where(lane == ROUTER_COL0 + e, gates_ref[...], 0.0), axis=1, keepdims=True)
    hmid = (a * jax.nn.sigmoid(a)) * b * ge
    acc_ref[...] += jnp.dot(hmid.astype(BF16), w2_ref[...], preferred_element_type=F32)

    @pl.when(e == pl.num_programs(1) - 1)
    def _():
        x2 = x1_ref[...] + mod_ref[0, 5:6, :] * acc_ref[...]
        ms = jnp.mean(x2 * x2, axis=-1, keepdims=True)
        o_ref[...] = x2 * lax.rsqrt(ms + EPS) * fw_ref[...]


def _moe(h2, gates, x1, mod, final_w, w1, w3, w2, seq):
    n, d = h2.shape
    tm = 512
    tpb = seq // tm
    return pl.pallas_call(
        _moe_kernel,
        grid=(n // tm, N_EXPERTS),
        in_specs=[pl.BlockSpec((tm, d), lambda i, e: (i, 0)),
                  pl.BlockSpec((tm, LANES), lambda i, e: (i, 0)),
                  pl.BlockSpec((tm, d), lambda i, e: (i, 0)),
                  pl.BlockSpec((1, 8, d), lambda i, e: (i // tpb, 0, 0)),
                  pl.BlockSpec((1, d), lambda i, e: (0, 0)),
                  pl.BlockSpec((None, d, D_EXPERT), lambda i, e: (e, 0, 0)),
                  pl.BlockSpec((None, d, D_EXPERT), lambda i, e: (e, 0, 0)),
                  pl.BlockSpec((None, D_EXPERT, d), lambda i, e: (e, 0, 0))],
        out_specs=pl.BlockSpec((tm, d), lambda i, e: (i, 0)),
        out_shape=jax.ShapeDtypeStruct((n, d), F32),
        scratch_shapes=[pltpu.VMEM((tm, d), F32)],
        compiler_params=_cparams(("parallel", "arbitrary")),
        name="moe",
    )(h2, gates, x1, mod, final_w, w1, w3, w2)


def kernel(x, c, w_ada, b_ada, norm1_w, norm2_w, final_norm_w, w_in, sg_ln_w, sg_ln_b, w_spatial,
           b_spatial, rel_bias, w_out_sg, w_out_att, w_o, w_router_group, w_router_expert,
           w_exp_gate, w_exp_up, w_exp_down):
    batch, seq, d = x.shape
    assert d == D_MODEL and w_ada.shape[0] == 1
    assert seq % 1024 == 0 and seq // MOBA_BLOCK <= LANES
    n = batch * seq
    x2 = x.reshape(n, d)

    c_pad = jnp.zeros((8, d), F32).at[:batch].set(c)
    mod = _adaln(c_pad, w_ada[0], b_ada[0].reshape(1, 6 * d))
    mod = mod[:batch].reshape(batch, 6, d)
    mod = jnp.concatenate([mod, jnp.zeros((batch, 2, d), F32)], axis=1)

    colscale = jnp.ones((1, IN_COLS), F32).at[:, CB_Q * LANES:CB_K * LANES].set(HEAD_DIM ** -0.5)
    w_in_r = jnp.concatenate([w_in[0][:, IN_COLS - N_GATE_COLS:], w_in[0][:, :IN_COLS - N_GATE_COLS]],
                             axis=1).astype(BF16)
    proj = _proj(x2, mod, norm1_w[0].reshape(1, d), w_in_r, colscale, seq)

    y_sg = _sgu(proj, sg_ln_w[0].reshape(1, D_SG), sg_ln_b[0].reshape(1, D_SG),
                w_spatial[0], b_spatial[0].T)

    qt, kx, vt, kmean = _moba_prep(proj, batch, seq)
    bias_tiles = _bias_tiles(rel_bias)
    y_att = _moba(rel_bias, qt, kx, vt, kmean, bias_tiles, batch, seq)

    w_router = jnp.zeros((d, LANES), F32)
    w_router = w_router.at[:, :N_GROUPS].set(w_router_group[0])
    w_router = w_router.at[:, ROUTER_COL0:ROUTER_COL0 + N_EXPERTS].set(w_router_expert[0])
    x1, h2, gates = _merge(x2, y_sg, y_att, proj, mod, norm2_w[0].reshape(1, d),
                           w_out_sg[0].astype(BF16), w_out_att[0].astype(BF16),
                           w_o[0].astype(BF16), w_router, seq)

    out = _moe(h2, gates, x1, mod, final_norm_w.reshape(1, d), w_exp_gate[0].astype(BF16),
               w_exp_up[0].astype(BF16), w_exp_down[0].astype(BF16), seq)
    return out.reshape(batch, seq, d)
```

```python
import functools
import math

import jax
import jax.numpy as jnp
from jax import lax
from jax.experimental import pallas as pl
from jax.experimental.pallas import tpu as pltpu

F32 = jnp.float32
BF16 = jnp.bfloat16

LANES = 128
D_MODEL = 2048
D_SG = D_MODEL // 2
SG_GROUPS = 8
SG_CHUNK = 128
ATT_HEADS = 8
HEAD_DIM = 128
D_ATT = ATT_HEADS * HEAD_DIM
MOBA_BLOCK = 256
MOBA_TOPK = 3
REL_BUCKETS = 32
REL_MAX_DIST = 128
N_GROUPS = 4
EXPERTS_PER_GROUP = 4
N_EXPERTS = N_GROUPS * EXPERTS_PER_GROUP
D_EXPERT = 512
EPS = 1e-6
IN_COLS = 2 * D_SG + 3 * D_ATT + 2 * D_MODEL

N_GATE_COLS = 2 * D_MODEL
BLK_GATE_SG = 0
BLK_GATE_ATT = 1
BLK_U = N_GATE_COLS // D_SG
BLK_V = BLK_U + 1
CB_Q = (N_GATE_COLS + 2 * D_SG) // LANES
CB_K = CB_Q + ATT_HEADS
CB_VAL = CB_K + ATT_HEADS

MASK_NEG = -1e9
ROUTER_COL0 = N_GROUPS
GROUP_LANE = 0
TOK_COLS = D_MODEL + LANES
VMEM_LIMIT = 56 * 1024 * 1024


def _cparams(sem):
    return pltpu.CompilerParams(dimension_semantics=sem, vmem_limit_bytes=VMEM_LIMIT)


def _nt_dot(a, b):
    return lax.dot_general(a, b, (((1,), (1,)), ((), ())), preferred_element_type=F32)


def _split_bf16(a):
    hi = a.astype(BF16)
    lo = (a - hi.astype(F32)).astype(BF16)
    return hi, lo


def _adaln_kernel(c_ref, w_ref, b_ref, o_ref):
    c = c_ref[...]
    ca = c * jax.nn.sigmoid(c)
    o_ref[...] = jnp.dot(ca, w_ref[...], preferred_element_type=F32,
                         precision=lax.Precision.HIGHEST) + b_ref[...]


def _adaln(c_pad, w, b):
    rows, d = c_pad.shape
    cols = w.shape[1]
    tn = 1024
    return pl.pallas_call(
        _adaln_kernel,
        grid=(cols // tn,),
        in_specs=[pl.BlockSpec((rows, d), lambda j: (0, 0)),
                  pl.BlockSpec((d, tn), lambda j: (0, j)),
                  pl.BlockSpec((1, tn), lambda j: (0, j))],
        out_specs=pl.BlockSpec((rows, tn), lambda j: (0, j)),
        out_shape=jax.ShapeDtypeStruct((rows, cols), F32),
        compiler_params=_cparams(("arbitrary",)),
        name="adaln",
    )(c_pad, w, b)


def _proj_kernel(x_ref, mod_ref, nw_ref, w_ref, cs_ref, o_ref, h_scr):
    @pl.when(pl.program_id(1) == 0)
    def _():
        x = x_ref[...]
        ms = jnp.mean(x * x, axis=-1, keepdims=True)
        y = x * lax.rsqrt(ms + EPS) * nw_ref[...]
        h = y * (1.0 + mod_ref[0, 1:2, :]) + mod_ref[0, 0:1, :]
        h_scr[...] = h.astype(BF16)

    acc = jnp.dot(h_scr[...], w_ref[...], preferred_element_type=F32)
    o_ref[...] = (acc * cs_ref[...]).astype(BF16)


def _proj(x2, mod, norm_w, w_in, colscale, seq):
    n, d = x2.shape
    cols = w_in.shape[1]
    tm = min(1024, seq)
    tn = 1024
    tpb = seq // tm
    return pl.pallas_call(
        _proj_kernel,
        grid=(n // tm, cols // tn),
        in_specs=[pl.BlockSpec((tm, d), lambda i, j: (i, 0)),
                  pl.BlockSpec((1, 8, d), lambda i, j: (i // tpb, 0, 0)),
                  pl.BlockSpec((1, d), lambda i, j: (0, 0)),
                  pl.BlockSpec((d, tn), lambda i, j: (0, j)),
                  pl.BlockSpec((1, tn), lambda i, j: (0, j))],
        out_specs=pl.BlockSpec((tm, tn), lambda i, j: (i, j)),
        out_shape=jax.ShapeDtypeStruct((n, cols), BF16),
        scratch_shapes=[pltpu.VMEM((tm, d), BF16)],
        compiler_params=_cparams(("parallel", "arbitrary")),
        name="proj",
    )(x2, mod, norm_w, w_in, colscale)


def _gelu(a):
    return 0.5 * a * (1.0 + lax.erf(a * (1.0 / math.sqrt(2.0))))


def _sgu_kernel(u_ref, v_ref, lnw_ref, lnb_ref, ws_ref, bst_ref, o_ref, *, chunks):
    u = _gelu(u_ref[...].astype(F32))
    v = _gelu(v_ref[...].astype(F32))
    mu = jnp.mean(v, axis=-1, keepdims=True)
    vc = v - mu
    var = jnp.mean(vc * vc, axis=-1, keepdims=True)
    vn = (vc * lax.rsqrt(var + EPS) * lnw_ref[...] + lnb_ref[...]).astype(BF16)
    row = lax.broadcasted_iota(jnp.int32, (SG_CHUNK, SG_CHUNK), 0)
    col = lax.broadcasted_iota(jnp.int32, (SG_CHUNK, SG_CHUNK), 1)
    causal = col <= row
    for g in range(SG_GROUPS):
        wm = jnp.where(causal, ws_ref[g], 0.0).astype(BF16)
        bcol = bst_ref[:, g:g + 1]
        gs = slice(g * LANES, (g + 1) * LANES)
        for c in range(chunks):
            rs = slice(c * SG_CHUNK, (c + 1) * SG_CHUNK)
            z = jnp.dot(wm, vn[rs, gs], preferred_element_type=F32) + bcol
            o_ref[rs, gs] = (u[rs, gs] * z).astype(BF16)


def _sgu(proj, ln_w, ln_b, w_s, b_s_t):
    n = proj.shape[0]
    chunks = 4
    tm = chunks * SG_CHUNK
    return pl.pallas_call(
        functools.partial(_sgu_kernel, chunks=chunks),
        grid=(n // tm,),
        in_specs=[pl.BlockSpec((tm, D_SG), lambda i: (i, BLK_U)),
                  pl.BlockSpec((tm, D_SG), lambda i: (i, BLK_V)),
                  pl.BlockSpec((1, D_SG), lambda i: (0, 0)),
                  pl.BlockSpec((1, D_SG), lambda i: (0, 0)),
                  pl.BlockSpec((SG_GROUPS, SG_CHUNK, SG_CHUNK), lambda i: (0, 0, 0)),
                  pl.BlockSpec((SG_CHUNK, SG_GROUPS), lambda i: (0, 0))],
        out_specs=pl.BlockSpec((tm, D_SG), lambda i: (i, 0)),
        out_shape=jax.ShapeDtypeStruct((n, D_SG), BF16),
        compiler_params=_cparams(("parallel",)),
        name="sgu",
    )(proj, proj, ln_w, ln_b, w_s, b_s_t)


VT_ROWS = HEAD_DIM + 16
EXT_LO = REL_BUCKETS


def _moba_prep_kernel(q_ref, k_ref, v_ref, qt_ref, kx_ref, vt_ref, km_ref, *, nb):
    lane = lax.broadcasted_iota(jnp.int32, (MOBA_BLOCK, LANES), 1)
    pad = jnp.concatenate([jnp.ones((1, MOBA_BLOCK), F32),
                           jnp.zeros((VT_ROWS - HEAD_DIM - 1, MOBA_BLOCK), F32)], axis=0)
    for j in range(nb):
        rows = slice(j * MOBA_BLOCK, (j + 1) * MOBA_BLOCK)
        kj = k_ref[rows, :]
        onehot = jnp.where((lane == j) | (lane == j + EXT_LO), 1.0, 0.0).astype(BF16)
        kx_ref[rows, :] = jnp.concatenate([kj, onehot], axis=1)
        km_ref[j:j + 1, :] = jnp.mean(kj.astype(F32), axis=0, keepdims=True)
        vt = v_ref[rows, :].astype(F32).T
        vt_ref[j] = jnp.concatenate([vt, pad], axis=0).astype(BF16)
        qt_ref[j] = q_ref[rows, :].astype(F32).T.astype(BF16)


def _moba_prep(proj, batch, seq):
    nb = seq // MOBA_BLOCK
    bh = lambda shape: pl.BlockSpec((None, None) + shape, lambda b, h: (b, h) + (0,) * len(shape))
    return pl.pallas_call(
        functools.partial(_moba_prep_kernel, nb=nb),
        grid=(batch, ATT_HEADS),
        in_specs=[pl.BlockSpec((seq, HEAD_DIM), lambda b, h: (b, CB_Q + h)),
                  pl.BlockSpec((seq, HEAD_DIM), lambda b, h: (b, CB_K + h)),
                  pl.BlockSpec((seq, HEAD_DIM), lambda b, h: (b, CB_VAL + h))],
        out_specs=[bh((nb, HEAD_DIM, MOBA_BLOCK)), bh((seq, 2 * HEAD_DIM)),
                   bh((nb, VT_ROWS, MOBA_BLOCK)), bh((nb, HEAD_DIM))],
        out_shape=[jax.ShapeDtypeStruct((batch, ATT_HEADS, nb, HEAD_DIM, MOBA_BLOCK), BF16),
                   jax.ShapeDtypeStruct((batch, ATT_HEADS, seq, 2 * HEAD_DIM), BF16),
                   jax.ShapeDtypeStruct((batch, ATT_HEADS, nb, VT_ROWS, MOBA_BLOCK), BF16),
                   jax.ShapeDtypeStruct((batch, ATT_HEADS, nb, HEAD_DIM), F32)],
        compiler_params=_cparams(("parallel", "parallel")),
        name="moba_prep",
    )(proj, proj, proj)


def _bias_tiles_kernel(rb_ref, o_ref):
    h = pl.program_id(0)
    kj = lax.broadcasted_iota(jnp.int32, (MOBA_BLOCK, MOBA_BLOCK), 0)
    qi = lax.broadcasted_iota(jnp.int32, (MOBA_BLOCK, MOBA_BLOCK), 1)
    max_exact = REL_BUCKETS // 2
    for t in range(2):
        rel = qi - kj + MOBA_BLOCK * t
        n = jnp.maximum(rel, 0)
        nf = jnp.maximum(n, max_exact).astype(F32)
        large = max_exact + (jnp.log(nf / max_exact) / math.log(REL_MAX_DIST / max_exact)
                             * (REL_BUCKETS - max_exact)).astype(jnp.int32)
        large = jnp.minimum(large, REL_BUCKETS - 1)
        bucket = jnp.where(n < max_exact, n, large)
        bias = jnp.zeros((MOBA_BLOCK, MOBA_BLOCK), F32)
        for r in range(REL_BUCKETS):
            bias = jnp.where(bucket == r, rb_ref[r, h], bias)
        o_ref[t] = jnp.where(rel >= 0, bias, MASK_NEG)


def _bias_tiles(rel_bias):
    return pl.pallas_call(
        _bias_tiles_kernel,
        grid=(ATT_HEADS,),
        in_specs=[pl.BlockSpec(memory_space=pltpu.SMEM)],
        out_specs=pl.BlockSpec((None, 2, MOBA_BLOCK, MOBA_BLOCK), lambda h: (h, 0, 0, 0)),
        out_shape=jax.ShapeDtypeStruct((ATT_HEADS, 2, MOBA_BLOCK, MOBA_BLOCK), F32),
        compiler_params=_cparams(("arbitrary",)),
        name="bias_tiles",
    )(rel_bias)


FAR_BLOCKS = 4
HEADS_PER_STEP = 2


def _moba_kernel(rb_ref, qt_ref, kx_ref, vt_ref, km_ref, bias_ref, o_ref, *, nb):
    i = pl.program_id(2)
    jp = jnp.maximum(i - 1, 0)
    prev0 = pl.multiple_of(jp * MOBA_BLOCK, MOBA_BLOCK)
    own0 = pl.multiple_of(i * MOBA_BLOCK, MOBA_BLOCK)
    no_prev = jnp.where(i > 0, 0.0, MASK_NEG)
    neg_inf = jnp.float32(-jnp.inf)

    def head_start(hh):
        h = pl.program_id(1) * HEADS_PER_STEP + hh
        qt = qt_ref[hh]
        km_hi, km_lo = _split_bf16(km_ref[hh])
        score = (jnp.dot(km_hi, qt, preferred_element_type=F32)
                 + jnp.dot(km_lo, qt, preferred_element_type=F32))
        bid = lax.broadcasted_iota(jnp.int32, score.shape, 0)
        score = jnp.where(bid < i, score, neg_inf)
        chosen = bid < 0
        for _ in range(MOBA_TOPK):
            mx = jnp.max(score, axis=0, keepdims=True)
            hit = (score == mx) & (mx > neg_inf)
            idx = jnp.min(jnp.where(hit, bid, nb), axis=0, keepdims=True)
            pick = bid == idx
            chosen = chosen | pick
            score = jnp.where(pick, neg_inf, score)

        def with_mask(val):
            hi, lo = _split_bf16(val)
            fill = jnp.zeros((HEAD_DIM - 2 * EXT_LO, MOBA_BLOCK), BF16)
            if nb < EXT_LO:
                gap = jnp.zeros((EXT_LO - nb, MOBA_BLOCK), BF16)
                return jnp.concatenate([qt, hi, gap, lo, gap, fill], axis=0)
            return jnp.concatenate([qt, hi, lo, fill], axis=0)

        far_bias = rb_ref[REL_BUCKETS - 1, h]
        qx_far = with_mask(jnp.where(chosen & (bid <= i - 2), far_bias, MASK_NEG))
        qx_near = with_mask(jnp.where((bid == i) | (chosen & (bid == i - 1)), 0.0, MASK_NEG))

        kx = jnp.concatenate([kx_ref[hh, pl.ds(prev0, MOBA_BLOCK), :],
                              kx_ref[hh, pl.ds(own0, MOBA_BLOCK), :]], axis=0)
        bias = jnp.concatenate([bias_ref[hh, 1] + no_prev, bias_ref[hh, 0]], axis=0)
        s = jnp.dot(kx, qx_near, preferred_element_type=F32) + bias
        m = jnp.max(s, axis=0, keepdims=True)
        p = jnp.exp(s - m).astype(BF16)
        vt = jnp.concatenate([vt_ref[hh, jp], vt_ref[hh, i]], axis=1)
        acc = jnp.dot(vt, p, preferred_element_type=F32)
        return qx_far, m, acc

    starts = [head_start(hh) for hh in range(HEADS_PER_STEP)]
    qx_fars = [st[0] for st in starts]

    def far_step(c, carry):
        row0 = pl.multiple_of(c * (FAR_BLOCKS * MOBA_BLOCK), FAR_BLOCKS * MOBA_BLOCK)
        heads = range(HEADS_PER_STEP)
        ss = [jnp.dot(kx_ref[hh, pl.ds(row0, FAR_BLOCKS * MOBA_BLOCK), :], qx_fars[hh],
                      preferred_element_type=F32) for hh in heads]
        m_news = [jnp.maximum(carry[hh][0], jnp.max(ss[hh], axis=0, keepdims=True)) for hh in heads]
        ps = [jnp.exp(ss[hh] - m_news[hh]).astype(BF16) for hh in heads]
        out = []
        for hh in heads:
            m, acc = carry[hh]
            alpha = jnp.exp(m - m_news[hh])
            vt = jnp.concatenate([vt_ref[hh, c * FAR_BLOCKS + k] for k in range(FAR_BLOCKS)], axis=1)
            out.append((m_news[hh], alpha * acc + jnp.dot(vt, ps[hh], preferred_element_type=F32)))
        return tuple(out)

    n_far = lax.shift_right_logical(jnp.maximum(i - 1, 0) + FAR_BLOCKS - 1, 2)
    final = lax.fori_loop(0, n_far, far_step, tuple((st[1], st[2]) for st in starts))
    for hh in range(HEADS_PER_STEP):
        acc = final[hh][1]
        out = acc[0:HEAD_DIM] / acc[HEAD_DIM:HEAD_DIM + 1]
        o_ref[:, hh * HEAD_DIM:(hh + 1) * HEAD_DIM] = out.T.astype(BF16)


def _moba(rel_bias, qt, kx, vt, kmean, bias_tiles, batch, seq):
    nb = seq // MOBA_BLOCK
    hps = HEADS_PER_STEP
    assert nb % FAR_BLOCKS == 0 and FAR_BLOCKS == 4 and ATT_HEADS % hps == 0
    bh = lambda shape: pl.BlockSpec((None, hps) + shape, lambda b, h, i: (b, h) + (0,) * len(shape))
    return pl.pallas_call(
        functools.partial(_moba_kernel, nb=nb),
        grid=(batch, ATT_HEADS // hps, nb),
        in_specs=[pl.BlockSpec(memory_space=pltpu.SMEM),
                  pl.BlockSpec((None, hps, None, HEAD_DIM, MOBA_BLOCK), lambda b, h, i: (b, h, i, 0, 0)),
                  bh((seq, 2 * HEAD_DIM)), bh((nb, VT_ROWS, MOBA_BLOCK)), bh((nb, HEAD_DIM)),
                  pl.BlockSpec((hps, 2, MOBA_BLOCK, MOBA_BLOCK), lambda b, h, i: (h, 0, 0, 0))],
        out_specs=pl.BlockSpec((MOBA_BLOCK, hps * HEAD_DIM), lambda b, h, i: (b * nb + i, h)),
        out_shape=jax.ShapeDtypeStruct((batch * seq, D_ATT), BF16),
        compiler_params=_cparams(("parallel", "parallel", "arbitrary")),
        name="moba",
    )(rel_bias, qt, kx, vt, kmean, bias_tiles)


def _merge_kernel(x_ref, ysg_ref, yatt_ref, gsg_ref, gatt_ref, mod_ref, nw_ref,
                  wsg_ref, watt_ref, wo_ref, wr_ref, x1_ref, tok_ref):
    a_sg = jnp.dot(ysg_ref[...], wsg_ref[...], preferred_element_type=F32)
    a_att = jnp.dot(yatt_ref[...], watt_ref[...], preferred_element_type=F32)
    merged = (jax.nn.sigmoid(gsg_ref[...].astype(F32)) * a_sg
              + jax.nn.sigmoid(gatt_ref[...].astype(F32)) * a_att)
    mixed = jnp.dot(merged.astype(BF16), wo_ref[...], preferred_element_type=F32)
    x1 = x_ref[...] + mod_ref[0, 2:3, :] * mixed
    x1_ref[...] = x1
    ms = jnp.mean(x1 * x1, axis=-1, keepdims=True)
    y = x1 * lax.rsqrt(ms + EPS) * nw_ref[...]
    h2 = y * (1.0 + mod_ref[0, 4:5, :]) + mod_ref[0, 3:4, :]
    tok_ref[:, 0:D_MODEL] = h2

    h_hi, h_lo = _split_bf16(h2)
    w_hi, w_lo = _split_bf16(wr_ref[...])
    logits = (jnp.dot(h_hi, w_hi, preferred_element_type=F32)
              + jnp.dot(h_lo, w_hi, preferred_element_type=F32)
              + jnp.dot(h_hi, w_lo, preferred_element_type=F32))

    lane = lax.broadcasted_iota(jnp.int32, logits.shape, 1)
    neg_inf = jnp.float32(-jnp.inf)
    in_g = lane < N_GROUPS
    lg = jnp.where(in_g, logits, neg_inf)
    mg = jnp.max(lg, axis=1, keepdims=True)
    eg = jnp.exp(lg - mg)
    g_prob = eg / jnp.sum(eg, axis=1, keepdims=True)
    g_p = jnp.max(g_prob, axis=1, keepdims=True)
    g_idx = jnp.min(jnp.where(g_prob == g_p, lane, LANES), axis=1, keepdims=True)
    lo_col = ROUTER_COL0 + EXPERTS_PER_GROUP * g_idx
    in_e = (lane >= lo_col) & (lane < lo_col + EXPERTS_PER_GROUP)
    le = jnp.where(in_e, logits, neg_inf)
    me = jnp.max(le, axis=1, keepdims=True)
    ee = jnp.exp(le - me)
    e_prob = jnp.where(in_e, ee / jnp.sum(ee, axis=1, keepdims=True), -1.0)
    p1 = jnp.max(e_prob, axis=1, keepdims=True)
    i1 = jnp.min(jnp.where(e_prob == p1, lane, LANES), axis=1, keepdims=True)
    rest = jnp.where(lane == i1, -1.0, e_prob)
    p2 = jnp.max(rest, axis=1, keepdims=True)
    i2 = jnp.min(jnp.where(rest == p2, lane, LANES), axis=1, keepdims=True)
    denom = p1 + p2
    tok_ref[:, D_MODEL:] = (jnp.where(lane == i1, g_p * (p1 / denom), 0.0)
                            + jnp.where(lane == i2, g_p * (p2 / denom), 0.0)
                            + jnp.where(lane == GROUP_LANE, g_idx.astype(F32), 0.0))


def _merge(x2, y_sg, y_att, proj, mod, norm2_w, w_sg, w_att, w_o, w_router, seq):
    n, d = x2.shape
    tm = 256
    tpb = seq // tm
    resident = lambda shape: pl.BlockSpec(shape, lambda i: (0, 0), pipeline_mode=pl.Buffered(1))
    return pl.pallas_call(
        _merge_kernel,
        grid=(n // tm,),
        in_specs=[pl.BlockSpec((tm, d), lambda i: (i, 0)),
                  pl.BlockSpec((tm, D_SG), lambda i: (i, 0)),
                  pl.BlockSpec((tm, D_ATT), lambda i: (i, 0)),
                  pl.BlockSpec((tm, d), lambda i: (i, BLK_GATE_SG)),
                  pl.BlockSpec((tm, d), lambda i: (i, BLK_GATE_ATT)),
                  pl.BlockSpec((1, 8, d), lambda i: (i // tpb, 0, 0)),
                  pl.BlockSpec((1, d), lambda i: (0, 0)),
                  resident((D_SG, d)), resident((D_ATT, d)), resident((d, d)),
                  resident((d, LANES))],
        out_specs=[pl.BlockSpec((tm, d), lambda i: (i, 0)),
                   pl.BlockSpec((tm, TOK_COLS), lambda i: (i, 0))],
        out_shape=[jax.ShapeDtypeStruct((n, d), F32),
                   jax.ShapeDtypeStruct((n, TOK_COLS), F32)],
        compiler_params=_cparams(("parallel",)),
        name="merge",
    )(x2, y_sg, y_att, proj, proj, mod, norm2_w, w_sg, w_att, w_o, w_router)


ROW_BATCH = 128


def _row_copy_kernel(src_ref, dst_ref, tab_ref, out_ref, sem, *, n_rows):
    def start_batch(b):
        def one(r, carry):
            row = b * ROW_BATCH + r
            pltpu.make_async_copy(tab_ref.at[pl.ds(src_ref[row], 1)],
                                  out_ref.at[pl.ds(dst_ref[row], 1)], sem).start()
            return carry
        lax.fori_loop(0, ROW_BATCH, one, 0, unroll=8)

    def wait_batch():
        for _ in range(ROW_BATCH):
            pltpu.make_async_copy(tab_ref.at[pl.ds(0, 1)], out_ref.at[pl.ds(0, 1)], sem).wait()

    start_batch(0)

    def step(b, carry):
        start_batch(b)
        wait_batch()
        return carry

    lax.fori_loop(1, n_rows // ROW_BATCH, step, 0)
    wait_batch()


def _row_copy(src, dst, table, name):
    n_rows = src.shape[0]
    assert n_rows % ROW_BATCH == 0 and dst.shape == src.shape
    return pl.pallas_call(
        functools.partial(_row_copy_kernel, n_rows=n_rows),
        grid_spec=pltpu.PrefetchScalarGridSpec(
            num_scalar_prefetch=2, grid=(1,),
            in_specs=[pl.BlockSpec(memory_space=pl.ANY)],
            out_specs=pl.BlockSpec(memory_space=pl.ANY),
            scratch_shapes=[pltpu.SemaphoreType.DMA(())]),
        out_shape=jax.ShapeDtypeStruct((n_rows, table.shape[1]), table.dtype),
        compiler_params=_cparams(("arbitrary",)),
        name=name,
    )(src, dst, table)


MOE_TILE = 256


def _moe_kernel(tg_ref, na_ref, tok_ref, w1_ref, w3_ref, w2_ref, o_ref):
    i = pl.program_id(0)

    @pl.when(i < na_ref[0])
    def _():
        t = tok_ref[:, 0:D_MODEL].astype(BF16)
        routing = tok_ref[:, D_MODEL:]
        lane = lax.broadcasted_iota(jnp.int32, routing.shape, 1)
        lane0 = ROUTER_COL0 + EXPERTS_PER_GROUP * tg_ref[i]
        for k in range(EXPERTS_PER_GROUP):
            a = jnp.dot(t, w1_ref[k], preferred_element_type=F32)
            b = jnp.dot(t, w3_ref[k], preferred_element_type=F32)
            ge = jnp.sum(jnp.where(lane == lane0 + k, routing, 0.0), axis=1, keepdims=True)
            hmid = ((a * jax.nn.sigmoid(a)) * b * ge).astype(BF16)
            y = jnp.dot(hmid, w2_ref[k], preferred_element_type=F32)
            if k == 0:
                o_ref[...] = y
            else:
                o_ref[...] += y

    @pl.when(i >= na_ref[0])
    def _():
        o_ref[...] = jnp.zeros(o_ref.shape, F32)


def _moe(tile_group, n_active, tok_sorted, w1, w3, w2):
    n_slots = tok_sorted.shape[0]
    d = D_MODEL
    row_blk = lambda i, tg, na: (i, 0)
    w_blk = lambda i, tg, na: (tg[i], 0, 0, 0)
    once = pl.Buffered(1)
    return pl.pallas_call(
        _moe_kernel,
        grid_spec=pltpu.PrefetchScalarGridSpec(
            num_scalar_prefetch=2, grid=(n_slots // MOE_TILE,),
            in_specs=[pl.BlockSpec((MOE_TILE, TOK_COLS), row_blk),
                      pl.BlockSpec((None, EXPERTS_PER_GROUP, d, D_EXPERT), w_blk, pipeline_mode=once),
                      pl.BlockSpec((None, EXPERTS_PER_GROUP, d, D_EXPERT), w_blk, pipeline_mode=once),
                      pl.BlockSpec((None, EXPERTS_PER_GROUP, D_EXPERT, d), w_blk, pipeline_mode=once)],
            out_specs=pl.BlockSpec((MOE_TILE, d), row_blk)),
        out_shape=jax.ShapeDtypeStruct((n_slots, d), F32),
        compiler_params=_cparams(("arbitrary",)),
        name="moe",
    )(tile_group, n_active, tok_sorted, w1, w3, w2)


def _final_kernel(x1_ref, y_ref, mod_ref, fw_ref, o_ref):
    x2 = x1_ref[...] + mod_ref[0, 5:6, :] * y_ref[...]
    ms = jnp.mean(x2 * x2, axis=-1, keepdims=True)
    o_ref[...] = x2 * lax.rsqrt(ms + EPS) * fw_ref[...]


def _final(x1, y, mod, final_w, seq):
    n, d = x1.shape
    tm = 512
    tpb = seq // tm
    return pl.pallas_call(
        _final_kernel,
        grid=(n // tm,),
        in_specs=[pl.BlockSpec((tm, d), lambda i: (i, 0)),
                  pl.BlockSpec((tm, d), lambda i: (i, 0)),
                  pl.BlockSpec((1, 8, d), lambda i: (i // tpb, 0, 0)),
                  pl.BlockSpec((1, d), lambda i: (0, 0))],
        out_specs=pl.BlockSpec((tm, d), lambda i: (i, 0)),
        out_shape=jax.ShapeDtypeStruct((n, d), F32),
        compiler_params=_cparams(("parallel",)),
        name="final",
    )(x1, y, mod, final_w)


def _group_sort_plan(group, n):
    onehot = (group[:, None] == jnp.arange(N_GROUPS, dtype=jnp.int32)[None, :]).astype(jnp.int32)
    incl = jnp.cumsum(onehot, axis=0)
    tiles = (incl[-1] + MOE_TILE - 1) // MOE_TILE
    tile_end = jnp.cumsum(tiles)
    start = (tile_end - tiles) * MOE_TILE
    rank = jnp.sum(incl * onehot, axis=1) - 1
    pos = jnp.sum(onehot * start[None, :], axis=1) + rank
    n_tiles = n // MOE_TILE + N_GROUPS
    tile_ids = jnp.arange(n_tiles, dtype=jnp.int32)
    tile_group = jnp.sum((tile_ids[:, None] >= tile_end[None, :]).astype(jnp.int32), axis=1)
    tile_group = jnp.minimum(tile_group, N_GROUPS - 1)
    pad = tiles * MOE_TILE - incl[-1]
    pad_end = jnp.cumsum(pad)
    k = jnp.arange(N_GROUPS * MOE_TILE, dtype=jnp.int32)
    seg = jnp.sum((k[:, None] >= pad_end[None, :]).astype(jnp.int32), axis=1)
    seg_hot = (seg[:, None] == jnp.arange(N_GROUPS, dtype=jnp.int32)[None, :]).astype(jnp.int32)
    in_group = jnp.sum(seg_hot * (start + incl[-1] - (pad_end - pad))[None, :], axis=1) + k
    tail = tile_end[-1] * MOE_TILE + k - pad_end[-1]
    free = jnp.where(seg < N_GROUPS, in_group, tail)
    return (pos.astype(jnp.int32), free.astype(jnp.int32), tile_group.astype(jnp.int32),
            tile_end[-1:].astype(jnp.int32))


def kernel(x, c, w_ada, b_ada, norm1_w, norm2_w, final_norm_w, w_in, sg_ln_w, sg_ln_b, w_spatial,
           b_spatial, rel_bias, w_out_sg, w_out_att, w_o, w_router_group, w_router_expert,
           w_exp_gate, w_exp_up, w_exp_down):
    batch, seq, d = x.shape
    assert d == D_MODEL and w_ada.shape[0] == 1
    assert seq % 1024 == 0 and seq // MOBA_BLOCK <= LANES
    n = batch * seq
    x2 = x.reshape(n, d)

    c_pad = jnp.zeros((8, d), F32).at[:batch].set(c)
    mod = _adaln(c_pad, w_ada[0], b_ada[0].reshape(1, 6 * d))
    mod = mod[:batch].reshape(batch, 6, d)
    mod = jnp.concatenate([mod, jnp.zeros((batch, 2, d), F32)], axis=1)

    colscale = jnp.ones((1, IN_COLS), F32).at[:, CB_Q * LANES:CB_K * LANES].set(HEAD_DIM ** -0.5)
    w_in_r = jnp.concatenate([w_in[0][:, IN_COLS - N_GATE_COLS:], w_in[0][:, :IN_COLS - N_GATE_COLS]],
                             axis=1).astype(BF16)
    proj = _proj(x2, mod, norm1_w[0].reshape(1, d), w_in_r, colscale, seq)

    y_sg = _sgu(proj, sg_ln_w[0].reshape(1, D_SG), sg_ln_b[0].reshape(1, D_SG),
                w_spatial[0], b_spatial[0].T)

    qt, kx, vt, kmean = _moba_prep(proj, batch, seq)
    bias_tiles = _bias_tiles(rel_bias)
    y_att = _moba(rel_bias, qt, kx, vt, kmean, bias_tiles, batch, seq)

    w_router = jnp.zeros((d, LANES), F32)
    w_router = w_router.at[:, :N_GROUPS].set(w_router_group[0])
    w_router = w_router.at[:, ROUTER_COL0:ROUTER_COL0 + N_EXPERTS].set(w_router_expert[0])
    x1, tok = _merge(x2, y_sg, y_att, proj, mod, norm2_w[0].reshape(1, d),
                     w_out_sg[0].astype(BF16), w_out_att[0].astype(BF16),
                     w_o[0].astype(BF16), w_router, seq)

    group = tok[:, D_MODEL + GROUP_LANE].astype(jnp.int32)
    pos, free, tile_group, n_active = _group_sort_plan(group, n)
    token_ids = jnp.arange(n, dtype=jnp.int32)
    tok_sorted = _row_copy(jnp.concatenate([token_ids, jnp.zeros_like(free)]),
                           jnp.concatenate([pos, free]), tok, "sort_rows")
    by_group = lambda w: w[0].astype(BF16).reshape((N_GROUPS, EXPERTS_PER_GROUP) + w.shape[2:])
    y_sorted = _moe(tile_group, n_active, tok_sorted, by_group(w_exp_gate), by_group(w_exp_up),
                    by_group(w_exp_down))
    y = _row_copy(pos, token_ids, y_sorted, "unsort_rows")
    out = _final(x1, y, mod, final_norm_w.reshape(1, d), seq)
    return out.reshape(batch, seq, d)
```

```python
import functools
import math

import jax
import jax.numpy as jnp
from jax import lax
from jax.experimental import pallas as pl
from jax.experimental.pallas import tpu as pltpu

F32 = jnp.float32
BF16 = jnp.bfloat16

LANES = 128
D_MODEL = 2048
D_SG = D_MODEL // 2
SG_GROUPS = 8
SG_CHUNK = 128
ATT_HEADS = 8
HEAD_DIM = 128
D_ATT = ATT_HEADS * HEAD_DIM
MOBA_BLOCK = 256
MOBA_TOPK = 3
REL_BUCKETS = 32
REL_MAX_DIST = 128
N_GROUPS = 4
EXPERTS_PER_GROUP = 4
N_EXPERTS = N_GROUPS * EXPERTS_PER_GROUP
D_EXPERT = 512
EPS = 1e-6
IN_COLS = 2 * D_SG + 3 * D_ATT + 2 * D_MODEL

N_GATE_COLS = 2 * D_MODEL
BLK_GATE_SG = 0
BLK_GATE_ATT = 1
BLK_U = N_GATE_COLS // D_SG
BLK_V = BLK_U + 1
CB_Q = (N_GATE_COLS + 2 * D_SG) // LANES
CB_K = CB_Q + ATT_HEADS
CB_VAL = CB_K + ATT_HEADS

MASK_NEG = -1e9
ROUTER_COL0 = N_GROUPS
GROUP_LANE = 0
D_ROWS = D_MODEL // LANES
TOK_ROWS = D_ROWS + 8
VMEM_LIMIT = 56 * 1024 * 1024


def _cparams(sem):
    return pltpu.CompilerParams(dimension_semantics=sem, vmem_limit_bytes=VMEM_LIMIT)


def _nt_dot(a, b):
    return lax.dot_general(a, b, (((1,), (1,)), ((), ())), preferred_element_type=F32)


def _split_bf16(a):
    hi = a.astype(BF16)
    lo = (a - hi.astype(F32)).astype(BF16)
    return hi, lo


def _adaln_kernel(c_ref, w_ref, b_ref, o_ref):
    c = c_ref[...]
    ca = c * jax.nn.sigmoid(c)
    o_ref[...] = jnp.dot(ca, w_ref[...], preferred_element_type=F32,
                         precision=lax.Precision.HIGHEST) + b_ref[...]


def _adaln(c_pad, w, b):
    rows, d = c_pad.shape
    cols = w.shape[1]
    tn = 1024
    return pl.pallas_call(
        _adaln_kernel,
        grid=(cols // tn,),
        in_specs=[pl.BlockSpec((rows, d), lambda j: (0, 0)),
                  pl.BlockSpec((d, tn), lambda j: (0, j)),
                  pl.BlockSpec((1, tn), lambda j: (0, j))],
        out_specs=pl.BlockSpec((rows, tn), lambda j: (0, j)),
        out_shape=jax.ShapeDtypeStruct((rows, cols), F32),
        compiler_params=_cparams(("arbitrary",)),
        name="adaln",
    )(c_pad, w, b)


def _proj_kernel(x_ref, mod_ref, nw_ref, w_ref, cs_ref, o_ref, h_scr):
    @pl.when(pl.program_id(1) == 0)
    def _():
        x = x_ref[...]
        ms = jnp.mean(x * x, axis=-1, keepdims=True)
        y = x * lax.rsqrt(ms + EPS) * nw_ref[...]
        h = y * (1.0 + mod_ref[0, 1:2, :]) + mod_ref[0, 0:1, :]
        h_scr[...] = h.astype(BF16)

    acc = jnp.dot(h_scr[...], w_ref[...], preferred_element_type=F32)
    o_ref[...] = (acc * cs_ref[...]).astype(BF16)


def _proj(x2, mod, norm_w, w_in, colscale, seq):
    n, d = x2.shape
    cols = w_in.shape[1]
    tm = min(1024, seq)
    tn = 1024
    tpb = seq // tm
    return pl.pallas_call(
        _proj_kernel,
        grid=(n // tm, cols // tn),
        in_specs=[pl.BlockSpec((tm, d), lambda i, j: (i, 0)),
                  pl.BlockSpec((1, 8, d), lambda i, j: (i // tpb, 0, 0)),
                  pl.BlockSpec((1, d), lambda i, j: (0, 0)),
                  pl.BlockSpec((d, tn), lambda i, j: (0, j)),
                  pl.BlockSpec((1, tn), lambda i, j: (0, j))],
        out_specs=pl.BlockSpec((tm, tn), lambda i, j: (i, j)),
        out_shape=jax.ShapeDtypeStruct((n, cols), BF16),
        scratch_shapes=[pltpu.VMEM((tm, d), BF16)],
        compiler_params=_cparams(("parallel", "arbitrary")),
        name="proj",
    )(x2, mod, norm_w, w_in, colscale)


def _gelu(a):
    return 0.5 * a * (1.0 + lax.erf(a * (1.0 / math.sqrt(2.0))))


def _sgu_kernel(u_ref, v_ref, lnw_ref, lnb_ref, ws_ref, bst_ref, o_ref, *, chunks):
    u = _gelu(u_ref[...].astype(F32))
    v = _gelu(v_ref[...].astype(F32))
    mu = jnp.mean(v, axis=-1, keepdims=True)
    vc = v - mu
    var = jnp.mean(vc * vc, axis=-1, keepdims=True)
    vn = (vc * lax.rsqrt(var + EPS) * lnw_ref[...] + lnb_ref[...]).astype(BF16)
    row = lax.broadcasted_iota(jnp.int32, (SG_CHUNK, SG_CHUNK), 0)
    col = lax.broadcasted_iota(jnp.int32, (SG_CHUNK, SG_CHUNK), 1)
    causal = col <= row
    for g in range(SG_GROUPS):
        wm = jnp.where(causal, ws_ref[g], 0.0).astype(BF16)
        bcol = bst_ref[:, g:g + 1]
        gs = slice(g * LANES, (g + 1) * LANES)
        for c in range(chunks):
            rs = slice(c * SG_CHUNK, (c + 1) * SG_CHUNK)
            z = jnp.dot(wm, vn[rs, gs], preferred_element_type=F32) + bcol
            o_ref[rs, gs] = (u[rs, gs] * z).astype(BF16)


def _sgu(proj, ln_w, ln_b, w_s, b_s_t):
    n = proj.shape[0]
    chunks = 4
    tm = chunks * SG_CHUNK
    return pl.pallas_call(
        functools.partial(_sgu_kernel, chunks=chunks),
        grid=(n // tm,),
        in_specs=[pl.BlockSpec((tm, D_SG), lambda i: (i, BLK_U)),
                  pl.BlockSpec((tm, D_SG), lambda i: (i, BLK_V)),
                  pl.BlockSpec((1, D_SG), lambda i: (0, 0)),
                  pl.BlockSpec((1, D_SG), lambda i: (0, 0)),
                  pl.BlockSpec((SG_GROUPS, SG_CHUNK, SG_CHUNK), lambda i: (0, 0, 0)),
                  pl.BlockSpec((SG_CHUNK, SG_GROUPS), lambda i: (0, 0))],
        out_specs=pl.BlockSpec((tm, D_SG), lambda i: (i, 0)),
        out_shape=jax.ShapeDtypeStruct((n, D_SG), BF16),
        compiler_params=_cparams(("parallel",)),
        name="sgu",
    )(proj, proj, ln_w, ln_b, w_s, b_s_t)


VT_ROWS = HEAD_DIM + 16
EXT_LO = REL_BUCKETS


def _moba_prep_kernel(q_ref, k_ref, v_ref, qt_ref, kx_ref, vt_ref, km_ref, *, nb):
    lane = lax.broadcasted_iota(jnp.int32, (MOBA_BLOCK, LANES), 1)
    pad = jnp.concatenate([jnp.ones((1, MOBA_BLOCK), F32),
                           jnp.zeros((VT_ROWS - HEAD_DIM - 1, MOBA_BLOCK), F32)], axis=0)
    for j in range(nb):
        rows = slice(j * MOBA_BLOCK, (j + 1) * MOBA_BLOCK)
        kj = k_ref[rows, :]
        onehot = jnp.where((lane == j) | (lane == j + EXT_LO), 1.0, 0.0).astype(BF16)
        kx_ref[rows, :] = jnp.concatenate([kj, onehot], axis=1)
        km_ref[j:j + 1, :] = jnp.mean(kj.astype(F32), axis=0, keepdims=True)
        vt = v_ref[rows, :].astype(F32).T
        vt_ref[j] = jnp.concatenate([vt, pad], axis=0).astype(BF16)
        qt_ref[j] = q_ref[rows, :].astype(F32).T.astype(BF16)


def _moba_prep(proj, batch, seq):
    nb = seq // MOBA_BLOCK
    bh = lambda shape: pl.BlockSpec((None, None) + shape, lambda b, h: (b, h) + (0,) * len(shape))
    return pl.pallas_call(
        functools.partial(_moba_prep_kernel, nb=nb),
        grid=(batch, ATT_HEADS),
        in_specs=[pl.BlockSpec((seq, HEAD_DIM), lambda b, h: (b, CB_Q + h)),
                  pl.BlockSpec((seq, HEAD_DIM), lambda b, h: (b, CB_K + h)),
                  pl.BlockSpec((seq, HEAD_DIM), lambda b, h: (b, CB_VAL + h))],
        out_specs=[bh((nb, HEAD_DIM, MOBA_BLOCK)), bh((seq, 2 * HEAD_DIM)),
                   bh((nb, VT_ROWS, MOBA_BLOCK)), bh((nb, HEAD_DIM))],
        out_shape=[jax.ShapeDtypeStruct((batch, ATT_HEADS, nb, HEAD_DIM, MOBA_BLOCK), BF16),
                   jax.ShapeDtypeStruct((batch, ATT_HEADS, seq, 2 * HEAD_DIM), BF16),
                   jax.ShapeDtypeStruct((batch, ATT_HEADS, nb, VT_ROWS, MOBA_BLOCK), BF16),
                   jax.ShapeDtypeStruct((batch, ATT_HEADS, nb, HEAD_DIM), F32)],
        compiler_params=_cparams(("parallel", "parallel")),
        name="moba_prep",
    )(proj, proj, proj)


def _bias_tiles_kernel(rb_ref, o_ref):
    h = pl.program_id(0)
    kj = lax.broadcasted_iota(jnp.int32, (MOBA_BLOCK, MOBA_BLOCK), 0)
    qi = lax.broadcasted_iota(jnp.int32, (MOBA_BLOCK, MOBA_BLOCK), 1)
    max_exact = REL_BUCKETS // 2
    for t in range(2):
        rel = qi - kj + MOBA_BLOCK * t
        n = jnp.maximum(rel, 0)
        nf = jnp.maximum(n, max_exact).astype(F32)
        large = max_exact + (jnp.log(nf / max_exact) / math.log(REL_MAX_DIST / max_exact)
                             * (REL_BUCKETS - max_exact)).astype(jnp.int32)
        large = jnp.minimum(large, REL_BUCKETS - 1)
        bucket = jnp.where(n < max_exact, n, large)
        bias = jnp.zeros((MOBA_BLOCK, MOBA_BLOCK), F32)
        for r in range(REL_BUCKETS):
            bias = jnp.where(bucket == r, rb_ref[r, h], bias)
        o_ref[t] = jnp.where(rel >= 0, bias, MASK_NEG)


def _bias_tiles(rel_bias):
    return pl.pallas_call(
        _bias_tiles_kernel,
        grid=(ATT_HEADS,),
        in_specs=[pl.BlockSpec(memory_space=pltpu.SMEM)],
        out_specs=pl.BlockSpec((None, 2, MOBA_BLOCK, MOBA_BLOCK), lambda h: (h, 0, 0, 0)),
        out_shape=jax.ShapeDtypeStruct((ATT_HEADS, 2, MOBA_BLOCK, MOBA_BLOCK), F32),
        compiler_params=_cparams(("arbitrary",)),
        name="bias_tiles",
    )(rel_bias)


FAR_BLOCKS = 4
HEADS_PER_STEP = 2


def _moba_kernel(rb_ref, qt_ref, kx_ref, vt_ref, km_ref, bias_ref, o_ref, *, nb):
    i = pl.program_id(2)
    jp = jnp.maximum(i - 1, 0)
    prev0 = pl.multiple_of(jp * MOBA_BLOCK, MOBA_BLOCK)
    own0 = pl.multiple_of(i * MOBA_BLOCK, MOBA_BLOCK)
    no_prev = jnp.where(i > 0, 0.0, MASK_NEG)
    neg_inf = jnp.float32(-jnp.inf)

    def head_start(hh):
        h = pl.program_id(1) * HEADS_PER_STEP + hh
        qt = qt_ref[hh]
        km_hi, km_lo = _split_bf16(km_ref[hh])
        score = (jnp.dot(km_hi, qt, preferred_element_type=F32)
                 + jnp.dot(km_lo, qt, preferred_element_type=F32))
        bid = lax.broadcasted_iota(jnp.int32, score.shape, 0)
        score = jnp.where(bid < i, score, neg_inf)
        chosen = bid < 0
        for _ in range(MOBA_TOPK):
            mx = jnp.max(score, axis=0, keepdims=True)
            hit = (score == mx) & (mx > neg_inf)
            idx = jnp.min(jnp.where(hit, bid, nb), axis=0, keepdims=True)
            pick = bid == idx
            chosen = chosen | pick
            score = jnp.where(pick, neg_inf, score)

        def with_mask(val):
            hi, lo = _split_bf16(val)
            fill = jnp.zeros((HEAD_DIM - 2 * EXT_LO, MOBA_BLOCK), BF16)
            if nb < EXT_LO:
                gap = jnp.zeros((EXT_LO - nb, MOBA_BLOCK), BF16)
                return jnp.concatenate([qt, hi, gap, lo, gap, fill], axis=0)
            return jnp.concatenate([qt, hi, lo, fill], axis=0)

        far_bias = rb_ref[REL_BUCKETS - 1, h]
        qx_far = with_mask(jnp.where(chosen & (bid <= i - 2), far_bias, MASK_NEG))
        qx_near = with_mask(jnp.where((bid == i) | (chosen & (bid == i - 1)), 0.0, MASK_NEG))

        kx = jnp.concatenate([kx_ref[hh, pl.ds(prev0, MOBA_BLOCK), :],
                              kx_ref[hh, pl.ds(own0, MOBA_BLOCK), :]], axis=0)
        bias = jnp.concatenate([bias_ref[hh, 1] + no_prev, bias_ref[hh, 0]], axis=0)
        s = jnp.dot(kx, qx_near, preferred_element_type=F32) + bias
        m = jnp.max(s, axis=0, keepdims=True)
        p = jnp.exp(s - m).astype(BF16)
        vt = jnp.concatenate([vt_ref[hh, jp], vt_ref[hh, i]], axis=1)
        acc = jnp.dot(vt, p, preferred_element_type=F32)
        return qx_far, m, acc

    starts = [head_start(hh) for hh in range(HEADS_PER_STEP)]
    qx_fars = [st[0] for st in starts]

    def far_step(c, carry):
        row0 = pl.multiple_of(c * (FAR_BLOCKS * MOBA_BLOCK), FAR_BLOCKS * MOBA_BLOCK)
        heads = range(HEADS_PER_STEP)
        ss = [jnp.dot(kx_ref[hh, pl.ds(row0, FAR_BLOCKS * MOBA_BLOCK), :], qx_fars[hh],
                      preferred_element_type=F32) for hh in heads]
        m_news = [jnp.maximum(carry[hh][0], jnp.max(ss[hh], axis=0, keepdims=True)) for hh in heads]
        ps = [jnp.exp(ss[hh] - m_news[hh]).astype(BF16) for hh in heads]
        out = []
        for hh in heads:
            m, acc = carry[hh]
            alpha = jnp.exp(m - m_news[hh])
            vt = jnp.concatenate([vt_ref[hh, c * FAR_BLOCKS + k] for k in range(FAR_BLOCKS)], axis=1)
            out.append((m_news[hh], alpha * acc + jnp.dot(vt, ps[hh], preferred_element_type=F32)))
        return tuple(out)

    n_far = lax.shift_right_logical(jnp.maximum(i - 1, 0) + FAR_BLOCKS - 1, 2)
    final = lax.fori_loop(0, n_far, far_step, tuple((st[1], st[2]) for st in starts))
    for hh in range(HEADS_PER_STEP):
        acc = final[hh][1]
        out = acc[0:HEAD_DIM] / acc[HEAD_DIM:HEAD_DIM + 1]
        o_ref[:, hh * HEAD_DIM:(hh + 1) * HEAD_DIM] = out.T.astype(BF16)


def _moba(rel_bias, qt, kx, vt, kmean, bias_tiles, batch, seq):
    nb = seq // MOBA_BLOCK
    hps = HEADS_PER_STEP
    assert nb % FAR_BLOCKS == 0 and FAR_BLOCKS == 4 and ATT_HEADS % hps == 0
    bh = lambda shape: pl.BlockSpec((None, hps) + shape, lambda b, h, i: (b, h) + (0,) * len(shape))
    return pl.pallas_call(
        functools.partial(_moba_kernel, nb=nb),
        grid=(batch, ATT_HEADS // hps, nb),
        in_specs=[pl.BlockSpec(memory_space=pltpu.SMEM),
                  pl.BlockSpec((None, hps, None, HEAD_DIM, MOBA_BLOCK), lambda b, h, i: (b, h, i, 0, 0)),
                  bh((seq, 2 * HEAD_DIM)), bh((nb, VT_ROWS, MOBA_BLOCK)), bh((nb, HEAD_DIM)),
                  pl.BlockSpec((hps, 2, MOBA_BLOCK, MOBA_BLOCK), lambda b, h, i: (h, 0, 0, 0))],
        out_specs=pl.BlockSpec((MOBA_BLOCK, hps * HEAD_DIM), lambda b, h, i: (b * nb + i, h)),
        out_shape=jax.ShapeDtypeStruct((batch * seq, D_ATT), BF16),
        compiler_params=_cparams(("parallel", "parallel", "arbitrary")),
        name="moba",
    )(rel_bias, qt, kx, vt, kmean, bias_tiles)


def _merge_kernel(x_ref, ysg_ref, yatt_ref, gsg_ref, gatt_ref, mod_ref, nw_ref,
                  wsg_ref, watt_ref, wo_ref, wr_ref, x1_ref, tok_ref):
    a_sg = jnp.dot(ysg_ref[...], wsg_ref[...], preferred_element_type=F32)
    a_att = jnp.dot(yatt_ref[...], watt_ref[...], preferred_element_type=F32)
    merged = (jax.nn.sigmoid(gsg_ref[...].astype(F32)) * a_sg
              + jax.nn.sigmoid(gatt_ref[...].astype(F32)) * a_att)
    mixed = jnp.dot(merged.astype(BF16), wo_ref[...], preferred_element_type=F32)
    x1 = x_ref[...] + mod_ref[0, 2:3, :] * mixed
    x1_ref[...] = x1
    ms = jnp.mean(x1 * x1, axis=-1, keepdims=True)
    y = x1 * lax.rsqrt(ms + EPS) * nw_ref[...]
    h2 = y * (1.0 + mod_ref[0, 4:5, :]) + mod_ref[0, 3:4, :]
    tm = h2.shape[0]
    for a in range(D_ROWS):
        tok_ref[pl.ds(a, tm, stride=TOK_ROWS), :] = h2[:, a * LANES:(a + 1) * LANES]
    for a in range(D_ROWS + 1, TOK_ROWS):
        tok_ref[pl.ds(a, tm, stride=TOK_ROWS), :] = jnp.zeros((tm, LANES), F32)

    h_hi, h_lo = _split_bf16(h2)
    w_hi, w_lo = _split_bf16(wr_ref[...])
    logits = (jnp.dot(h_hi, w_hi, preferred_element_type=F32)
              + jnp.dot(h_lo, w_hi, preferred_element_type=F32)
              + jnp.dot(h_hi, w_lo, preferred_element_type=F32))

    lane = lax.broadcasted_iota(jnp.int32, logits.shape, 1)
    neg_inf = jnp.float32(-jnp.inf)
    in_g = lane < N_GROUPS
    lg = jnp.where(in_g, logits, neg_inf)
    mg = jnp.max(lg, axis=1, keepdims=True)
    eg = jnp.exp(lg - mg)
    g_prob = eg / jnp.sum(eg, axis=1, keepdims=True)
    g_p = jnp.max(g_prob, axis=1, keepdims=True)
    g_idx = jnp.min(jnp.where(g_prob == g_p, lane, LANES), axis=1, keepdims=True)
    lo_col = ROUTER_COL0 + EXPERTS_PER_GROUP * g_idx
    in_e = (lane >= lo_col) & (lane < lo_col + EXPERTS_PER_GROUP)
    le = jnp.where(in_e, logits, neg_inf)
    me = jnp.max(le, axis=1, keepdims=True)
    ee = jnp.exp(le - me)
    e_prob = jnp.where(in_e, ee / jnp.sum(ee, axis=1, keepdims=True), -1.0)
    p1 = jnp.max(e_prob, axis=1, keepdims=True)
    i1 = jnp.min(jnp.where(e_prob == p1, lane, LANES), axis=1, keepdims=True)
    rest = jnp.where(lane == i1, -1.0, e_prob)
    p2 = jnp.max(rest, axis=1, keepdims=True)
    i2 = jnp.min(jnp.where(rest == p2, lane, LANES), axis=1, keepdims=True)
    denom = p1 + p2
    tok_ref[pl.ds(D_ROWS, tm, stride=TOK_ROWS), :] = (
        jnp.where(lane == i1, g_p * (p1 / denom), 0.0)
        + jnp.where(lane == i2, g_p * (p2 / denom), 0.0)
        + jnp.where(lane == GROUP_LANE, g_idx.astype(F32), 0.0))


def _merge(x2, y_sg, y_att, proj, mod, norm2_w, w_sg, w_att, w_o, w_router, seq):
    n, d = x2.shape
    tm = 256
    tpb = seq // tm
    resident = lambda shape: pl.BlockSpec(shape, lambda i: (0, 0), pipeline_mode=pl.Buffered(1))
    return pl.pallas_call(
        _merge_kernel,
        grid=(n // tm,),
        in_specs=[pl.BlockSpec((tm, d), lambda i: (i, 0)),
                  pl.BlockSpec((tm, D_SG), lambda i: (i, 0)),
                  pl.BlockSpec((tm, D_ATT), lambda i: (i, 0)),
                  pl.BlockSpec((tm, d), lambda i: (i, BLK_GATE_SG)),
                  pl.BlockSpec((tm, d), lambda i: (i, BLK_GATE_ATT)),
                  pl.BlockSpec((1, 8, d), lambda i: (i // tpb, 0, 0)),
                  pl.BlockSpec((1, d), lambda i: (0, 0)),
                  resident((D_SG, d)), resident((D_ATT, d)), resident((d, d)),
                  resident((d, LANES))],
        out_specs=[pl.BlockSpec((tm, d), lambda i: (i, 0)),
                   pl.BlockSpec((tm * TOK_ROWS, LANES), lambda i: (i, 0))],
        out_shape=[jax.ShapeDtypeStruct((n, d), F32),
                   jax.ShapeDtypeStruct((n * TOK_ROWS, LANES), F32)],
        compiler_params=_cparams(("parallel",)),
        name="merge",
    )(x2, y_sg, y_att, proj, proj, mod, norm2_w, w_sg, w_att, w_o, w_router)


def _invert_kernel(pos_ref, free_ref, inv_ref, *, n_tok, n_free):
    def place(t, carry):
        inv_ref[pos_ref[t]] = t
        return carry

    def place_free(k, carry):
        inv_ref[free_ref[k]] = n_tok + k
        return carry

    lax.fori_loop(0, n_tok, place, 0, unroll=8)
    lax.fori_loop(0, n_free, place_free, 0, unroll=8)


def _invert(pos, free):
    n_tok, n_free = pos.shape[0], free.shape[0]
    return pl.pallas_call(
        functools.partial(_invert_kernel, n_tok=n_tok, n_free=n_free),
        grid_spec=pltpu.PrefetchScalarGridSpec(
            num_scalar_prefetch=2, grid=(1,), in_specs=[],
            out_specs=pl.BlockSpec(memory_space=pltpu.SMEM)),
        out_shape=jax.ShapeDtypeStruct((n_tok + n_free,), jnp.int32),
        compiler_params=_cparams(("arbitrary",)),
        name="invert",
    )(pos, free)


MOE_TILE = 256


def _moe_kernel(src_ref, dst_ref, tg_ref, na_ref, tok_hbm, w1_ref, w3_ref, w2_ref, y_hbm,
                gbuf, stage, acc_ref, gsem, ssem):
    i = pl.program_id(0)
    n_act = na_ref[0]
    slot = lax.rem(i, 2)

    def gather_start(tile, sl):
        def one(r, carry):
            row0 = pl.multiple_of(r * TOK_ROWS, 8)
            pltpu.make_async_copy(tok_hbm.at[src_ref[tile * MOE_TILE + r]],
                                  gbuf.at[sl, pl.ds(row0, TOK_ROWS), :], gsem.at[sl]).start()
            return carry
        lax.fori_loop(0, MOE_TILE, one, 0, unroll=8)

    def gather_wait(sl):
        for _ in range(MOE_TILE):
            pltpu.make_async_copy(tok_hbm.at[0], gbuf.at[sl, pl.ds(0, TOK_ROWS), :], gsem.at[sl]).wait()

    def scatter_start(tile, sl):
        def one(r, carry):
            row0 = pl.multiple_of(r * D_ROWS, 8)
            pltpu.make_async_copy(stage.at[sl, pl.ds(row0, D_ROWS), :],
                                  y_hbm.at[dst_ref[tile * MOE_TILE + r]], ssem.at[sl]).start()
            return carry
        lax.fori_loop(0, MOE_TILE, one, 0, unroll=8)

    def scatter_wait(sl):
        for _ in range(MOE_TILE):
            pltpu.make_async_copy(stage.at[sl, pl.ds(0, D_ROWS), :], y_hbm.at[0], ssem.at[sl]).wait()

    @pl.when(i < n_act)
    def _():
        @pl.when(i == 0)
        def _():
            gather_start(0, 0)

        @pl.when(i + 1 < n_act)
        def _():
            gather_start(i + 1, 1 - slot)

        gather_wait(slot)
        t = jnp.concatenate([gbuf[slot, pl.ds(a, MOE_TILE, stride=TOK_ROWS), :] for a in range(D_ROWS)],
                            axis=1).astype(BF16)
        routing = gbuf[slot, pl.ds(D_ROWS, MOE_TILE, stride=TOK_ROWS), :]
        lane = lax.broadcasted_iota(jnp.int32, routing.shape, 1)
        lane0 = ROUTER_COL0 + EXPERTS_PER_GROUP * tg_ref[i]
        for k in range(EXPERTS_PER_GROUP):
            a = jnp.dot(t, w1_ref[k], preferred_element_type=F32)
            b = jnp.dot(t, w3_ref[k], preferred_element_type=F32)
            ge = jnp.sum(jnp.where(lane == lane0 + k, routing, 0.0), axis=1, keepdims=True)
            hmid = ((a * jax.nn.sigmoid(a)) * b * ge).astype(BF16)
            y = jnp.dot(hmid, w2_ref[k], preferred_element_type=F32)
            if k == 0:
                acc_ref[...] = y
            else:
                acc_ref[...] += y

        @pl.when(i >= 2)
        def _():
            scatter_wait(slot)

        for a in range(D_ROWS):
            stage[slot, pl.ds(a, MOE_TILE, stride=D_ROWS), :] = acc_ref[:, a * LANES:(a + 1) * LANES]
        scatter_start(i, slot)

        @pl.when(i == n_act - 1)
        def _():
            scatter_wait(slot)

            @pl.when(i >= 1)
            def _():
                scatter_wait(1 - slot)

    @pl.when(i >= n_act)
    def _():
        stage[slot] = jnp.zeros(stage.shape[1:], F32)
        scatter_start(i, slot)
        scatter_wait(slot)


def _moe(src_tok, dst_row, tile_group, n_active, tok_blocks, w1, w3, w2):
    n_slots = src_tok.shape[0]
    d = D_MODEL
    w_blk = lambda i, src, dst, tg, na: (tg[i], 0, 0, 0)
    once = pl.Buffered(1)
    return pl.pallas_call(
        _moe_kernel,
        grid_spec=pltpu.PrefetchScalarGridSpec(
            num_scalar_prefetch=4, grid=(n_slots // MOE_TILE,),
            in_specs=[pl.BlockSpec(memory_space=pl.ANY),
                      pl.BlockSpec((None, EXPERTS_PER_GROUP, d, D_EXPERT), w_blk, pipeline_mode=once),
                      pl.BlockSpec((None, EXPERTS_PER_GROUP, d, D_EXPERT), w_blk, pipeline_mode=once),
                      pl.BlockSpec((None, EXPERTS_PER_GROUP, D_EXPERT, d), w_blk, pipeline_mode=once)],
            out_specs=pl.BlockSpec(memory_space=pl.ANY),
            scratch_shapes=[pltpu.VMEM((2, MOE_TILE * TOK_ROWS, LANES), F32),
                            pltpu.VMEM((2, MOE_TILE * D_ROWS, LANES), F32),
                            pltpu.VMEM((MOE_TILE, d), F32),
                            pltpu.SemaphoreType.DMA((2,)),
                            pltpu.SemaphoreType.DMA((2,))]),
        out_shape=jax.ShapeDtypeStruct((n_slots, D_ROWS, LANES), F32),
        compiler_params=_cparams(("arbitrary",)),
        name="moe",
    )(src_tok, dst_row, tile_group, n_active, tok_blocks, w1, w3, w2)


def _final_kernel(x1_ref, y_ref, mod_ref, fw_ref, o_ref):
    tm = x1_ref.shape[0]
    parts = []
    ss = jnp.zeros((tm, 1), F32)
    for a in range(D_ROWS):
        cols = slice(a * LANES, (a + 1) * LANES)
        part = x1_ref[:, cols] + mod_ref[0, 5:6, cols] * y_ref[pl.ds(a, tm, stride=D_ROWS), :]
        ss = ss + jnp.sum(part * part, axis=1, keepdims=True)
        parts.append(part)
    inv = lax.rsqrt(ss * (1.0 / D_MODEL) + EPS)
    for a in range(D_ROWS):
        cols = slice(a * LANES, (a + 1) * LANES)
        o_ref[:, cols] = parts[a] * inv * fw_ref[:, cols]


def _final(x1, y_blocks, mod, final_w, seq):
    n, d = x1.shape
    tm = 256
    tpb = seq // tm
    return pl.pallas_call(
        _final_kernel,
        grid=(n // tm,),
        in_specs=[pl.BlockSpec((tm, d), lambda i: (i, 0)),
                  pl.BlockSpec((tm * D_ROWS, LANES), lambda i: (i, 0)),
                  pl.BlockSpec((1, 8, d), lambda i: (i // tpb, 0, 0)),
                  pl.BlockSpec((1, d), lambda i: (0, 0))],
        out_specs=pl.BlockSpec((tm, d), lambda i: (i, 0)),
        out_shape=jax.ShapeDtypeStruct((n, d), F32),
        compiler_params=_cparams(("parallel",)),
        name="final",
    )(x1, y_blocks, mod, final_w)


def _group_sort_plan(group, n):
    onehot = (group[:, None] == jnp.arange(N_GROUPS, dtype=jnp.int32)[None, :]).astype(jnp.int32)
    incl = jnp.cumsum(onehot, axis=0)
    tiles = (incl[-1] + MOE_TILE - 1) // MOE_TILE
    tile_end = jnp.cumsum(tiles)
    start = (tile_end - tiles) * MOE_TILE
    rank = jnp.sum(incl * onehot, axis=1) - 1
    pos = jnp.sum(onehot * start[None, :], axis=1) + rank
    n_tiles = n // MOE_TILE + N_GROUPS
    tile_ids = jnp.arange(n_tiles, dtype=jnp.int32)
    tile_group = jnp.sum((tile_ids[:, None] >= tile_end[None, :]).astype(jnp.int32), axis=1)
    tile_group = jnp.minimum(tile_group, N_GROUPS - 1)
    pad = tiles * MOE_TILE - incl[-1]
    pad_end = jnp.cumsum(pad)
    k = jnp.arange(N_GROUPS * MOE_TILE, dtype=jnp.int32)
    seg = jnp.sum((k[:, None] >= pad_end[None, :]).astype(jnp.int32), axis=1)
    seg_hot = (seg[:, None] == jnp.arange(N_GROUPS, dtype=jnp.int32)[None, :]).astype(jnp.int32)
    in_group = jnp.sum(seg_hot * (start + incl[-1] - (pad_end - pad))[None, :], axis=1) + k
    tail = tile_end[-1] * MOE_TILE + k - pad_end[-1]
    free = jnp.where(seg < N_GROUPS, in_group, tail)
    return (pos.astype(jnp.int32), free.astype(jnp.int32), tile_group.astype(jnp.int32),
            tile_end[-1:].astype(jnp.int32))


def kernel(x, c, w_ada, b_ada, norm1_w, norm2_w, final_norm_w, w_in, sg_ln_w, sg_ln_b, w_spatial,
           b_spatial, rel_bias, w_out_sg, w_out_att, w_o, w_router_group, w_router_expert,
           w_exp_gate, w_exp_up, w_exp_down):
    batch, seq, d = x.shape
    assert d == D_MODEL and w_ada.shape[0] == 1
    assert seq % 1024 == 0 and seq // MOBA_BLOCK <= LANES
    n = batch * seq
    x2 = x.reshape(n, d)

    c_pad = jnp.zeros((8, d), F32).at[:batch].set(c)
    mod = _adaln(c_pad, w_ada[0], b_ada[0].reshape(1, 6 * d))
    mod = mod[:batch].reshape(batch, 6, d)
    mod = jnp.concatenate([mod, jnp.zeros((batch, 2, d), F32)], axis=1)

    colscale = jnp.ones((1, IN_COLS), F32).at[:, CB_Q * LANES:CB_K * LANES].set(HEAD_DIM ** -0.5)
    w_in_r = jnp.concatenate([w_in[0][:, IN_COLS - N_GATE_COLS:], w_in[0][:, :IN_COLS - N_GATE_COLS]],
                             axis=1).astype(BF16)
    proj = _proj(x2, mod, norm1_w[0].reshape(1, d), w_in_r, colscale, seq)

    y_sg = _sgu(proj, sg_ln_w[0].reshape(1, D_SG), sg_ln_b[0].reshape(1, D_SG),
                w_spatial[0], b_spatial[0].T)

    qt, kx, vt, kmean = _moba_prep(proj, batch, seq)
    bias_tiles = _bias_tiles(rel_bias)
    y_att = _moba(rel_bias, qt, kx, vt, kmean, bias_tiles, batch, seq)

    w_router = jnp.zeros((d, LANES), F32)
    w_router = w_router.at[:, :N_GROUPS].set(w_router_group[0])
    w_router = w_router.at[:, ROUTER_COL0:ROUTER_COL0 + N_EXPERTS].set(w_router_expert[0])
    x1, tok = _merge(x2, y_sg, y_att, proj, mod, norm2_w[0].reshape(1, d),
                     w_out_sg[0].astype(BF16), w_out_att[0].astype(BF16),
                     w_o[0].astype(BF16), w_router, seq)

    tok = tok.reshape(n, TOK_ROWS, LANES)
    group = tok[:, D_ROWS, GROUP_LANE].astype(jnp.int32)
    pos, free, tile_group, n_active = _group_sort_plan(group, n)
    inv = _invert(pos, free)
    src_tok = jnp.where(inv < n, inv, 0)
    by_group = lambda w: w[0].astype(BF16).reshape((N_GROUPS, EXPERTS_PER_GROUP) + w.shape[2:])
    y = _moe(src_tok, inv, tile_group, n_active, tok, by_group(w_exp_gate), by_group(w_exp_up),
             by_group(w_exp_down))
    out = _final(x1, y.reshape(-1, LANES), mod, final_norm_w.reshape(1, d), seq)
    return out.reshape(batch, seq, d)
```

```python
import functools
import math

import jax
import jax.numpy as jnp
from jax import lax
from jax.experimental import pallas as pl
from jax.experimental.pallas import tpu as pltpu

F32 = jnp.float32
BF16 = jnp.bfloat16

LANES = 128
D_MODEL = 2048
D_SG = D_MODEL // 2
SG_GROUPS = 8
SG_CHUNK = 128
ATT_HEADS = 8
HEAD_DIM = 128
D_ATT = ATT_HEADS * HEAD_DIM
MOBA_BLOCK = 256
MOBA_TOPK = 3
REL_BUCKETS = 32
REL_MAX_DIST = 128
N_GROUPS = 4
EXPERTS_PER_GROUP = 4
N_EXPERTS = N_GROUPS * EXPERTS_PER_GROUP
D_EXPERT = 512
EPS = 1e-6
IN_COLS = 2 * D_SG + 3 * D_ATT + 2 * D_MODEL

N_GATE_COLS = 2 * D_MODEL
BLK_GATE_SG = 0
BLK_GATE_ATT = 1
BLK_U = N_GATE_COLS // D_SG
BLK_V = BLK_U + 1
CB_Q = (N_GATE_COLS + 2 * D_SG) // LANES
CB_K = CB_Q + ATT_HEADS
CB_VAL = CB_K + ATT_HEADS

MASK_NEG = -1e9
LOG2E = math.log2(math.e)
ROUTER_COL0 = N_GROUPS
GROUP_LANE = 0
D_ROWS = D_MODEL // LANES
TOK_ROWS = D_ROWS + 8
VMEM_LIMIT = 56 * 1024 * 1024


def _cparams(sem):
    return pltpu.CompilerParams(dimension_semantics=sem, vmem_limit_bytes=VMEM_LIMIT)


def _nt_dot(a, b):
    return lax.dot_general(a, b, (((1,), (1,)), ((), ())), preferred_element_type=F32)


def _split_bf16(a):
    hi = a.astype(BF16)
    lo = (a - hi.astype(F32)).astype(BF16)
    return hi, lo


def _adaln_kernel(c_ref, w_ref, b_ref, o_ref):
    c = c_ref[...]
    ca = c * jax.nn.sigmoid(c)
    o_ref[...] = jnp.dot(ca, w_ref[...], preferred_element_type=F32,
                         precision=lax.Precision.HIGHEST) + b_ref[...]


def _adaln(c_pad, w, b):
    rows, d = c_pad.shape
    cols = w.shape[1]
    tn = 1024
    return pl.pallas_call(
        _adaln_kernel,
        grid=(cols // tn,),
        in_specs=[pl.BlockSpec((rows, d), lambda j: (0, 0)),
                  pl.BlockSpec((d, tn), lambda j: (0, j)),
                  pl.BlockSpec((1, tn), lambda j: (0, j))],
        out_specs=pl.BlockSpec((rows, tn), lambda j: (0, j)),
        out_shape=jax.ShapeDtypeStruct((rows, cols), F32),
        compiler_params=_cparams(("arbitrary",)),
        name="adaln",
    )(c_pad, w, b)


def _proj_kernel(x_ref, mod_ref, nw_ref, w_ref, cs_ref, o_ref, h_scr):
    @pl.when(pl.program_id(1) == 0)
    def _():
        x = x_ref[...]
        ms = jnp.mean(x * x, axis=-1, keepdims=True)
        y = x * lax.rsqrt(ms + EPS) * nw_ref[...]
        h = y * (1.0 + mod_ref[0, 1:2, :]) + mod_ref[0, 0:1, :]
        h_scr[...] = h.astype(BF16)

    acc = jnp.dot(h_scr[...], w_ref[...], preferred_element_type=F32)
    o_ref[...] = (acc * cs_ref[...]).astype(BF16)


def _proj(x2, mod, norm_w, w_in, colscale, seq):
    n, d = x2.shape
    cols = w_in.shape[1]
    tm = min(1024, seq)
    tn = 1024
    tpb = seq // tm
    return pl.pallas_call(
        _proj_kernel,
        grid=(n // tm, cols // tn),
        in_specs=[pl.BlockSpec((tm, d), lambda i, j: (i, 0)),
                  pl.BlockSpec((1, 8, d), lambda i, j: (i // tpb, 0, 0)),
                  pl.BlockSpec((1, d), lambda i, j: (0, 0)),
                  pl.BlockSpec((d, tn), lambda i, j: (0, j)),
                  pl.BlockSpec((1, tn), lambda i, j: (0, j))],
        out_specs=pl.BlockSpec((tm, tn), lambda i, j: (i, j)),
        out_shape=jax.ShapeDtypeStruct((n, cols), BF16),
        scratch_shapes=[pltpu.VMEM((tm, d), BF16)],
        compiler_params=_cparams(("parallel", "arbitrary")),
        name="proj",
    )(x2, mod, norm_w, w_in, colscale)


def _gelu(a):
    return 0.5 * a * (1.0 + lax.erf(a * (1.0 / math.sqrt(2.0))))


def _sgu_kernel(u_ref, v_ref, lnw_ref, lnb_ref, ws_ref, bst_ref, o_ref, *, chunks):
    u = _gelu(u_ref[...].astype(F32))
    v = _gelu(v_ref[...].astype(F32))
    mu = jnp.mean(v, axis=-1, keepdims=True)
    vc = v - mu
    var = jnp.mean(vc * vc, axis=-1, keepdims=True)
    vn = (vc * lax.rsqrt(var + EPS) * lnw_ref[...] + lnb_ref[...]).astype(BF16)
    row = lax.broadcasted_iota(jnp.int32, (SG_CHUNK, SG_CHUNK), 0)
    col = lax.broadcasted_iota(jnp.int32, (SG_CHUNK, SG_CHUNK), 1)
    causal = col <= row
    for g in range(SG_GROUPS):
        wm = jnp.where(causal, ws_ref[g], 0.0).astype(BF16)
        bcol = bst_ref[:, g:g + 1]
        gs = slice(g * LANES, (g + 1) * LANES)
        for c in range(chunks):
            rs = slice(c * SG_CHUNK, (c + 1) * SG_CHUNK)
            z = jnp.dot(wm, vn[rs, gs], preferred_element_type=F32) + bcol
            o_ref[rs, gs] = (u[rs, gs] * z).astype(BF16)


def _sgu(proj, ln_w, ln_b, w_s, b_s_t):
    n = proj.shape[0]
    chunks = 4
    tm = chunks * SG_CHUNK
    return pl.pallas_call(
        functools.partial(_sgu_kernel, chunks=chunks),
        grid=(n // tm,),
        in_specs=[pl.BlockSpec((tm, D_SG), lambda i: (i, BLK_U)),
                  pl.BlockSpec((tm, D_SG), lambda i: (i, BLK_V)),
                  pl.BlockSpec((1, D_SG), lambda i: (0, 0)),
                  pl.BlockSpec((1, D_SG), lambda i: (0, 0)),
                  pl.BlockSpec((SG_GROUPS, SG_CHUNK, SG_CHUNK), lambda i: (0, 0, 0)),
                  pl.BlockSpec((SG_CHUNK, SG_GROUPS), lambda i: (0, 0))],
        out_specs=pl.BlockSpec((tm, D_SG), lambda i: (i, 0)),
        out_shape=jax.ShapeDtypeStruct((n, D_SG), BF16),
        compiler_params=_cparams(("parallel",)),
        name="sgu",
    )(proj, proj, ln_w, ln_b, w_s, b_s_t)


VT_ROWS = HEAD_DIM + 16
EXT_LO = REL_BUCKETS


def _moba_prep_kernel(q_ref, k_ref, v_ref, qt_ref, kx_ref, vt_ref, km_ref, *, nb):
    lane = lax.broadcasted_iota(jnp.int32, (MOBA_BLOCK, LANES), 1)
    pad = jnp.concatenate([jnp.ones((1, MOBA_BLOCK), F32),
                           jnp.zeros((VT_ROWS - HEAD_DIM - 1, MOBA_BLOCK), F32)], axis=0)
    for j in range(nb):
        rows = slice(j * MOBA_BLOCK, (j + 1) * MOBA_BLOCK)
        kj = k_ref[rows, :]
        onehot = jnp.where((lane == j) | (lane == j + EXT_LO), 1.0, 0.0).astype(BF16)
        kx_ref[rows, :] = jnp.concatenate([kj, onehot], axis=1)
        km_ref[j:j + 1, :] = jnp.mean(kj.astype(F32), axis=0, keepdims=True)
        vt = v_ref[rows, :].astype(F32).T
        vt_ref[j] = jnp.concatenate([vt, pad], axis=0).astype(BF16)
        qt_ref[j] = q_ref[rows, :].astype(F32).T.astype(BF16)


def _moba_prep(proj, batch, seq):
    nb = seq // MOBA_BLOCK
    bh = lambda shape: pl.BlockSpec((None, None) + shape, lambda b, h: (b, h) + (0,) * len(shape))
    return pl.pallas_call(
        functools.partial(_moba_prep_kernel, nb=nb),
        grid=(batch, ATT_HEADS),
        in_specs=[pl.BlockSpec((seq, HEAD_DIM), lambda b, h: (b, CB_Q + h)),
                  pl.BlockSpec((seq, HEAD_DIM), lambda b, h: (b, CB_K + h)),
                  pl.BlockSpec((seq, HEAD_DIM), lambda b, h: (b, CB_VAL + h))],
        out_specs=[bh((nb, HEAD_DIM, MOBA_BLOCK)), bh((seq, 2 * HEAD_DIM)),
                   bh((nb, VT_ROWS, MOBA_BLOCK)), bh((nb, HEAD_DIM))],
        out_shape=[jax.ShapeDtypeStruct((batch, ATT_HEADS, nb, HEAD_DIM, MOBA_BLOCK), BF16),
                   jax.ShapeDtypeStruct((batch, ATT_HEADS, seq, 2 * HEAD_DIM), BF16),
                   jax.ShapeDtypeStruct((batch, ATT_HEADS, nb, VT_ROWS, MOBA_BLOCK), BF16),
                   jax.ShapeDtypeStruct((batch, ATT_HEADS, nb, HEAD_DIM), F32)],
        compiler_params=_cparams(("parallel", "parallel")),
        name="moba_prep",
    )(proj, proj, proj)


def _bias_tiles_kernel(rb_ref, o_ref):
    h = pl.program_id(0)
    kj = lax.broadcasted_iota(jnp.int32, (MOBA_BLOCK, MOBA_BLOCK), 0)
    qi = lax.broadcasted_iota(jnp.int32, (MOBA_BLOCK, MOBA_BLOCK), 1)
    max_exact = REL_BUCKETS // 2
    for t in range(2):
        rel = qi - kj + MOBA_BLOCK * t
        n = jnp.maximum(rel, 0)
        nf = jnp.maximum(n, max_exact).astype(F32)
        large = max_exact + (jnp.log(nf / max_exact) / math.log(REL_MAX_DIST / max_exact)
                             * (REL_BUCKETS - max_exact)).astype(jnp.int32)
        large = jnp.minimum(large, REL_BUCKETS - 1)
        bucket = jnp.where(n < max_exact, n, large)
        bias = jnp.zeros((MOBA_BLOCK, MOBA_BLOCK), F32)
        for r in range(REL_BUCKETS):
            bias = jnp.where(bucket == r, rb_ref[r, h], bias)
        o_ref[t] = jnp.where(rel >= 0, bias * LOG2E, MASK_NEG)


def _bias_tiles(rel_bias):
    return pl.pallas_call(
        _bias_tiles_kernel,
        grid=(ATT_HEADS,),
        in_specs=[pl.BlockSpec(memory_space=pltpu.SMEM)],
        out_specs=pl.BlockSpec((None, 2, MOBA_BLOCK, MOBA_BLOCK), lambda h: (h, 0, 0, 0)),
        out_shape=jax.ShapeDtypeStruct((ATT_HEADS, 2, MOBA_BLOCK, MOBA_BLOCK), F32),
        compiler_params=_cparams(("arbitrary",)),
        name="bias_tiles",
    )(rel_bias)


FAR_BLOCKS = 4
HEADS_PER_STEP = 4


def _moba_kernel(rb_ref, qt_ref, kx_ref, vt_ref, km_ref, bias_ref, o_ref, *, nb):
    i = pl.program_id(2)
    jp = jnp.maximum(i - 1, 0)
    prev0 = pl.multiple_of(jp * MOBA_BLOCK, MOBA_BLOCK)
    own0 = pl.multiple_of(i * MOBA_BLOCK, MOBA_BLOCK)
    no_prev = jnp.where(i > 0, 0.0, MASK_NEG)
    neg_inf = jnp.float32(-jnp.inf)

    heads = range(HEADS_PER_STEP)

    def choose_blocks(hh):
        km_hi, km_lo = _split_bf16(km_ref[hh])
        score = (jnp.dot(km_hi, qt_ref[hh], preferred_element_type=F32)
                 + jnp.dot(km_lo, qt_ref[hh], preferred_element_type=F32))
        bid = lax.broadcasted_iota(jnp.int32, score.shape, 0)
        score = jnp.where(bid < i, score, neg_inf)
        chosen = bid < 0
        for _ in range(MOBA_TOPK):
            mx = jnp.max(score, axis=0, keepdims=True)
            hit = (score == mx) & (mx > neg_inf)
            idx = jnp.min(jnp.where(hit, bid, nb), axis=0, keepdims=True)
            pick = bid == idx
            chosen = chosen | pick
            score = jnp.where(pick, neg_inf, score)
        return chosen

    def with_mask(hh, val):
        hi, lo = _split_bf16(val)
        fill = jnp.zeros((HEAD_DIM - 2 * EXT_LO, MOBA_BLOCK), BF16)
        if nb < EXT_LO:
            gap = jnp.zeros((EXT_LO - nb, MOBA_BLOCK), BF16)
            return jnp.concatenate([qt_ref[hh], hi, gap, lo, gap, fill], axis=0)
        return jnp.concatenate([qt_ref[hh], hi, lo, fill], axis=0)

    chosen = [choose_blocks(hh) for hh in heads]
    bid = lax.broadcasted_iota(jnp.int32, chosen[0].shape, 0)
    far_bias = [rb_ref[REL_BUCKETS - 1, pl.program_id(1) * HEADS_PER_STEP + hh] * LOG2E for hh in heads]
    qx_fars = [with_mask(hh, jnp.where(chosen[hh] & (bid <= i - 2), far_bias[hh], MASK_NEG)) for hh in heads]
    qx_nears = [with_mask(hh, jnp.where((bid == i) | (chosen[hh] & (bid == i - 1)), 0.0, MASK_NEG))
                for hh in heads]

    ss = [jnp.dot(jnp.concatenate([kx_ref[hh, pl.ds(prev0, MOBA_BLOCK), :],
                                   kx_ref[hh, pl.ds(own0, MOBA_BLOCK), :]], axis=0),
                  qx_nears[hh], preferred_element_type=F32)
          + jnp.concatenate([bias_ref[hh, 1] + no_prev, bias_ref[hh, 0]], axis=0) for hh in heads]
    ms = [jnp.max(ss[hh], axis=0, keepdims=True) for hh in heads]
    ps = [jnp.exp2(ss[hh] - ms[hh]).astype(BF16) for hh in heads]
    accs = [jnp.dot(jnp.concatenate([vt_ref[hh, jp], vt_ref[hh, i]], axis=1), ps[hh],
                    preferred_element_type=F32) for hh in heads]

    def far_step(c, carry):
        row0 = pl.multiple_of(c * (FAR_BLOCKS * MOBA_BLOCK), FAR_BLOCKS * MOBA_BLOCK)
        ss = [jnp.dot(kx_ref[hh, pl.ds(row0, FAR_BLOCKS * MOBA_BLOCK), :], qx_fars[hh],
                      preferred_element_type=F32) for hh in heads]
        m_news = [jnp.maximum(carry[hh][0], jnp.max(ss[hh], axis=0, keepdims=True)) for hh in heads]
        ps = [jnp.exp2(ss[hh] - m_news[hh]).astype(BF16) for hh in heads]
        out = []
        for hh in heads:
            m, acc = carry[hh]
            alpha = jnp.exp2(m - m_news[hh])
            vt = jnp.concatenate([vt_ref[hh, c * FAR_BLOCKS + k] for k in range(FAR_BLOCKS)], axis=1)
            out.append((m_news[hh], alpha * acc + jnp.dot(vt, ps[hh], preferred_element_type=F32)))
        return tuple(out)

    n_far = lax.shift_right_logical(jnp.maximum(i - 1, 0) + FAR_BLOCKS - 1, 2)
    final = lax.fori_loop(0, n_far, far_step, tuple((ms[hh], accs[hh]) for hh in heads))
    for hh in heads:
        acc = final[hh][1]
        out = acc[0:HEAD_DIM] / acc[HEAD_DIM:HEAD_DIM + 1]
        o_ref[:, hh * HEAD_DIM:(hh + 1) * HEAD_DIM] = out.T.astype(BF16)


def _moba(rel_bias, qt, kx, vt, kmean, bias_tiles, batch, seq):
    nb = seq // MOBA_BLOCK
    hps = HEADS_PER_STEP
    assert nb % FAR_BLOCKS == 0 and FAR_BLOCKS == 4 and ATT_HEADS % hps == 0
    bh = lambda shape: pl.BlockSpec((None, hps) + shape, lambda b, h, i: (b, h) + (0,) * len(shape),
                                    pipeline_mode=pl.Buffered(1))
    return pl.pallas_call(
        functools.partial(_moba_kernel, nb=nb),
        grid=(batch, ATT_HEADS // hps, nb),
        in_specs=[pl.BlockSpec(memory_space=pltpu.SMEM),
                  pl.BlockSpec((None, hps, None, HEAD_DIM, MOBA_BLOCK), lambda b, h, i: (b, h, i, 0, 0)),
                  bh((seq, 2 * HEAD_DIM)), bh((nb, VT_ROWS, MOBA_BLOCK)), bh((nb, HEAD_DIM)),
                  pl.BlockSpec((hps, 2, MOBA_BLOCK, MOBA_BLOCK), lambda b, h, i: (h, 0, 0, 0))],
        out_specs=pl.BlockSpec((MOBA_BLOCK, hps * HEAD_DIM), lambda b, h, i: (b * nb + i, h)),
        out_shape=jax.ShapeDtypeStruct((batch * seq, D_ATT), BF16),
        compiler_params=_cparams(("parallel", "parallel", "arbitrary")),
        name="moba",
    )(rel_bias, qt, kx, vt, kmean, bias_tiles)


def _merge_kernel(x_ref, ysg_ref, yatt_ref, gsg_ref, gatt_ref, mod_ref, nw_ref,
                  wsg_ref, watt_ref, wo_ref, wr_ref, x1_ref, tok_ref):
    a_sg = jnp.dot(ysg_ref[...], wsg_ref[...], preferred_element_type=F32)
    a_att = jnp.dot(yatt_ref[...], watt_ref[...], preferred_element_type=F32)
    merged = (jax.nn.sigmoid(gsg_ref[...].astype(F32)) * a_sg
              + jax.nn.sigmoid(gatt_ref[...].astype(F32)) * a_att)
    mixed = jnp.dot(merged.astype(BF16), wo_ref[...], preferred_element_type=F32)
    x1 = x_ref[...] + mod_ref[0, 2:3, :] * mixed
    x1_ref[...] = x1
    ms = jnp.mean(x1 * x1, axis=-1, keepdims=True)
    y = x1 * lax.rsqrt(ms + EPS) * nw_ref[...]
    h2 = y * (1.0 + mod_ref[0, 4:5, :]) + mod_ref[0, 3:4, :]
    tm = h2.shape[0]
    for a in range(D_ROWS):
        tok_ref[pl.ds(a, tm, stride=TOK_ROWS), :] = h2[:, a * LANES:(a + 1) * LANES]
    for a in range(D_ROWS + 1, TOK_ROWS):
        tok_ref[pl.ds(a, tm, stride=TOK_ROWS), :] = jnp.zeros((tm, LANES), F32)

    h_hi, h_lo = _split_bf16(h2)
    w_hi, w_lo = _split_bf16(wr_ref[...])
    logits = (jnp.dot(h_hi, w_hi, preferred_element_type=F32)
              + jnp.dot(h_lo, w_hi, preferred_element_type=F32)
              + jnp.dot(h_hi, w_lo, preferred_element_type=F32))

    lane = lax.broadcasted_iota(jnp.int32, logits.shape, 1)
    neg_inf = jnp.float32(-jnp.inf)
    in_g = lane < N_GROUPS
    lg = jnp.where(in_g, logits, neg_inf)
    mg = jnp.max(lg, axis=1, keepdims=True)
    eg = jnp.exp(lg - mg)
    g_prob = eg / jnp.sum(eg, axis=1, keepdims=True)
    g_p = jnp.max(g_prob, axis=1, keepdims=True)
    g_idx = jnp.min(jnp.where(g_prob == g_p, lane, LANES), axis=1, keepdims=True)
    lo_col = ROUTER_COL0 + EXPERTS_PER_GROUP * g_idx
    in_e = (lane >= lo_col) & (lane < lo_col + EXPERTS_PER_GROUP)
    le = jnp.where(in_e, logits, neg_inf)
    me = jnp.max(le, axis=1, keepdims=True)
    ee = jnp.exp(le - me)
    e_prob = jnp.where(in_e, ee / jnp.sum(ee, axis=1, keepdims=True), -1.0)
    p1 = jnp.max(e_prob, axis=1, keepdims=True)
    i1 = jnp.min(jnp.where(e_prob == p1, lane, LANES), axis=1, keepdims=True)
    rest = jnp.where(lane == i1, -1.0, e_prob)
    p2 = jnp.max(rest, axis=1, keepdims=True)
    i2 = jnp.min(jnp.where(rest == p2, lane, LANES), axis=1, keepdims=True)
    denom = p1 + p2
    tok_ref[pl.ds(D_ROWS, tm, stride=TOK_ROWS), :] = (
        jnp.where(lane == i1, g_p * (p1 / denom), 0.0)
        + jnp.where(lane == i2, g_p * (p2 / denom), 0.0)
        + jnp.where(lane == GROUP_LANE, g_idx.astype(F32), 0.0))


def _merge(x2, y_sg, y_att, proj, mod, norm2_w, w_sg, w_att, w_o, w_router, seq):
    n, d = x2.shape
    tm = 256
    tpb = seq // tm
    resident = lambda shape: pl.BlockSpec(shape, lambda i: (0, 0), pipeline_mode=pl.Buffered(1))
    return pl.pallas_call(
        _merge_kernel,
        grid=(n // tm,),
        in_specs=[pl.BlockSpec((tm, d), lambda i: (i, 0)),
                  pl.BlockSpec((tm, D_SG), lambda i: (i, 0)),
                  pl.BlockSpec((tm, D_ATT), lambda i: (i, 0)),
                  pl.BlockSpec((tm, d), lambda i: (i, BLK_GATE_SG)),
                  pl.BlockSpec((tm, d), lambda i: (i, BLK_GATE_ATT)),
                  pl.BlockSpec((1, 8, d), lambda i: (i // tpb, 0, 0)),
                  pl.BlockSpec((1, d), lambda i: (0, 0)),
                  resident((D_SG, d)), resident((D_ATT, d)), resident((d, d)),
                  resident((d, LANES))],
        out_specs=[pl.BlockSpec((tm, d), lambda i: (i, 0)),
                   pl.BlockSpec((tm * TOK_ROWS, LANES), lambda i: (i, 0))],
        out_shape=[jax.ShapeDtypeStruct((n, d), F32),
                   jax.ShapeDtypeStruct((n * TOK_ROWS, LANES), F32)],
        compiler_params=_cparams(("parallel",)),
        name="merge",
    )(x2, y_sg, y_att, proj, proj, mod, norm2_w, w_sg, w_att, w_o, w_router)


def _invert_kernel(pos_ref, free_ref, inv_ref, *, n_tok, n_free):
    def place(t, carry):
        inv_ref[pos_ref[t]] = t
        return carry

    def place_free(k, carry):
        inv_ref[free_ref[k]] = n_tok + k
        return carry

    lax.fori_loop(0, n_tok, place, 0, unroll=8)
    lax.fori_loop(0, n_free, place_free, 0, unroll=8)


def _invert(pos, free):
    n_tok, n_free = pos.shape[0], free.shape[0]
    return pl.pallas_call(
        functools.partial(_invert_kernel, n_tok=n_tok, n_free=n_free),
        grid_spec=pltpu.PrefetchScalarGridSpec(
            num_scalar_prefetch=2, grid=(1,), in_specs=[],
            out_specs=pl.BlockSpec(memory_space=pltpu.SMEM)),
        out_shape=jax.ShapeDtypeStruct((n_tok + n_free,), jnp.int32),
        compiler_params=_cparams(("arbitrary",)),
        name="invert",
    )(pos, free)


MOE_TILE = 256


def _moe_kernel(src_ref, dst_ref, tg_ref, na_ref, tok_hbm, w1_ref, w3_ref, w2_ref, y_hbm,
                gbuf, stage, acc_ref, gsem, ssem):
    i = pl.program_id(0)
    n_act = na_ref[0]
    slot = lax.rem(i, 2)

    def gather_start(tile, sl):
        def one(r, carry):
            row0 = pl.multiple_of(r * TOK_ROWS, 8)
            pltpu.make_async_copy(tok_hbm.at[src_ref[tile * MOE_TILE + r]],
                                  gbuf.at[sl, pl.ds(row0, TOK_ROWS), :], gsem.at[sl]).start()
            return carry
        lax.fori_loop(0, MOE_TILE, one, 0, unroll=8)

    def gather_wait(sl):
        for _ in range(MOE_TILE):
            pltpu.make_async_copy(tok_hbm.at[0], gbuf.at[sl, pl.ds(0, TOK_ROWS), :], gsem.at[sl]).wait()

    def scatter_start(tile, sl):
        def one(r, carry):
            row0 = pl.multiple_of(r * D_ROWS, 8)
            pltpu.make_async_copy(stage.at[sl, pl.ds(row0, D_ROWS), :],
                                  y_hbm.at[dst_ref[tile * MOE_TILE + r]], ssem.at[sl]).start()
            return carry
        lax.fori_loop(0, MOE_TILE, one, 0, unroll=8)

    def scatter_wait(sl):
        for _ in range(MOE_TILE):
            pltpu.make_async_copy(stage.at[sl, pl.ds(0, D_ROWS), :], y_hbm.at[0], ssem.at[sl]).wait()

    @pl.when(i < n_act)
    def _():
        @pl.when(i == 0)
        def _():
            gather_start(0, 0)

        @pl.when(i + 1 < n_act)
        def _():
            gather_start(i + 1, 1 - slot)

        gather_wait(slot)
        t = jnp.concatenate([gbuf[slot, pl.ds(a, MOE_TILE, stride=TOK_ROWS), :] for a in range(D_ROWS)],
                            axis=1).astype(BF16)
        routing = gbuf[slot, pl.ds(D_ROWS, MOE_TILE, stride=TOK_ROWS), :]
        lane = lax.broadcasted_iota(jnp.int32, routing.shape, 1)
        lane0 = ROUTER_COL0 + EXPERTS_PER_GROUP * tg_ref[i]
        for k in range(EXPERTS_PER_GROUP):
            a = jnp.dot(t, w1_ref[k], preferred_element_type=F32)
            b = jnp.dot(t, w3_ref[k], preferred_element_type=F32)
            ge = jnp.sum(jnp.where(lane == lane0 + k, routing, 0.0), axis=1, keepdims=True)
            hmid = ((a * jax.nn.sigmoid(a)) * b * ge).astype(BF16)
            y = jnp.dot(hmid, w2_ref[k], preferred_element_type=F32)
            if k == 0:
                acc_ref[...] = y
            else:
                acc_ref[...] += y

        @pl.when(i >= 2)
        def _():
            scatter_wait(slot)

        for a in range(D_ROWS):
            stage[slot, pl.ds(a, MOE_TILE, stride=D_ROWS), :] = acc_ref[:, a * LANES:(a + 1) * LANES]
        scatter_start(i, slot)

        @pl.when(i == n_act - 1)
        def _():
            scatter_wait(slot)

            @pl.when(i >= 1)
            def _():
                scatter_wait(1 - slot)

    @pl.when(i >= n_act)
    def _():
        stage[slot] = jnp.zeros(stage.shape[1:], F32)
        scatter_start(i, slot)
        scatter_wait(slot)


def _moe(src_tok, dst_row, tile_group, n_active, tok_blocks, w1, w3, w2):
    n_slots = src_tok.shape[0]
    d = D_MODEL
    w_blk = lambda i, src, dst, tg, na: (tg[i], 0, 0, 0)
    once = pl.Buffered(1)
    return pl.pallas_call(
        _moe_kernel,
        grid_spec=pltpu.PrefetchScalarGridSpec(
            num_scalar_prefetch=4, grid=(n_slots // MOE_TILE,),
            in_specs=[pl.BlockSpec(memory_space=pl.ANY),
                      pl.BlockSpec((None, EXPERTS_PER_GROUP, d, D_EXPERT), w_blk, pipeline_mode=once),
                      pl.BlockSpec((None, EXPERTS_PER_GROUP, d, D_EXPERT), w_blk, pipeline_mode=once),
                      pl.BlockSpec((None, EXPERTS_PER_GROUP, D_EXPERT, d), w_blk, pipeline_mode=once)],
            out_specs=pl.BlockSpec(memory_space=pl.ANY),
            scratch_shapes=[pltpu.VMEM((2, MOE_TILE * TOK_ROWS, LANES), F32),
                            pltpu.VMEM((2, MOE_TILE * D_ROWS, LANES), F32),
                            pltpu.VMEM((MOE_TILE, d), F32),
                            pltpu.SemaphoreType.DMA((2,)),
                            pltpu.SemaphoreType.DMA((2,))]),
        out_shape=jax.ShapeDtypeStruct((n_slots, D_ROWS, LANES), F32),
        compiler_params=_cparams(("arbitrary",)),
        name="moe",
    )(src_tok, dst_row, tile_group, n_active, tok_blocks, w1, w3, w2)


def _final_kernel(x1_ref, y_ref, mod_ref, fw_ref, o_ref):
    tm = x1_ref.shape[0]
    parts = []
    ss = jnp.zeros((tm, 1), F32)
    for a in range(D_ROWS):
        cols = slice(a * LANES, (a + 1) * LANES)
        part = x1_ref[:, cols] + mod_ref[0, 5:6, cols] * y_ref[pl.ds(a, tm, stride=D_ROWS), :]
        ss = ss + jnp.sum(part * part, axis=1, keepdims=True)
        parts.append(part)
    inv = lax.rsqrt(ss * (1.0 / D_MODEL) + EPS)
    for a in range(D_ROWS):
        cols = slice(a * LANES, (a + 1) * LANES)
        o_ref[:, cols] = parts[a] * inv * fw_ref[:, cols]


def _final(x1, y_blocks, mod, final_w, seq):
    n, d = x1.shape
    tm = 256
    tpb = seq // tm
    return pl.pallas_call(
        _final_kernel,
        grid=(n // tm,),
        in_specs=[pl.BlockSpec((tm, d), lambda i: (i, 0)),
                  pl.BlockSpec((tm * D_ROWS, LANES), lambda i: (i, 0)),
                  pl.BlockSpec((1, 8, d), lambda i: (i // tpb, 0, 0)),
                  pl.BlockSpec((1, d), lambda i: (0, 0))],
        out_specs=pl.BlockSpec((tm, d), lambda i: (i, 0)),
        out_shape=jax.ShapeDtypeStruct((n, d), F32),
        compiler_params=_cparams(("parallel",)),
        name="final",
    )(x1, y_blocks, mod, final_w)


def _group_sort_plan(group, n):
    onehot = (group[:, None] == jnp.arange(N_GROUPS, dtype=jnp.int32)[None, :]).astype(jnp.int32)
    blk = LANES
    within = jnp.einsum("ts,bsg->btg", jnp.tril(jnp.ones((blk, blk), F32)),
                        onehot.astype(F32).reshape(n // blk, blk, N_GROUPS))
    before = jnp.tril(jnp.ones((n // blk, n // blk), F32), -1) @ within[:, -1, :]
    incl = (within + before[:, None, :]).reshape(n, N_GROUPS).astype(jnp.int32)
    tiles = (incl[-1] + MOE_TILE - 1) // MOE_TILE
    tile_end = jnp.cumsum(tiles)
    start = (tile_end - tiles) * MOE_TILE
    rank = jnp.sum(incl * onehot, axis=1) - 1
    pos = jnp.sum(onehot * start[None, :], axis=1) + rank
    n_tiles = n // MOE_TILE + N_GROUPS
    tile_ids = jnp.arange(n_tiles, dtype=jnp.int32)
    tile_group = jnp.sum((tile_ids[:, None] >= tile_end[None, :]).astype(jnp.int32), axis=1)
    tile_group = jnp.minimum(tile_group, N_GROUPS - 1)
    pad = tiles * MOE_TILE - incl[-1]
    pad_end = jnp.cumsum(pad)
    k = jnp.arange(N_GROUPS * MOE_TILE, dtype=jnp.int32)
    seg = jnp.sum((k[:, None] >= pad_end[None, :]).astype(jnp.int32), axis=1)
    seg_hot = (seg[:, None] == jnp.arange(N_GROUPS, dtype=jnp.int32)[None, :]).astype(jnp.int32)
    in_group = jnp.sum(seg_hot * (start + incl[-1] - (pad_end - pad))[None, :], axis=1) + k
    tail = tile_end[-1] * MOE_TILE + k - pad_end[-1]
    free = jnp.where(seg < N_GROUPS, in_group, tail)
    return (pos.astype(jnp.int32), free.astype(jnp.int32), tile_group.astype(jnp.int32),
            tile_end[-1:].astype(jnp.int32))


def kernel(x, c, w_ada, b_ada, norm1_w, norm2_w, final_norm_w, w_in, sg_ln_w, sg_ln_b, w_spatial,
           b_spatial, rel_bias, w_out_sg, w_out_att, w_o, w_router_group, w_router_expert,
           w_exp_gate, w_exp_up, w_exp_down):
    batch, seq, d = x.shape
    assert d == D_MODEL and w_ada.shape[0] == 1
    assert seq % 1024 == 0 and seq // MOBA_BLOCK <= LANES
    n = batch * seq
    x2 = x.reshape(n, d)

    c_pad = jnp.zeros((8, d), F32).at[:batch].set(c)
    mod = _adaln(c_pad, w_ada[0], b_ada[0].reshape(1, 6 * d))
    mod = mod[:batch].reshape(batch, 6, d)
    mod = jnp.concatenate([mod, jnp.zeros((batch, 2, d), F32)], axis=1)

    colscale = jnp.ones((1, IN_COLS), F32).at[:, CB_Q * LANES:CB_K * LANES].set(HEAD_DIM ** -0.5 * LOG2E)
    w_in_r = jnp.concatenate([w_in[0][:, IN_COLS - N_GATE_COLS:], w_in[0][:, :IN_COLS - N_GATE_COLS]],
                             axis=1).astype(BF16)
    proj = _proj(x2, mod, norm1_w[0].reshape(1, d), w_in_r, colscale, seq)

    y_sg = _sgu(proj, sg_ln_w[0].reshape(1, D_SG), sg_ln_b[0].reshape(1, D_SG),
                w_spatial[0], b_spatial[0].T)

    qt, kx, vt, kmean = _moba_prep(proj, batch, seq)
    bias_tiles = _bias_tiles(rel_bias)
    y_att = _moba(rel_bias, qt, kx, vt, kmean, bias_tiles, batch, seq)

    w_router = jnp.zeros((d, LANES), F32)
    w_router = w_router.at[:, :N_GROUPS].set(w_router_group[0])
    w_router = w_router.at[:, ROUTER_COL0:ROUTER_COL0 + N_EXPERTS].set(w_router_expert[0])
    x1, tok = _merge(x2, y_sg, y_att, proj, mod, norm2_w[0].reshape(1, d),
                     w_out_sg[0].astype(BF16), w_out_att[0].astype(BF16),
                     w_o[0].astype(BF16), w_router, seq)

    tok = tok.reshape(n, TOK_ROWS, LANES)
    group = tok[:, D_ROWS, GROUP_LANE].astype(jnp.int32)
    pos, free, tile_group, n_active = _group_sort_plan(group, n)
    inv = _invert(pos, free)
    src_tok = jnp.where(inv < n, inv, 0)
    by_group = lambda w: w[0].astype(BF16).reshape((N_GROUPS, EXPERTS_PER_GROUP) + w.shape[2:])
    y = _moe(src_tok, inv, tile_group, n_active, tok, by_group(w_exp_gate), by_group(w_exp_up),
             by_group(w_exp_down))
    out = _final(x1, y.reshape(-1, LANES), mod, final_norm_w.reshape(1, d), seq)
    return out.reshape(batch, seq, d)
```

```python
import functools
import math

import jax
import jax.numpy as jnp
from jax import lax
from jax.experimental import pallas as pl
from jax.experimental.pallas import tpu as pltpu

F32 = jnp.float32
BF16 = jnp.bfloat16

LANES = 128
D_MODEL = 2048
D_SG = D_MODEL // 2
SG_GROUPS = 8
SG_CHUNK = 128
ATT_HEADS = 8
HEAD_DIM = 128
D_ATT = ATT_HEADS * HEAD_DIM
MOBA_BLOCK = 256
MOBA_TOPK = 3
REL_BUCKETS = 32
REL_MAX_DIST = 128
N_GROUPS = 4
EXPERTS_PER_GROUP = 4
N_EXPERTS = N_GROUPS * EXPERTS_PER_GROUP
D_EXPERT = 512
EPS = 1e-6
IN_COLS = 2 * D_SG + 3 * D_ATT + 2 * D_MODEL

N_GATE_COLS = 2 * D_MODEL
BLK_GATE_SG = 0
BLK_GATE_ATT = 1
BLK_U = N_GATE_COLS // D_SG
BLK_V = BLK_U + 1
CB_Q = (N_GATE_COLS + 2 * D_SG) // LANES
CB_K = CB_Q + ATT_HEADS
CB_VAL = CB_K + ATT_HEADS

MASK_NEG = -1e9
LOG2E = math.log2(math.e)
ROUTER_COL0 = N_GROUPS
GROUP_LANE = 0
D_ROWS = D_MODEL // LANES
TOK_ROWS = D_ROWS + 8
VMEM_LIMIT = 56 * 1024 * 1024


def _cparams(sem):
    return pltpu.CompilerParams(dimension_semantics=sem, vmem_limit_bytes=VMEM_LIMIT)


def _nt_dot(a, b):
    return lax.dot_general(a, b, (((1,), (1,)), ((), ())), preferred_element_type=F32)


def _split_bf16(a):
    hi = a.astype(BF16)
    lo = (a - hi.astype(F32)).astype(BF16)
    return hi, lo


def _adaln_kernel(c_ref, w_ref, b_ref, o_ref):
    c = c_ref[...]
    ca = c * jax.nn.sigmoid(c)
    o_ref[...] = jnp.dot(ca, w_ref[...], preferred_element_type=F32,
                         precision=lax.Precision.HIGHEST) + b_ref[...]


def _adaln(c_pad, w, b):
    rows, d = c_pad.shape
    cols = w.shape[1]
    tn = 1024
    return pl.pallas_call(
        _adaln_kernel,
        grid=(cols // tn,),
        in_specs=[pl.BlockSpec((rows, d), lambda j: (0, 0)),
                  pl.BlockSpec((d, tn), lambda j: (0, j)),
                  pl.BlockSpec((1, tn), lambda j: (0, j))],
        out_specs=pl.BlockSpec((rows, tn), lambda j: (0, j)),
        out_shape=jax.ShapeDtypeStruct((rows, cols), F32),
        compiler_params=_cparams(("arbitrary",)),
        name="adaln",
    )(c_pad, w, b)


def _proj_kernel(x_ref, mod_ref, nw_ref, w_ref, cs_ref, o_ref, h_scr):
    @pl.when(pl.program_id(1) == 0)
    def _():
        x = x_ref[...]
        ms = jnp.mean(x * x, axis=-1, keepdims=True)
        y = x * lax.rsqrt(ms + EPS) * nw_ref[...]
        h = y * (1.0 + mod_ref[0, 1:2, :]) + mod_ref[0, 0:1, :]
        h_scr[...] = h.astype(BF16)

    acc = jnp.dot(h_scr[...], w_ref[...], preferred_element_type=F32)
    o_ref[...] = (acc * cs_ref[...]).astype(BF16)


def _proj(x2, mod, norm_w, w_in, colscale, seq):
    n, d = x2.shape
    cols = w_in.shape[1]
    tm = min(1024, seq)
    tn = 1024
    tpb = seq // tm
    n_cb = cols // tn
    first = (cols - N_GATE_COLS) // tn
    assert (cols - N_GATE_COLS) % tn == 0
    return pl.pallas_call(
        _proj_kernel,
        grid=(n // tm, n_cb),
        in_specs=[pl.BlockSpec((tm, d), lambda i, j: (i, 0)),
                  pl.BlockSpec((1, 8, d), lambda i, j: (i // tpb, 0, 0)),
                  pl.BlockSpec((1, d), lambda i, j: (0, 0)),
                  pl.BlockSpec((d, tn), lambda i, j: (0, (j + first) % n_cb)),
                  pl.BlockSpec((1, tn), lambda i, j: (0, j))],
        out_specs=pl.BlockSpec((tm, tn), lambda i, j: (i, j)),
        out_shape=jax.ShapeDtypeStruct((n, cols), BF16),
        scratch_shapes=[pltpu.VMEM((tm, d), BF16)],
        compiler_params=_cparams(("parallel", "arbitrary")),
        name="proj",
    )(x2, mod, norm_w, w_in, colscale)


def _gelu(a):
    return 0.5 * a * (1.0 + lax.erf(a * (1.0 / math.sqrt(2.0))))


def _sgu_kernel(u_ref, v_ref, lnw_ref, lnb_ref, ws_ref, bst_ref, o_ref, *, chunks):
    u = _gelu(u_ref[...].astype(F32))
    v = _gelu(v_ref[...].astype(F32))
    mu = jnp.mean(v, axis=-1, keepdims=True)
    vc = v - mu
    var = jnp.mean(vc * vc, axis=-1, keepdims=True)
    vn = (vc * lax.rsqrt(var + EPS) * lnw_ref[...] + lnb_ref[...]).astype(BF16)
    row = lax.broadcasted_iota(jnp.int32, (SG_CHUNK, SG_CHUNK), 0)
    col = lax.broadcasted_iota(jnp.int32, (SG_CHUNK, SG_CHUNK), 1)
    causal = col <= row
    for g in range(SG_GROUPS):
        wm = jnp.where(causal, ws_ref[g], 0.0).astype(BF16)
        bcol = bst_ref[:, g:g + 1]
        gs = slice(g * LANES, (g + 1) * LANES)
        for c in range(chunks):
            rs = slice(c * SG_CHUNK, (c + 1) * SG_CHUNK)
            z = jnp.dot(wm, vn[rs, gs], preferred_element_type=F32) + bcol
            o_ref[rs, gs] = (u[rs, gs] * z).astype(BF16)


def _sgu(proj, ln_w, ln_b, w_s, b_s_t):
    n = proj.shape[0]
    chunks = 4
    tm = chunks * SG_CHUNK
    return pl.pallas_call(
        functools.partial(_sgu_kernel, chunks=chunks),
        grid=(n // tm,),
        in_specs=[pl.BlockSpec((tm, D_SG), lambda i: (i, BLK_U)),
                  pl.BlockSpec((tm, D_SG), lambda i: (i, BLK_V)),
                  pl.BlockSpec((1, D_SG), lambda i: (0, 0)),
                  pl.BlockSpec((1, D_SG), lambda i: (0, 0)),
                  pl.BlockSpec((SG_GROUPS, SG_CHUNK, SG_CHUNK), lambda i: (0, 0, 0)),
                  pl.BlockSpec((SG_CHUNK, SG_GROUPS), lambda i: (0, 0))],
        out_specs=pl.BlockSpec((tm, D_SG), lambda i: (i, 0)),
        out_shape=jax.ShapeDtypeStruct((n, D_SG), BF16),
        compiler_params=_cparams(("parallel",)),
        name="sgu",
    )(proj, proj, ln_w, ln_b, w_s, b_s_t)


VT_ROWS = HEAD_DIM + 16
EXT_LO = REL_BUCKETS


def _moba_prep_kernel(q_ref, k_ref, v_ref, qt_ref, kx_ref, vt_ref, km_ref, *, nb):
    lane = lax.broadcasted_iota(jnp.int32, (MOBA_BLOCK, LANES), 1)
    pad = jnp.concatenate([jnp.ones((1, MOBA_BLOCK), F32),
                           jnp.zeros((VT_ROWS - HEAD_DIM - 1, MOBA_BLOCK), F32)], axis=0)
    for j in range(nb):
        rows = slice(j * MOBA_BLOCK, (j + 1) * MOBA_BLOCK)
        kj = k_ref[rows, :]
        onehot = jnp.where((lane == j) | (lane == j + EXT_LO), 1.0, 0.0).astype(BF16)
        kx_ref[rows, :] = jnp.concatenate([kj, onehot], axis=1)
        km_ref[j:j + 1, :] = jnp.mean(kj.astype(F32), axis=0, keepdims=True)
        vt = v_ref[rows, :].astype(F32).T
        vt_ref[j] = jnp.concatenate([vt, pad], axis=0).astype(BF16)
        qt_ref[j] = q_ref[rows, :].astype(F32).T.astype(BF16)


def _moba_prep(proj, batch, seq):
    nb = seq // MOBA_BLOCK
    bh = lambda shape: pl.BlockSpec((None, None) + shape, lambda b, h: (b, h) + (0,) * len(shape))
    return pl.pallas_call(
        functools.partial(_moba_prep_kernel, nb=nb),
        grid=(batch, ATT_HEADS),
        in_specs=[pl.BlockSpec((seq, HEAD_DIM), lambda b, h: (b, CB_Q + h)),
                  pl.BlockSpec((seq, HEAD_DIM), lambda b, h: (b, CB_K + h)),
                  pl.BlockSpec((seq, HEAD_DIM), lambda b, h: (b, CB_VAL + h))],
        out_specs=[bh((nb, HEAD_DIM, MOBA_BLOCK)), bh((seq, 2 * HEAD_DIM)),
                   bh((nb, VT_ROWS, MOBA_BLOCK)), bh((nb, HEAD_DIM))],
        out_shape=[jax.ShapeDtypeStruct((batch, ATT_HEADS, nb, HEAD_DIM, MOBA_BLOCK), BF16),
                   jax.ShapeDtypeStruct((batch, ATT_HEADS, seq, 2 * HEAD_DIM), BF16),
                   jax.ShapeDtypeStruct((batch, ATT_HEADS, nb, VT_ROWS, MOBA_BLOCK), BF16),
                   jax.ShapeDtypeStruct((batch, ATT_HEADS, nb, HEAD_DIM), F32)],
        compiler_params=_cparams(("parallel", "parallel")),
        name="moba_prep",
    )(proj, proj, proj)


def _bias_tiles_kernel(rb_ref, o_ref):
    h = pl.program_id(0)
    kj = lax.broadcasted_iota(jnp.int32, (MOBA_BLOCK, MOBA_BLOCK), 0)
    qi = lax.broadcasted_iota(jnp.int32, (MOBA_BLOCK, MOBA_BLOCK), 1)
    max_exact = REL_BUCKETS // 2
    for t in range(2):
        rel = qi - kj + MOBA_BLOCK * t
        n = jnp.maximum(rel, 0)
        nf = jnp.maximum(n, max_exact).astype(F32)
        large = max_exact + (jnp.log(nf / max_exact) / math.log(REL_MAX_DIST / max_exact)
                             * (REL_BUCKETS - max_exact)).astype(jnp.int32)
        large = jnp.minimum(large, REL_BUCKETS - 1)
        bucket = jnp.where(n < max_exact, n, large)
        bias = jnp.zeros((MOBA_BLOCK, MOBA_BLOCK), F32)
        for r in range(REL_BUCKETS):
            bias = jnp.where(bucket == r, rb_ref[r, h], bias)
        o_ref[t] = jnp.where(rel >= 0, bias * LOG2E, MASK_NEG)


def _bias_tiles(rel_bias):
    return pl.pallas_call(
        _bias_tiles_kernel,
        grid=(ATT_HEADS,),
        in_specs=[pl.BlockSpec(memory_space=pltpu.SMEM)],
        out_specs=pl.BlockSpec((None, 2, MOBA_BLOCK, MOBA_BLOCK), lambda h: (h, 0, 0, 0)),
        out_shape=jax.ShapeDtypeStruct((ATT_HEADS, 2, MOBA_BLOCK, MOBA_BLOCK), F32),
        compiler_params=_cparams(("arbitrary",)),
        name="bias_tiles",
    )(rel_bias)


FAR_BLOCKS = 4
HEADS_PER_STEP = 4


def _moba_kernel(rb_ref, qt_ref, kx_ref, vt_ref, km_ref, bias_ref, o_ref, *, nb):
    i = pl.program_id(2)
    jp = jnp.maximum(i - 1, 0)
    prev0 = pl.multiple_of(jp * MOBA_BLOCK, MOBA_BLOCK)
    own0 = pl.multiple_of(i * MOBA_BLOCK, MOBA_BLOCK)
    no_prev = jnp.where(i > 0, 0.0, MASK_NEG)
    neg_inf = jnp.float32(-jnp.inf)

    heads = range(HEADS_PER_STEP)

    def choose_blocks(hh):
        km_hi, km_lo = _split_bf16(km_ref[hh])
        score = (jnp.dot(km_hi, qt_ref[hh], preferred_element_type=F32)
                 + jnp.dot(km_lo, qt_ref[hh], preferred_element_type=F32))
        bid = lax.broadcasted_iota(jnp.int32, score.shape, 0)
        score = jnp.where(bid < i, score, neg_inf)
        chosen = bid < 0
        for _ in range(MOBA_TOPK):
            mx = jnp.max(score, axis=0, keepdims=True)
            hit = (score == mx) & (mx > neg_inf)
            idx = jnp.min(jnp.where(hit, bid, nb), axis=0, keepdims=True)
            pick = bid == idx
            chosen = chosen | pick
            score = jnp.where(pick, neg_inf, score)
        return chosen

    def with_mask(hh, val):
        hi, lo = _split_bf16(val)
        fill = jnp.zeros((HEAD_DIM - 2 * EXT_LO, MOBA_BLOCK), BF16)
        if nb < EXT_LO:
            gap = jnp.zeros((EXT_LO - nb, MOBA_BLOCK), BF16)
            return jnp.concatenate([qt_ref[hh], hi, gap, lo, gap, fill], axis=0)
        return jnp.concatenate([qt_ref[hh], hi, lo, fill], axis=0)

    chosen = [choose_blocks(hh) for hh in heads]
    bid = lax.broadcasted_iota(jnp.int32, chosen[0].shape, 0)
    far_bias = [rb_ref[REL_BUCKETS - 1, pl.program_id(1) * HEADS_PER_STEP + hh] * LOG2E for hh in heads]
    qx_fars = [with_mask(hh, jnp.where(chosen[hh] & (bid <= i - 2), far_bias[hh], MASK_NEG)) for hh in heads]
    qx_nears = [with_mask(hh, jnp.where((bid == i) | (chosen[hh] & (bid == i - 1)), 0.0, MASK_NEG))
                for hh in heads]

    ss = [jnp.dot(jnp.concatenate([kx_ref[hh, pl.ds(prev0, MOBA_BLOCK), :],
                                   kx_ref[hh, pl.ds(own0, MOBA_BLOCK), :]], axis=0),
                  qx_nears[hh], preferred_element_type=F32)
          + jnp.concatenate([bias_ref[hh, 1] + no_prev, bias_ref[hh, 0]], axis=0) for hh in heads]
    ms = [jnp.max(ss[hh], axis=0, keepdims=True) for hh in heads]
    ps = [jnp.exp2(ss[hh] - ms[hh]).astype(BF16) for hh in heads]
    accs = [jnp.dot(jnp.concatenate([vt_ref[hh, jp], vt_ref[hh, i]], axis=1), ps[hh],
                    preferred_element_type=F32) for hh in heads]

    def far_step(c, carry):
        row0 = pl.multiple_of(c * (FAR_BLOCKS * MOBA_BLOCK), FAR_BLOCKS * MOBA_BLOCK)
        ss = [jnp.dot(kx_ref[hh, pl.ds(row0, FAR_BLOCKS * MOBA_BLOCK), :], qx_fars[hh],
                      preferred_element_type=F32) for hh in heads]
        m_news = [jnp.maximum(carry[hh][0], jnp.max(ss[hh], axis=0, keepdims=True)) for hh in heads]
        ps = [jnp.exp2(ss[hh] - m_news[hh]).astype(BF16) for hh in heads]
        out = []
        for hh in heads:
            m, acc = carry[hh]
            alpha = jnp.exp2(m - m_news[hh])
            vt = jnp.concatenate([vt_ref[hh, c * FAR_BLOCKS + k] for k in range(FAR_BLOCKS)], axis=1)
            out.append((m_news[hh], alpha * acc + jnp.dot(vt, ps[hh], preferred_element_type=F32)))
        return tuple(out)

    n_far = lax.shift_right_logical(jnp.maximum(i - 1, 0) + FAR_BLOCKS - 1, 2)
    final = lax.fori_loop(0, n_far, far_step, tuple((ms[hh], accs[hh]) for hh in heads))
    for hh in heads:
        acc = final[hh][1]
        out = acc[0:HEAD_DIM] / acc[HEAD_DIM:HEAD_DIM + 1]
        o_ref[:, hh * HEAD_DIM:(hh + 1) * HEAD_DIM] = out.T.astype(BF16)


def _moba(rel_bias, qt, kx, vt, kmean, bias_tiles, batch, seq):
    nb = seq // MOBA_BLOCK
    hps = HEADS_PER_STEP
    assert nb % FAR_BLOCKS == 0 and FAR_BLOCKS == 4 and ATT_HEADS % hps == 0
    bh = lambda shape: pl.BlockSpec((None, hps) + shape, lambda b, h, i: (b, h) + (0,) * len(shape),
                                    pipeline_mode=pl.Buffered(1))
    return pl.pallas_call(
        functools.partial(_moba_kernel, nb=nb),
        grid=(batch, ATT_HEADS // hps, nb),
        in_specs=[pl.BlockSpec(memory_space=pltpu.SMEM),
                  pl.BlockSpec((None, hps, None, HEAD_DIM, MOBA_BLOCK), lambda b, h, i: (b, h, i, 0, 0)),
                  bh((seq, 2 * HEAD_DIM)), bh((nb, VT_ROWS, MOBA_BLOCK)), bh((nb, HEAD_DIM)),
                  pl.BlockSpec((hps, 2, MOBA_BLOCK, MOBA_BLOCK), lambda b, h, i: (h, 0, 0, 0))],
        out_specs=pl.BlockSpec((MOBA_BLOCK, hps * HEAD_DIM), lambda b, h, i: (b * nb + i, h)),
        out_shape=jax.ShapeDtypeStruct((batch * seq, D_ATT), BF16),
        compiler_params=_cparams(("parallel", "parallel", "arbitrary")),
        name="moba",
    )(rel_bias, qt, kx, vt, kmean, bias_tiles)


def _merge_kernel(x_ref, ysg_ref, yatt_ref, gsg_ref, gatt_ref, mod_ref, nw_ref,
                  wsg_ref, watt_ref, wo_ref, wr_ref, x1_ref, tok_ref, gid_ref):
    a_sg = jnp.dot(ysg_ref[...], wsg_ref[...], preferred_element_type=F32)
    a_att = jnp.dot(yatt_ref[...], watt_ref[...], preferred_element_type=F32)
    merged = (jax.nn.sigmoid(gsg_ref[...].astype(F32)) * a_sg
              + jax.nn.sigmoid(gatt_ref[...].astype(F32)) * a_att)
    mixed = jnp.dot(merged.astype(BF16), wo_ref[...], preferred_element_type=F32)
    x1 = x_ref[...] + mod_ref[0, 2:3, :] * mixed
    x1_ref[...] = x1
    ms = jnp.mean(x1 * x1, axis=-1, keepdims=True)
    y = x1 * lax.rsqrt(ms + EPS) * nw_ref[...]
    h2 = y * (1.0 + mod_ref[0, 4:5, :]) + mod_ref[0, 3:4, :]
    tm = h2.shape[0]
    for a in range(D_ROWS):
        tok_ref[pl.ds(a, tm, stride=TOK_ROWS), :] = h2[:, a * LANES:(a + 1) * LANES]
    for a in range(D_ROWS + 1, TOK_ROWS):
        tok_ref[pl.ds(a, tm, stride=TOK_ROWS), :] = jnp.zeros((tm, LANES), F32)

    h_hi, h_lo = _split_bf16(h2)
    w_hi, w_lo = _split_bf16(wr_ref[...])
    logits = (jnp.dot(h_hi, w_hi, preferred_element_type=F32)
              + jnp.dot(h_lo, w_hi, preferred_element_type=F32)
              + jnp.dot(h_hi, w_lo, preferred_element_type=F32))

    lane = lax.broadcasted_iota(jnp.int32, logits.shape, 1)
    neg_inf = jnp.float32(-jnp.inf)
    in_g = lane < N_GROUPS
    lg = jnp.where(in_g, logits, neg_inf)
    mg = jnp.max(lg, axis=1, keepdims=True)
    eg = jnp.exp(lg - mg)
    g_prob = eg / jnp.sum(eg, axis=1, keepdims=True)
    g_p = jnp.max(g_prob, axis=1, keepdims=True)
    g_idx = jnp.min(jnp.where(g_prob == g_p, lane, LANES), axis=1, keepdims=True)
    lo_col = ROUTER_COL0 + EXPERTS_PER_GROUP * g_idx
    in_e = (lane >= lo_col) & (lane < lo_col + EXPERTS_PER_GROUP)
    le = jnp.where(in_e, logits, neg_inf)
    me = jnp.max(le, axis=1, keepdims=True)
    ee = jnp.exp(le - me)
    e_prob = jnp.where(in_e, ee / jnp.sum(ee, axis=1, keepdims=True), -1.0)
    p1 = jnp.max(e_prob, axis=1, keepdims=True)
    i1 = jnp.min(jnp.where(e_prob == p1, lane, LANES), axis=1, keepdims=True)
    rest = jnp.where(lane == i1, -1.0, e_prob)
    p2 = jnp.max(rest, axis=1, keepdims=True)
    i2 = jnp.min(jnp.where(rest == p2, lane, LANES), axis=1, keepdims=True)
    denom = p1 + p2
    routing = (jnp.where(lane == i1, g_p * (p1 / denom), 0.0)
               + jnp.where(lane == i2, g_p * (p2 / denom), 0.0)
               + jnp.where(lane == GROUP_LANE, g_idx.astype(F32), 0.0))
    tok_ref[pl.ds(D_ROWS, tm, stride=TOK_ROWS), :] = routing
    gid_ref[...] = routing.T[GROUP_LANE:GROUP_LANE + 1, :].astype(jnp.int32)


def _merge(x2, y_sg, y_att, proj, mod, norm2_w, w_sg, w_att, w_o, w_router, seq):
    n, d = x2.shape
    tm = 256
    tpb = seq // tm
    resident = lambda shape: pl.BlockSpec(shape, lambda i: (0, 0), pipeline_mode=pl.Buffered(1))
    return pl.pallas_call(
        _merge_kernel,
        grid=(n // tm,),
        in_specs=[pl.BlockSpec((tm, d), lambda i: (i, 0)),
                  pl.BlockSpec((tm, D_SG), lambda i: (i, 0)),
                  pl.BlockSpec((tm, D_ATT), lambda i: (i, 0)),
                  pl.BlockSpec((tm, d), lambda i: (i, BLK_GATE_SG)),
                  pl.BlockSpec((tm, d), lambda i: (i, BLK_GATE_ATT)),
                  pl.BlockSpec((1, 8, d), lambda i: (i // tpb, 0, 0)),
                  pl.BlockSpec((1, d), lambda i: (0, 0)),
                  resident((D_SG, d)), resident((D_ATT, d)), resident((d, d)),
                  resident((d, LANES))],
        out_specs=[pl.BlockSpec((tm, d), lambda i: (i, 0)),
                   pl.BlockSpec((tm * TOK_ROWS, LANES), lambda i: (i, 0)),
                   pl.BlockSpec((None, 1, tm), lambda i: (i, 0, 0))],
        out_shape=[jax.ShapeDtypeStruct((n, d), F32),
                   jax.ShapeDtypeStruct((n * TOK_ROWS, LANES), F32),
                   jax.ShapeDtypeStruct((n // tm, 1, tm), jnp.int32)],
        compiler_params=_cparams(("parallel",)),
        name="merge",
    )(x2, y_sg, y_att, proj, proj, mod, norm2_w, w_sg, w_att, w_o, w_router)


def _invert_kernel(pos_ref, free_ref, inv_ref, *, n_tok, n_free):
    def place(t, carry):
        inv_ref[pos_ref[t]] = t
        return carry

    def place_free(k, carry):
        inv_ref[free_ref[k]] = n_tok + k
        return carry

    lax.fori_loop(0, n_tok, place, 0, unroll=8)
    lax.fori_loop(0, n_free, place_free, 0, unroll=8)


def _invert(pos, free):
    n_tok, n_free = pos.shape[0], free.shape[0]
    return pl.pallas_call(
        functools.partial(_invert_kernel, n_tok=n_tok, n_free=n_free),
        grid_spec=pltpu.PrefetchScalarGridSpec(
            num_scalar_prefetch=2, grid=(1,), in_specs=[],
            out_specs=pl.BlockSpec(memory_space=pltpu.SMEM)),
        out_shape=jax.ShapeDtypeStruct((n_tok + n_free,), jnp.int32),
        compiler_params=_cparams(("arbitrary",)),
        name="invert",
    )(pos, free)


MOE_TILE = 256


def _moe_kernel(src_ref, dst_ref, tg_ref, na_ref, tok_hbm, w1_ref, w3_ref, w2_ref, y_hbm,
                gbuf, stage, acc_ref, gsem, ssem):
    i = pl.program_id(0)
    n_act = na_ref[0]
    slot = lax.rem(i, 2)

    def gather_start(tile, sl):
        def one(r, carry):
            row0 = pl.multiple_of(r * TOK_ROWS, 8)
            pltpu.make_async_copy(tok_hbm.at[src_ref[tile * MOE_TILE + r]],
                                  gbuf.at[sl, pl.ds(row0, TOK_ROWS), :], gsem.at[sl]).start()
            return carry
        lax.fori_loop(0, MOE_TILE, one, 0, unroll=8)

    def gather_wait(sl):
        for _ in range(MOE_TILE):
            pltpu.make_async_copy(tok_hbm.at[0], gbuf.at[sl, pl.ds(0, TOK_ROWS), :], gsem.at[sl]).wait()

    def scatter_start(tile, sl):
        def one(r, carry):
            row0 = pl.multiple_of(r * D_ROWS, 8)
            pltpu.make_async_copy(stage.at[sl, pl.ds(row0, D_ROWS), :],
                                  y_hbm.at[dst_ref[tile * MOE_TILE + r]], ssem.at[sl]).start()
            return carry
        lax.fori_loop(0, MOE_TILE, one, 0, unroll=8)

    def scatter_wait(sl):
        for _ in range(MOE_TILE):
            pltpu.make_async_copy(stage.at[sl, pl.ds(0, D_ROWS), :], y_hbm.at[0], ssem.at[sl]).wait()

    @pl.when(i < n_act)
    def _():
        @pl.when(i == 0)
        def _():
            gather_start(0, 0)

        @pl.when(i + 1 < n_act)
        def _():
            gather_start(i + 1, 1 - slot)

        gather_wait(slot)
        t = jnp.concatenate([gbuf[slot, pl.ds(a, MOE_TILE, stride=TOK_ROWS), :] for a in range(D_ROWS)],
                            axis=1).astype(BF16)
        routing = gbuf[slot, pl.ds(D_ROWS, MOE_TILE, stride=TOK_ROWS), :]
        lane = lax.broadcasted_iota(jnp.int32, routing.shape, 1)
        lane0 = ROUTER_COL0 + EXPERTS_PER_GROUP * tg_ref[i]
        for k in range(EXPERTS_PER_GROUP):
            a = jnp.dot(t, w1_ref[k], preferred_element_type=F32)
            b = jnp.dot(t, w3_ref[k], preferred_element_type=F32)
            ge = jnp.sum(jnp.where(lane == lane0 + k, routing, 0.0), axis=1, keepdims=True)
            hmid = ((a * jax.nn.sigmoid(a)) * b * ge).astype(BF16)
            y = jnp.dot(hmid, w2_ref[k], preferred_element_type=F32)
            if k == 0:
                acc_ref[...] = y
            else:
                acc_ref[...] += y

        @pl.when(i >= 2)
        def _():
            scatter_wait(slot)

        for a in range(D_ROWS):
            stage[slot, pl.ds(a, MOE_TILE, stride=D_ROWS), :] = acc_ref[:, a * LANES:(a + 1) * LANES]
        scatter_start(i, slot)

        @pl.when(i == n_act - 1)
        def _():
            scatter_wait(slot)

            @pl.when(i >= 1)
            def _():
                scatter_wait(1 - slot)

    @pl.when(i >= n_act)
    def _():
        stage[slot] = jnp.zeros(stage.shape[1:], F32)
        scatter_start(i, slot)
        scatter_wait(slot)


def _moe(src_tok, dst_row, tile_group, n_active, tok_blocks, w1, w3, w2):
    n_slots = src_tok.shape[0]
    d = D_MODEL
    w_blk = lambda i, src, dst, tg, na: (tg[i], 0, 0, 0)
    once = pl.Buffered(1)
    return pl.pallas_call(
        _moe_kernel,
        grid_spec=pltpu.PrefetchScalarGridSpec(
            num_scalar_prefetch=4, grid=(n_slots // MOE_TILE,),
            in_specs=[pl.BlockSpec(memory_space=pl.ANY),
                      pl.BlockSpec((None, EXPERTS_PER_GROUP, d, D_EXPERT), w_blk, pipeline_mode=once),
                      pl.BlockSpec((None, EXPERTS_PER_GROUP, d, D_EXPERT), w_blk, pipeline_mode=once),
                      pl.BlockSpec((None, EXPERTS_PER_GROUP, D_EXPERT, d), w_blk, pipeline_mode=once)],
            out_specs=pl.BlockSpec(memory_space=pl.ANY),
            scratch_shapes=[pltpu.VMEM((2, MOE_TILE * TOK_ROWS, LANES), F32),
                            pltpu.VMEM((2, MOE_TILE * D_ROWS, LANES), F32),
                            pltpu.VMEM((MOE_TILE, d), F32),
                            pltpu.SemaphoreType.DMA((2,)),
                            pltpu.SemaphoreType.DMA((2,))]),
        out_shape=jax.ShapeDtypeStruct((n_slots, D_ROWS, LANES), F32),
        compiler_params=_cparams(("arbitrary",)),
        name="moe",
    )(src_tok, dst_row, tile_group, n_active, tok_blocks, w1, w3, w2)


def _final_kernel(x1_ref, y_ref, mod_ref, fw_ref, o_ref):
    tm = x1_ref.shape[0]
    parts = []
    ss = jnp.zeros((tm, 1), F32)
    for a in range(D_ROWS):
        cols = slice(a * LANES, (a + 1) * LANES)
        part = x1_ref[:, cols] + mod_ref[0, 5:6, cols] * y_ref[pl.ds(a, tm, stride=D_ROWS), :]
        ss = ss + jnp.sum(part * part, axis=1, keepdims=True)
        parts.append(part)
    inv = lax.rsqrt(ss * (1.0 / D_MODEL) + EPS)
    for a in range(D_ROWS):
        cols = slice(a * LANES, (a + 1) * LANES)
        o_ref[:, cols] = parts[a] * inv * fw_ref[:, cols]


def _final(x1, y_blocks, mod, final_w, seq):
    n, d = x1.shape
    tm = 256
    tpb = seq // tm
    return pl.pallas_call(
        _final_kernel,
        grid=(n // tm,),
        in_specs=[pl.BlockSpec((tm, d), lambda i: (i, 0)),
                  pl.BlockSpec((tm * D_ROWS, LANES), lambda i: (i, 0)),
                  pl.BlockSpec((1, 8, d), lambda i: (i // tpb, 0, 0)),
                  pl.BlockSpec((1, d), lambda i: (0, 0))],
        out_specs=pl.BlockSpec((tm, d), lambda i: (i, 0)),
        out_shape=jax.ShapeDtypeStruct((n, d), F32),
        compiler_params=_cparams(("parallel",)),
        name="final",
    )(x1, y_blocks, mod, final_w)


def _group_sort_plan(group, n):
    onehot = (group[:, None] == jnp.arange(N_GROUPS, dtype=jnp.int32)[None, :]).astype(jnp.int32)
    blk = LANES
    within = jnp.einsum("ts,bsg->btg", jnp.tril(jnp.ones((blk, blk), F32)),
                        onehot.astype(F32).reshape(n // blk, blk, N_GROUPS))
    before = jnp.tril(jnp.ones((n // blk, n // blk), F32), -1) @ within[:, -1, :]
    incl = (within + before[:, None, :]).reshape(n, N_GROUPS).astype(jnp.int32)
    tiles = (incl[-1] + MOE_TILE - 1) // MOE_TILE
    tile_end = jnp.cumsum(tiles)
    start = (tile_end - tiles) * MOE_TILE
    rank = jnp.sum(incl * onehot, axis=1) - 1
    pos = jnp.sum(onehot * start[None, :], axis=1) + rank
    n_tiles = n // MOE_TILE + N_GROUPS
    tile_ids = jnp.arange(n_tiles, dtype=jnp.int32)
    tile_group = jnp.sum((tile_ids[:, None] >= tile_end[None, :]).astype(jnp.int32), axis=1)
    tile_group = jnp.minimum(tile_group, N_GROUPS - 1)
    pad = tiles * MOE_TILE - incl[-1]
    pad_end = jnp.cumsum(pad)
    k = jnp.arange(N_GROUPS * MOE_TILE, dtype=jnp.int32)
    seg = jnp.sum((k[:, None] >= pad_end[None, :]).astype(jnp.int32), axis=1)
    seg_hot = (seg[:, None] == jnp.arange(N_GROUPS, dtype=jnp.int32)[None, :]).astype(jnp.int32)
    in_group = jnp.sum(seg_hot * (start + incl[-1] - (pad_end - pad))[None, :], axis=1) + k
    tail = tile_end[-1] * MOE_TILE + k - pad_end[-1]
    free = jnp.where(seg < N_GROUPS, in_group, tail)
    return (pos.astype(jnp.int32), free.astype(jnp.int32), tile_group.astype(jnp.int32),
            tile_end[-1:].astype(jnp.int32))


def kernel(x, c, w_ada, b_ada, norm1_w, norm2_w, final_norm_w, w_in, sg_ln_w, sg_ln_b, w_spatial,
           b_spatial, rel_bias, w_out_sg, w_out_att, w_o, w_router_group, w_router_expert,
           w_exp_gate, w_exp_up, w_exp_down):
    batch, seq, d = x.shape
    assert d == D_MODEL and w_ada.shape[0] == 1
    assert seq % 1024 == 0 and seq // MOBA_BLOCK <= LANES
    n = batch * seq
    x2 = x.reshape(n, d)

    c_pad = jnp.zeros((8, d), F32).at[:batch].set(c)
    mod = _adaln(c_pad, w_ada[0], b_ada[0].reshape(1, 6 * d))
    mod = mod[:batch].reshape(batch, 6, d)
    mod = jnp.concatenate([mod, jnp.zeros((batch, 2, d), F32)], axis=1)

    colscale = jnp.ones((1, IN_COLS), F32).at[:, CB_Q * LANES:CB_K * LANES].set(HEAD_DIM ** -0.5 * LOG2E)
    proj = _proj(x2, mod, norm1_w[0].reshape(1, d), w_in[0].astype(BF16), colscale, seq)

    y_sg = _sgu(proj, sg_ln_w[0].reshape(1, D_SG), sg_ln_b[0].reshape(1, D_SG),
                w_spatial[0], b_spatial[0].T)

    qt, kx, vt, kmean = _moba_prep(proj, batch, seq)
    bias_tiles = _bias_tiles(rel_bias)
    y_att = _moba(rel_bias, qt, kx, vt, kmean, bias_tiles, batch, seq)

    w_router = jnp.zeros((d, LANES), F32)
    w_router = w_router.at[:, :N_GROUPS].set(w_router_group[0])
    w_router = w_router.at[:, ROUTER_COL0:ROUTER_COL0 + N_EXPERTS].set(w_router_expert[0])
    x1, tok, gid = _merge(x2, y_sg, y_att, proj, mod, norm2_w[0].reshape(1, d),
                          w_out_sg[0].astype(BF16), w_out_att[0].astype(BF16),
                          w_o[0].astype(BF16), w_router, seq)

    tok = tok.reshape(n, TOK_ROWS, LANES)
    pos, free, tile_group, n_active = _group_sort_plan(gid.reshape(n), n)
    inv = _invert(pos, free)
    src_tok = jnp.where(inv < n, inv, 0)
    by_group = lambda w: w[0].astype(BF16).reshape((N_GROUPS, EXPERTS_PER_GROUP) + w.shape[2:])
    y = _moe(src_tok, inv, tile_group, n_active, tok, by_group(w_exp_gate), by_group(w_exp_up),
             by_group(w_exp_down))
    out = _final(x1, y.reshape(-1, LANES), mod, final_norm_w.reshape(1, d), seq)
    return out.reshape(batch, seq, d)
```

```python
import functools
import math

import jax
import jax.numpy as jnp
from jax import lax
from jax.experimental import pallas as pl
from jax.experimental.pallas import tpu as pltpu

F32 = jnp.float32
BF16 = jnp.bfloat16

LANES = 128
D_MODEL = 2048
D_SG = D_MODEL // 2
SG_GROUPS = 8
SG_CHUNK = 128
ATT_HEADS = 8
HEAD_DIM = 128
D_ATT = ATT_HEADS * HEAD_DIM
MOBA_BLOCK = 256
MOBA_TOPK = 3
REL_BUCKETS = 32
REL_MAX_DIST = 128
N_GROUPS = 4
EXPERTS_PER_GROUP = 4
N_EXPERTS = N_GROUPS * EXPERTS_PER_GROUP
D_EXPERT = 512
EPS = 1e-6
IN_COLS = 2 * D_SG + 3 * D_ATT + 2 * D_MODEL

N_GATE_COLS = 2 * D_MODEL
BLK_GATE_SG = 0
BLK_GATE_ATT = 1
BLK_U = N_GATE_COLS // D_SG
BLK_V = BLK_U + 1
CB_Q = (N_GATE_COLS + 2 * D_SG) // LANES
CB_K = CB_Q + ATT_HEADS
CB_VAL = CB_K + ATT_HEADS

MASK_NEG = -1e9
LOG2E = math.log2(math.e)
ROUTER_COL0 = N_GROUPS
CLASS_LANE = 0
PAIRS_PER_GROUP = EXPERTS_PER_GROUP * (EXPERTS_PER_GROUP - 1) // 2
N_CLASSES = N_GROUPS * PAIRS_PER_GROUP
PAIRS = [(a, b) for a in range(EXPERTS_PER_GROUP) for b in range(a + 1, EXPERTS_PER_GROUP)]
D_ROWS = D_MODEL // LANES
TOK_ROWS = D_ROWS + 8
VMEM_LIMIT = 56 * 1024 * 1024


def _cparams(sem):
    return pltpu.CompilerParams(dimension_semantics=sem, vmem_limit_bytes=VMEM_LIMIT)


def _nt_dot(a, b):
    return lax.dot_general(a, b, (((1,), (1,)), ((), ())), preferred_element_type=F32)


def _split_bf16(a):
    hi = a.astype(BF16)
    lo = (a - hi.astype(F32)).astype(BF16)
    return hi, lo


def _adaln_kernel(c_ref, w_ref, b_ref, o_ref):
    c = c_ref[...]
    ca = c * jax.nn.sigmoid(c)
    o_ref[...] = jnp.dot(ca, w_ref[...], preferred_element_type=F32,
                         precision=lax.Precision.HIGHEST) + b_ref[...]


def _adaln(c_pad, w, b):
    rows, d = c_pad.shape
    cols = w.shape[1]
    tn = 1024
    return pl.pallas_call(
        _adaln_kernel,
        grid=(cols // tn,),
        in_specs=[pl.BlockSpec((rows, d), lambda j: (0, 0)),
                  pl.BlockSpec((d, tn), lambda j: (0, j)),
                  pl.BlockSpec((1, tn), lambda j: (0, j))],
        out_specs=pl.BlockSpec((rows, tn), lambda j: (0, j)),
        out_shape=jax.ShapeDtypeStruct((rows, cols), F32),
        compiler_params=_cparams(("arbitrary",)),
        name="adaln",
    )(c_pad, w, b)


def _proj_kernel(x_ref, mod_ref, nw_ref, w_ref, cs_ref, o_ref, h_scr):
    @pl.when(pl.program_id(1) == 0)
    def _():
        x = x_ref[...]
        ms = jnp.mean(x * x, axis=-1, keepdims=True)
        y = x * lax.rsqrt(ms + EPS) * nw_ref[...]
        h = y * (1.0 + mod_ref[0, 1:2, :]) + mod_ref[0, 0:1, :]
        h_scr[...] = h.astype(BF16)

    acc = jnp.dot(h_scr[...], w_ref[...], preferred_element_type=F32)
    o_ref[...] = (acc * cs_ref[...]).astype(BF16)


def _proj(x2, mod, norm_w, w_in, colscale, seq):
    n, d = x2.shape
    cols = w_in.shape[1]
    tm = min(1024, seq)
    tn = 1024
    tpb = seq // tm
    n_cb = cols // tn
    first = (cols - N_GATE_COLS) // tn
    assert (cols - N_GATE_COLS) % tn == 0
    return pl.pallas_call(
        _proj_kernel,
        grid=(n // tm, n_cb),
        in_specs=[pl.BlockSpec((tm, d), lambda i, j: (i, 0)),
                  pl.BlockSpec((1, 8, d), lambda i, j: (i // tpb, 0, 0)),
                  pl.BlockSpec((1, d), lambda i, j: (0, 0)),
                  pl.BlockSpec((d, tn), lambda i, j: (0, (j + first) % n_cb)),
                  pl.BlockSpec((1, tn), lambda i, j: (0, j))],
        out_specs=pl.BlockSpec((tm, tn), lambda i, j: (i, j)),
        out_shape=jax.ShapeDtypeStruct((n, cols), BF16),
        scratch_shapes=[pltpu.VMEM((tm, d), BF16)],
        compiler_params=_cparams(("parallel", "arbitrary")),
        name="proj",
    )(x2, mod, norm_w, w_in, colscale)


def _gelu(a):
    return 0.5 * a * (1.0 + lax.erf(a * (1.0 / math.sqrt(2.0))))


def _sgu_kernel(u_ref, v_ref, lnw_ref, lnb_ref, ws_ref, bst_ref, o_ref, *, chunks):
    u = _gelu(u_ref[...].astype(F32))
    v = _gelu(v_ref[...].astype(F32))
    mu = jnp.mean(v, axis=-1, keepdims=True)
    vc = v - mu
    var = jnp.mean(vc * vc, axis=-1, keepdims=True)
    vn = (vc * lax.rsqrt(var + EPS) * lnw_ref[...] + lnb_ref[...]).astype(BF16)
    row = lax.broadcasted_iota(jnp.int32, (SG_CHUNK, SG_CHUNK), 0)
    col = lax.broadcasted_iota(jnp.int32, (SG_CHUNK, SG_CHUNK), 1)
    causal = col <= row
    for g in range(SG_GROUPS):
        wm = jnp.where(causal, ws_ref[g], 0.0).astype(BF16)
        bcol = bst_ref[:, g:g + 1]
        gs = slice(g * LANES, (g + 1) * LANES)
        for c in range(chunks):
            rs = slice(c * SG_CHUNK, (c + 1) * SG_CHUNK)
            z = jnp.dot(wm, vn[rs, gs], preferred_element_type=F32) + bcol
            o_ref[rs, gs] = (u[rs, gs] * z).astype(BF16)


def _sgu(proj, ln_w, ln_b, w_s, b_s_t):
    n = proj.shape[0]
    chunks = 4
    tm = chunks * SG_CHUNK
    return pl.pallas_call(
        functools.partial(_sgu_kernel, chunks=chunks),
        grid=(n // tm,),
        in_specs=[pl.BlockSpec((tm, D_SG), lambda i: (i, BLK_U)),
                  pl.BlockSpec((tm, D_SG), lambda i: (i, BLK_V)),
                  pl.BlockSpec((1, D_SG), lambda i: (0, 0)),
                  pl.BlockSpec((1, D_SG), lambda i: (0, 0)),
                  pl.BlockSpec((SG_GROUPS, SG_CHUNK, SG_CHUNK), lambda i: (0, 0, 0)),
                  pl.BlockSpec((SG_CHUNK, SG_GROUPS), lambda i: (0, 0))],
        out_specs=pl.BlockSpec((tm, D_SG), lambda i: (i, 0)),
        out_shape=jax.ShapeDtypeStruct((n, D_SG), BF16),
        compiler_params=_cparams(("parallel",)),
        name="sgu",
    )(proj, proj, ln_w, ln_b, w_s, b_s_t)


VT_ROWS = HEAD_DIM + 16
EXT_LO = REL_BUCKETS


def _moba_prep_kernel(q_ref, k_ref, v_ref, qt_ref, kx_ref, vt_ref, km_ref, *, nb):
    lane = lax.broadcasted_iota(jnp.int32, (MOBA_BLOCK, LANES), 1)
    pad = jnp.concatenate([jnp.ones((1, MOBA_BLOCK), F32),
                           jnp.zeros((VT_ROWS - HEAD_DIM - 1, MOBA_BLOCK), F32)], axis=0)
    for j in range(nb):
        rows = slice(j * MOBA_BLOCK, (j + 1) * MOBA_BLOCK)
        kj = k_ref[rows, :]
        onehot = jnp.where((lane == j) | (lane == j + EXT_LO), 1.0, 0.0).astype(BF16)
        kx_ref[rows, :] = jnp.concatenate([kj, onehot], axis=1)
        km_ref[j:j + 1, :] = jnp.mean(kj.astype(F32), axis=0, keepdims=True)
        vt = v_ref[rows, :].astype(F32).T
        vt_ref[j] = jnp.concatenate([vt, pad], axis=0).astype(BF16)
        qt_ref[j] = q_ref[rows, :].astype(F32).T.astype(BF16)


def _moba_prep(proj, batch, seq):
    nb = seq // MOBA_BLOCK
    bh = lambda shape: pl.BlockSpec((None, None) + shape, lambda b, h: (b, h) + (0,) * len(shape))
    return pl.pallas_call(
        functools.partial(_moba_prep_kernel, nb=nb),
        grid=(batch, ATT_HEADS),
        in_specs=[pl.BlockSpec((seq, HEAD_DIM), lambda b, h: (b, CB_Q + h)),
                  pl.BlockSpec((seq, HEAD_DIM), lambda b, h: (b, CB_K + h)),
                  pl.BlockSpec((seq, HEAD_DIM), lambda b, h: (b, CB_VAL + h))],
        out_specs=[bh((nb, HEAD_DIM, MOBA_BLOCK)), bh((seq, 2 * HEAD_DIM)),
                   bh((nb, VT_ROWS, MOBA_BLOCK)), bh((nb, HEAD_DIM))],
        out_shape=[jax.ShapeDtypeStruct((batch, ATT_HEADS, nb, HEAD_DIM, MOBA_BLOCK), BF16),
                   jax.ShapeDtypeStruct((batch, ATT_HEADS, seq, 2 * HEAD_DIM), BF16),
                   jax.ShapeDtypeStruct((batch, ATT_HEADS, nb, VT_ROWS, MOBA_BLOCK), BF16),
                   jax.ShapeDtypeStruct((batch, ATT_HEADS, nb, HEAD_DIM), F32)],
        compiler_params=_cparams(("parallel", "parallel")),
        name="moba_prep",
    )(proj, proj, proj)


def _bias_tiles_kernel(rb_ref, o_ref):
    h = pl.program_id(0)
    kj = lax.broadcasted_iota(jnp.int32, (MOBA_BLOCK, MOBA_BLOCK), 0)
    qi = lax.broadcasted_iota(jnp.int32, (MOBA_BLOCK, MOBA_BLOCK), 1)
    max_exact = REL_BUCKETS // 2
    for t in range(2):
        rel = qi - kj + MOBA_BLOCK * t
        n = jnp.maximum(rel, 0)
        nf = jnp.maximum(n, max_exact).astype(F32)
        large = max_exact + (jnp.log(nf / max_exact) / math.log(REL_MAX_DIST / max_exact)
                             * (REL_BUCKETS - max_exact)).astype(jnp.int32)
        large = jnp.minimum(large, REL_BUCKETS - 1)
        bucket = jnp.where(n < max_exact, n, large)
        bias = jnp.zeros((MOBA_BLOCK, MOBA_BLOCK), F32)
        for r in range(REL_BUCKETS):
            bias = jnp.where(bucket == r, rb_ref[r, h], bias)
        o_ref[t] = jnp.where(rel >= 0, bias * LOG2E, MASK_NEG)


def _bias_tiles(rel_bias):
    return pl.pallas_call(
        _bias_tiles_kernel,
        grid=(ATT_HEADS,),
        in_specs=[pl.BlockSpec(memory_space=pltpu.SMEM)],
        out_specs=pl.BlockSpec((None, 2, MOBA_BLOCK, MOBA_BLOCK), lambda h: (h, 0, 0, 0)),
        out_shape=jax.ShapeDtypeStruct((ATT_HEADS, 2, MOBA_BLOCK, MOBA_BLOCK), F32),
        compiler_params=_cparams(("arbitrary",)),
        name="bias_tiles",
    )(rel_bias)


FAR_BLOCKS = 4
HEADS_PER_STEP = 4


def _moba_kernel(rb_ref, qt_ref, kx_ref, vt_ref, km_ref, bias_ref, o_ref, *, nb):
    i = pl.program_id(2)
    jp = jnp.maximum(i - 1, 0)
    prev0 = pl.multiple_of(jp * MOBA_BLOCK, MOBA_BLOCK)
    own0 = pl.multiple_of(i * MOBA_BLOCK, MOBA_BLOCK)
    no_prev = jnp.where(i > 0, 0.0, MASK_NEG)
    neg_inf = jnp.float32(-jnp.inf)

    heads = range(HEADS_PER_STEP)

    def choose_blocks(hh):
        km_hi, km_lo = _split_bf16(km_ref[hh])
        score = (jnp.dot(km_hi, qt_ref[hh], preferred_element_type=F32)
                 + jnp.dot(km_lo, qt_ref[hh], preferred_element_type=F32))
        bid = lax.broadcasted_iota(jnp.int32, score.shape, 0)
        score = jnp.where(bid < i, score, neg_inf)
        chosen = bid < 0
        for _ in range(MOBA_TOPK):
            mx = jnp.max(score, axis=0, keepdims=True)
            hit = (score == mx) & (mx > neg_inf)
            idx = jnp.min(jnp.where(hit, bid, nb), axis=0, keepdims=True)
            pick = bid == idx
            chosen = chosen | pick
            score = jnp.where(pick, neg_inf, score)
        return chosen

    def with_mask(hh, val):
        hi, lo = _split_bf16(val)
        fill = jnp.zeros((HEAD_DIM - 2 * EXT_LO, MOBA_BLOCK), BF16)
        if nb < EXT_LO:
            gap = jnp.zeros((EXT_LO - nb, MOBA_BLOCK), BF16)
            return jnp.concatenate([qt_ref[hh], hi, gap, lo, gap, fill], axis=0)
        return jnp.concatenate([qt_ref[hh], hi, lo, fill], axis=0)

    chosen = [choose_blocks(hh) for hh in heads]
    bid = lax.broadcasted_iota(jnp.int32, chosen[0].shape, 0)
    far_bias = [rb_ref[REL_BUCKETS - 1, pl.program_id(1) * HEADS_PER_STEP + hh] * LOG2E for hh in heads]
    qx_fars = [with_mask(hh, jnp.where(chosen[hh] & (bid <= i - 2), far_bias[hh], MASK_NEG)) for hh in heads]
    qx_nears = [with_mask(hh, jnp.where((bid == i) | (chosen[hh] & (bid == i - 1)), 0.0, MASK_NEG))
                for hh in heads]

    ss = [jnp.dot(jnp.concatenate([kx_ref[hh, pl.ds(prev0, MOBA_BLOCK), :],
                                   kx_ref[hh, pl.ds(own0, MOBA_BLOCK), :]], axis=0),
                  qx_nears[hh], preferred_element_type=F32)
          + jnp.concatenate([bias_ref[hh, 1] + no_prev, bias_ref[hh, 0]], axis=0) for hh in heads]
    ms = [jnp.max(ss[hh], axis=0, keepdims=True) for hh in heads]
    ps = [jnp.exp2(ss[hh] - ms[hh]).astype(BF16) for hh in heads]
    accs = [jnp.dot(jnp.concatenate([vt_ref[hh, jp], vt_ref[hh, i]], axis=1), ps[hh],
                    preferred_element_type=F32) for hh in heads]

    def far_step(c, carry):
        row0 = pl.multiple_of(c * (FAR_BLOCKS * MOBA_BLOCK), FAR_BLOCKS * MOBA_BLOCK)
        ss = [jnp.dot(kx_ref[hh, pl.ds(row0, FAR_BLOCKS * MOBA_BLOCK), :], qx_fars[hh],
                      preferred_element_type=F32) for hh in heads]
        m_news = [jnp.maximum(carry[hh][0], jnp.max(ss[hh], axis=0, keepdims=True)) for hh in heads]
        ps = [jnp.exp2(ss[hh] - m_news[hh]).astype(BF16) for hh in heads]
        out = []
        for hh in heads:
            m, acc = carry[hh]
            alpha = jnp.exp2(m - m_news[hh])
            vt = jnp.concatenate([vt_ref[hh, c * FAR_BLOCKS + k] for k in range(FAR_BLOCKS)], axis=1)
            out.append((m_news[hh], alpha * acc + jnp.dot(vt, ps[hh], preferred_element_type=F32)))
        return tuple(out)

    n_far = lax.shift_right_logical(jnp.maximum(i - 1, 0) + FAR_BLOCKS - 1, 2)
    final = lax.fori_loop(0, n_far, far_step, tuple((ms[hh], accs[hh]) for hh in heads))
    for hh in heads:
        acc = final[hh][1]
        out = acc[0:HEAD_DIM] / acc[HEAD_DIM:HEAD_DIM + 1]
        o_ref[:, hh * HEAD_DIM:(hh + 1) * HEAD_DIM] = out.T.astype(BF16)


def _moba(rel_bias, qt, kx, vt, kmean, bias_tiles, batch, seq):
    nb = seq // MOBA_BLOCK
    hps = HEADS_PER_STEP
    assert nb % FAR_BLOCKS == 0 and FAR_BLOCKS == 4 and ATT_HEADS % hps == 0
    bh = lambda shape: pl.BlockSpec((None, hps) + shape, lambda b, h, i: (b, h) + (0,) * len(shape),
                                    pipeline_mode=pl.Buffered(1))
    return pl.pallas_call(
        functools.partial(_moba_kernel, nb=nb),
        grid=(batch, ATT_HEADS // hps, nb),
        in_specs=[pl.BlockSpec(memory_space=pltpu.SMEM),
                  pl.BlockSpec((None, hps, None, HEAD_DIM, MOBA_BLOCK), lambda b, h, i: (b, h, i, 0, 0)),
                  bh((seq, 2 * HEAD_DIM)), bh((nb, VT_ROWS, MOBA_BLOCK)), bh((nb, HEAD_DIM)),
                  pl.BlockSpec((hps, 2, MOBA_BLOCK, MOBA_BLOCK), lambda b, h, i: (h, 0, 0, 0))],
        out_specs=pl.BlockSpec((MOBA_BLOCK, hps * HEAD_DIM), lambda b, h, i: (b * nb + i, h)),
        out_shape=jax.ShapeDtypeStruct((batch * seq, D_ATT), BF16),
        compiler_params=_cparams(("parallel", "parallel", "arbitrary")),
        name="moba",
    )(rel_bias, qt, kx, vt, kmean, bias_tiles)


def _merge_kernel(x_ref, ysg_ref, yatt_ref, gsg_ref, gatt_ref, mod_ref, nw_ref,
                  wsg_ref, watt_ref, wo_ref, wr_ref, x1_ref, tok_ref, cls_ref):
    a_sg = jnp.dot(ysg_ref[...], wsg_ref[...], preferred_element_type=F32)
    a_att = jnp.dot(yatt_ref[...], watt_ref[...], preferred_element_type=F32)
    merged = (jax.nn.sigmoid(gsg_ref[...].astype(F32)) * a_sg
              + jax.nn.sigmoid(gatt_ref[...].astype(F32)) * a_att)
    mixed = jnp.dot(merged.astype(BF16), wo_ref[...], preferred_element_type=F32)
    x1 = x_ref[...] + mod_ref[0, 2:3, :] * mixed
    x1_ref[...] = x1
    ms = jnp.mean(x1 * x1, axis=-1, keepdims=True)
    y = x1 * lax.rsqrt(ms + EPS) * nw_ref[...]
    h2 = y * (1.0 + mod_ref[0, 4:5, :]) + mod_ref[0, 3:4, :]
    tm = h2.shape[0]
    for a in range(D_ROWS):
        tok_ref[pl.ds(a, tm, stride=TOK_ROWS), :] = h2[:, a * LANES:(a + 1) * LANES]
    for a in range(D_ROWS + 1, TOK_ROWS):
        tok_ref[pl.ds(a, tm, stride=TOK_ROWS), :] = jnp.zeros((tm, LANES), F32)

    h_hi, h_lo = _split_bf16(h2)
    w_hi, w_lo = _split_bf16(wr_ref[...])
    logits = (jnp.dot(h_hi, w_hi, preferred_element_type=F32)
              + jnp.dot(h_lo, w_hi, preferred_element_type=F32)
              + jnp.dot(h_hi, w_lo, preferred_element_type=F32))

    lane = lax.broadcasted_iota(jnp.int32, logits.shape, 1)
    neg_inf = jnp.float32(-jnp.inf)
    in_g = lane < N_GROUPS
    lg = jnp.where(in_g, logits, neg_inf)
    mg = jnp.max(lg, axis=1, keepdims=True)
    eg = jnp.exp(lg - mg)
    g_prob = eg / jnp.sum(eg, axis=1, keepdims=True)
    g_p = jnp.max(g_prob, axis=1, keepdims=True)
    g_idx = jnp.min(jnp.where(g_prob == g_p, lane, LANES), axis=1, keepdims=True)
    lo_col = ROUTER_COL0 + EXPERTS_PER_GROUP * g_idx
    in_e = (lane >= lo_col) & (lane < lo_col + EXPERTS_PER_GROUP)
    le = jnp.where(in_e, logits, neg_inf)
    me = jnp.max(le, axis=1, keepdims=True)
    ee = jnp.exp(le - me)
    e_prob = jnp.where(in_e, ee / jnp.sum(ee, axis=1, keepdims=True), -1.0)
    p1 = jnp.max(e_prob, axis=1, keepdims=True)
    i1 = jnp.min(jnp.where(e_prob == p1, lane, LANES), axis=1, keepdims=True)
    rest = jnp.where(lane == i1, -1.0, e_prob)
    p2 = jnp.max(rest, axis=1, keepdims=True)
    i2 = jnp.min(jnp.where(rest == p2, lane, LANES), axis=1, keepdims=True)
    denom = p1 + p2
    ea = jnp.minimum(i1, i2) - lo_col
    eb = jnp.maximum(i1, i2) - lo_col
    pair = lax.shift_right_logical(ea * (2 * EXPERTS_PER_GROUP - 1 - ea), 1) + eb - ea - 1
    cls = g_idx * PAIRS_PER_GROUP + pair
    routing = (jnp.where(lane == i1, g_p * (p1 / denom), 0.0)
               + jnp.where(lane == i2, g_p * (p2 / denom), 0.0)
               + jnp.where(lane == CLASS_LANE, cls.astype(F32), 0.0))
    tok_ref[pl.ds(D_ROWS, tm, stride=TOK_ROWS), :] = routing
    cls_ref[...] = routing.T[CLASS_LANE:CLASS_LANE + 1, :].astype(jnp.int32)


def _merge(x2, y_sg, y_att, proj, mod, norm2_w, w_sg, w_att, w_o, w_router, seq):
    n, d = x2.shape
    tm = 256
    tpb = seq // tm
    resident = lambda shape: pl.BlockSpec(shape, lambda i: (0, 0), pipeline_mode=pl.Buffered(1))
    return pl.pallas_call(
        _merge_kernel,
        grid=(n // tm,),
        in_specs=[pl.BlockSpec((tm, d), lambda i: (i, 0)),
                  pl.BlockSpec((tm, D_SG), lambda i: (i, 0)),
                  pl.BlockSpec((tm, D_ATT), lambda i: (i, 0)),
                  pl.BlockSpec((tm, d), lambda i: (i, BLK_GATE_SG)),
                  pl.BlockSpec((tm, d), lambda i: (i, BLK_GATE_ATT)),
                  pl.BlockSpec((1, 8, d), lambda i: (i // tpb, 0, 0)),
                  pl.BlockSpec((1, d), lambda i: (0, 0)),
                  resident((D_SG, d)), resident((D_ATT, d)), resident((d, d)),
                  resident((d, LANES))],
        out_specs=[pl.BlockSpec((tm, d), lambda i: (i, 0)),
                   pl.BlockSpec((tm * TOK_ROWS, LANES), lambda i: (i, 0)),
                   pl.BlockSpec((None, 1, tm), lambda i: (i, 0, 0))],
        out_shape=[jax.ShapeDtypeStruct((n, d), F32),
                   jax.ShapeDtypeStruct((n * TOK_ROWS, LANES), F32),
                   jax.ShapeDtypeStruct((n // tm, 1, tm), jnp.int32)],
        compiler_params=_cparams(("parallel",)),
        name="merge",
    )(x2, y_sg, y_att, proj, proj, mod, norm2_w, w_sg, w_att, w_o, w_router)


def _invert_kernel(pos_ref, free_ref, inv_ref, *, n_tok, n_free):
    def place(t, carry):
        inv_ref[pos_ref[t]] = t
        return carry

    def place_free(k, carry):
        inv_ref[free_ref[k]] = n_tok + k
        return carry

    lax.fori_loop(0, n_tok, place, 0, unroll=8)
    lax.fori_loop(0, n_free, place_free, 0, unroll=8)


def _invert(pos, free):
    n_tok, n_free = pos.shape[0], free.shape[0]
    return pl.pallas_call(
        functools.partial(_invert_kernel, n_tok=n_tok, n_free=n_free),
        grid_spec=pltpu.PrefetchScalarGridSpec(
            num_scalar_prefetch=2, grid=(1,), in_specs=[],
            out_specs=pl.BlockSpec(memory_space=pltpu.SMEM)),
        out_shape=jax.ShapeDtypeStruct((n_tok + n_free,), jnp.int32),
        compiler_params=_cparams(("arbitrary",)),
        name="invert",
    )(pos, free)


MOE_TILE = 256
ROW_UNROLL_LOG2 = 3
ROW_UNROLL = 1 << ROW_UNROLL_LOG2


def _for_rows(rows, fn):
    groups = lax.shift_right_logical(rows, ROW_UNROLL_LOG2)

    def group(g, carry):
        for u in range(ROW_UNROLL):
            fn(g * ROW_UNROLL + u)
        return carry

    def single(r, carry):
        fn(r)
        return carry

    lax.fori_loop(0, groups, group, 0)
    lax.fori_loop(groups * ROW_UNROLL, rows, single, 0)


def _moe_kernel(src_ref, dst_ref, ea_ref, eb_ref, rows_ref, na_ref, tok_hbm,
                w1a_ref, w3a_ref, w2a_ref, w1b_ref, w3b_ref, w2b_ref, y_hbm,
                gbuf, stage, acc_ref, gsem, ssem):
    i = pl.program_id(0)
    n_act = na_ref[0]
    slot = lax.rem(i, 2)

    def gather_start(tile, sl):
        def one(r, carry):
            row0 = pl.multiple_of(r * TOK_ROWS, 8)
            pltpu.make_async_copy(tok_hbm.at[src_ref[tile * MOE_TILE + r]],
                                  gbuf.at[sl, pl.ds(row0, TOK_ROWS), :], gsem.at[sl]).start()
            return carry
        lax.fori_loop(0, MOE_TILE, one, 0, unroll=8)

    def gather_wait(sl):
        for _ in range(MOE_TILE):
            pltpu.make_async_copy(tok_hbm.at[0], gbuf.at[sl, pl.ds(0, TOK_ROWS), :], gsem.at[sl]).wait()

    def scatter_start(tile, sl):
        def one(r):
            row0 = pl.multiple_of(r * D_ROWS, 8)
            pltpu.make_async_copy(stage.at[sl, pl.ds(row0, D_ROWS), :],
                                  y_hbm.at[dst_ref[tile * MOE_TILE + r]], ssem.at[sl]).start()
        _for_rows(rows_ref[tile], one)

    def scatter_wait(tile, sl):
        def one(r):
            pltpu.make_async_copy(stage.at[sl, pl.ds(0, D_ROWS), :], y_hbm.at[0], ssem.at[sl]).wait()
        _for_rows(rows_ref[tile], one)

    @pl.when(i < n_act)
    def _():
        @pl.when(i == 0)
        def _():
            gather_start(0, 0)

        @pl.when(i + 1 < n_act)
        def _():
            gather_start(i + 1, 1 - slot)

        gather_wait(slot)
        t = jnp.concatenate([gbuf[slot, pl.ds(a, MOE_TILE, stride=TOK_ROWS), :] for a in range(D_ROWS)],
                            axis=1).astype(BF16)
        routing = gbuf[slot, pl.ds(D_ROWS, MOE_TILE, stride=TOK_ROWS), :]
        lane = lax.broadcasted_iota(jnp.int32, routing.shape, 1)
        experts = ((ea_ref[i], w1a_ref, w3a_ref, w2a_ref), (eb_ref[i], w1b_ref, w3b_ref, w2b_ref))
        for k, (e, w1_ref, w3_ref, w2_ref) in enumerate(experts):
            a = jnp.dot(t, w1_ref[...], preferred_element_type=F32)
            b = jnp.dot(t, w3_ref[...], preferred_element_type=F32)
            ge = jnp.sum(jnp.where(lane == ROUTER_COL0 + e, routing, 0.0), axis=1, keepdims=True)
            hmid = ((a * jax.nn.sigmoid(a)) * b * ge).astype(BF16)
            y = jnp.dot(hmid, w2_ref[...], preferred_element_type=F32)
            if k == 0:
                acc_ref[...] = y
            else:
                acc_ref[...] += y

        @pl.when(i >= 2)
        def _():
            scatter_wait(i - 2, slot)

        for a in range(D_ROWS):
            stage[slot, pl.ds(a, MOE_TILE, stride=D_ROWS), :] = acc_ref[:, a * LANES:(a + 1) * LANES]
        scatter_start(i, slot)

        @pl.when(i == n_act - 1)
        def _():
            scatter_wait(i, slot)

            @pl.when(i >= 1)
            def _():
                scatter_wait(i - 1, 1 - slot)


def _moe(src_tok, dst_row, tile_ea, tile_eb, tile_rows, n_active, tok_blocks, w1, w3, w2, n_tok):
    n_slots = src_tok.shape[0]
    d = D_MODEL
    blk_a = lambda i, src, dst, ea, eb, rows, na: (ea[i], 0, 0)
    blk_b = lambda i, src, dst, ea, eb, rows, na: (eb[i], 0, 0)
    up = lambda blk: pl.BlockSpec((None, d, D_EXPERT), blk)
    down = lambda blk: pl.BlockSpec((None, D_EXPERT, d), blk)
    return pl.pallas_call(
        _moe_kernel,
        grid_spec=pltpu.PrefetchScalarGridSpec(
            num_scalar_prefetch=6, grid=(n_slots // MOE_TILE,),
            in_specs=[pl.BlockSpec(memory_space=pl.ANY),
                      up(blk_a), up(blk_a), down(blk_a), up(blk_b), up(blk_b), down(blk_b)],
            out_specs=pl.BlockSpec(memory_space=pl.ANY),
            scratch_shapes=[pltpu.VMEM((2, MOE_TILE * TOK_ROWS, LANES), F32),
                            pltpu.VMEM((2, MOE_TILE * D_ROWS, LANES), F32),
                            pltpu.VMEM((MOE_TILE, d), F32),
                            pltpu.SemaphoreType.DMA((2,)),
                            pltpu.SemaphoreType.DMA((2,))]),
        out_shape=jax.ShapeDtypeStruct((n_tok, D_ROWS, LANES), F32),
        compiler_params=_cparams(("arbitrary",)),
        name="moe",
    )(src_tok, dst_row, tile_ea, tile_eb, tile_rows, n_active, tok_blocks, w1, w3, w2, w1, w3, w2)


def _final_kernel(x1_ref, y_ref, mod_ref, fw_ref, o_ref):
    tm = x1_ref.shape[0]
    parts = []
    ss = jnp.zeros((tm, 1), F32)
    for a in range(D_ROWS):
        cols = slice(a * LANES, (a + 1) * LANES)
        part = x1_ref[:, cols] + mod_ref[0, 5:6, cols] * y_ref[pl.ds(a, tm, stride=D_ROWS), :]
        ss = ss + jnp.sum(part * part, axis=1, keepdims=True)
        parts.append(part)
    inv = lax.rsqrt(ss * (1.0 / D_MODEL) + EPS)
    for a in range(D_ROWS):
        cols = slice(a * LANES, (a + 1) * LANES)
        o_ref[:, cols] = parts[a] * inv * fw_ref[:, cols]


def _final(x1, y_blocks, mod, final_w, seq):
    n, d = x1.shape
    tm = 256
    tpb = seq // tm
    return pl.pallas_call(
        _final_kernel,
        grid=(n // tm,),
        in_specs=[pl.BlockSpec((tm, d), lambda i: (i, 0)),
                  pl.BlockSpec((tm * D_ROWS, LANES), lambda i: (i, 0)),
                  pl.BlockSpec((1, 8, d), lambda i: (i // tpb, 0, 0)),
                  pl.BlockSpec((1, d), lambda i: (0, 0))],
        out_specs=pl.BlockSpec((tm, d), lambda i: (i, 0)),
        out_shape=jax.ShapeDtypeStruct((n, d), F32),
        compiler_params=_cparams(("parallel",)),
        name="final",
    )(x1, y_blocks, mod, final_w)


def _class_sort_plan(cls, n):
    class_ids = jnp.arange(N_CLASSES, dtype=jnp.int32)
    onehot = (cls[:, None] == class_ids[None, :]).astype(jnp.int32)
    blk = LANES
    within = jnp.einsum("ts,bsg->btg", jnp.tril(jnp.ones((blk, blk), F32)),
                        onehot.astype(F32).reshape(n // blk, blk, N_CLASSES))
    before = jnp.tril(jnp.ones((n // blk, n // blk), F32), -1) @ within[:, -1, :]
    incl = (within + before[:, None, :]).reshape(n, N_CLASSES).astype(jnp.int32)
    count = incl[-1]
    tiles = (count + MOE_TILE - 1) // MOE_TILE
    tile_end = jnp.cumsum(tiles)
    start = (tile_end - tiles) * MOE_TILE
    rank = jnp.sum(incl * onehot, axis=1) - 1
    pos = jnp.sum(onehot * start[None, :], axis=1) + rank
    n_tiles = n // MOE_TILE + N_CLASSES
    tile_ids = jnp.arange(n_tiles, dtype=jnp.int32)
    tile_class = jnp.sum((tile_ids[:, None] >= tile_end[None, :]).astype(jnp.int32), axis=1)
    tile_class = jnp.minimum(tile_class, N_CLASSES - 1)
    tile_hot = (tile_class[:, None] == class_ids[None, :]).astype(jnp.int32)
    done = (tile_ids - jnp.sum(tile_hot * (tile_end - tiles)[None, :], axis=1)) * MOE_TILE
    tile_rows = jnp.clip(jnp.sum(tile_hot * count[None, :], axis=1) - done, 0, MOE_TILE)
    first = jnp.asarray([a for a, _ in PAIRS], jnp.int32)
    second = jnp.asarray([b for _, b in PAIRS], jnp.int32)
    group0 = (class_ids // PAIRS_PER_GROUP) * EXPERTS_PER_GROUP
    tile_ea = jnp.sum(tile_hot * (group0 + first[class_ids % PAIRS_PER_GROUP])[None, :], axis=1)
    tile_eb = jnp.sum(tile_hot * (group0 + second[class_ids % PAIRS_PER_GROUP])[None, :], axis=1)
    pad = tiles * MOE_TILE - count
    pad_end = jnp.cumsum(pad)
    k = jnp.arange(N_CLASSES * MOE_TILE, dtype=jnp.int32)
    seg = jnp.sum((k[:, None] >= pad_end[None, :]).astype(jnp.int32), axis=1)
    seg_hot = (seg[:, None] == class_ids[None, :]).astype(jnp.int32)
    in_class = jnp.sum(seg_hot * (start + count - (pad_end - pad))[None, :], axis=1) + k
    tail = tile_end[-1] * MOE_TILE + k - pad_end[-1]
    free = jnp.where(seg < N_CLASSES, in_class, tail)
    i32 = lambda a: a.astype(jnp.int32)
    return i32(pos), i32(free), i32(tile_ea), i32(tile_eb), i32(tile_rows), i32(tile_end[-1:])


def kernel(x, c, w_ada, b_ada, norm1_w, norm2_w, final_norm_w, w_in, sg_ln_w, sg_ln_b, w_spatial,
           b_spatial, rel_bias, w_out_sg, w_out_att, w_o, w_router_group, w_router_expert,
           w_exp_gate, w_exp_up, w_exp_down):
    batch, seq, d = x.shape
    assert d == D_MODEL and w_ada.shape[0] == 1
    assert seq % 1024 == 0 and seq // MOBA_BLOCK <= LANES
    n = batch * seq
    x2 = x.reshape(n, d)

    c_pad = jnp.zeros((8, d), F32).at[:batch].set(c)
    mod = _adaln(c_pad, w_ada[0], b_ada[0].reshape(1, 6 * d))
    mod = mod[:batch].reshape(batch, 6, d)
    mod = jnp.concatenate([mod, jnp.zeros((batch, 2, d), F32)], axis=1)

    colscale = jnp.ones((1, IN_COLS), F32).at[:, CB_Q * LANES:CB_K * LANES].set(HEAD_DIM ** -0.5 * LOG2E)
    proj = _proj(x2, mod, norm1_w[0].reshape(1, d), w_in[0].astype(BF16), colscale, seq)

    y_sg = _sgu(proj, sg_ln_w[0].reshape(1, D_SG), sg_ln_b[0].reshape(1, D_SG),
                w_spatial[0], b_spatial[0].T)

    qt, kx, vt, kmean = _moba_prep(proj, batch, seq)
    bias_tiles = _bias_tiles(rel_bias)
    y_att = _moba(rel_bias, qt, kx, vt, kmean, bias_tiles, batch, seq)

    w_router = jnp.zeros((d, LANES), F32)
    w_router = w_router.at[:, :N_GROUPS].set(w_router_group[0])
    w_router = w_router.at[:, ROUTER_COL0:ROUTER_COL0 + N_EXPERTS].set(w_router_expert[0])
    x1, tok, cls = _merge(x2, y_sg, y_att, proj, mod, norm2_w[0].reshape(1, d),
                          w_out_sg[0].astype(BF16), w_out_att[0].astype(BF16),
                          w_o[0].astype(BF16), w_router, seq)

    tok = tok.reshape(n, TOK_ROWS, LANES)
    pos, free, tile_ea, tile_eb, tile_rows, n_active = _class_sort_plan(cls.reshape(n), n)
    inv = _invert(pos, free)
    src_tok = jnp.where(inv < n, inv, 0)
    y = _moe(src_tok, inv, tile_ea, tile_eb, tile_rows, n_active, tok, w_exp_gate[0].astype(BF16),
             w_exp_up[0].astype(BF16), w_exp_down[0].astype(BF16), n)
    out = _final(x1, y.reshape(-1, LANES), mod, final_norm_w.reshape(1, d), seq)
    return out.reshape(batch, seq, d)
```

```python
import functools
import math

import jax
import jax.numpy as jnp
from jax import lax
from jax.experimental import pallas as pl
from jax.experimental.pallas import tpu as pltpu

F32 = jnp.float32
BF16 = jnp.bfloat16

LANES = 128
D_MODEL = 2048
D_SG = D_MODEL // 2
SG_GROUPS = 8
SG_CHUNK = 128
ATT_HEADS = 8
HEAD_DIM = 128
D_ATT = ATT_HEADS * HEAD_DIM
MOBA_BLOCK = 256
MOBA_TOPK = 3
REL_BUCKETS = 32
REL_MAX_DIST = 128
N_GROUPS = 4
EXPERTS_PER_GROUP = 4
N_EXPERTS = N_GROUPS * EXPERTS_PER_GROUP
D_EXPERT = 512
EPS = 1e-6
IN_COLS = 2 * D_SG + 3 * D_ATT + 2 * D_MODEL

N_GATE_COLS = 2 * D_MODEL
BLK_GATE_SG = 0
BLK_GATE_ATT = 1
BLK_U = N_GATE_COLS // D_SG
BLK_V = BLK_U + 1
CB_Q = (N_GATE_COLS + 2 * D_SG) // LANES
CB_K = CB_Q + ATT_HEADS
CB_VAL = CB_K + ATT_HEADS

MASK_NEG = -1e9
LOG2E = math.log2(math.e)
ROUTER_COL0 = N_GROUPS
CLASS_LANE = 0
PAIRS_PER_GROUP = EXPERTS_PER_GROUP * (EXPERTS_PER_GROUP - 1) // 2
N_CLASSES = N_GROUPS * PAIRS_PER_GROUP
PAIRS = [(a, b) for a in range(EXPERTS_PER_GROUP) for b in range(a + 1, EXPERTS_PER_GROUP)]
D_ROWS = D_MODEL // LANES
TOK_ROWS = D_ROWS + 8
VMEM_LIMIT = 56 * 1024 * 1024


def _cparams(sem):
    return pltpu.CompilerParams(dimension_semantics=sem, vmem_limit_bytes=VMEM_LIMIT)


def _nt_dot(a, b):
    return lax.dot_general(a, b, (((1,), (1,)), ((), ())), preferred_element_type=F32)


def _split_bf16(a):
    hi = a.astype(BF16)
    lo = (a - hi.astype(F32)).astype(BF16)
    return hi, lo


def _adaln_kernel(c_ref, w_ref, b_ref, o_ref):
    c = c_ref[...]
    ca = c * jax.nn.sigmoid(c)
    o_ref[...] = jnp.dot(ca, w_ref[...], preferred_element_type=F32,
                         precision=lax.Precision.HIGHEST) + b_ref[...]


def _adaln(c_pad, w, b):
    rows, d = c_pad.shape
    cols = w.shape[1]
    tn = 1024
    return pl.pallas_call(
        _adaln_kernel,
        grid=(cols // tn,),
        in_specs=[pl.BlockSpec((rows, d), lambda j: (0, 0)),
                  pl.BlockSpec((d, tn), lambda j: (0, j)),
                  pl.BlockSpec((1, tn), lambda j: (0, j))],
        out_specs=pl.BlockSpec((rows, tn), lambda j: (0, j)),
        out_shape=jax.ShapeDtypeStruct((rows, cols), F32),
        compiler_params=_cparams(("arbitrary",)),
        name="adaln",
    )(c_pad, w, b)


def _proj_kernel(x_ref, mod_ref, nw_ref, w_ref, cs_ref, o_ref, h_scr):
    @pl.when(pl.program_id(1) == 0)
    def _():
        x = x_ref[...]
        ms = jnp.mean(x * x, axis=-1, keepdims=True)
        y = x * lax.rsqrt(ms + EPS) * nw_ref[...]
        h = y * (1.0 + mod_ref[0, 1:2, :]) + mod_ref[0, 0:1, :]
        h_scr[...] = h.astype(BF16)

    acc = jnp.dot(h_scr[...], w_ref[...], preferred_element_type=F32)
    o_ref[...] = (acc * cs_ref[...]).astype(BF16)


def _proj(x2, mod, norm_w, w_in, colscale, seq):
    n, d = x2.shape
    cols = w_in.shape[1]
    tm = min(1024, seq)
    tn = 1024
    tpb = seq // tm
    n_cb = cols // tn
    first = (cols - N_GATE_COLS) // tn
    assert (cols - N_GATE_COLS) % tn == 0
    return pl.pallas_call(
        _proj_kernel,
        grid=(n // tm, n_cb),
        in_specs=[pl.BlockSpec((tm, d), lambda i, j: (i, 0)),
                  pl.BlockSpec((1, 8, d), lambda i, j: (i // tpb, 0, 0)),
                  pl.BlockSpec((1, d), lambda i, j: (0, 0)),
                  pl.BlockSpec((d, tn), lambda i, j: (0, (j + first) % n_cb)),
                  pl.BlockSpec((1, tn), lambda i, j: (0, j))],
        out_specs=pl.BlockSpec((tm, tn), lambda i, j: (i, j)),
        out_shape=jax.ShapeDtypeStruct((n, cols), BF16),
        scratch_shapes=[pltpu.VMEM((tm, d), BF16)],
        compiler_params=_cparams(("parallel", "arbitrary")),
        name="proj",
    )(x2, mod, norm_w, w_in, colscale)


def _gelu(a):
    return 0.5 * a * (1.0 + lax.erf(a * (1.0 / math.sqrt(2.0))))


def _sgu_kernel(u_ref, v_ref, lnw_ref, lnb_ref, ws_ref, bst_ref, o_ref, *, chunks):
    u = _gelu(u_ref[...].astype(F32))
    v = _gelu(v_ref[...].astype(F32))
    mu = jnp.mean(v, axis=-1, keepdims=True)
    vc = v - mu
    var = jnp.mean(vc * vc, axis=-1, keepdims=True)
    vn = (vc * lax.rsqrt(var + EPS) * lnw_ref[...] + lnb_ref[...]).astype(BF16)
    row = lax.broadcasted_iota(jnp.int32, (SG_CHUNK, SG_CHUNK), 0)
    col = lax.broadcasted_iota(jnp.int32, (SG_CHUNK, SG_CHUNK), 1)
    causal = col <= row
    for g in range(SG_GROUPS):
        wm = jnp.where(causal, ws_ref[g], 0.0).astype(BF16)
        bcol = bst_ref[:, g:g + 1]
        gs = slice(g * LANES, (g + 1) * LANES)
        for c in range(chunks):
            rs = slice(c * SG_CHUNK, (c + 1) * SG_CHUNK)
            z = jnp.dot(wm, vn[rs, gs], preferred_element_type=F32) + bcol
            o_ref[rs, gs] = (u[rs, gs] * z).astype(BF16)


def _sgu(proj, ln_w, ln_b, w_s, b_s_t):
    n = proj.shape[0]
    chunks = 4
    tm = chunks * SG_CHUNK
    return pl.pallas_call(
        functools.partial(_sgu_kernel, chunks=chunks),
        grid=(n // tm,),
        in_specs=[pl.BlockSpec((tm, D_SG), lambda i: (i, BLK_U)),
                  pl.BlockSpec((tm, D_SG), lambda i: (i, BLK_V)),
                  pl.BlockSpec((1, D_SG), lambda i: (0, 0)),
                  pl.BlockSpec((1, D_SG), lambda i: (0, 0)),
                  pl.BlockSpec((SG_GROUPS, SG_CHUNK, SG_CHUNK), lambda i: (0, 0, 0)),
                  pl.BlockSpec((SG_CHUNK, SG_GROUPS), lambda i: (0, 0))],
        out_specs=pl.BlockSpec((tm, D_SG), lambda i: (i, 0)),
        out_shape=jax.ShapeDtypeStruct((n, D_SG), BF16),
        compiler_params=_cparams(("parallel",)),
        name="sgu",
    )(proj, proj, ln_w, ln_b, w_s, b_s_t)


VT_ROWS = HEAD_DIM + 16
EXT_LO = REL_BUCKETS


def _moba_prep_kernel(q_ref, k_ref, v_ref, qt_ref, kx_ref, vt_ref, km_ref, *, nb):
    lane = lax.broadcasted_iota(jnp.int32, (MOBA_BLOCK, LANES), 1)
    pad = jnp.concatenate([jnp.ones((1, MOBA_BLOCK), F32),
                           jnp.zeros((VT_ROWS - HEAD_DIM - 1, MOBA_BLOCK), F32)], axis=0)
    for j in range(nb):
        rows = slice(j * MOBA_BLOCK, (j + 1) * MOBA_BLOCK)
        kj = k_ref[rows, :]
        onehot = jnp.where((lane == j) | (lane == j + EXT_LO), 1.0, 0.0).astype(BF16)
        kx_ref[rows, :] = jnp.concatenate([kj, onehot], axis=1)
        km_ref[j:j + 1, :] = jnp.mean(kj.astype(F32), axis=0, keepdims=True)
        vt = v_ref[rows, :].astype(F32).T
        vt_ref[j] = jnp.concatenate([vt, pad], axis=0).astype(BF16)
        qt_ref[j] = q_ref[rows, :].astype(F32).T.astype(BF16)


def _moba_prep(proj, batch, seq):
    nb = seq // MOBA_BLOCK
    bh = lambda shape: pl.BlockSpec((None, None) + shape, lambda b, h: (b, h) + (0,) * len(shape))
    return pl.pallas_call(
        functools.partial(_moba_prep_kernel, nb=nb),
        grid=(batch, ATT_HEADS),
        in_specs=[pl.BlockSpec((seq, HEAD_DIM), lambda b, h: (b, CB_Q + h)),
                  pl.BlockSpec((seq, HEAD_DIM), lambda b, h: (b, CB_K + h)),
                  pl.BlockSpec((seq, HEAD_DIM), lambda b, h: (b, CB_VAL + h))],
        out_specs=[bh((nb, HEAD_DIM, MOBA_BLOCK)), bh((seq, 2 * HEAD_DIM)),
                   bh((nb, VT_ROWS, MOBA_BLOCK)), bh((nb, HEAD_DIM))],
        out_shape=[jax.ShapeDtypeStruct((batch, ATT_HEADS, nb, HEAD_DIM, MOBA_BLOCK), BF16),
                   jax.ShapeDtypeStruct((batch, ATT_HEADS, seq, 2 * HEAD_DIM), BF16),
                   jax.ShapeDtypeStruct((batch, ATT_HEADS, nb, VT_ROWS, MOBA_BLOCK), BF16),
                   jax.ShapeDtypeStruct((batch, ATT_HEADS, nb, HEAD_DIM), F32)],
        compiler_params=_cparams(("parallel", "parallel")),
        name="moba_prep",
    )(proj, proj, proj)


def _bias_tiles_kernel(rb_ref, o_ref):
    h = pl.program_id(0)
    kj = lax.broadcasted_iota(jnp.int32, (MOBA_BLOCK, MOBA_BLOCK), 0)
    qi = lax.broadcasted_iota(jnp.int32, (MOBA_BLOCK, MOBA_BLOCK), 1)
    max_exact = REL_BUCKETS // 2
    for t in range(2):
        rel = qi - kj + MOBA_BLOCK * t
        n = jnp.maximum(rel, 0)
        nf = jnp.maximum(n, max_exact).astype(F32)
        large = max_exact + (jnp.log(nf / max_exact) / math.log(REL_MAX_DIST / max_exact)
                             * (REL_BUCKETS - max_exact)).astype(jnp.int32)
        large = jnp.minimum(large, REL_BUCKETS - 1)
        bucket = jnp.where(n < max_exact, n, large)
        bias = jnp.zeros((MOBA_BLOCK, MOBA_BLOCK), F32)
        for r in range(REL_BUCKETS):
            bias = jnp.where(bucket == r, rb_ref[r, h], bias)
        o_ref[t] = jnp.where(rel >= 0, bias * LOG2E, MASK_NEG)


def _bias_tiles(rel_bias):
    return pl.pallas_call(
        _bias_tiles_kernel,
        grid=(ATT_HEADS,),
        in_specs=[pl.BlockSpec(memory_space=pltpu.SMEM)],
        out_specs=pl.BlockSpec((None, 2, MOBA_BLOCK, MOBA_BLOCK), lambda h: (h, 0, 0, 0)),
        out_shape=jax.ShapeDtypeStruct((ATT_HEADS, 2, MOBA_BLOCK, MOBA_BLOCK), F32),
        compiler_params=_cparams(("arbitrary",)),
        name="bias_tiles",
    )(rel_bias)


FAR_BLOCKS = 4
MAX_LOG2_RISE = 64.0
HEADS_PER_STEP = 4


def _moba_kernel(rb_ref, qt_ref, kx_ref, vt_ref, km_ref, bias_ref, o_ref, *, nb):
    i = pl.program_id(2)
    jp = jnp.maximum(i - 1, 0)
    prev0 = pl.multiple_of(jp * MOBA_BLOCK, MOBA_BLOCK)
    own0 = pl.multiple_of(i * MOBA_BLOCK, MOBA_BLOCK)
    no_prev = jnp.where(i > 0, 0.0, MASK_NEG)
    neg_inf = jnp.float32(-jnp.inf)

    heads = range(HEADS_PER_STEP)

    def choose_blocks(hh):
        km_hi, km_lo = _split_bf16(km_ref[hh])
        score = (jnp.dot(km_hi, qt_ref[hh], preferred_element_type=F32)
                 + jnp.dot(km_lo, qt_ref[hh], preferred_element_type=F32))
        bid = lax.broadcasted_iota(jnp.int32, score.shape, 0)
        score = jnp.where(bid < i, score, neg_inf)
        chosen = bid < 0
        for _ in range(MOBA_TOPK):
            mx = jnp.max(score, axis=0, keepdims=True)
            hit = (score == mx) & (mx > neg_inf)
            idx = jnp.min(jnp.where(hit, bid, nb), axis=0, keepdims=True)
            pick = bid == idx
            chosen = chosen | pick
            score = jnp.where(pick, neg_inf, score)
        return chosen

    def with_mask(hh, val):
        hi, lo = _split_bf16(val)
        fill = jnp.zeros((HEAD_DIM - 2 * EXT_LO, MOBA_BLOCK), BF16)
        if nb < EXT_LO:
            gap = jnp.zeros((EXT_LO - nb, MOBA_BLOCK), BF16)
            return jnp.concatenate([qt_ref[hh], hi, gap, lo, gap, fill], axis=0)
        return jnp.concatenate([qt_ref[hh], hi, lo, fill], axis=0)

    chosen = [choose_blocks(hh) for hh in heads]
    bid = lax.broadcasted_iota(jnp.int32, chosen[0].shape, 0)
    far_bias = [rb_ref[REL_BUCKETS - 1, pl.program_id(1) * HEADS_PER_STEP + hh] * LOG2E for hh in heads]
    qx_fars = [with_mask(hh, jnp.where(chosen[hh] & (bid <= i - 2), far_bias[hh], MASK_NEG)) for hh in heads]
    qx_nears = [with_mask(hh, jnp.where((bid == i) | (chosen[hh] & (bid == i - 1)), 0.0, MASK_NEG))
                for hh in heads]

    ss = [jnp.dot(jnp.concatenate([kx_ref[hh, pl.ds(prev0, MOBA_BLOCK), :],
                                   kx_ref[hh, pl.ds(own0, MOBA_BLOCK), :]], axis=0),
                  qx_nears[hh], preferred_element_type=F32)
          + jnp.concatenate([bias_ref[hh, 1] + no_prev, bias_ref[hh, 0]], axis=0) for hh in heads]
    ms = [jnp.max(ss[hh], axis=0, keepdims=True) for hh in heads]
    ps = [jnp.exp2(ss[hh] - ms[hh]).astype(BF16) for hh in heads]
    accs = [jnp.dot(jnp.concatenate([vt_ref[hh, jp], vt_ref[hh, i]], axis=1), ps[hh],
                    preferred_element_type=F32) for hh in heads]

    def far_step(c, carry):
        row0 = pl.multiple_of(c * (FAR_BLOCKS * MOBA_BLOCK), FAR_BLOCKS * MOBA_BLOCK)
        m_olds = [carry[hh][0] for hh in heads]
        accs = [carry[hh][1] for hh in heads]

        def logits(hh):
            return jnp.dot(kx_ref[hh, pl.ds(row0, FAR_BLOCKS * MOBA_BLOCK), :], qx_fars[hh],
                           preferred_element_type=F32)

        def values(hh):
            return jnp.concatenate([vt_ref[hh, c * FAR_BLOCKS + k] for k in range(FAR_BLOCKS)], axis=1)

        ss = [logits(hh) for hh in heads]
        ps = [jnp.exp2(ss[hh] - m_olds[hh]).astype(BF16) for hh in heads]
        m_news = [jnp.maximum(m_olds[hh], jnp.max(ss[hh], axis=0, keepdims=True)) for hh in heads]
        alphas = [jnp.exp2(m_olds[hh] - m_news[hh]) for hh in heads]
        usual = tuple((m_news[hh], alphas[hh] * (accs[hh] + jnp.dot(values(hh), ps[hh],
                                                                    preferred_element_type=F32)))
                      for hh in heads)
        rise = functools.reduce(jnp.maximum, [jnp.max(m_news[hh] - m_olds[hh]) for hh in heads])

        def redo():
            out = []
            for hh in heads:
                p = jnp.exp2(logits(hh) - m_news[hh]).astype(BF16)
                out.append((m_news[hh], alphas[hh] * accs[hh]
                            + jnp.dot(values(hh), p, preferred_element_type=F32)))
            return tuple(out)

        return lax.cond(rise > MAX_LOG2_RISE, redo, lambda: usual)

    n_far = lax.shift_right_logical(jnp.maximum(i - 1, 0) + FAR_BLOCKS - 1, 2)
    final = lax.fori_loop(0, n_far, far_step, tuple((ms[hh], accs[hh]) for hh in heads))
    for hh in heads:
        acc = final[hh][1]
        out = acc[0:HEAD_DIM] / acc[HEAD_DIM:HEAD_DIM + 1]
        o_ref[:, hh * HEAD_DIM:(hh + 1) * HEAD_DIM] = out.T.astype(BF16)


def _moba(rel_bias, qt, kx, vt, kmean, bias_tiles, batch, seq):
    nb = seq // MOBA_BLOCK
    hps = HEADS_PER_STEP
    assert nb % FAR_BLOCKS == 0 and FAR_BLOCKS == 4 and ATT_HEADS % hps == 0
    bh = lambda shape: pl.BlockSpec((None, hps) + shape, lambda b, h, i: (b, h) + (0,) * len(shape),
                                    pipeline_mode=pl.Buffered(1))
    return pl.pallas_call(
        functools.partial(_moba_kernel, nb=nb),
        grid=(batch, ATT_HEADS // hps, nb),
        in_specs=[pl.BlockSpec(memory_space=pltpu.SMEM),
                  pl.BlockSpec((None, hps, None, HEAD_DIM, MOBA_BLOCK), lambda b, h, i: (b, h, i, 0, 0)),
                  bh((seq, 2 * HEAD_DIM)), bh((nb, VT_ROWS, MOBA_BLOCK)), bh((nb, HEAD_DIM)),
                  pl.BlockSpec((hps, 2, MOBA_BLOCK, MOBA_BLOCK), lambda b, h, i: (h, 0, 0, 0))],
        out_specs=pl.BlockSpec((MOBA_BLOCK, hps * HEAD_DIM), lambda b, h, i: (b * nb + i, h)),
        out_shape=jax.ShapeDtypeStruct((batch * seq, D_ATT), BF16),
        compiler_params=_cparams(("parallel", "parallel", "arbitrary")),
        name="moba",
    )(rel_bias, qt, kx, vt, kmean, bias_tiles)


MERGE_PARTS = 1


def _merge_kernel(x_ref, ysg_ref, yatt_ref, gsg_ref, gatt_ref, mod_ref, nw_ref,
                  wsg_ref, watt_ref, wo_ref, wr_ref, x1_ref, tok_ref, cls_ref):
    tm = x_ref.shape[0]
    part = tm // MERGE_PARTS
    parts = [slice(k * part, (k + 1) * part) for k in range(MERGE_PARTS)]
    a_sg = [jnp.dot(ysg_ref[rs, :], wsg_ref[...], preferred_element_type=F32) for rs in parts]
    a_att = [jnp.dot(yatt_ref[rs, :], watt_ref[...], preferred_element_type=F32) for rs in parts]
    merged = [(jax.nn.sigmoid(gsg_ref[rs, :].astype(F32)) * a_sg[k]
               + jax.nn.sigmoid(gatt_ref[rs, :].astype(F32)) * a_att[k]).astype(BF16)
              for k, rs in enumerate(parts)]
    mixed = [jnp.dot(merged[k], wo_ref[...], preferred_element_type=F32) for k in range(MERGE_PARTS)]
    w_hi, w_lo = _split_bf16(wr_ref[...])
    for k, rs in enumerate(parts):
        _merge_epilogue(k * part, part, x_ref[rs, :], mixed[k], mod_ref, nw_ref, w_hi, w_lo,
                        x1_ref, tok_ref, cls_ref)


def _merge_epilogue(row0, tm, x, mixed, mod_ref, nw_ref, w_hi, w_lo, x1_ref, tok_ref, cls_ref):
    x1 = x + mod_ref[0, 2:3, :] * mixed
    x1_ref[row0:row0 + tm, :] = x1
    ms = jnp.mean(x1 * x1, axis=-1, keepdims=True)
    y = x1 * lax.rsqrt(ms + EPS) * nw_ref[...]
    h2 = y * (1.0 + mod_ref[0, 4:5, :]) + mod_ref[0, 3:4, :]
    blk0 = row0 * TOK_ROWS
    for a in range(D_ROWS):
        tok_ref[pl.ds(blk0 + a, tm, stride=TOK_ROWS), :] = h2[:, a * LANES:(a + 1) * LANES]
    for a in range(D_ROWS + 1, TOK_ROWS):
        tok_ref[pl.ds(blk0 + a, tm, stride=TOK_ROWS), :] = jnp.zeros((tm, LANES), F32)

    h_hi, h_lo = _split_bf16(h2)
    hi_both = jnp.dot(h_hi, jnp.concatenate([w_hi, w_lo], axis=1), preferred_element_type=F32)
    logits = (hi_both[:, :LANES] + jnp.dot(h_lo, w_hi, preferred_element_type=F32)
              + hi_both[:, LANES:])

    lane = lax.broadcasted_iota(jnp.int32, logits.shape, 1)
    neg_inf = jnp.float32(-jnp.inf)
    in_g = lane < N_GROUPS
    lg = jnp.where(in_g, logits, neg_inf)
    mg = jnp.max(lg, axis=1, keepdims=True)
    eg = jnp.exp(lg - mg)
    g_prob = eg / jnp.sum(eg, axis=1, keepdims=True)
    g_p = jnp.max(g_prob, axis=1, keepdims=True)
    g_idx = jnp.min(jnp.where(g_prob == g_p, lane, LANES), axis=1, keepdims=True)
    lo_col = ROUTER_COL0 + EXPERTS_PER_GROUP * g_idx
    in_e = (lane >= lo_col) & (lane < lo_col + EXPERTS_PER_GROUP)
    le = jnp.where(in_e, logits, neg_inf)
    me = jnp.max(le, axis=1, keepdims=True)
    ee = jnp.exp(le - me)
    e_prob = jnp.where(in_e, ee / jnp.sum(ee, axis=1, keepdims=True), -1.0)
    p1 = jnp.max(e_prob, axis=1, keepdims=True)
    i1 = jnp.min(jnp.where(e_prob == p1, lane, LANES), axis=1, keepdims=True)
    rest = jnp.where(lane == i1, -1.0, e_prob)
    p2 = jnp.max(rest, axis=1, keepdims=True)
    i2 = jnp.min(jnp.where(rest == p2, lane, LANES), axis=1, keepdims=True)
    denom = p1 + p2
    ea = jnp.minimum(i1, i2) - lo_col
    eb = jnp.maximum(i1, i2) - lo_col
    pair = lax.shift_right_logical(ea * (2 * EXPERTS_PER_GROUP - 1 - ea), 1) + eb - ea - 1
    cls = g_idx * PAIRS_PER_GROUP + pair
    routing = (jnp.where(lane == i1, g_p * (p1 / denom), 0.0)
               + jnp.where(lane == i2, g_p * (p2 / denom), 0.0)
               + jnp.where(lane == CLASS_LANE, cls.astype(F32), 0.0))
    tok_ref[pl.ds(blk0 + D_ROWS, tm, stride=TOK_ROWS), :] = routing
    cls_ref[:, row0:row0 + tm] = routing.T[CLASS_LANE:CLASS_LANE + 1, :].astype(jnp.int32)


def _merge(x2, y_sg, y_att, proj, mod, norm2_w, w_sg, w_att, w_o, w_router, seq):
    n, d = x2.shape
    tm = 256
    tpb = seq // tm
    resident = lambda shape: pl.BlockSpec(shape, lambda i: (0, 0), pipeline_mode=pl.Buffered(1))
    return pl.pallas_call(
        _merge_kernel,
        grid=(n // tm,),
        in_specs=[pl.BlockSpec((tm, d), lambda i: (i, 0)),
                  pl.BlockSpec((tm, D_SG), lambda i: (i, 0)),
                  pl.BlockSpec((tm, D_ATT), lambda i: (i, 0)),
                  pl.BlockSpec((tm, d), lambda i: (i, BLK_GATE_SG)),
                  pl.BlockSpec((tm, d), lambda i: (i, BLK_GATE_ATT)),
                  pl.BlockSpec((1, 8, d), lambda i: (i // tpb, 0, 0)),
                  pl.BlockSpec((1, d), lambda i: (0, 0)),
                  resident((D_SG, d)), resident((D_ATT, d)), resident((d, d)),
                  resident((d, LANES))],
        out_specs=[pl.BlockSpec((tm, d), lambda i: (i, 0)),
                   pl.BlockSpec((tm * TOK_ROWS, LANES), lambda i: (i, 0)),
                   pl.BlockSpec((None, 1, tm), lambda i: (i, 0, 0))],
        out_shape=[jax.ShapeDtypeStruct((n, d), F32),
                   jax.ShapeDtypeStruct((n * TOK_ROWS, LANES), F32),
                   jax.ShapeDtypeStruct((n // tm, 1, tm), jnp.int32)],
        compiler_params=_cparams(("parallel",)),
        name="merge",
    )(x2, y_sg, y_att, proj, proj, mod, norm2_w, w_sg, w_att, w_o, w_router)


def _invert_kernel(pos_ref, free_ref, inv_ref, *, n_tok, n_free):
    def place(t, carry):
        inv_ref[pos_ref[t]] = t
        return carry

    def place_free(k, carry):
        inv_ref[free_ref[k]] = n_tok + k
        return carry

    lax.fori_loop(0, n_tok, place, 0, unroll=8)
    lax.fori_loop(0, n_free, place_free, 0, unroll=8)


def _invert(pos, free):
    n_tok, n_free = pos.shape[0], free.shape[0]
    return pl.pallas_call(
        functools.partial(_invert_kernel, n_tok=n_tok, n_free=n_free),
        grid_spec=pltpu.PrefetchScalarGridSpec(
            num_scalar_prefetch=2, grid=(1,), in_specs=[],
            out_specs=pl.BlockSpec(memory_space=pltpu.SMEM)),
        out_shape=jax.ShapeDtypeStruct((n_tok + n_free,), jnp.int32),
        compiler_params=_cparams(("arbitrary",)),
        name="invert",
    )(pos, free)


MOE_TILE = 256
ROW_UNROLL_LOG2 = 3
ROW_UNROLL = 1 << ROW_UNROLL_LOG2


def _for_rows(rows, fn):
    groups = lax.shift_right_logical(rows, ROW_UNROLL_LOG2)

    def group(g, carry):
        for u in range(ROW_UNROLL):
            fn(g * ROW_UNROLL + u)
        return carry

    def single(r, carry):
        fn(r)
        return carry

    lax.fori_loop(0, groups, group, 0)
    lax.fori_loop(groups * ROW_UNROLL, rows, single, 0)


def _moe_kernel(src_ref, dst_ref, ea_ref, eb_ref, rows_ref, na_ref, tok_hbm,
                w1a_ref, w3a_ref, w2a_ref, w1b_ref, w3b_ref, w2b_ref, y_hbm,
                gbuf, stage, acc_ref, gsem, ssem):
    i = pl.program_id(0)
    n_act = na_ref[0]
    slot = lax.rem(i, 2)

    def gather_start(tile, sl):
        def one(r, carry):
            row0 = pl.multiple_of(r * TOK_ROWS, 8)
            pltpu.make_async_copy(tok_hbm.at[src_ref[tile * MOE_TILE + r]],
                                  gbuf.at[sl, pl.ds(row0, TOK_ROWS), :], gsem.at[sl]).start()
            return carry
        lax.fori_loop(0, MOE_TILE, one, 0, unroll=8)

    def gather_wait(sl):
        for _ in range(MOE_TILE):
            pltpu.make_async_copy(tok_hbm.at[0], gbuf.at[sl, pl.ds(0, TOK_ROWS), :], gsem.at[sl]).wait()

    def scatter_start(tile, sl):
        def one(r):
            row0 = pl.multiple_of(r * D_ROWS, 8)
            pltpu.make_async_copy(stage.at[sl, pl.ds(row0, D_ROWS), :],
                                  y_hbm.at[dst_ref[tile * MOE_TILE + r]], ssem.at[sl]).start()
        _for_rows(rows_ref[tile], one)

    def scatter_wait(tile, sl):
        def one(r):
            pltpu.make_async_copy(stage.at[sl, pl.ds(0, D_ROWS), :], y_hbm.at[0], ssem.at[sl]).wait()
        _for_rows(rows_ref[tile], one)

    @pl.when(i < n_act)
    def _():
        @pl.when(i == 0)
        def _():
            gather_start(0, 0)

        @pl.when(i + 1 < n_act)
        def _():
            gather_start(i + 1, 1 - slot)

        gather_wait(slot)
        t = jnp.concatenate([gbuf[slot, pl.ds(a, MOE_TILE, stride=TOK_ROWS), :] for a in range(D_ROWS)],
                            axis=1).astype(BF16)
        routing = gbuf[slot, pl.ds(D_ROWS, MOE_TILE, stride=TOK_ROWS), :]
        lane = lax.broadcasted_iota(jnp.int32, routing.shape, 1)
        experts = ((ea_ref[i], w1a_ref, w3a_ref, w2a_ref), (eb_ref[i], w1b_ref, w3b_ref, w2b_ref))
        ups = [(jnp.dot(t, w1_ref[...], preferred_element_type=F32),
                jnp.dot(t, w3_ref[...], preferred_element_type=F32)) for _, w1_ref, w3_ref, _ in experts]
        hmids = []
        for (e, _, _, _), (a, b) in zip(experts, ups):
            ge = jnp.sum(jnp.where(lane == ROUTER_COL0 + e, routing, 0.0), axis=1, keepdims=True)
            hmids.append(((a * jax.nn.sigmoid(a)) * b * ge).astype(BF16))
        acc_ref[...] = (jnp.dot(hmids[0], w2a_ref[...], preferred_element_type=F32)
                        + jnp.dot(hmids[1], w2b_ref[...], preferred_element_type=F32))

        @pl.when(i >= 2)
        def _():
            scatter_wait(i - 2, slot)

        for a in range(D_ROWS):
            stage[slot, pl.ds(a, MOE_TILE, stride=D_ROWS), :] = acc_ref[:, a * LANES:(a + 1) * LANES]
        scatter_start(i, slot)

        @pl.when(i == n_act - 1)
        def _():
            scatter_wait(i, slot)

            @pl.when(i >= 1)
            def _():
                scatter_wait(i - 1, 1 - slot)


def _moe(src_tok, dst_row, tile_ea, tile_eb, tile_rows, n_active, tok_blocks, w1, w3, w2, n_tok):
    n_slots = src_tok.shape[0]
    d = D_MODEL
    blk_a = lambda i, src, dst, ea, eb, rows, na: (ea[i], 0, 0)
    blk_b = lambda i, src, dst, ea, eb, rows, na: (eb[i], 0, 0)
    up = lambda blk: pl.BlockSpec((None, d, D_EXPERT), blk)
    down = lambda blk: pl.BlockSpec((None, D_EXPERT, d), blk)
    return pl.pallas_call(
        _moe_kernel,
        grid_spec=pltpu.PrefetchScalarGridSpec(
            num_scalar_prefetch=6, grid=(n_slots // MOE_TILE,),
            in_specs=[pl.BlockSpec(memory_space=pl.ANY),
                      up(blk_a), up(blk_a), down(blk_a), up(blk_b), up(blk_b), down(blk_b)],
            out_specs=pl.BlockSpec(memory_space=pl.ANY),
            scratch_shapes=[pltpu.VMEM((2, MOE_TILE * TOK_ROWS, LANES), F32),
                            pltpu.VMEM((2, MOE_TILE * D_ROWS, LANES), F32),
                            pltpu.VMEM((MOE_TILE, d), F32),
                            pltpu.SemaphoreType.DMA((2,)),
                            pltpu.SemaphoreType.DMA((2,))]),
        out_shape=jax.ShapeDtypeStruct((n_tok, D_ROWS, LANES), F32),
        compiler_params=_cparams(("arbitrary",)),
        name="moe",
    )(src_tok, dst_row, tile_ea, tile_eb, tile_rows, n_active, tok_blocks, w1, w3, w2, w1, w3, w2)


def _final_kernel(x1_ref, y_ref, mod_ref, fw_ref, o_ref):
    tm = x1_ref.shape[0]
    parts = []
    ss = jnp.zeros((tm, 1), F32)
    for a in range(D_ROWS):
        cols = slice(a * LANES, (a + 1) * LANES)
        part = x1_ref[:, cols] + mod_ref[0, 5:6, cols] * y_ref[pl.ds(a, tm, stride=D_ROWS), :]
        ss = ss + jnp.sum(part * part, axis=1, keepdims=True)
        parts.append(part)
    inv = lax.rsqrt(ss * (1.0 / D_MODEL) + EPS)
    for a in range(D_ROWS):
        cols = slice(a * LANES, (a + 1) * LANES)
        o_ref[:, cols] = parts[a] * inv * fw_ref[:, cols]


def _final(x1, y_blocks, mod, final_w, seq):
    n, d = x1.shape
    tm = 256
    tpb = seq // tm
    return pl.pallas_call(
        _final_kernel,
        grid=(n // tm,),
        in_specs=[pl.BlockSpec((tm, d), lambda i: (i, 0)),
                  pl.BlockSpec((tm * D_ROWS, LANES), lambda i: (i, 0)),
                  pl.BlockSpec((1, 8, d), lambda i: (i // tpb, 0, 0)),
                  pl.BlockSpec((1, d), lambda i: (0, 0))],
        out_specs=pl.BlockSpec((tm, d), lambda i: (i, 0)),
        out_shape=jax.ShapeDtypeStruct((n, d), F32),
        compiler_params=_cparams(("parallel",)),
        name="final",
    )(x1, y_blocks, mod, final_w)


def _class_sort_plan(cls, n):
    class_ids = jnp.arange(N_CLASSES, dtype=jnp.int32)
    onehot = (cls[:, None] == class_ids[None, :]).astype(jnp.int32)
    blk = LANES
    within = jnp.einsum("ts,bsg->btg", jnp.tril(jnp.ones((blk, blk), F32)),
                        onehot.astype(F32).reshape(n // blk, blk, N_CLASSES))
    before = jnp.tril(jnp.ones((n // blk, n // blk), F32), -1) @ within[:, -1, :]
    incl = (within + before[:, None, :]).reshape(n, N_CLASSES).astype(jnp.int32)
    count = incl[-1]
    tiles = (count + MOE_TILE - 1) // MOE_TILE
    tile_end = jnp.cumsum(tiles)
    start = (tile_end - tiles) * MOE_TILE
    rank = jnp.sum(incl * onehot, axis=1) - 1
    pos = jnp.sum(onehot * start[None, :], axis=1) + rank
    n_tiles = n // MOE_TILE + N_CLASSES
    tile_ids = jnp.arange(n_tiles, dtype=jnp.int32)
    tile_class = jnp.sum((tile_ids[:, None] >= tile_end[None, :]).astype(jnp.int32), axis=1)
    tile_class = jnp.minimum(tile_class, N_CLASSES - 1)
    tile_hot = (tile_class[:, None] == class_ids[None, :]).astype(jnp.int32)
    done = (tile_ids - jnp.sum(tile_hot * (tile_end - tiles)[None, :], axis=1)) * MOE_TILE
    tile_rows = jnp.clip(jnp.sum(tile_hot * count[None, :], axis=1) - done, 0, MOE_TILE)
    first = jnp.asarray([a for a, _ in PAIRS], jnp.int32)
    second = jnp.asarray([b for _, b in PAIRS], jnp.int32)
    group0 = (class_ids // PAIRS_PER_GROUP) * EXPERTS_PER_GROUP
    tile_ea = jnp.sum(tile_hot * (group0 + first[class_ids % PAIRS_PER_GROUP])[None, :], axis=1)
    tile_eb = jnp.sum(tile_hot * (group0 + second[class_ids % PAIRS_PER_GROUP])[None, :], axis=1)
    pad = tiles * MOE_TILE - count
    pad_end = jnp.cumsum(pad)
    k = jnp.arange(N_CLASSES * MOE_TILE, dtype=jnp.int32)
    seg = jnp.sum((k[:, None] >= pad_end[None, :]).astype(jnp.int32), axis=1)
    seg_hot = (seg[:, None] == class_ids[None, :]).astype(jnp.int32)
    in_class = jnp.sum(seg_hot * (start + count - (pad_end - pad))[None, :], axis=1) + k
    tail = tile_end[-1] * MOE_TILE + k - pad_end[-1]
    free = jnp.where(seg < N_CLASSES, in_class, tail)
    i32 = lambda a: a.astype(jnp.int32)
    return i32(pos), i32(free), i32(tile_ea), i32(tile_eb), i32(tile_rows), i32(tile_end[-1:])


def kernel(x, c, w_ada, b_ada, norm1_w, norm2_w, final_norm_w, w_in, sg_ln_w, sg_ln_b, w_spatial,
           b_spatial, rel_bias, w_out_sg, w_out_att, w_o, w_router_group, w_router_expert,
           w_exp_gate, w_exp_up, w_exp_down):
    batch, seq, d = x.shape
    assert d == D_MODEL and w_ada.shape[0] == 1
    assert seq % 1024 == 0 and seq // MOBA_BLOCK <= LANES
    n = batch * seq
    x2 = x.reshape(n, d)

    c_pad = jnp.zeros((8, d), F32).at[:batch].set(c)
    mod = _adaln(c_pad, w_ada[0], b_ada[0].reshape(1, 6 * d))
    mod = mod[:batch].reshape(batch, 6, d)
    mod = jnp.concatenate([mod, jnp.zeros((batch, 2, d), F32)], axis=1)

    colscale = jnp.ones((1, IN_COLS), F32).at[:, CB_Q * LANES:CB_K * LANES].set(HEAD_DIM ** -0.5 * LOG2E)
    proj = _proj(x2, mod, norm1_w[0].reshape(1, d), w_in[0].astype(BF16), colscale, seq)

    y_sg = _sgu(proj, sg_ln_w[0].reshape(1, D_SG), sg_ln_b[0].reshape(1, D_SG),
                w_spatial[0], b_spatial[0].T)

    qt, kx, vt, kmean = _moba_prep(proj, batch, seq)
    bias_tiles = _bias_tiles(rel_bias)
    y_att = _moba(rel_bias, qt, kx, vt, kmean, bias_tiles, batch, seq)

    w_router = jnp.zeros((d, LANES), F32)
    w_router = w_router.at[:, :N_GROUPS].set(w_router_group[0])
    w_router = w_router.at[:, ROUTER_COL0:ROUTER_COL0 + N_EXPERTS].set(w_router_expert[0])
    x1, tok, cls = _merge(x2, y_sg, y_att, proj, mod, norm2_w[0].reshape(1, d),
                          w_out_sg[0].astype(BF16), w_out_att[0].astype(BF16),
                          w_o[0].astype(BF16), w_router, seq)

    tok = tok.reshape(n, TOK_ROWS, LANES)
    pos, free, tile_ea, tile_eb, tile_rows, n_active = _class_sort_plan(cls.reshape(n), n)
    inv = _invert(pos, free)
    src_tok = jnp.where(inv < n, inv, 0)
    y = _moe(src_tok, inv, tile_ea, tile_eb, tile_rows, n_active, tok, w_exp_gate[0].astype(BF16),
             w_exp_up[0].astype(BF16), w_exp_down[0].astype(BF16), n)
    out = _final(x1, y.reshape(-1, LANES), mod, final_norm_w.reshape(1, d), seq)
    return out.reshape(batch, seq, d)
```

```python
import functools
import math

import jax
import jax.numpy as jnp
from jax import lax
from jax.experimental import pallas as pl
from jax.experimental.pallas import tpu as pltpu

F32 = jnp.float32
BF16 = jnp.bfloat16

LANES = 128
BF16_SUBLANES = 16
D_MODEL = 2048
D_SG = D_MODEL // 2
SG_GROUPS = 8
SG_CHUNK = 128
ATT_HEADS = 8
HEAD_DIM = 128
D_ATT = ATT_HEADS * HEAD_DIM
MOBA_BLOCK = 256
MOBA_TOPK = 3
REL_BUCKETS = 32
REL_MAX_DIST = 128
N_GROUPS = 4
EXPERTS_PER_GROUP = 4
N_EXPERTS = N_GROUPS * EXPERTS_PER_GROUP
D_EXPERT = 512
EPS = 1e-6
IN_COLS = 2 * D_SG + 3 * D_ATT + 2 * D_MODEL

N_GATE_COLS = 2 * D_MODEL
BLK_GATE_SG = 0
BLK_GATE_ATT = 1
BLK_U = N_GATE_COLS // D_SG
BLK_V = BLK_U + 1
CB_Q = (N_GATE_COLS + 2 * D_SG) // LANES
CB_K = CB_Q + ATT_HEADS
CB_VAL = CB_K + ATT_HEADS

MASK_NEG = -1e9
LOG2E = math.log2(math.e)
ROUTER_COL0 = N_GROUPS
CLASS_LANE = 0
PAIRS_PER_GROUP = EXPERTS_PER_GROUP * (EXPERTS_PER_GROUP - 1) // 2
N_CLASSES = N_GROUPS * PAIRS_PER_GROUP
PAIRS = [(a, b) for a in range(EXPERTS_PER_GROUP) for b in range(a + 1, EXPERTS_PER_GROUP)]
D_ROWS = D_MODEL // LANES
TOK_ROWS = D_ROWS + 8
VMEM_LIMIT = 56 * 1024 * 1024

ADALN_TN = 1024
PROJ_TM = 1024
PROJ_TN = 1024
SGU_CHUNKS = 4
MERGE_TM = 256
FINAL_TM = 256


def _cparams(sem):
    return pltpu.CompilerParams(dimension_semantics=sem, vmem_limit_bytes=VMEM_LIMIT)


def _split_bf16(a):
    hi = a.astype(BF16)
    lo = (a - hi.astype(F32)).astype(BF16)
    return hi, lo


def _adaln_kernel(c_ref, w_ref, b_ref, o_ref):
    c = c_ref[...]
    rows = c.shape[0]
    c_hi, c_lo = _split_bf16(c * jax.nn.sigmoid(c))
    w_hi, w_lo = _split_bf16(w_ref[...])
    both = jnp.dot(jnp.concatenate([c_hi, c_lo], axis=0), w_hi, preferred_element_type=F32)
    o_ref[...] = (both[:rows] + both[rows:] + jnp.dot(c_hi, w_lo, preferred_element_type=F32)
                  + b_ref[...])


def _adaln(c_pad, w, b):
    rows, d = c_pad.shape
    cols = w.shape[1]
    tn = ADALN_TN
    return pl.pallas_call(
        _adaln_kernel,
        grid=(cols // tn,),
        in_specs=[pl.BlockSpec((rows, d), lambda j: (0, 0)),
                  pl.BlockSpec((d, tn), lambda j: (0, j)),
                  pl.BlockSpec((1, tn), lambda j: (0, j))],
        out_specs=pl.BlockSpec((rows, tn), lambda j: (0, j)),
        out_shape=jax.ShapeDtypeStruct((rows, cols), F32),
        compiler_params=_cparams(("arbitrary",)),
        name="adaln",
    )(c_pad, w, b)


def _proj_kernel(x_ref, mod_ref, nw_ref, w_ref, cs_ref, o_ref, h_scr):
    @pl.when(pl.program_id(1) == 0)
    def _():
        x = x_ref[...]
        ms = jnp.mean(x * x, axis=-1, keepdims=True)
        y = x * lax.rsqrt(ms + EPS) * nw_ref[...]
        h = y * (1.0 + mod_ref[0, 1:2, :]) + mod_ref[0, 0:1, :]
        h_scr[...] = h.astype(BF16)

    acc = jnp.dot(h_scr[...], w_ref[...].astype(BF16), preferred_element_type=F32)
    o_ref[...] = (acc * cs_ref[...]).astype(BF16)


def _proj(x2, mod, norm_w, w_in, colscale, seq):
    n, d = x2.shape
    cols = w_in.shape[1]
    tm = min(PROJ_TM, seq)
    tn = PROJ_TN
    tpb = seq // tm
    n_cb = cols // tn
    first = (cols - N_GATE_COLS) // tn
    assert (cols - N_GATE_COLS) % tn == 0
    return pl.pallas_call(
        _proj_kernel,
        grid=(n // tm, n_cb),
        in_specs=[pl.BlockSpec((tm, d), lambda i, j: (i, 0)),
                  pl.BlockSpec((1, 8, d), lambda i, j: (i // tpb, 0, 0)),
                  pl.BlockSpec((1, d), lambda i, j: (0, 0)),
                  pl.BlockSpec((d, tn), lambda i, j: (0, (j + first) % n_cb)),
                  pl.BlockSpec((1, tn), lambda i, j: (0, j))],
        out_specs=pl.BlockSpec((tm, tn), lambda i, j: (i, j)),
        out_shape=jax.ShapeDtypeStruct((n, cols), BF16),
        scratch_shapes=[pltpu.VMEM((tm, d), BF16)],
        compiler_params=_cparams(("parallel", "arbitrary")),
        name="proj",
    )(x2, mod, norm_w, w_in, colscale)


def _gelu(a):
    return 0.5 * a * (1.0 + lax.erf(a * (1.0 / math.sqrt(2.0))))


def _sgu_kernel(u_ref, v_ref, lnw_ref, lnb_ref, ws_ref, bst_ref, o_ref, *, chunks):
    u = _gelu(u_ref[...].astype(F32))
    v = _gelu(v_ref[...].astype(F32))
    mu = jnp.mean(v, axis=-1, keepdims=True)
    vc = v - mu
    var = jnp.mean(vc * vc, axis=-1, keepdims=True)
    vn = (vc * lax.rsqrt(var + EPS) * lnw_ref[...] + lnb_ref[...]).astype(BF16)
    row = lax.broadcasted_iota(jnp.int32, (SG_CHUNK, SG_CHUNK), 0)
    col = lax.broadcasted_iota(jnp.int32, (SG_CHUNK, SG_CHUNK), 1)
    causal = col <= row
    for g in range(SG_GROUPS):
        wm = jnp.where(causal, ws_ref[g], 0.0).astype(BF16)
        bcol = bst_ref[:, g:g + 1]
        gs = slice(g * LANES, (g + 1) * LANES)
        for c in range(chunks):
            rs = slice(c * SG_CHUNK, (c + 1) * SG_CHUNK)
            z = jnp.dot(wm, vn[rs, gs], preferred_element_type=F32) + bcol
            o_ref[rs, gs] = (u[rs, gs] * z).astype(BF16)


def _sgu(proj, ln_w, ln_b, w_s, b_s_t):
    n = proj.shape[0]
    chunks = SGU_CHUNKS
    tm = chunks * SG_CHUNK
    return pl.pallas_call(
        functools.partial(_sgu_kernel, chunks=chunks),
        grid=(n // tm,),
        in_specs=[pl.BlockSpec((tm, D_SG), lambda i: (i, BLK_U)),
                  pl.BlockSpec((tm, D_SG), lambda i: (i, BLK_V)),
                  pl.BlockSpec((1, D_SG), lambda i: (0, 0)),
                  pl.BlockSpec((1, D_SG), lambda i: (0, 0)),
                  pl.BlockSpec((SG_GROUPS, SG_CHUNK, SG_CHUNK), lambda i: (0, 0, 0)),
                  pl.BlockSpec((SG_CHUNK, SG_GROUPS), lambda i: (0, 0))],
        out_specs=pl.BlockSpec((tm, D_SG), lambda i: (i, 0)),
        out_shape=jax.ShapeDtypeStruct((n, D_SG), BF16),
        compiler_params=_cparams(("parallel",)),
        name="sgu",
    )(proj, proj, ln_w, ln_b, w_s, b_s_t)


VT_ROWS = HEAD_DIM + 16
EXT_LO = REL_BUCKETS


def _moba_prep_kernel(q_ref, k_ref, v_ref, qt_ref, kx_ref, vt_ref, km_ref, *, nb):
    lane = lax.broadcasted_iota(jnp.int32, (MOBA_BLOCK, LANES), 1)
    pad = jnp.concatenate([jnp.ones((1, MOBA_BLOCK), F32),
                           jnp.zeros((VT_ROWS - HEAD_DIM - 1, MOBA_BLOCK), F32)], axis=0)
    for j in range(nb):
        rows = slice(j * MOBA_BLOCK, (j + 1) * MOBA_BLOCK)
        kj = k_ref[rows, :]
        onehot = jnp.where((lane == j) | (lane == j + EXT_LO), 1.0, 0.0).astype(BF16)
        kx_ref[rows, :] = jnp.concatenate([kj, onehot], axis=1)
        km_ref[j:j + 1, :] = jnp.mean(kj.astype(F32), axis=0, keepdims=True)
        vt = v_ref[rows, :].astype(F32).T
        vt_ref[j] = jnp.concatenate([vt, pad], axis=0).astype(BF16)
        qt_ref[j] = q_ref[rows, :].astype(F32).T.astype(BF16)


def _moba_prep(proj, batch, seq):
    nb = seq // MOBA_BLOCK
    bh = lambda shape: pl.BlockSpec((None, None) + shape, lambda b, h: (b, h) + (0,) * len(shape))
    return pl.pallas_call(
        functools.partial(_moba_prep_kernel, nb=nb),
        grid=(batch, ATT_HEADS),
        in_specs=[pl.BlockSpec((seq, HEAD_DIM), lambda b, h: (b, CB_Q + h)),
                  pl.BlockSpec((seq, HEAD_DIM), lambda b, h: (b, CB_K + h)),
                  pl.BlockSpec((seq, HEAD_DIM), lambda b, h: (b, CB_VAL + h))],
        out_specs=[bh((nb, HEAD_DIM, MOBA_BLOCK)), bh((seq, 2 * HEAD_DIM)),
                   bh((nb, VT_ROWS, MOBA_BLOCK)), bh((nb, HEAD_DIM))],
        out_shape=[jax.ShapeDtypeStruct((batch, ATT_HEADS, nb, HEAD_DIM, MOBA_BLOCK), BF16),
                   jax.ShapeDtypeStruct((batch, ATT_HEADS, seq, 2 * HEAD_DIM), BF16),
                   jax.ShapeDtypeStruct((batch, ATT_HEADS, nb, VT_ROWS, MOBA_BLOCK), BF16),
                   jax.ShapeDtypeStruct((batch, ATT_HEADS, nb, HEAD_DIM), F32)],
        compiler_params=_cparams(("parallel", "parallel")),
        name="moba_prep",
    )(proj, proj, proj)


def _bias_tiles_kernel(rb_ref, o_ref):
    h = pl.program_id(0)
    kj = lax.broadcasted_iota(jnp.int32, (MOBA_BLOCK, MOBA_BLOCK), 0)
    qi = lax.broadcasted_iota(jnp.int32, (MOBA_BLOCK, MOBA_BLOCK), 1)
    max_exact = REL_BUCKETS // 2
    for t in range(2):
        rel = qi - kj + MOBA_BLOCK * t
        n = jnp.maximum(rel, 0)
        nf = jnp.maximum(n, max_exact).astype(F32)
        large = max_exact + (jnp.log(nf / max_exact) / math.log(REL_MAX_DIST / max_exact)
                             * (REL_BUCKETS - max_exact)).astype(jnp.int32)
        large = jnp.minimum(large, REL_BUCKETS - 1)
        bucket = jnp.where(n < max_exact, n, large)
        bias = jnp.zeros((MOBA_BLOCK, MOBA_BLOCK), F32)
        for r in range(REL_BUCKETS):
            bias = jnp.where(bucket == r, rb_ref[r, h], bias)
        o_ref[t] = jnp.where(rel >= 0, bias * LOG2E, MASK_NEG)


def _bias_tiles(rel_bias):
    return pl.pallas_call(
        _bias_tiles_kernel,
        grid=(ATT_HEADS,),
        in_specs=[pl.BlockSpec(memory_space=pltpu.SMEM)],
        out_specs=pl.BlockSpec((None, 2, MOBA_BLOCK, MOBA_BLOCK), lambda h: (h, 0, 0, 0)),
        out_shape=jax.ShapeDtypeStruct((ATT_HEADS, 2, MOBA_BLOCK, MOBA_BLOCK), F32),
        compiler_params=_cparams(("arbitrary",)),
        name="bias_tiles",
    )(rel_bias)


FAR_BLOCKS = 4
MAX_LOG2_RISE = 64.0
HEADS_PER_STEP = 4


def _moba_kernel(rb_ref, qt_ref, kx_ref, vt_ref, km_ref, bias_ref, o_ref, *, nb):
    i = pl.program_id(2)
    jp = jnp.maximum(i - 1, 0)
    prev0 = pl.multiple_of(jp * MOBA_BLOCK, MOBA_BLOCK)
    own0 = pl.multiple_of(i * MOBA_BLOCK, MOBA_BLOCK)
    no_prev = jnp.where(i > 0, 0.0, MASK_NEG)
    neg_inf = jnp.float32(-jnp.inf)

    heads = range(HEADS_PER_STEP)

    def choose_blocks(hh):
        km_hi, km_lo = _split_bf16(km_ref[hh])
        score = (jnp.dot(km_hi, qt_ref[hh], preferred_element_type=F32)
                 + jnp.dot(km_lo, qt_ref[hh], preferred_element_type=F32))
        bid = lax.broadcasted_iota(jnp.int32, score.shape, 0)
        score = jnp.where(bid < i, score, neg_inf)
        chosen = bid < 0
        for _ in range(MOBA_TOPK):
            mx = jnp.max(score, axis=0, keepdims=True)
            hit = (score == mx) & (mx > neg_inf)
            idx = jnp.min(jnp.where(hit, bid, nb), axis=0, keepdims=True)
            pick = bid == idx
            chosen = chosen | pick
            score = jnp.where(pick, neg_inf, score)
        return chosen

    def with_mask(hh, val):
        hi, lo = _split_bf16(val)
        fill = jnp.zeros((HEAD_DIM - 2 * EXT_LO, MOBA_BLOCK), BF16)
        if nb < EXT_LO:
            gap = jnp.zeros((EXT_LO - nb, MOBA_BLOCK), BF16)
            return jnp.concatenate([qt_ref[hh], hi, gap, lo, gap, fill], axis=0)
        return jnp.concatenate([qt_ref[hh], hi, lo, fill], axis=0)

    chosen = [choose_blocks(hh) for hh in heads]
    bid = lax.broadcasted_iota(jnp.int32, chosen[0].shape, 0)
    far_bias = [rb_ref[REL_BUCKETS - 1, pl.program_id(1) * HEADS_PER_STEP + hh] * LOG2E for hh in heads]
    qx_fars = [with_mask(hh, jnp.where(chosen[hh] & (bid <= i - 2), far_bias[hh], MASK_NEG)) for hh in heads]
    qx_nears = [with_mask(hh, jnp.where((bid == i) | (chosen[hh] & (bid == i - 1)), 0.0, MASK_NEG))
                for hh in heads]

    ss = [jnp.dot(jnp.concatenate([kx_ref[hh, pl.ds(prev0, MOBA_BLOCK), :],
                                   kx_ref[hh, pl.ds(own0, MOBA_BLOCK), :]], axis=0),
                  qx_nears[hh], preferred_element_type=F32)
          + jnp.concatenate([bias_ref[hh, 1] + no_prev, bias_ref[hh, 0]], axis=0) for hh in heads]
    ms = [jnp.max(ss[hh], axis=0, keepdims=True) for hh in heads]
    ps = [jnp.exp2(ss[hh] - ms[hh]).astype(BF16) for hh in heads]
    accs = [jnp.dot(jnp.concatenate([vt_ref[hh, jp], vt_ref[hh, i]], axis=1), ps[hh],
                    preferred_element_type=F32) for hh in heads]

    def far_step(c, carry):
        row0 = pl.multiple_of(c * (FAR_BLOCKS * MOBA_BLOCK), FAR_BLOCKS * MOBA_BLOCK)
        m_olds = [carry[hh][0] for hh in heads]
        accs = [carry[hh][1] for hh in heads]

        def logits(hh):
            return jnp.dot(kx_ref[hh, pl.ds(row0, FAR_BLOCKS * MOBA_BLOCK), :], qx_fars[hh],
                           preferred_element_type=F32)

        def values(hh):
            return jnp.concatenate([vt_ref[hh, c * FAR_BLOCKS + k] for k in range(FAR_BLOCKS)], axis=1)

        ss = [logits(hh) for hh in heads]
        ps = [jnp.exp2(ss[hh] - m_olds[hh]).astype(BF16) for hh in heads]
        m_news = [jnp.maximum(m_olds[hh], jnp.max(ss[hh], axis=0, keepdims=True)) for hh in heads]
        alphas = [jnp.exp2(m_olds[hh] - m_news[hh]) for hh in heads]
        usual = tuple((m_news[hh], alphas[hh] * (accs[hh] + jnp.dot(values(hh), ps[hh],
                                                                    preferred_element_type=F32)))
                      for hh in heads)
        rise = functools.reduce(jnp.maximum, [jnp.max(m_news[hh] - m_olds[hh]) for hh in heads])

        def redo():
            out = []
            for hh in heads:
                p = jnp.exp2(logits(hh) - m_news[hh]).astype(BF16)
                out.append((m_news[hh], alphas[hh] * accs[hh]
                            + jnp.dot(values(hh), p, preferred_element_type=F32)))
            return tuple(out)

        return lax.cond(rise > MAX_LOG2_RISE, redo, lambda: usual)

    n_far = lax.shift_right_logical(jnp.maximum(i - 1, 0) + FAR_BLOCKS - 1, 2)
    final = lax.fori_loop(0, n_far, far_step, tuple((ms[hh], accs[hh]) for hh in heads))
    for hh in heads:
        acc = final[hh][1]
        out = acc[0:HEAD_DIM] / acc[HEAD_DIM:HEAD_DIM + 1]
        o_ref[:, hh * HEAD_DIM:(hh + 1) * HEAD_DIM] = out.T.astype(BF16)


def _moba(rel_bias, qt, kx, vt, kmean, bias_tiles, batch, seq):
    nb = seq // MOBA_BLOCK
    hps = HEADS_PER_STEP
    assert nb % FAR_BLOCKS == 0 and FAR_BLOCKS == 4 and ATT_HEADS % hps == 0
    bh = lambda shape: pl.BlockSpec((None, hps) + shape, lambda b, h, i: (b, h) + (0,) * len(shape),
                                    pipeline_mode=pl.Buffered(1))
    return pl.pallas_call(
        functools.partial(_moba_kernel, nb=nb),
        grid=(batch, ATT_HEADS // hps, nb),
        in_specs=[pl.BlockSpec(memory_space=pltpu.SMEM),
                  pl.BlockSpec((None, hps, None, HEAD_DIM, MOBA_BLOCK), lambda b, h, i: (b, h, i, 0, 0)),
                  bh((seq, 2 * HEAD_DIM)), bh((nb, VT_ROWS, MOBA_BLOCK)), bh((nb, HEAD_DIM)),
                  pl.BlockSpec((hps, 2, MOBA_BLOCK, MOBA_BLOCK), lambda b, h, i: (h, 0, 0, 0))],
        out_specs=pl.BlockSpec((MOBA_BLOCK, hps * HEAD_DIM), lambda b, h, i: (b * nb + i, h)),
        out_shape=jax.ShapeDtypeStruct((batch * seq, D_ATT), BF16),
        compiler_params=_cparams(("parallel", "parallel", "arbitrary")),
        name="moba",
    )(rel_bias, qt, kx, vt, kmean, bias_tiles)


MERGE_PARTS = 1


def _merge_kernel(x_ref, ysg_ref, yatt_ref, gsg_ref, gatt_ref, mod_ref, nw_ref,
                  wsg_ref, watt_ref, wo_ref, wr_ref, x1_ref, tok_ref, cls_ref):
    tm = x_ref.shape[0]
    part = tm // MERGE_PARTS
    parts = [slice(k * part, (k + 1) * part) for k in range(MERGE_PARTS)]
    a_sg = [jnp.dot(ysg_ref[rs, :], wsg_ref[...], preferred_element_type=F32) for rs in parts]
    a_att = [jnp.dot(yatt_ref[rs, :], watt_ref[...], preferred_element_type=F32) for rs in parts]
    merged = [(jax.nn.sigmoid(gsg_ref[rs, :].astype(F32)) * a_sg[k]
               + jax.nn.sigmoid(gatt_ref[rs, :].astype(F32)) * a_att[k]).astype(BF16)
              for k, rs in enumerate(parts)]
    mixed = [jnp.dot(merged[k], wo_ref[...], preferred_element_type=F32) for k in range(MERGE_PARTS)]
    w_hi, w_lo = _split_bf16(wr_ref[...])
    for k, rs in enumerate(parts):
        _merge_epilogue(k * part, part, x_ref[rs, :], mixed[k], mod_ref, nw_ref, w_hi, w_lo,
                        x1_ref, tok_ref, cls_ref)


def _merge_epilogue(row0, tm, x, mixed, mod_ref, nw_ref, w_hi, w_lo, x1_ref, tok_ref, cls_ref):
    x1 = x + mod_ref[0, 2:3, :] * mixed
    x1_ref[row0:row0 + tm, :] = x1
    ms = jnp.mean(x1 * x1, axis=-1, keepdims=True)
    y = x1 * lax.rsqrt(ms + EPS) * nw_ref[...]
    h2 = y * (1.0 + mod_ref[0, 4:5, :]) + mod_ref[0, 3:4, :]
    blk0 = row0 * TOK_ROWS
    for a in range(D_ROWS):
        tok_ref[pl.ds(blk0 + a, tm, stride=TOK_ROWS), :] = h2[:, a * LANES:(a + 1) * LANES]
    for a in range(D_ROWS + 1, TOK_ROWS):
        tok_ref[pl.ds(blk0 + a, tm, stride=TOK_ROWS), :] = jnp.zeros((tm, LANES), F32)

    h_hi, h_lo = _split_bf16(h2)
    hi_both = jnp.dot(h_hi, jnp.concatenate([w_hi, w_lo], axis=1), preferred_element_type=F32)
    logits = (hi_both[:, :LANES] + jnp.dot(h_lo, w_hi, preferred_element_type=F32)
              + hi_both[:, LANES:])

    lane = lax.broadcasted_iota(jnp.int32, logits.shape, 1)
    neg_inf = jnp.float32(-jnp.inf)
    in_g = lane < N_GROUPS
    lg = jnp.where(in_g, logits, neg_inf)
    mg = jnp.max(lg, axis=1, keepdims=True)
    eg = jnp.exp(lg - mg)
    g_prob = eg / jnp.sum(eg, axis=1, keepdims=True)
    g_p = jnp.max(g_prob, axis=1, keepdims=True)
    g_idx = jnp.min(jnp.where(g_prob == g_p, lane, LANES), axis=1, keepdims=True)
    lo_col = ROUTER_COL0 + EXPERTS_PER_GROUP * g_idx
    in_e = (lane >= lo_col) & (lane < lo_col + EXPERTS_PER_GROUP)
    le = jnp.where(in_e, logits, neg_inf)
    me = jnp.max(le, axis=1, keepdims=True)
    ee = jnp.exp(le - me)
    e_prob = jnp.where(in_e, ee / jnp.sum(ee, axis=1, keepdims=True), -1.0)
    p1 = jnp.max(e_prob, axis=1, keepdims=True)
    i1 = jnp.min(jnp.where(e_prob == p1, lane, LANES), axis=1, keepdims=True)
    rest = jnp.where(lane == i1, -1.0, e_prob)
    p2 = jnp.max(rest, axis=1, keepdims=True)
    i2 = jnp.min(jnp.where(rest == p2, lane, LANES), axis=1, keepdims=True)
    denom = p1 + p2
    ea = jnp.minimum(i1, i2) - lo_col
    eb = jnp.maximum(i1, i2) - lo_col
    pair = lax.shift_right_logical(ea * (2 * EXPERTS_PER_GROUP - 1 - ea), 1) + eb - ea - 1
    cls = g_idx * PAIRS_PER_GROUP + pair
    routing = (jnp.where(lane == i1, g_p * (p1 / denom), 0.0)
               + jnp.where(lane == i2, g_p * (p2 / denom), 0.0)
               + jnp.where(lane == CLASS_LANE, cls.astype(F32), 0.0))
    tok_ref[pl.ds(blk0 + D_ROWS, tm, stride=TOK_ROWS), :] = routing
    cls_ref[:, row0:row0 + tm] = routing.T[CLASS_LANE:CLASS_LANE + 1, :].astype(jnp.int32)


def _merge(x2, y_sg, y_att, proj, mod, norm2_w, w_sg, w_att, w_o, w_router, seq):
    n, d = x2.shape
    tm = MERGE_TM
    tpb = seq // tm
    resident = lambda shape: pl.BlockSpec(shape, lambda i: (0, 0), pipeline_mode=pl.Buffered(1))
    return pl.pallas_call(
        _merge_kernel,
        grid=(n // tm,),
        in_specs=[pl.BlockSpec((tm, d), lambda i: (i, 0)),
                  pl.BlockSpec((tm, D_SG), lambda i: (i, 0)),
                  pl.BlockSpec((tm, D_ATT), lambda i: (i, 0)),
                  pl.BlockSpec((tm, d), lambda i: (i, BLK_GATE_SG)),
                  pl.BlockSpec((tm, d), lambda i: (i, BLK_GATE_ATT)),
                  pl.BlockSpec((1, 8, d), lambda i: (i // tpb, 0, 0)),
                  pl.BlockSpec((1, d), lambda i: (0, 0)),
                  resident((D_SG, d)), resident((D_ATT, d)), resident((d, d)),
                  resident((d, LANES))],
        out_specs=[pl.BlockSpec((tm, d), lambda i: (i, 0)),
                   pl.BlockSpec((tm * TOK_ROWS, LANES), lambda i: (i, 0)),
                   pl.BlockSpec((None, 1, tm), lambda i: (i, 0, 0))],
        out_shape=[jax.ShapeDtypeStruct((n, d), F32),
                   jax.ShapeDtypeStruct((n * TOK_ROWS, LANES), F32),
                   jax.ShapeDtypeStruct((n // tm, 1, tm), jnp.int32)],
        compiler_params=_cparams(("parallel",)),
        name="merge",
    )(x2, y_sg, y_att, proj, proj, mod, norm2_w, w_sg, w_att, w_o, w_router)


def _invert_kernel(pos_ref, free_ref, inv_ref, *, n_tok, n_free):
    def place(t, carry):
        inv_ref[pos_ref[t]] = t
        return carry

    def place_free(k, carry):
        inv_ref[free_ref[k]] = n_tok + k
        return carry

    lax.fori_loop(0, n_tok, place, 0, unroll=8)
    lax.fori_loop(0, n_free, place_free, 0, unroll=8)


def _invert(pos, free):
    n_tok, n_free = pos.shape[0], free.shape[0]
    return pl.pallas_call(
        functools.partial(_invert_kernel, n_tok=n_tok, n_free=n_free),
        grid_spec=pltpu.PrefetchScalarGridSpec(
            num_scalar_prefetch=2, grid=(1,), in_specs=[],
            out_specs=pl.BlockSpec(memory_space=pltpu.SMEM)),
        out_shape=jax.ShapeDtypeStruct((n_tok + n_free,), jnp.int32),
        compiler_params=_cparams(("arbitrary",)),
        name="invert",
    )(pos, free)


MOE_TILE = 256
ROW_UNROLL_LOG2 = 3
ROW_UNROLL = 1 << ROW_UNROLL_LOG2


def _for_rows(rows, fn):
    groups = lax.shift_right_logical(rows, ROW_UNROLL_LOG2)

    def group(g, carry):
        for u in range(ROW_UNROLL):
            fn(g * ROW_UNROLL + u)
        return carry

    def single(r, carry):
        fn(r)
        return carry

    lax.fori_loop(0, groups, group, 0)
    lax.fori_loop(groups * ROW_UNROLL, rows, single, 0)


def _moe_kernel(src_ref, dst_ref, ea_ref, eb_ref, rows_ref, na_ref, tok_hbm,
                w1a_ref, w3a_ref, w2a_ref, w1b_ref, w3b_ref, w2b_ref, y_hbm,
                gbuf, stage, acc_ref, gsem, ssem):
    i = pl.program_id(0)
    n_act = na_ref[0]
    slot = lax.rem(i, 2)

    def gather_start(tile, sl):
        def one(r, carry):
            row0 = pl.multiple_of(r * TOK_ROWS, 8)
            pltpu.make_async_copy(tok_hbm.at[src_ref[tile * MOE_TILE + r]],
                                  gbuf.at[sl, pl.ds(row0, TOK_ROWS), :], gsem.at[sl]).start()
            return carry
        lax.fori_loop(0, MOE_TILE, one, 0, unroll=8)

    def gather_wait(sl):
        for _ in range(MOE_TILE):
            pltpu.make_async_copy(tok_hbm.at[0], gbuf.at[sl, pl.ds(0, TOK_ROWS), :], gsem.at[sl]).wait()

    def scatter_start(tile, sl):
        def one(r):
            row0 = pl.multiple_of(r * D_ROWS, 8)
            pltpu.make_async_copy(stage.at[sl, pl.ds(row0, D_ROWS), :],
                                  y_hbm.at[dst_ref[tile * MOE_TILE + r]], ssem.at[sl]).start()
        _for_rows(rows_ref[tile], one)

    def scatter_wait(tile, sl):
        def one(r):
            pltpu.make_async_copy(stage.at[sl, pl.ds(0, D_ROWS), :], y_hbm.at[0], ssem.at[sl]).wait()
        _for_rows(rows_ref[tile], one)

    @pl.when(i < n_act)
    def _():
        @pl.when(i == 0)
        def _():
            gather_start(0, 0)

        @pl.when(i + 1 < n_act)
        def _():
            gather_start(i + 1, 1 - slot)

        gather_wait(slot)
        t = jnp.concatenate([gbuf[slot, pl.ds(a, MOE_TILE, stride=TOK_ROWS), :] for a in range(D_ROWS)],
                            axis=1).astype(BF16)
        routing = gbuf[slot, pl.ds(D_ROWS, MOE_TILE, stride=TOK_ROWS), :]
        lane = lax.broadcasted_iota(jnp.int32, routing.shape, 1)
        experts = ((ea_ref[i], w1a_ref, w3a_ref, w2a_ref), (eb_ref[i], w1b_ref, w3b_ref, w2b_ref))
        ups = [(jnp.dot(t, w1_ref[...], preferred_element_type=F32),
                jnp.dot(t, w3_ref[...], preferred_element_type=F32)) for _, w1_ref, w3_ref, _ in experts]
        hmids = []
        for (e, _, _, _), (a, b) in zip(experts, ups):
            ge = jnp.sum(jnp.where(lane == ROUTER_COL0 + e, routing, 0.0), axis=1, keepdims=True)
            hmids.append(((a * jax.nn.sigmoid(a)) * b * ge).astype(BF16))
        acc_ref[...] = (jnp.dot(hmids[0], w2a_ref[...], preferred_element_type=F32)
                        + jnp.dot(hmids[1], w2b_ref[...], preferred_element_type=F32))

        @pl.when(i >= 2)
        def _():
            scatter_wait(i - 2, slot)

        for a in range(D_ROWS):
            stage[slot, pl.ds(a, MOE_TILE, stride=D_ROWS), :] = acc_ref[:, a * LANES:(a + 1) * LANES]
        scatter_start(i, slot)

        @pl.when(i == n_act - 1)
        def _():
            scatter_wait(i, slot)

            @pl.when(i >= 1)
            def _():
                scatter_wait(i - 1, 1 - slot)


def _moe(src_tok, dst_row, tile_ea, tile_eb, tile_rows, n_active, tok_blocks, w1, w3, w2, n_tok):
    n_slots = src_tok.shape[0]
    d = D_MODEL
    blk_a = lambda i, src, dst, ea, eb, rows, na: (ea[i], 0, 0)
    blk_b = lambda i, src, dst, ea, eb, rows, na: (eb[i], 0, 0)
    up = lambda blk: pl.BlockSpec((None, d, D_EXPERT), blk)
    down = lambda blk: pl.BlockSpec((None, D_EXPERT, d), blk)
    return pl.pallas_call(
        _moe_kernel,
        grid_spec=pltpu.PrefetchScalarGridSpec(
            num_scalar_prefetch=6, grid=(n_slots // MOE_TILE,),
            in_specs=[pl.BlockSpec(memory_space=pl.ANY),
                      up(blk_a), up(blk_a), down(blk_a), up(blk_b), up(blk_b), down(blk_b)],
            out_specs=pl.BlockSpec(memory_space=pl.ANY),
            scratch_shapes=[pltpu.VMEM((2, MOE_TILE * TOK_ROWS, LANES), F32),
                            pltpu.VMEM((2, MOE_TILE * D_ROWS, LANES), F32),
                            pltpu.VMEM((MOE_TILE, d), F32),
                            pltpu.SemaphoreType.DMA((2,)),
                            pltpu.SemaphoreType.DMA((2,))]),
        out_shape=jax.ShapeDtypeStruct((n_tok, D_ROWS, LANES), F32),
        compiler_params=_cparams(("arbitrary",)),
        name="moe",
    )(src_tok, dst_row, tile_ea, tile_eb, tile_rows, n_active, tok_blocks, w1, w3, w2, w1, w3, w2)


def _final_kernel(x1_ref, y_ref, mod_ref, fw_ref, o_ref):
    tm = x1_ref.shape[0]
    parts = []
    ss = jnp.zeros((tm, 1), F32)
    for a in range(D_ROWS):
        cols = slice(a * LANES, (a + 1) * LANES)
        part = x1_ref[:, cols] + mod_ref[0, 5:6, cols] * y_ref[pl.ds(a, tm, stride=D_ROWS), :]
        ss = ss + jnp.sum(part * part, axis=1, keepdims=True)
        parts.append(part)
    inv = lax.rsqrt(ss * (1.0 / D_MODEL) + EPS)
    for a in range(D_ROWS):
        cols = slice(a * LANES, (a + 1) * LANES)
        o_ref[:, cols] = parts[a] * inv * fw_ref[:, cols]


def _final(x1, y_blocks, mod, final_w, seq):
    n, d = x1.shape
    tm = FINAL_TM
    tpb = seq // tm
    return pl.pallas_call(
        _final_kernel,
        grid=(n // tm,),
        in_specs=[pl.BlockSpec((tm, d), lambda i: (i, 0)),
                  pl.BlockSpec((tm * D_ROWS, LANES), lambda i: (i, 0)),
                  pl.BlockSpec((1, 8, d), lambda i: (i // tpb, 0, 0)),
                  pl.BlockSpec((1, d), lambda i: (0, 0))],
        out_specs=pl.BlockSpec((tm, d), lambda i: (i, 0)),
        out_shape=jax.ShapeDtypeStruct((n, d), F32),
        compiler_params=_cparams(("parallel",)),
        name="final",
    )(x1, y_blocks, mod, final_w)


def _class_sort_plan(cls, n):
    class_ids = jnp.arange(N_CLASSES, dtype=jnp.int32)
    onehot = (cls[:, None] == class_ids[None, :]).astype(jnp.int32)
    blk = LANES
    within = jnp.einsum("ts,bsg->btg", jnp.tril(jnp.ones((blk, blk), F32)),
                        onehot.astype(F32).reshape(n // blk, blk, N_CLASSES))
    before = jnp.tril(jnp.ones((n // blk, n // blk), F32), -1) @ within[:, -1, :]
    incl = (within + before[:, None, :]).reshape(n, N_CLASSES).astype(jnp.int32)
    count = incl[-1]
    tiles = (count + MOE_TILE - 1) // MOE_TILE
    tile_end = jnp.cumsum(tiles)
    start = (tile_end - tiles) * MOE_TILE
    rank = jnp.sum(incl * onehot, axis=1) - 1
    pos = jnp.sum(onehot * start[None, :], axis=1) + rank
    n_tiles = n // MOE_TILE + N_CLASSES
    tile_ids = jnp.arange(n_tiles, dtype=jnp.int32)
    tile_class = jnp.sum((tile_ids[:, None] >= tile_end[None, :]).astype(jnp.int32), axis=1)
    tile_class = jnp.minimum(tile_class, N_CLASSES - 1)
    tile_hot = (tile_class[:, None] == class_ids[None, :]).astype(jnp.int32)
    done = (tile_ids - jnp.sum(tile_hot * (tile_end - tiles)[None, :], axis=1)) * MOE_TILE
    tile_rows = jnp.clip(jnp.sum(tile_hot * count[None, :], axis=1) - done, 0, MOE_TILE)
    first = jnp.asarray([a for a, _ in PAIRS], jnp.int32)
    second = jnp.asarray([b for _, b in PAIRS], jnp.int32)
    group0 = (class_ids // PAIRS_PER_GROUP) * EXPERTS_PER_GROUP
    tile_ea = jnp.sum(tile_hot * (group0 + first[class_ids % PAIRS_PER_GROUP])[None, :], axis=1)
    tile_eb = jnp.sum(tile_hot * (group0 + second[class_ids % PAIRS_PER_GROUP])[None, :], axis=1)
    pad = tiles * MOE_TILE - count
    pad_end = jnp.cumsum(pad)
    k = jnp.arange(N_CLASSES * MOE_TILE, dtype=jnp.int32)
    seg = jnp.sum((k[:, None] >= pad_end[None, :]).astype(jnp.int32), axis=1)
    seg_hot = (seg[:, None] == class_ids[None, :]).astype(jnp.int32)
    in_class = jnp.sum(seg_hot * (start + count - (pad_end - pad))[None, :], axis=1) + k
    tail = tile_end[-1] * MOE_TILE + k - pad_end[-1]
    free = jnp.where(seg < N_CLASSES, in_class, tail)
    i32 = lambda a: a.astype(jnp.int32)
    return i32(pos), i32(free), i32(tile_ea), i32(tile_eb), i32(tile_rows), i32(tile_end[-1:])


def kernel(x, c, w_ada, b_ada, norm1_w, norm2_w, final_norm_w, w_in, sg_ln_w, sg_ln_b, w_spatial,
           b_spatial, rel_bias, w_out_sg, w_out_att, w_o, w_router_group, w_router_expert,
           w_exp_gate, w_exp_up, w_exp_down):
    batch, seq, d = x.shape
    assert d == D_MODEL and w_ada.shape[0] == 1
    assert seq % 1024 == 0 and seq // MOBA_BLOCK <= LANES
    n = batch * seq
    x2 = x.reshape(n, d)

    assert batch <= BF16_SUBLANES
    c_pad = jnp.zeros((BF16_SUBLANES, d), F32).at[:batch].set(c)
    mod = _adaln(c_pad, w_ada[0], b_ada[0].reshape(1, 6 * d))
    mod = mod[:batch].reshape(batch, 6, d)
    mod = jnp.concatenate([mod, jnp.zeros((batch, 2, d), F32)], axis=1)

    colscale = jnp.ones((1, IN_COLS), F32).at[:, CB_Q * LANES:CB_K * LANES].set(HEAD_DIM ** -0.5 * LOG2E)
    proj = _proj(x2, mod, norm1_w[0].reshape(1, d), w_in[0], colscale, seq)

    y_sg = _sgu(proj, sg_ln_w[0].reshape(1, D_SG), sg_ln_b[0].reshape(1, D_SG),
                w_spatial[0], b_spatial[0].T)

    qt, kx, vt, kmean = _moba_prep(proj, batch, seq)
    bias_tiles = _bias_tiles(rel_bias)
    y_att = _moba(rel_bias, qt, kx, vt, kmean, bias_tiles, batch, seq)

    w_router = jnp.zeros((d, LANES), F32)
    w_router = w_router.at[:, :N_GROUPS].set(w_router_group[0])
    w_router = w_router.at[:, ROUTER_COL0:ROUTER_COL0 + N_EXPERTS].set(w_router_expert[0])
    x1, tok, cls = _merge(x2, y_sg, y_att, proj, mod, norm2_w[0].reshape(1, d),
                          w_out_sg[0].astype(BF16), w_out_att[0].astype(BF16),
                          w_o[0].astype(BF16), w_router, seq)

    tok = tok.reshape(n, TOK_ROWS, LANES)
    pos, free, tile_ea, tile_eb, tile_rows, n_active = _class_sort_plan(cls.reshape(n), n)
    inv = _invert(pos, free)
    src_tok = jnp.where(inv < n, inv, 0)
    y = _moe(src_tok, inv, tile_ea, tile_eb, tile_rows, n_active, tok, w_exp_gate[0].astype(BF16),
             w_exp_up[0].astype(BF16), w_exp_down[0].astype(BF16), n)
    out = _final(x1, y.reshape(-1, LANES), mod, final_norm_w.reshape(1, d), seq)
    return out.reshape(batch, seq, d)
```

```python
import functools
import math

import jax
import jax.numpy as jnp
from jax import lax
from jax.experimental import pallas as pl
from jax.experimental.pallas import tpu as pltpu

F32 = jnp.float32
BF16 = jnp.bfloat16

LANES = 128
BF16_SUBLANES = 16
D_MODEL = 2048
D_SG = D_MODEL // 2
SG_GROUPS = 8
SG_CHUNK = 128
ATT_HEADS = 8
HEAD_DIM = 128
D_ATT = ATT_HEADS * HEAD_DIM
MOBA_BLOCK = 256
MOBA_TOPK = 3
REL_BUCKETS = 32
REL_MAX_DIST = 128
N_GROUPS = 4
EXPERTS_PER_GROUP = 4
N_EXPERTS = N_GROUPS * EXPERTS_PER_GROUP
D_EXPERT = 512
EPS = 1e-6
IN_COLS = 2 * D_SG + 3 * D_ATT + 2 * D_MODEL

N_GATE_COLS = 2 * D_MODEL
BLK_GATE_SG = 0
BLK_GATE_ATT = 1
BLK_U = N_GATE_COLS // D_SG
BLK_V = BLK_U + 1
CB_Q = (N_GATE_COLS + 2 * D_SG) // LANES
CB_K = CB_Q + ATT_HEADS
CB_VAL = CB_K + ATT_HEADS

MASK_NEG = -1e9
LOG2E = math.log2(math.e)
ROUTER_COL0 = N_GROUPS
CLASS_LANE = 0
PAIRS_PER_GROUP = EXPERTS_PER_GROUP * (EXPERTS_PER_GROUP - 1) // 2
N_CLASSES = N_GROUPS * PAIRS_PER_GROUP
PAIRS = [(a, b) for a in range(EXPERTS_PER_GROUP) for b in range(a + 1, EXPERTS_PER_GROUP)]
D_ROWS = D_MODEL // LANES
TOK_ROWS = D_ROWS + 8
VMEM_LIMIT = 56 * 1024 * 1024

ADALN_TN = 1024
PROJ_TM = 1024
PROJ_TN = 1024
PROJ_CHUNK = 256
SGU_CHUNKS = 4
MERGE_TM = 256
FINAL_TM = 256


def _cparams(sem):
    return pltpu.CompilerParams(dimension_semantics=sem, vmem_limit_bytes=VMEM_LIMIT)


def _split_bf16(a):
    hi = a.astype(BF16)
    lo = (a - hi.astype(F32)).astype(BF16)
    return hi, lo


def _adaln_kernel(c_ref, w_ref, b_ref, o_ref):
    c = c_ref[...]
    rows = c.shape[0]
    c_hi, c_lo = _split_bf16(c * jax.nn.sigmoid(c))
    w_hi, w_lo = _split_bf16(w_ref[...])
    both = jnp.dot(jnp.concatenate([c_hi, c_lo], axis=0), w_hi, preferred_element_type=F32)
    o_ref[...] = (both[:rows] + both[rows:] + jnp.dot(c_hi, w_lo, preferred_element_type=F32)
                  + b_ref[...])


def _adaln(c_pad, w, b):
    rows, d = c_pad.shape
    cols = w.shape[1]
    tn = ADALN_TN
    return pl.pallas_call(
        _adaln_kernel,
        grid=(cols // tn,),
        in_specs=[pl.BlockSpec((rows, d), lambda j: (0, 0)),
                  pl.BlockSpec((d, tn), lambda j: (0, j)),
                  pl.BlockSpec((1, tn), lambda j: (0, j))],
        out_specs=pl.BlockSpec((rows, tn), lambda j: (0, j)),
        out_shape=jax.ShapeDtypeStruct((rows, cols), F32),
        compiler_params=_cparams(("arbitrary",)),
        name="adaln",
    )(c_pad, w, b)


def _proj_kernel(x_ref, mod_ref, nw_ref, w_ref, cs_ref, o_ref, h_scr):
    @pl.when(pl.program_id(1) == 0)
    def _():
        w = w_ref[...].astype(BF16)
        scale = nw_ref[...] * (1.0 + mod_ref[0, 1:2, :])
        for k in range(x_ref.shape[0] // PROJ_CHUNK):
            rows = slice(k * PROJ_CHUNK, (k + 1) * PROJ_CHUNK)
            x = x_ref[rows, :]
            ms = jnp.mean(x * x, axis=-1, keepdims=True)
            h = (x * lax.rsqrt(ms + EPS) * scale + mod_ref[0, 0:1, :]).astype(BF16)
            h_scr[rows, :] = h
            o_ref[rows, :] = (jnp.dot(h, w, preferred_element_type=F32) * cs_ref[...]).astype(BF16)

    @pl.when(pl.program_id(1) > 0)
    def _():
        acc = jnp.dot(h_scr[...], w_ref[...].astype(BF16), preferred_element_type=F32)
        o_ref[...] = (acc * cs_ref[...]).astype(BF16)


def _proj(x2, mod, norm_w, w_in, colscale, seq):
    n, d = x2.shape
    cols = w_in.shape[1]
    tm = min(PROJ_TM, seq)
    tn = PROJ_TN
    tpb = seq // tm
    n_cb = cols // tn
    first = (cols - N_GATE_COLS) // tn
    assert (cols - N_GATE_COLS) % tn == 0
    return pl.pallas_call(
        _proj_kernel,
        grid=(n // tm, n_cb),
        in_specs=[pl.BlockSpec((tm, d), lambda i, j: (i, 0)),
                  pl.BlockSpec((1, 8, d), lambda i, j: (i // tpb, 0, 0)),
                  pl.BlockSpec((1, d), lambda i, j: (0, 0)),
                  pl.BlockSpec((d, tn), lambda i, j: (0, (j + first) % n_cb)),
                  pl.BlockSpec((1, tn), lambda i, j: (0, j))],
        out_specs=pl.BlockSpec((tm, tn), lambda i, j: (i, j)),
        out_shape=jax.ShapeDtypeStruct((n, cols), BF16),
        scratch_shapes=[pltpu.VMEM((tm, d), BF16)],
        compiler_params=_cparams(("parallel", "arbitrary")),
        name="proj",
    )(x2, mod, norm_w, w_in, colscale)


def _gelu(a):
    return 0.5 * a * (1.0 + lax.erf(a * (1.0 / math.sqrt(2.0))))


def _sgu_kernel(u_ref, v_ref, lnw_ref, lnb_ref, ws_ref, bst_ref, o_ref, *, chunks):
    u = _gelu(u_ref[...].astype(F32))
    v = _gelu(v_ref[...].astype(F32))
    mu = jnp.mean(v, axis=-1, keepdims=True)
    vc = v - mu
    var = jnp.mean(vc * vc, axis=-1, keepdims=True)
    vn = (vc * lax.rsqrt(var + EPS) * lnw_ref[...] + lnb_ref[...]).astype(BF16)
    row = lax.broadcasted_iota(jnp.int32, (SG_CHUNK, SG_CHUNK), 0)
    col = lax.broadcasted_iota(jnp.int32, (SG_CHUNK, SG_CHUNK), 1)
    causal = col <= row
    for g in range(SG_GROUPS):
        wm = jnp.where(causal, ws_ref[g], 0.0).astype(BF16)
        bcol = bst_ref[:, g:g + 1]
        gs = slice(g * LANES, (g + 1) * LANES)
        for c in range(chunks):
            rs = slice(c * SG_CHUNK, (c + 1) * SG_CHUNK)
            z = jnp.dot(wm, vn[rs, gs], preferred_element_type=F32) + bcol
            o_ref[rs, gs] = (u[rs, gs] * z).astype(BF16)


def _sgu(proj, ln_w, ln_b, w_s, b_s_t):
    n = proj.shape[0]
    chunks = SGU_CHUNKS
    tm = chunks * SG_CHUNK
    return pl.pallas_call(
        functools.partial(_sgu_kernel, chunks=chunks),
        grid=(n // tm,),
        in_specs=[pl.BlockSpec((tm, D_SG), lambda i: (i, BLK_U)),
                  pl.BlockSpec((tm, D_SG), lambda i: (i, BLK_V)),
                  pl.BlockSpec((1, D_SG), lambda i: (0, 0)),
                  pl.BlockSpec((1, D_SG), lambda i: (0, 0)),
                  pl.BlockSpec((SG_GROUPS, SG_CHUNK, SG_CHUNK), lambda i: (0, 0, 0)),
                  pl.BlockSpec((SG_CHUNK, SG_GROUPS), lambda i: (0, 0))],
        out_specs=pl.BlockSpec((tm, D_SG), lambda i: (i, 0)),
        out_shape=jax.ShapeDtypeStruct((n, D_SG), BF16),
        compiler_params=_cparams(("parallel",)),
        name="sgu",
    )(proj, proj, ln_w, ln_b, w_s, b_s_t)


VT_ROWS = HEAD_DIM + 16
EXT_LO = REL_BUCKETS


def _moba_prep_kernel(q_ref, k_ref, v_ref, qt_ref, kx_ref, vt_ref, km_ref, *, nb):
    lane = lax.broadcasted_iota(jnp.int32, (MOBA_BLOCK, LANES), 1)
    pad = jnp.concatenate([jnp.ones((1, MOBA_BLOCK), F32),
                           jnp.zeros((VT_ROWS - HEAD_DIM - 1, MOBA_BLOCK), F32)], axis=0)
    for j in range(nb):
        rows = slice(j * MOBA_BLOCK, (j + 1) * MOBA_BLOCK)
        kj = k_ref[rows, :]
        onehot = jnp.where((lane == j) | (lane == j + EXT_LO), 1.0, 0.0).astype(BF16)
        kx_ref[rows, :] = jnp.concatenate([kj, onehot], axis=1)
        km_ref[j:j + 1, :] = jnp.mean(kj.astype(F32), axis=0, keepdims=True)
        vt = v_ref[rows, :].astype(F32).T
        vt_ref[j] = jnp.concatenate([vt, pad], axis=0).astype(BF16)
        qt_ref[j] = q_ref[rows, :].astype(F32).T.astype(BF16)


def _moba_prep(proj, batch, seq):
    nb = seq // MOBA_BLOCK
    bh = lambda shape: pl.BlockSpec((None, None) + shape, lambda b, h: (b, h) + (0,) * len(shape))
    return pl.pallas_call(
        functools.partial(_moba_prep_kernel, nb=nb),
        grid=(batch, ATT_HEADS),
        in_specs=[pl.BlockSpec((seq, HEAD_DIM), lambda b, h: (b, CB_Q + h)),
                  pl.BlockSpec((seq, HEAD_DIM), lambda b, h: (b, CB_K + h)),
                  pl.BlockSpec((seq, HEAD_DIM), lambda b, h: (b, CB_VAL + h))],
        out_specs=[bh((nb, HEAD_DIM, MOBA_BLOCK)), bh((seq, 2 * HEAD_DIM)),
                   bh((nb, VT_ROWS, MOBA_BLOCK)), bh((nb, HEAD_DIM))],
        out_shape=[jax.ShapeDtypeStruct((batch, ATT_HEADS, nb, HEAD_DIM, MOBA_BLOCK), BF16),
                   jax.ShapeDtypeStruct((batch, ATT_HEADS, seq, 2 * HEAD_DIM), BF16),
                   jax.ShapeDtypeStruct((batch, ATT_HEADS, nb, VT_ROWS, MOBA_BLOCK), BF16),
                   jax.ShapeDtypeStruct((batch, ATT_HEADS, nb, HEAD_DIM), F32)],
        compiler_params=_cparams(("parallel", "parallel")),
        name="moba_prep",
    )(proj, proj, proj)


def _bias_tiles_kernel(rb_ref, o_ref):
    h = pl.program_id(0)
    kj = lax.broadcasted_iota(jnp.int32, (MOBA_BLOCK, MOBA_BLOCK), 0)
    qi = lax.broadcasted_iota(jnp.int32, (MOBA_BLOCK, MOBA_BLOCK), 1)
    max_exact = REL_BUCKETS // 2
    for t in range(2):
        rel = qi - kj + MOBA_BLOCK * t
        n = jnp.maximum(rel, 0)
        nf = jnp.maximum(n, max_exact).astype(F32)
        large = max_exact + (jnp.log(nf / max_exact) / math.log(REL_MAX_DIST / max_exact)
                             * (REL_BUCKETS - max_exact)).astype(jnp.int32)
        large = jnp.minimum(large, REL_BUCKETS - 1)
        bucket = jnp.where(n < max_exact, n, large)
        bias = jnp.zeros((MOBA_BLOCK, MOBA_BLOCK), F32)
        for r in range(REL_BUCKETS):
            bias = jnp.where(bucket == r, rb_ref[r, h], bias)
        o_ref[t] = jnp.where(rel >= 0, bias * LOG2E, MASK_NEG)


def _bias_tiles(rel_bias):
    return pl.pallas_call(
        _bias_tiles_kernel,
        grid=(ATT_HEADS,),
        in_specs=[pl.BlockSpec(memory_space=pltpu.SMEM)],
        out_specs=pl.BlockSpec((None, 2, MOBA_BLOCK, MOBA_BLOCK), lambda h: (h, 0, 0, 0)),
        out_shape=jax.ShapeDtypeStruct((ATT_HEADS, 2, MOBA_BLOCK, MOBA_BLOCK), F32),
        compiler_params=_cparams(("arbitrary",)),
        name="bias_tiles",
    )(rel_bias)


FAR_BLOCKS = 4
MAX_LOG2_RISE = 64.0
HEADS_PER_STEP = 4


def _moba_kernel(rb_ref, qt_ref, kx_ref, vt_ref, km_ref, bias_ref, o_ref, *, nb):
    i = pl.program_id(2)
    jp = jnp.maximum(i - 1, 0)
    prev0 = pl.multiple_of(jp * MOBA_BLOCK, MOBA_BLOCK)
    own0 = pl.multiple_of(i * MOBA_BLOCK, MOBA_BLOCK)
    no_prev = jnp.where(i > 0, 0.0, MASK_NEG)
    neg_inf = jnp.float32(-jnp.inf)

    heads = range(HEADS_PER_STEP)

    def choose_blocks(hh):
        km_hi, km_lo = _split_bf16(km_ref[hh])
        score = (jnp.dot(km_hi, qt_ref[hh], preferred_element_type=F32)
                 + jnp.dot(km_lo, qt_ref[hh], preferred_element_type=F32))
        bid = lax.broadcasted_iota(jnp.int32, score.shape, 0)
        score = jnp.where(bid < i, score, neg_inf)
        chosen = bid < 0
        for _ in range(MOBA_TOPK):
            mx = jnp.max(score, axis=0, keepdims=True)
            hit = (score == mx) & (mx > neg_inf)
            idx = jnp.min(jnp.where(hit, bid, nb), axis=0, keepdims=True)
            pick = bid == idx
            chosen = chosen | pick
            score = jnp.where(pick, neg_inf, score)
        return chosen

    def with_mask(hh, val):
        hi, lo = _split_bf16(val)
        fill = jnp.zeros((HEAD_DIM - 2 * EXT_LO, MOBA_BLOCK), BF16)
        if nb < EXT_LO:
            gap = jnp.zeros((EXT_LO - nb, MOBA_BLOCK), BF16)
            return jnp.concatenate([qt_ref[hh], hi, gap, lo, gap, fill], axis=0)
        return jnp.concatenate([qt_ref[hh], hi, lo, fill], axis=0)

    chosen = [choose_blocks(hh) for hh in heads]
    bid = lax.broadcasted_iota(jnp.int32, chosen[0].shape, 0)
    far_bias = [rb_ref[REL_BUCKETS - 1, pl.program_id(1) * HEADS_PER_STEP + hh] * LOG2E for hh in heads]
    qx_fars = [with_mask(hh, jnp.where(chosen[hh] & (bid <= i - 2), far_bias[hh], MASK_NEG)) for hh in heads]
    qx_nears = [with_mask(hh, jnp.where((bid == i) | (chosen[hh] & (bid == i - 1)), 0.0, MASK_NEG))
                for hh in heads]

    ss = [jnp.dot(jnp.concatenate([kx_ref[hh, pl.ds(prev0, MOBA_BLOCK), :],
                                   kx_ref[hh, pl.ds(own0, MOBA_BLOCK), :]], axis=0),
                  qx_nears[hh], preferred_element_type=F32)
          + jnp.concatenate([bias_ref[hh, 1] + no_prev, bias_ref[hh, 0]], axis=0) for hh in heads]
    ms = [jnp.max(ss[hh], axis=0, keepdims=True) for hh in heads]
    ps = [jnp.exp2(ss[hh] - ms[hh]).astype(BF16) for hh in heads]
    accs = [jnp.dot(jnp.concatenate([vt_ref[hh, jp], vt_ref[hh, i]], axis=1), ps[hh],
                    preferred_element_type=F32) for hh in heads]

    def far_step(c, carry):
        row0 = pl.multiple_of(c * (FAR_BLOCKS * MOBA_BLOCK), FAR_BLOCKS * MOBA_BLOCK)
        m_olds = [carry[hh][0] for hh in heads]
        accs = [carry[hh][1] for hh in heads]

        def logits(hh):
            return jnp.dot(kx_ref[hh, pl.ds(row0, FAR_BLOCKS * MOBA_BLOCK), :], qx_fars[hh],
                           preferred_element_type=F32)

        def values(hh):
            return jnp.concatenate([vt_ref[hh, c * FAR_BLOCKS + k] for k in range(FAR_BLOCKS)], axis=1)

        ss = [logits(hh) for hh in heads]
        ps = [jnp.exp2(ss[hh] - m_olds[hh]).astype(BF16) for hh in heads]
        m_news = [jnp.maximum(m_olds[hh], jnp.max(ss[hh], axis=0, keepdims=True)) for hh in heads]
        alphas = [jnp.exp2(m_olds[hh] - m_news[hh]) for hh in heads]
        usual = tuple((m_news[hh], alphas[hh] * (accs[hh] + jnp.dot(values(hh), ps[hh],
                                                                    preferred_element_type=F32)))
                      for hh in heads)
        rise = functools.reduce(jnp.maximum, [jnp.max(m_news[hh] - m_olds[hh]) for hh in heads])

        def redo():
            out = []
            for hh in heads:
                p = jnp.exp2(logits(hh) - m_news[hh]).astype(BF16)
                out.append((m_news[hh], alphas[hh] * accs[hh]
                            + jnp.dot(values(hh), p, preferred_element_type=F32)))
            return tuple(out)

        return lax.cond(rise > MAX_LOG2_RISE, redo, lambda: usual)

    n_far = lax.shift_right_logical(jnp.maximum(i - 1, 0) + FAR_BLOCKS - 1, 2)
    final = lax.fori_loop(0, n_far, far_step, tuple((ms[hh], accs[hh]) for hh in heads))
    for hh in heads:
        acc = final[hh][1]
        out = acc[0:HEAD_DIM] / acc[HEAD_DIM:HEAD_DIM + 1]
        o_ref[:, hh * HEAD_DIM:(hh + 1) * HEAD_DIM] = out.T.astype(BF16)


def _moba(rel_bias, qt, kx, vt, kmean, bias_tiles, batch, seq):
    nb = seq // MOBA_BLOCK
    hps = HEADS_PER_STEP
    assert nb % FAR_BLOCKS == 0 and FAR_BLOCKS == 4 and ATT_HEADS % hps == 0
    bh = lambda shape: pl.BlockSpec((None, hps) + shape, lambda b, h, i: (b, h) + (0,) * len(shape),
                                    pipeline_mode=pl.Buffered(1))
    return pl.pallas_call(
        functools.partial(_moba_kernel, nb=nb),
        grid=(batch, ATT_HEADS // hps, nb),
        in_specs=[pl.BlockSpec(memory_space=pltpu.SMEM),
                  pl.BlockSpec((None, hps, None, HEAD_DIM, MOBA_BLOCK), lambda b, h, i: (b, h, i, 0, 0)),
                  bh((seq, 2 * HEAD_DIM)), bh((nb, VT_ROWS, MOBA_BLOCK)), bh((nb, HEAD_DIM)),
                  pl.BlockSpec((hps, 2, MOBA_BLOCK, MOBA_BLOCK), lambda b, h, i: (h, 0, 0, 0))],
        out_specs=pl.BlockSpec((MOBA_BLOCK, hps * HEAD_DIM), lambda b, h, i: (b * nb + i, h)),
        out_shape=jax.ShapeDtypeStruct((batch * seq, D_ATT), BF16),
        compiler_params=_cparams(("parallel", "parallel", "arbitrary")),
        name="moba",
    )(rel_bias, qt, kx, vt, kmean, bias_tiles)


MERGE_PARTS = 1


def _merge_kernel(x_ref, ysg_ref, yatt_ref, gsg_ref, gatt_ref, mod_ref, nw_ref,
                  wsg_ref, watt_ref, wo_ref, wr_ref, x1_ref, tok_ref, cls_ref):
    tm = x_ref.shape[0]
    part = tm // MERGE_PARTS
    parts = [slice(k * part, (k + 1) * part) for k in range(MERGE_PARTS)]
    w_hi, w_lo = _split_bf16(wr_ref[...])
    for k, rs in enumerate(parts):
        a_sg = jnp.dot(ysg_ref[rs, :], wsg_ref[...], preferred_element_type=F32)
        a_att = jnp.dot(yatt_ref[rs, :], watt_ref[...], preferred_element_type=F32)
        merged = (jax.nn.sigmoid(gsg_ref[rs, :].astype(F32)) * a_sg
                  + jax.nn.sigmoid(gatt_ref[rs, :].astype(F32)) * a_att).astype(BF16)
        mixed = jnp.dot(merged, wo_ref[...], preferred_element_type=F32)
        _merge_epilogue(k * part, part, x_ref[rs, :], mixed, mod_ref, nw_ref, w_hi, w_lo,
                        x1_ref, tok_ref, cls_ref)


def _merge_epilogue(row0, tm, x, mixed, mod_ref, nw_ref, w_hi, w_lo, x1_ref, tok_ref, cls_ref):
    x1 = x + mod_ref[0, 2:3, :] * mixed
    x1_ref[row0:row0 + tm, :] = x1
    ms = jnp.mean(x1 * x1, axis=-1, keepdims=True)
    y = x1 * lax.rsqrt(ms + EPS) * nw_ref[...]
    h2 = y * (1.0 + mod_ref[0, 4:5, :]) + mod_ref[0, 3:4, :]
    blk0 = row0 * TOK_ROWS
    for a in range(D_ROWS):
        tok_ref[pl.ds(blk0 + a, tm, stride=TOK_ROWS), :] = h2[:, a * LANES:(a + 1) * LANES]
    for a in range(D_ROWS + 1, TOK_ROWS):
        tok_ref[pl.ds(blk0 + a, tm, stride=TOK_ROWS), :] = jnp.zeros((tm, LANES), F32)

    h_hi, h_lo = _split_bf16(h2)
    hi_both = jnp.dot(h_hi, jnp.concatenate([w_hi, w_lo], axis=1), preferred_element_type=F32)
    logits = (hi_both[:, :LANES] + jnp.dot(h_lo, w_hi, preferred_element_type=F32)
              + hi_both[:, LANES:])

    lane = lax.broadcasted_iota(jnp.int32, logits.shape, 1)
    neg_inf = jnp.float32(-jnp.inf)
    in_g = lane < N_GROUPS
    lg = jnp.where(in_g, logits, neg_inf)
    mg = jnp.max(lg, axis=1, keepdims=True)
    eg = jnp.exp(lg - mg)
    g_prob = eg / jnp.sum(eg, axis=1, keepdims=True)
    g_p = jnp.max(g_prob, axis=1, keepdims=True)
    g_idx = jnp.min(jnp.where(g_prob == g_p, lane, LANES), axis=1, keepdims=True)
    lo_col = ROUTER_COL0 + EXPERTS_PER_GROUP * g_idx
    in_e = (lane >= lo_col) & (lane < lo_col + EXPERTS_PER_GROUP)
    le = jnp.where(in_e, logits, neg_inf)
    me = jnp.max(le, axis=1, keepdims=True)
    ee = jnp.exp(le - me)
    e_prob = jnp.where(in_e, ee / jnp.sum(ee, axis=1, keepdims=True), -1.0)
    p1 = jnp.max(e_prob, axis=1, keepdims=True)
    i1 = jnp.min(jnp.where(e_prob == p1, lane, LANES), axis=1, keepdims=True)
    rest = jnp.where(lane == i1, -1.0, e_prob)
    p2 = jnp.max(rest, axis=1, keepdims=True)
    i2 = jnp.min(jnp.where(rest == p2, lane, LANES), axis=1, keepdims=True)
    denom = p1 + p2
    ea = jnp.minimum(i1, i2) - lo_col
    eb = jnp.maximum(i1, i2) - lo_col
    pair = lax.shift_right_logical(ea * (2 * EXPERTS_PER_GROUP - 1 - ea), 1) + eb - ea - 1
    cls = g_idx * PAIRS_PER_GROUP + pair
    routing = (jnp.where(lane == i1, g_p * (p1 / denom), 0.0)
               + jnp.where(lane == i2, g_p * (p2 / denom), 0.0)
               + jnp.where(lane == CLASS_LANE, cls.astype(F32), 0.0))
    tok_ref[pl.ds(blk0 + D_ROWS, tm, stride=TOK_ROWS), :] = routing
    cls_ref[:, row0:row0 + tm] = routing.T[CLASS_LANE:CLASS_LANE + 1, :].astype(jnp.int32)


def _merge(x2, y_sg, y_att, proj, mod, norm2_w, w_sg, w_att, w_o, w_router, seq):
    n, d = x2.shape
    tm = MERGE_TM
    tpb = seq // tm
    resident = lambda shape: pl.BlockSpec(shape, lambda i: (0, 0), pipeline_mode=pl.Buffered(1))
    return pl.pallas_call(
        _merge_kernel,
        grid=(n // tm,),
        in_specs=[pl.BlockSpec((tm, d), lambda i: (i, 0)),
                  pl.BlockSpec((tm, D_SG), lambda i: (i, 0)),
                  pl.BlockSpec((tm, D_ATT), lambda i: (i, 0)),
                  pl.BlockSpec((tm, d), lambda i: (i, BLK_GATE_SG)),
                  pl.BlockSpec((tm, d), lambda i: (i, BLK_GATE_ATT)),
                  pl.BlockSpec((1, 8, d), lambda i: (i // tpb, 0, 0)),
                  pl.BlockSpec((1, d), lambda i: (0, 0)),
                  resident((D_SG, d)), resident((D_ATT, d)), resident((d, d)),
                  resident((d, LANES))],
        out_specs=[pl.BlockSpec((tm, d), lambda i: (i, 0)),
                   pl.BlockSpec((tm * TOK_ROWS, LANES), lambda i: (i, 0)),
                   pl.BlockSpec((None, 1, tm), lambda i: (i, 0, 0))],
        out_shape=[jax.ShapeDtypeStruct((n, d), F32),
                   jax.ShapeDtypeStruct((n * TOK_ROWS, LANES), F32),
                   jax.ShapeDtypeStruct((n // tm, 1, tm), jnp.int32)],
        compiler_params=_cparams(("parallel",)),
        name="merge",
    )(x2, y_sg, y_att, proj, proj, mod, norm2_w, w_sg, w_att, w_o, w_router)


def _invert_kernel(pos_ref, free_ref, inv_ref, *, n_tok, n_free):
    def place(t, carry):
        inv_ref[pos_ref[t]] = t
        return carry

    def place_free(k, carry):
        inv_ref[free_ref[k]] = n_tok + k
        return carry

    lax.fori_loop(0, n_tok, place, 0, unroll=8)
    lax.fori_loop(0, n_free, place_free, 0, unroll=8)


def _invert(pos, free):
    n_tok, n_free = pos.shape[0], free.shape[0]
    return pl.pallas_call(
        functools.partial(_invert_kernel, n_tok=n_tok, n_free=n_free),
        grid_spec=pltpu.PrefetchScalarGridSpec(
            num_scalar_prefetch=2, grid=(1,), in_specs=[],
            out_specs=pl.BlockSpec(memory_space=pltpu.SMEM)),
        out_shape=jax.ShapeDtypeStruct((n_tok + n_free,), jnp.int32),
        compiler_params=_cparams(("arbitrary",)),
        name="invert",
    )(pos, free)


MOE_TILE = 256
ROW_UNROLL_LOG2 = 3
ROW_UNROLL = 1 << ROW_UNROLL_LOG2


def _for_rows(rows, fn):
    groups = lax.shift_right_logical(rows, ROW_UNROLL_LOG2)

    def group(g, carry):
        for u in range(ROW_UNROLL):
            fn(g * ROW_UNROLL + u)
        return carry

    def single(r, carry):
        fn(r)
        return carry

    lax.fori_loop(0, groups, group, 0)
    lax.fori_loop(groups * ROW_UNROLL, rows, single, 0)


def _moe_kernel(src_ref, dst_ref, ea_ref, eb_ref, rows_ref, na_ref, tok_hbm,
                w1a_ref, w3a_ref, w2a_ref, w1b_ref, w3b_ref, w2b_ref, y_hbm,
                gbuf, stage, acc_ref, gsem, ssem):
    i = pl.program_id(0)
    n_act = na_ref[0]
    slot = lax.rem(i, 2)

    def gather_start(tile, sl):
        def one(r, carry):
            row0 = pl.multiple_of(r * TOK_ROWS, 8)
            pltpu.make_async_copy(tok_hbm.at[src_ref[tile * MOE_TILE + r]],
                                  gbuf.at[sl, pl.ds(row0, TOK_ROWS), :], gsem.at[sl]).start()
            return carry
        lax.fori_loop(0, MOE_TILE, one, 0, unroll=8)

    def gather_wait(sl):
        for _ in range(MOE_TILE):
            pltpu.make_async_copy(tok_hbm.at[0], gbuf.at[sl, pl.ds(0, TOK_ROWS), :], gsem.at[sl]).wait()

    def scatter_start(tile, sl):
        def one(r):
            row0 = pl.multiple_of(r * D_ROWS, 8)
            pltpu.make_async_copy(stage.at[sl, pl.ds(row0, D_ROWS), :],
                                  y_hbm.at[dst_ref[tile * MOE_TILE + r]], ssem.at[sl]).start()
        _for_rows(rows_ref[tile], one)

    def scatter_wait(tile, sl):
        def one(r):
            pltpu.make_async_copy(stage.at[sl, pl.ds(0, D_ROWS), :], y_hbm.at[0], ssem.at[sl]).wait()
        _for_rows(rows_ref[tile], one)

    @pl.when(i < n_act)
    def _():
        @pl.when(i == 0)
        def _():
            gather_start(0, 0)

        @pl.when(i + 1 < n_act)
        def _():
            gather_start(i + 1, 1 - slot)

        gather_wait(slot)
        t = jnp.concatenate([gbuf[slot, pl.ds(a, MOE_TILE, stride=TOK_ROWS), :] for a in range(D_ROWS)],
                            axis=1).astype(BF16)
        routing = gbuf[slot, pl.ds(D_ROWS, MOE_TILE, stride=TOK_ROWS), :]
        lane = lax.broadcasted_iota(jnp.int32, routing.shape, 1)
        experts = ((ea_ref[i], w1a_ref, w3a_ref, w2a_ref), (eb_ref[i], w1b_ref, w3b_ref, w2b_ref))
        ups = [(jnp.dot(t, w1_ref[...], preferred_element_type=F32),
                jnp.dot(t, w3_ref[...], preferred_element_type=F32)) for _, w1_ref, w3_ref, _ in experts]
        hmids = []
        for (e, _, _, _), (a, b) in zip(experts, ups):
            ge = jnp.sum(jnp.where(lane == ROUTER_COL0 + e, routing, 0.0), axis=1, keepdims=True)
            hmids.append(((a * jax.nn.sigmoid(a)) * b * ge).astype(BF16))
        acc_ref[...] = (jnp.dot(hmids[0], w2a_ref[...], preferred_element_type=F32)
                        + jnp.dot(hmids[1], w2b_ref[...], preferred_element_type=F32))

        @pl.when(i >= 2)
        def _():
            scatter_wait(i - 2, slot)

        for a in range(D_ROWS):
            stage[slot, pl.ds(a, MOE_TILE, stride=D_ROWS), :] = acc_ref[:, a * LANES:(a + 1) * LANES]
        scatter_start(i, slot)

        @pl.when(i == n_act - 1)
        def _():
            scatter_wait(i, slot)

            @pl.when(i >= 1)
            def _():
                scatter_wait(i - 1, 1 - slot)


def _moe(src_tok, dst_row, tile_ea, tile_eb, tile_rows, n_active, tok_blocks, w1, w3, w2, n_tok):
    n_slots = src_tok.shape[0]
    d = D_MODEL
    blk_a = lambda i, src, dst, ea, eb, rows, na: (ea[i], 0, 0)
    blk_b = lambda i, src, dst, ea, eb, rows, na: (eb[i], 0, 0)
    up = lambda blk: pl.BlockSpec((None, d, D_EXPERT), blk)
    down = lambda blk: pl.BlockSpec((None, D_EXPERT, d), blk)
    return pl.pallas_call(
        _moe_kernel,
        grid_spec=pltpu.PrefetchScalarGridSpec(
            num_scalar_prefetch=6, grid=(n_slots // MOE_TILE,),
            in_specs=[pl.BlockSpec(memory_space=pl.ANY),
                      up(blk_a), up(blk_a), down(blk_a), up(blk_b), up(blk_b), down(blk_b)],
            out_specs=pl.BlockSpec(memory_space=pl.ANY),
            scratch_shapes=[pltpu.VMEM((2, MOE_TILE * TOK_ROWS, LANES), F32),
                            pltpu.VMEM((2, MOE_TILE * D_ROWS, LANES), F32),
                            pltpu.VMEM((MOE_TILE, d), F32),
                            pltpu.SemaphoreType.DMA((2,)),
                            pltpu.SemaphoreType.DMA((2,))]),
        out_shape=jax.ShapeDtypeStruct((n_tok, D_ROWS, LANES), F32),
        compiler_params=_cparams(("arbitrary",)),
        name="moe",
    )(src_tok, dst_row, tile_ea, tile_eb, tile_rows, n_active, tok_blocks, w1, w3, w2, w1, w3, w2)


def _final_kernel(x1_ref, y_ref, mod_ref, fw_ref, o_ref):
    tm = x1_ref.shape[0]
    parts = []
    ss = jnp.zeros((tm, 1), F32)
    for a in range(D_ROWS):
        cols = slice(a * LANES, (a + 1) * LANES)
        part = x1_ref[:, cols] + mod_ref[0, 5:6, cols] * y_ref[pl.ds(a, tm, stride=D_ROWS), :]
        ss = ss + jnp.sum(part * part, axis=1, keepdims=True)
        parts.append(part)
    inv = lax.rsqrt(ss * (1.0 / D_MODEL) + EPS)
    for a in range(D_ROWS):
        cols = slice(a * LANES, (a + 1) * LANES)
        o_ref[:, cols] = parts[a] * inv * fw_ref[:, cols]


def _final(x1, y_blocks, mod, final_w, seq):
    n, d = x1.shape
    tm = FINAL_TM
    tpb = seq // tm
    return pl.pallas_call(
        _final_kernel,
        grid=(n // tm,),
        in_specs=[pl.BlockSpec((tm, d), lambda i: (i, 0)),
                  pl.BlockSpec((tm * D_ROWS, LANES), lambda i: (i, 0)),
                  pl.BlockSpec((1, 8, d), lambda i: (i // tpb, 0, 0)),
                  pl.BlockSpec((1, d), lambda i: (0, 0))],
        out_specs=pl.BlockSpec((tm, d), lambda i: (i, 0)),
        out_shape=jax.ShapeDtypeStruct((n, d), F32),
        compiler_params=_cparams(("parallel",)),
        name="final",
    )(x1, y_blocks, mod, final_w)


def _class_sort_plan(cls, n):
    class_ids = jnp.arange(N_CLASSES, dtype=jnp.int32)
    onehot = (cls[:, None] == class_ids[None, :]).astype(jnp.int32)
    blk = LANES
    within = jnp.einsum("ts,bsg->btg", jnp.tril(jnp.ones((blk, blk), F32)),
                        onehot.astype(F32).reshape(n // blk, blk, N_CLASSES))
    before = jnp.tril(jnp.ones((n // blk, n // blk), F32), -1) @ within[:, -1, :]
    incl = (within + before[:, None, :]).reshape(n, N_CLASSES).astype(jnp.int32)
    count = incl[-1]
    tiles = (count + MOE_TILE - 1) // MOE_TILE
    tile_end = jnp.cumsum(tiles)
    start = (tile_end - tiles) * MOE_TILE
    rank = jnp.sum(incl * onehot, axis=1) - 1
    pos = jnp.sum(onehot * start[None, :], axis=1) + rank
    n_tiles = n // MOE_TILE + N_CLASSES
    tile_ids = jnp.arange(n_tiles, dtype=jnp.int32)
    tile_class = jnp.sum((tile_ids[:, None] >= tile_end[None, :]).astype(jnp.int32), axis=1)
    tile_class = jnp.minimum(tile_class, N_CLASSES - 1)
    tile_hot = (tile_class[:, None] == class_ids[None, :]).astype(jnp.int32)
    done = (tile_ids - jnp.sum(tile_hot * (tile_end - tiles)[None, :], axis=1)) * MOE_TILE
    tile_rows = jnp.clip(jnp.sum(tile_hot * count[None, :], axis=1) - done, 0, MOE_TILE)
    first = jnp.asarray([a for a, _ in PAIRS], jnp.int32)
    second = jnp.asarray([b for _, b in PAIRS], jnp.int32)
    group0 = (class_ids // PAIRS_PER_GROUP) * EXPERTS_PER_GROUP
    tile_ea = jnp.sum(tile_hot * (group0 + first[class_ids % PAIRS_PER_GROUP])[None, :], axis=1)
    tile_eb = jnp.sum(tile_hot * (group0 + second[class_ids % PAIRS_PER_GROUP])[None, :], axis=1)
    pad = tiles * MOE_TILE - count
    pad_end = jnp.cumsum(pad)
    k = jnp.arange(N_CLASSES * MOE_TILE, dtype=jnp.int32)
    seg = jnp.sum((k[:, None] >= pad_end[None, :]).astype(jnp.int32), axis=1)
    seg_hot = (seg[:, None] == class_ids[None, :]).astype(jnp.int32)
    in_class = jnp.sum(seg_hot * (start + count - (pad_end - pad))[None, :], axis=1) + k
    tail = tile_end[-1] * MOE_TILE + k - pad_end[-1]
    free = jnp.where(seg < N_CLASSES, in_class, tail)
    i32 = lambda a: a.astype(jnp.int32)
    return i32(pos), i32(free), i32(tile_ea), i32(tile_eb), i32(tile_rows), i32(tile_end[-1:])


def kernel(x, c, w_ada, b_ada, norm1_w, norm2_w, final_norm_w, w_in, sg_ln_w, sg_ln_b, w_spatial,
           b_spatial, rel_bias, w_out_sg, w_out_att, w_o, w_router_group, w_router_expert,
           w_exp_gate, w_exp_up, w_exp_down):
    batch, seq, d = x.shape
    assert d == D_MODEL and w_ada.shape[0] == 1
    assert seq % 1024 == 0 and seq // MOBA_BLOCK <= LANES
    n = batch * seq
    x2 = x.reshape(n, d)

    assert batch <= BF16_SUBLANES
    c_pad = jnp.zeros((BF16_SUBLANES, d), F32).at[:batch].set(c)
    mod = _adaln(c_pad, w_ada[0], b_ada[0].reshape(1, 6 * d))
    mod = mod[:batch].reshape(batch, 6, d)
    mod = jnp.concatenate([mod, jnp.zeros((batch, 2, d), F32)], axis=1)

    colscale = jnp.ones((1, IN_COLS), F32).at[:, CB_Q * LANES:CB_K * LANES].set(HEAD_DIM ** -0.5 * LOG2E)
    proj = _proj(x2, mod, norm1_w[0].reshape(1, d), w_in[0], colscale, seq)

    y_sg = _sgu(proj, sg_ln_w[0].reshape(1, D_SG), sg_ln_b[0].reshape(1, D_SG),
                w_spatial[0], b_spatial[0].T)

    qt, kx, vt, kmean = _moba_prep(proj, batch, seq)
    bias_tiles = _bias_tiles(rel_bias)
    y_att = _moba(rel_bias, qt, kx, vt, kmean, bias_tiles, batch, seq)

    w_router = jnp.zeros((d, LANES), F32)
    w_router = w_router.at[:, :N_GROUPS].set(w_router_group[0])
    w_router = w_router.at[:, ROUTER_COL0:ROUTER_COL0 + N_EXPERTS].set(w_router_expert[0])
    x1, tok, cls = _merge(x2, y_sg, y_att, proj, mod, norm2_w[0].reshape(1, d),
                          w_out_sg[0].astype(BF16), w_out_att[0].astype(BF16),
                          w_o[0].astype(BF16), w_router, seq)

    tok = tok.reshape(n, TOK_ROWS, LANES)
    pos, free, tile_ea, tile_eb, tile_rows, n_active = _class_sort_plan(cls.reshape(n), n)
    inv = _invert(pos, free)
    src_tok = jnp.where(inv < n, inv, 0)
    y = _moe(src_tok, inv, tile_ea, tile_eb, tile_rows, n_active, tok, w_exp_gate[0].astype(BF16),
             w_exp_up[0].astype(BF16), w_exp_down[0].astype(BF16), n)
    out = _final(x1, y.reshape(-1, LANES), mod, final_norm_w.reshape(1, d), seq)
    return out.reshape(batch, seq, d)
```

```python
import functools
import math

import jax
import jax.numpy as jnp
from jax import lax
from jax.experimental import pallas as pl
from jax.experimental.pallas import tpu as pltpu

F32 = jnp.float32
BF16 = jnp.bfloat16

LANES = 128
BF16_SUBLANES = 16
D_MODEL = 2048
D_SG = D_MODEL // 2
SG_GROUPS = 8
SG_CHUNK = 128
ATT_HEADS = 8
HEAD_DIM = 128
D_ATT = ATT_HEADS * HEAD_DIM
MOBA_BLOCK = 256
MOBA_TOPK = 3
REL_BUCKETS = 32
REL_MAX_DIST = 128
N_GROUPS = 4
EXPERTS_PER_GROUP = 4
N_EXPERTS = N_GROUPS * EXPERTS_PER_GROUP
D_EXPERT = 512
EPS = 1e-6
IN_COLS = 2 * D_SG + 3 * D_ATT + 2 * D_MODEL

N_GATE_COLS = 2 * D_MODEL
BLK_GATE_SG = 0
BLK_GATE_ATT = 1
BLK_U = N_GATE_COLS // D_SG
BLK_V = BLK_U + 1
CB_Q = (N_GATE_COLS + 2 * D_SG) // LANES
CB_K = CB_Q + ATT_HEADS
CB_VAL = CB_K + ATT_HEADS

MASK_NEG = -1e9
LOG2E = math.log2(math.e)
ROUTER_COL0 = N_GROUPS
CLASS_LANE = 0
PAIRS_PER_GROUP = EXPERTS_PER_GROUP * (EXPERTS_PER_GROUP - 1) // 2
N_CLASSES = N_GROUPS * PAIRS_PER_GROUP
PAIRS = [(a, b) for a in range(EXPERTS_PER_GROUP) for b in range(a + 1, EXPERTS_PER_GROUP)]
D_ROWS = D_MODEL // LANES
TOK_ROWS = D_ROWS + 8
VMEM_LIMIT = 56 * 1024 * 1024

ADALN_TN = 1024
PROJ_TM = 1024
PROJ_TN = 1024
PROJ_CHUNK = 256
SGU_CHUNKS = 4
MERGE_TM = 256
FINAL_TM = 256


def _cparams(sem):
    return pltpu.CompilerParams(dimension_semantics=sem, vmem_limit_bytes=VMEM_LIMIT)


def _split_bf16(a):
    hi = a.astype(BF16)
    lo = (a - hi.astype(F32)).astype(BF16)
    return hi, lo


def _adaln_kernel(c_ref, w_ref, b_ref, o_ref):
    c = c_ref[...]
    rows = c.shape[0]
    c_hi, c_lo = _split_bf16(c * jax.nn.sigmoid(c))
    w_hi, w_lo = _split_bf16(w_ref[...])
    both = jnp.dot(jnp.concatenate([c_hi, c_lo], axis=0), w_hi, preferred_element_type=F32)
    o_ref[...] = (both[:rows] + both[rows:] + jnp.dot(c_hi, w_lo, preferred_element_type=F32)
                  + b_ref[...])


def _adaln(c_pad, w, b):
    rows, d = c_pad.shape
    cols = w.shape[1]
    tn = ADALN_TN
    return pl.pallas_call(
        _adaln_kernel,
        grid=(cols // tn,),
        in_specs=[pl.BlockSpec((rows, d), lambda j: (0, 0)),
                  pl.BlockSpec((d, tn), lambda j: (0, j)),
                  pl.BlockSpec((1, tn), lambda j: (0, j))],
        out_specs=pl.BlockSpec((rows, tn), lambda j: (0, j)),
        out_shape=jax.ShapeDtypeStruct((rows, cols), F32),
        compiler_params=_cparams(("arbitrary",)),
        name="adaln",
    )(c_pad, w, b)


def _proj_kernel(x_ref, mod_ref, nw_ref, w_ref, cs_ref, o_ref, h_scr):
    @pl.when(pl.program_id(1) == 0)
    def _():
        w = w_ref[...].astype(BF16)
        scale = nw_ref[...] * (1.0 + mod_ref[0, 1:2, :])
        for k in range(x_ref.shape[0] // PROJ_CHUNK):
            rows = slice(k * PROJ_CHUNK, (k + 1) * PROJ_CHUNK)
            x = x_ref[rows, :]
            ms = jnp.mean(x * x, axis=-1, keepdims=True)
            h = (x * lax.rsqrt(ms + EPS) * scale + mod_ref[0, 0:1, :]).astype(BF16)
            h_scr[rows, :] = h
            o_ref[rows, :] = (jnp.dot(h, w, preferred_element_type=F32) * cs_ref[...]).astype(BF16)

    @pl.when(pl.program_id(1) > 0)
    def _():
        acc = jnp.dot(h_scr[...], w_ref[...].astype(BF16), preferred_element_type=F32)
        o_ref[...] = (acc * cs_ref[...]).astype(BF16)


def _proj(x2, mod, norm_w, w_in, colscale, seq):
    n, d = x2.shape
    cols = w_in.shape[1]
    tm = min(PROJ_TM, seq)
    tn = PROJ_TN
    tpb = seq // tm
    n_cb = cols // tn
    first = (cols - N_GATE_COLS) // tn
    assert (cols - N_GATE_COLS) % tn == 0
    return pl.pallas_call(
        _proj_kernel,
        grid=(n // tm, n_cb),
        in_specs=[pl.BlockSpec((tm, d), lambda i, j: (i, 0)),
                  pl.BlockSpec((1, 8, d), lambda i, j: (i // tpb, 0, 0)),
                  pl.BlockSpec((1, d), lambda i, j: (0, 0)),
                  pl.BlockSpec((d, tn), lambda i, j: (0, (j + first) % n_cb)),
                  pl.BlockSpec((1, tn), lambda i, j: (0, j))],
        out_specs=pl.BlockSpec((tm, tn), lambda i, j: (i, j)),
        out_shape=jax.ShapeDtypeStruct((n, cols), BF16),
        scratch_shapes=[pltpu.VMEM((tm, d), BF16)],
        compiler_params=_cparams(("parallel", "arbitrary")),
        name="proj",
    )(x2, mod, norm_w, w_in, colscale)


def _gelu(a):
    return 0.5 * a * (1.0 + lax.erf(a * (1.0 / math.sqrt(2.0))))


def _sgu_kernel(u_ref, v_ref, lnw_ref, lnb_ref, ws_ref, bst_ref, o_ref, *, chunks):
    u = _gelu(u_ref[...].astype(F32))
    v = _gelu(v_ref[...].astype(F32))
    mu = jnp.mean(v, axis=-1, keepdims=True)
    vc = v - mu
    var = jnp.mean(vc * vc, axis=-1, keepdims=True)
    vn = (vc * lax.rsqrt(var + EPS) * lnw_ref[...] + lnb_ref[...]).astype(BF16)
    row = lax.broadcasted_iota(jnp.int32, (SG_CHUNK, SG_CHUNK), 0)
    col = lax.broadcasted_iota(jnp.int32, (SG_CHUNK, SG_CHUNK), 1)
    causal = col <= row
    for g in range(SG_GROUPS):
        wm = jnp.where(causal, ws_ref[g], 0.0).astype(BF16)
        bcol = bst_ref[:, g:g + 1]
        gs = slice(g * LANES, (g + 1) * LANES)
        for c in range(chunks):
            rs = slice(c * SG_CHUNK, (c + 1) * SG_CHUNK)
            z = jnp.dot(wm, vn[rs, gs], preferred_element_type=F32) + bcol
            o_ref[rs, gs] = (u[rs, gs] * z).astype(BF16)


def _sgu(proj, ln_w, ln_b, w_s, b_s_t):
    n = proj.shape[0]
    chunks = SGU_CHUNKS
    tm = chunks * SG_CHUNK
    return pl.pallas_call(
        functools.partial(_sgu_kernel, chunks=chunks),
        grid=(n // tm,),
        in_specs=[pl.BlockSpec((tm, D_SG), lambda i: (i, BLK_U)),
                  pl.BlockSpec((tm, D_SG), lambda i: (i, BLK_V)),
                  pl.BlockSpec((1, D_SG), lambda i: (0, 0)),
                  pl.BlockSpec((1, D_SG), lambda i: (0, 0)),
                  pl.BlockSpec((SG_GROUPS, SG_CHUNK, SG_CHUNK), lambda i: (0, 0, 0)),
                  pl.BlockSpec((SG_CHUNK, SG_GROUPS), lambda i: (0, 0))],
        out_specs=pl.BlockSpec((tm, D_SG), lambda i: (i, 0)),
        out_shape=jax.ShapeDtypeStruct((n, D_SG), BF16),
        compiler_params=_cparams(("parallel",)),
        name="sgu",
    )(proj, proj, ln_w, ln_b, w_s, b_s_t)


VT_ROWS = HEAD_DIM + 16
EXT_LO = REL_BUCKETS


def _moba_prep_kernel(q_ref, k_ref, v_ref, qt_ref, kx_ref, vt_ref, km_ref, *, nb):
    lane = lax.broadcasted_iota(jnp.int32, (MOBA_BLOCK, LANES), 1)
    pad = jnp.concatenate([jnp.ones((1, MOBA_BLOCK), F32),
                           jnp.zeros((VT_ROWS - HEAD_DIM - 1, MOBA_BLOCK), F32)], axis=0)
    for j in range(nb):
        rows = slice(j * MOBA_BLOCK, (j + 1) * MOBA_BLOCK)
        kj = k_ref[rows, :]
        onehot = jnp.where((lane == j) | (lane == j + EXT_LO), 1.0, 0.0).astype(BF16)
        kx_ref[rows, :] = jnp.concatenate([kj, onehot], axis=1)
        km_ref[j:j + 1, :] = jnp.mean(kj.astype(F32), axis=0, keepdims=True)
        vt = v_ref[rows, :].astype(F32).T
        vt_ref[j] = jnp.concatenate([vt, pad], axis=0).astype(BF16)
        qt_ref[j] = q_ref[rows, :].astype(F32).T.astype(BF16)


def _moba_prep(proj, batch, seq):
    nb = seq // MOBA_BLOCK
    bh = lambda shape: pl.BlockSpec((None, None) + shape, lambda b, h: (b, h) + (0,) * len(shape))
    return pl.pallas_call(
        functools.partial(_moba_prep_kernel, nb=nb),
        grid=(batch, ATT_HEADS),
        in_specs=[pl.BlockSpec((seq, HEAD_DIM), lambda b, h: (b, CB_Q + h)),
                  pl.BlockSpec((seq, HEAD_DIM), lambda b, h: (b, CB_K + h)),
                  pl.BlockSpec((seq, HEAD_DIM), lambda b, h: (b, CB_VAL + h))],
        out_specs=[bh((nb, HEAD_DIM, MOBA_BLOCK)), bh((seq, 2 * HEAD_DIM)),
                   bh((nb, VT_ROWS, MOBA_BLOCK)), bh((nb, HEAD_DIM))],
        out_shape=[jax.ShapeDtypeStruct((batch, ATT_HEADS, nb, HEAD_DIM, MOBA_BLOCK), BF16),
                   jax.ShapeDtypeStruct((batch, ATT_HEADS, seq, 2 * HEAD_DIM), BF16),
                   jax.ShapeDtypeStruct((batch, ATT_HEADS, nb, VT_ROWS, MOBA_BLOCK), BF16),
                   jax.ShapeDtypeStruct((batch, ATT_HEADS, nb, HEAD_DIM), F32)],
        compiler_params=_cparams(("parallel", "parallel")),
        name="moba_prep",
    )(proj, proj, proj)


def _bias_tiles_kernel(rb_ref, o_ref):
    h = pl.program_id(0)
    kj = lax.broadcasted_iota(jnp.int32, (MOBA_BLOCK, MOBA_BLOCK), 0)
    qi = lax.broadcasted_iota(jnp.int32, (MOBA_BLOCK, MOBA_BLOCK), 1)
    max_exact = REL_BUCKETS // 2
    for t in range(2):
        rel = qi - kj + MOBA_BLOCK * t
        n = jnp.maximum(rel, 0)
        nf = jnp.maximum(n, max_exact).astype(F32)
        large = max_exact + (jnp.log(nf / max_exact) / math.log(REL_MAX_DIST / max_exact)
                             * (REL_BUCKETS - max_exact)).astype(jnp.int32)
        large = jnp.minimum(large, REL_BUCKETS - 1)
        bucket = jnp.where(n < max_exact, n, large)
        bias = jnp.zeros((MOBA_BLOCK, MOBA_BLOCK), F32)
        for r in range(REL_BUCKETS):
            bias = jnp.where(bucket == r, rb_ref[r, h], bias)
        o_ref[t] = jnp.where(rel >= 0, bias * LOG2E, MASK_NEG)


def _bias_tiles(rel_bias):
    return pl.pallas_call(
        _bias_tiles_kernel,
        grid=(ATT_HEADS,),
        in_specs=[pl.BlockSpec(memory_space=pltpu.SMEM)],
        out_specs=pl.BlockSpec((None, 2, MOBA_BLOCK, MOBA_BLOCK), lambda h: (h, 0, 0, 0)),
        out_shape=jax.ShapeDtypeStruct((ATT_HEADS, 2, MOBA_BLOCK, MOBA_BLOCK), F32),
        compiler_params=_cparams(("arbitrary",)),
        name="bias_tiles",
    )(rel_bias)


FAR_BLOCKS = 4
MAX_LOG2_RISE = 64.0
HEADS_PER_STEP = 4


def _moba_kernel(rb_ref, qt_ref, kx_ref, vt_ref, km_ref, bias_ref, o_ref, *, nb):
    i = pl.program_id(2)
    jp = jnp.maximum(i - 1, 0)
    prev0 = pl.multiple_of(jp * MOBA_BLOCK, MOBA_BLOCK)
    own0 = pl.multiple_of(i * MOBA_BLOCK, MOBA_BLOCK)
    no_prev = jnp.where(i > 0, 0.0, MASK_NEG)
    neg_inf = jnp.float32(-jnp.inf)

    heads = range(HEADS_PER_STEP)

    def choose_blocks(hh):
        km_hi, km_lo = _split_bf16(km_ref[hh])
        score = (jnp.dot(km_hi, qt_ref[hh], preferred_element_type=F32)
                 + jnp.dot(km_lo, qt_ref[hh], preferred_element_type=F32))
        bid = lax.broadcasted_iota(jnp.int32, score.shape, 0)
        score = jnp.where(bid < i, score, neg_inf)
        chosen = bid < 0
        for _ in range(MOBA_TOPK):
            mx = jnp.max(score, axis=0, keepdims=True)
            hit = (score == mx) & (mx > neg_inf)
            idx = jnp.min(jnp.where(hit, bid, nb), axis=0, keepdims=True)
            pick = bid == idx
            chosen = chosen | pick
            score = jnp.where(pick, neg_inf, score)
        return chosen

    def with_mask(hh, val):
        hi, lo = _split_bf16(val)
        fill = jnp.zeros((HEAD_DIM - 2 * EXT_LO, MOBA_BLOCK), BF16)
        if nb < EXT_LO:
            gap = jnp.zeros((EXT_LO - nb, MOBA_BLOCK), BF16)
            return jnp.concatenate([qt_ref[hh], hi, gap, lo, gap, fill], axis=0)
        return jnp.concatenate([qt_ref[hh], hi, lo, fill], axis=0)

    chosen = [choose_blocks(hh) for hh in heads]
    bid = lax.broadcasted_iota(jnp.int32, chosen[0].shape, 0)
    far_bias = [rb_ref[REL_BUCKETS - 1, pl.program_id(1) * HEADS_PER_STEP + hh] * LOG2E for hh in heads]
    qx_fars = [with_mask(hh, jnp.where(chosen[hh] & (bid <= i - 2), far_bias[hh], MASK_NEG)) for hh in heads]
    qx_nears = [with_mask(hh, jnp.where((bid == i) | (chosen[hh] & (bid == i - 1)), 0.0, MASK_NEG))
                for hh in heads]

    ss = [jnp.dot(jnp.concatenate([kx_ref[hh, pl.ds(prev0, MOBA_BLOCK), :],
                                   kx_ref[hh, pl.ds(own0, MOBA_BLOCK), :]], axis=0),
                  qx_nears[hh], preferred_element_type=F32)
          + jnp.concatenate([bias_ref[hh, 1] + no_prev, bias_ref[hh, 0]], axis=0) for hh in heads]
    ms = [jnp.max(ss[hh], axis=0, keepdims=True) for hh in heads]
    ps = [jnp.exp2(ss[hh] - ms[hh]).astype(BF16) for hh in heads]
    accs = [jnp.dot(jnp.concatenate([vt_ref[hh, jp], vt_ref[hh, i]], axis=1), ps[hh],
                    preferred_element_type=F32) for hh in heads]

    def far_step(c, carry):
        row0 = pl.multiple_of(c * (FAR_BLOCKS * MOBA_BLOCK), FAR_BLOCKS * MOBA_BLOCK)
        m_olds = [carry[hh][0] for hh in heads]
        accs = [carry[hh][1] for hh in heads]

        def logits(hh):
            return jnp.dot(kx_ref[hh, pl.ds(row0, FAR_BLOCKS * MOBA_BLOCK), :], qx_fars[hh],
                           preferred_element_type=F32)

        def values(hh):
            return jnp.concatenate([vt_ref[hh, c * FAR_BLOCKS + k] for k in range(FAR_BLOCKS)], axis=1)

        ss = [logits(hh) for hh in heads]
        ps = [jnp.exp2(ss[hh] - m_olds[hh]).astype(BF16) for hh in heads]
        m_news = [jnp.maximum(m_olds[hh], jnp.max(ss[hh], axis=0, keepdims=True)) for hh in heads]
        alphas = [jnp.exp2(m_olds[hh] - m_news[hh]) for hh in heads]
        usual = tuple((m_news[hh], alphas[hh] * (accs[hh] + jnp.dot(values(hh), ps[hh],
                                                                    preferred_element_type=F32)))
                      for hh in heads)
        rise = functools.reduce(jnp.maximum, [jnp.max(m_news[hh] - m_olds[hh]) for hh in heads])

        def redo():
            out = []
            for hh in heads:
                p = jnp.exp2(logits(hh) - m_news[hh]).astype(BF16)
                out.append((m_news[hh], alphas[hh] * accs[hh]
                            + jnp.dot(values(hh), p, preferred_element_type=F32)))
            return tuple(out)

        return lax.cond(rise > MAX_LOG2_RISE, redo, lambda: usual)

    n_far = lax.shift_right_logical(jnp.maximum(i - 1, 0) + FAR_BLOCKS - 1, 2)
    final = lax.fori_loop(0, n_far, far_step, tuple((ms[hh], accs[hh]) for hh in heads))
    for hh in heads:
        acc = final[hh][1]
        out = acc[0:HEAD_DIM] / acc[HEAD_DIM:HEAD_DIM + 1]
        o_ref[:, hh * HEAD_DIM:(hh + 1) * HEAD_DIM] = out.T.astype(BF16)


def _moba(rel_bias, qt, kx, vt, kmean, bias_tiles, batch, seq):
    nb = seq // MOBA_BLOCK
    hps = HEADS_PER_STEP
    assert nb % FAR_BLOCKS == 0 and FAR_BLOCKS == 4 and ATT_HEADS % hps == 0
    bh = lambda shape: pl.BlockSpec((None, hps) + shape, lambda b, h, i: (b, h) + (0,) * len(shape),
                                    pipeline_mode=pl.Buffered(1))
    return pl.pallas_call(
        functools.partial(_moba_kernel, nb=nb),
        grid=(batch, ATT_HEADS // hps, nb),
        in_specs=[pl.BlockSpec(memory_space=pltpu.SMEM),
                  pl.BlockSpec((None, hps, None, HEAD_DIM, MOBA_BLOCK), lambda b, h, i: (b, h, i, 0, 0)),
                  bh((seq, 2 * HEAD_DIM)), bh((nb, VT_ROWS, MOBA_BLOCK)), bh((nb, HEAD_DIM)),
                  pl.BlockSpec((hps, 2, MOBA_BLOCK, MOBA_BLOCK), lambda b, h, i: (h, 0, 0, 0))],
        out_specs=pl.BlockSpec((MOBA_BLOCK, hps * HEAD_DIM), lambda b, h, i: (b * nb + i, h)),
        out_shape=jax.ShapeDtypeStruct((batch * seq, D_ATT), BF16),
        compiler_params=_cparams(("parallel", "parallel", "arbitrary")),
        name="moba",
    )(rel_bias, qt, kx, vt, kmean, bias_tiles)


def _merge_kernel(x_ref, ysg_ref, yatt_ref, gsg_ref, gatt_ref, mod_ref, nw_ref,
                  wsg_ref, watt_ref, wo_ref, wr_ref, x1_ref, tok_ref, cls_ref):
    tm = x_ref.shape[0]
    a_sg = jnp.dot(ysg_ref[...], wsg_ref[...], preferred_element_type=F32)
    a_att = jnp.dot(yatt_ref[...], watt_ref[...], preferred_element_type=F32)
    merged = (jax.nn.sigmoid(gsg_ref[...].astype(F32)) * a_sg
              + jax.nn.sigmoid(gatt_ref[...].astype(F32)) * a_att).astype(BF16)
    mixed = jnp.dot(merged, wo_ref[...], preferred_element_type=F32)
    x1 = x_ref[...] + mod_ref[0, 2:3, :] * mixed
    x1_ref[...] = x1
    ms = jnp.mean(x1 * x1, axis=-1, keepdims=True)
    scale = nw_ref[...] * (1.0 + mod_ref[0, 4:5, :])
    h2 = x1 * lax.rsqrt(ms + EPS) * scale + mod_ref[0, 3:4, :]
    for a in range(D_ROWS):
        tok_ref[pl.ds(a, tm, stride=TOK_ROWS), :] = h2[:, a * LANES:(a + 1) * LANES]
    for a in range(D_ROWS + 1, TOK_ROWS):
        tok_ref[pl.ds(a, tm, stride=TOK_ROWS), :] = jnp.zeros((tm, LANES), F32)

    w_hi, w_lo = _split_bf16(wr_ref[...])
    h_hi, h_lo = _split_bf16(h2)
    hi_both = jnp.dot(h_hi, jnp.concatenate([w_hi, w_lo], axis=1), preferred_element_type=F32)
    logits = (hi_both[:, :LANES] + jnp.dot(h_lo, w_hi, preferred_element_type=F32)
              + hi_both[:, LANES:])

    lane = lax.broadcasted_iota(jnp.int32, logits.shape, 1)
    neg_inf = jnp.float32(-jnp.inf)
    in_g = lane < N_GROUPS
    lg = jnp.where(in_g, logits, neg_inf)
    mg = jnp.max(lg, axis=1, keepdims=True)
    eg = jnp.exp(lg - mg)
    g_prob = eg / jnp.sum(eg, axis=1, keepdims=True)
    g_p = jnp.max(g_prob, axis=1, keepdims=True)
    g_idx = jnp.min(jnp.where(g_prob == g_p, lane, LANES), axis=1, keepdims=True)
    lo_col = ROUTER_COL0 + EXPERTS_PER_GROUP * g_idx
    in_e = (lane >= lo_col) & (lane < lo_col + EXPERTS_PER_GROUP)
    le = jnp.where(in_e, logits, neg_inf)
    me = jnp.max(le, axis=1, keepdims=True)
    ee = jnp.exp(le - me)
    e_prob = jnp.where(in_e, ee / jnp.sum(ee, axis=1, keepdims=True), -1.0)
    p1 = jnp.max(e_prob, axis=1, keepdims=True)
    i1 = jnp.min(jnp.where(e_prob == p1, lane, LANES), axis=1, keepdims=True)
    rest = jnp.where(lane == i1, -1.0, e_prob)
    p2 = jnp.max(rest, axis=1, keepdims=True)
    i2 = jnp.min(jnp.where(rest == p2, lane, LANES), axis=1, keepdims=True)
    denom = p1 + p2
    ea = jnp.minimum(i1, i2) - lo_col
    eb = jnp.maximum(i1, i2) - lo_col
    pair = lax.shift_right_logical(ea * (2 * EXPERTS_PER_GROUP - 1 - ea), 1) + eb - ea - 1
    cls = g_idx * PAIRS_PER_GROUP + pair
    routing = (jnp.where(lane == i1, g_p * (p1 / denom), 0.0)
               + jnp.where(lane == i2, g_p * (p2 / denom), 0.0)
               + jnp.where(lane == CLASS_LANE, cls.astype(F32), 0.0))
    tok_ref[pl.ds(D_ROWS, tm, stride=TOK_ROWS), :] = routing
    cls_ref[...] = routing.T[CLASS_LANE:CLASS_LANE + 1, :].astype(jnp.int32)


def _merge(x2, y_sg, y_att, proj, mod, norm2_w, w_sg, w_att, w_o, w_router, seq):
    n, d = x2.shape
    tm = MERGE_TM
    tpb = seq // tm
    resident = lambda shape: pl.BlockSpec(shape, lambda i: (0, 0), pipeline_mode=pl.Buffered(1))
    return pl.pallas_call(
        _merge_kernel,
        grid=(n // tm,),
        in_specs=[pl.BlockSpec((tm, d), lambda i: (i, 0)),
                  pl.BlockSpec((tm, D_SG), lambda i: (i, 0)),
                  pl.BlockSpec((tm, D_ATT), lambda i: (i, 0)),
                  pl.BlockSpec((tm, d), lambda i: (i, BLK_GATE_SG)),
                  pl.BlockSpec((tm, d), lambda i: (i, BLK_GATE_ATT)),
                  pl.BlockSpec((1, 8, d), lambda i: (i // tpb, 0, 0)),
                  pl.BlockSpec((1, d), lambda i: (0, 0)),
                  resident((D_SG, d)), resident((D_ATT, d)), resident((d, d)),
                  resident((d, LANES))],
        out_specs=[pl.BlockSpec((tm, d), lambda i: (i, 0)),
                   pl.BlockSpec((tm * TOK_ROWS, LANES), lambda i: (i, 0)),
                   pl.BlockSpec((None, 1, tm), lambda i: (i, 0, 0))],
        out_shape=[jax.ShapeDtypeStruct((n, d), F32),
                   jax.ShapeDtypeStruct((n * TOK_ROWS, LANES), F32),
                   jax.ShapeDtypeStruct((n // tm, 1, tm), jnp.int32)],
        compiler_params=_cparams(("parallel",)),
        name="merge",
    )(x2, y_sg, y_att, proj, proj, mod, norm2_w, w_sg, w_att, w_o, w_router)


def _invert_kernel(pos_ref, free_ref, inv_ref, *, n_tok, n_free):
    def place(t, carry):
        inv_ref[pos_ref[t]] = t
        return carry

    def place_free(k, carry):
        inv_ref[free_ref[k]] = n_tok + k
        return carry

    lax.fori_loop(0, n_tok, place, 0, unroll=8)
    lax.fori_loop(0, n_free, place_free, 0, unroll=8)


def _invert(pos, free):
    n_tok, n_free = pos.shape[0], free.shape[0]
    return pl.pallas_call(
        functools.partial(_invert_kernel, n_tok=n_tok, n_free=n_free),
        grid_spec=pltpu.PrefetchScalarGridSpec(
            num_scalar_prefetch=2, grid=(1,), in_specs=[],
            out_specs=pl.BlockSpec(memory_space=pltpu.SMEM)),
        out_shape=jax.ShapeDtypeStruct((n_tok + n_free,), jnp.int32),
        compiler_params=_cparams(("arbitrary",)),
        name="invert",
    )(pos, free)


MOE_TILE = 256
ROW_UNROLL_LOG2 = 3
ROW_UNROLL = 1 << ROW_UNROLL_LOG2


def _for_rows(rows, fn):
    groups = lax.shift_right_logical(rows, ROW_UNROLL_LOG2)

    def group(g, carry):
        for u in range(ROW_UNROLL):
            fn(g * ROW_UNROLL + u)
        return carry

    def single(r, carry):
        fn(r)
        return carry

    lax.fori_loop(0, groups, group, 0)
    lax.fori_loop(groups * ROW_UNROLL, rows, single, 0)


def _moe_kernel(src_ref, dst_ref, ea_ref, eb_ref, rows_ref, na_ref, tok_hbm,
                w1a_ref, w3a_ref, w2a_ref, w1b_ref, w3b_ref, w2b_ref, y_hbm,
                gbuf, stage, acc_ref, gsem, ssem):
    i = pl.program_id(0)
    n_act = na_ref[0]
    slot = lax.rem(i, 2)

    def gather_start(tile, sl):
        def one(r, carry):
            row0 = pl.multiple_of(r * TOK_ROWS, 8)
            pltpu.make_async_copy(tok_hbm.at[src_ref[tile * MOE_TILE + r]],
                                  gbuf.at[sl, pl.ds(row0, TOK_ROWS), :], gsem.at[sl]).start()
            return carry
        lax.fori_loop(0, MOE_TILE, one, 0, unroll=8)

    def gather_wait(sl):
        for _ in range(MOE_TILE):
            pltpu.make_async_copy(tok_hbm.at[0], gbuf.at[sl, pl.ds(0, TOK_ROWS), :], gsem.at[sl]).wait()

    def scatter_start(tile, sl):
        def one(r):
            row0 = pl.multiple_of(r * D_ROWS, 8)
            pltpu.make_async_copy(stage.at[sl, pl.ds(row0, D_ROWS), :],
                                  y_hbm.at[dst_ref[tile * MOE_TILE + r]], ssem.at[sl]).start()
        _for_rows(rows_ref[tile], one)

    def scatter_wait(tile, sl):
        def one(r):
            pltpu.make_async_copy(stage.at[sl, pl.ds(0, D_ROWS), :], y_hbm.at[0], ssem.at[sl]).wait()
        _for_rows(rows_ref[tile], one)

    @pl.when(i < n_act)
    def _():
        @pl.when(i == 0)
        def _():
            gather_start(0, 0)

        @pl.when(i + 1 < n_act)
        def _():
            gather_start(i + 1, 1 - slot)

        gather_wait(slot)
        t = jnp.concatenate([gbuf[slot, pl.ds(a, MOE_TILE, stride=TOK_ROWS), :] for a in range(D_ROWS)],
                            axis=1).astype(BF16)
        routing = gbuf[slot, pl.ds(D_ROWS, MOE_TILE, stride=TOK_ROWS), :]
        lane = lax.broadcasted_iota(jnp.int32, routing.shape, 1)
        experts = ((ea_ref[i], w1a_ref, w3a_ref, w2a_ref), (eb_ref[i], w1b_ref, w3b_ref, w2b_ref))
        ups = [(jnp.dot(t, w1_ref[...], preferred_element_type=F32),
                jnp.dot(t, w3_ref[...], preferred_element_type=F32)) for _, w1_ref, w3_ref, _ in experts]
        hmids = []
        for (e, _, _, _), (a, b) in zip(experts, ups):
            ge = jnp.sum(jnp.where(lane == ROUTER_COL0 + e, routing, 0.0), axis=1, keepdims=True)
            hmids.append(((a * jax.nn.sigmoid(a)) * b * ge).astype(BF16))
        acc_ref[...] = (jnp.dot(hmids[0], w2a_ref[...].astype(BF16), preferred_element_type=F32)
                        + jnp.dot(hmids[1], w2b_ref[...].astype(BF16), preferred_element_type=F32))

        @pl.when(i >= 2)
        def _():
            scatter_wait(i - 2, slot)

        for a in range(D_ROWS):
            stage[slot, pl.ds(a, MOE_TILE, stride=D_ROWS), :] = acc_ref[:, a * LANES:(a + 1) * LANES]
        scatter_start(i, slot)

        @pl.when(i == n_act - 1)
        def _():
            scatter_wait(i, slot)

            @pl.when(i >= 1)
            def _():
                scatter_wait(i - 1, 1 - slot)


def _moe(src_tok, dst_row, tile_ea, tile_eb, tile_rows, n_active, tok_blocks, w1, w3, w2, n_tok):
    n_slots = src_tok.shape[0]
    d = D_MODEL
    blk_a = lambda i, src, dst, ea, eb, rows, na: (ea[i], 0, 0)
    blk_b = lambda i, src, dst, ea, eb, rows, na: (eb[i], 0, 0)
    up = lambda blk: pl.BlockSpec((None, d, D_EXPERT), blk)
    down = lambda blk: pl.BlockSpec((None, D_EXPERT, d), blk)
    return pl.pallas_call(
        _moe_kernel,
        grid_spec=pltpu.PrefetchScalarGridSpec(
            num_scalar_prefetch=6, grid=(n_slots // MOE_TILE,),
            in_specs=[pl.BlockSpec(memory_space=pl.ANY),
                      up(blk_a), up(blk_a), down(blk_a), up(blk_b), up(blk_b), down(blk_b)],
            out_specs=pl.BlockSpec(memory_space=pl.ANY),
            scratch_shapes=[pltpu.VMEM((2, MOE_TILE * TOK_ROWS, LANES), F32),
                            pltpu.VMEM((2, MOE_TILE * D_ROWS, LANES), F32),
                            pltpu.VMEM((MOE_TILE, d), F32),
                            pltpu.SemaphoreType.DMA((2,)),
                            pltpu.SemaphoreType.DMA((2,))]),
        out_shape=jax.ShapeDtypeStruct((n_tok, D_ROWS, LANES), F32),
        compiler_params=_cparams(("arbitrary",)),
        name="moe",
    )(src_tok, dst_row, tile_ea, tile_eb, tile_rows, n_active, tok_blocks, w1, w3, w2, w1, w3, w2)


def _final_kernel(x1_ref, y_ref, mod_ref, fw_ref, o_ref):
    tm = x1_ref.shape[0]
    parts = []
    ss = jnp.zeros((tm, 1), F32)
    for a in range(D_ROWS):
        cols = slice(a * LANES, (a + 1) * LANES)
        part = x1_ref[:, cols] + mod_ref[0, 5:6, cols] * y_ref[pl.ds(a, tm, stride=D_ROWS), :]
        ss = ss + jnp.sum(part * part, axis=1, keepdims=True)
        parts.append(part)
    inv = lax.rsqrt(ss * (1.0 / D_MODEL) + EPS)
    for a in range(D_ROWS):
        cols = slice(a * LANES, (a + 1) * LANES)
        o_ref[:, cols] = parts[a] * inv * fw_ref[:, cols]


def _final(x1, y_blocks, mod, final_w, seq):
    n, d = x1.shape
    tm = FINAL_TM
    tpb = seq // tm
    return pl.pallas_call(
        _final_kernel,
        grid=(n // tm,),
        in_specs=[pl.BlockSpec((tm, d), lambda i: (i, 0)),
                  pl.BlockSpec((tm * D_ROWS, LANES), lambda i: (i, 0)),
                  pl.BlockSpec((1, 8, d), lambda i: (i // tpb, 0, 0)),
                  pl.BlockSpec((1, d), lambda i: (0, 0))],
        out_specs=pl.BlockSpec((tm, d), lambda i: (i, 0)),
        out_shape=jax.ShapeDtypeStruct((n, d), F32),
        compiler_params=_cparams(("parallel",)),
        name="final",
    )(x1, y_blocks, mod, final_w)


def _class_sort_plan(cls, n):
    class_ids = jnp.arange(N_CLASSES, dtype=jnp.int32)
    onehot = (cls[:, None] == class_ids[None, :]).astype(jnp.int32)
    blk = LANES
    within = jnp.einsum("ts,bsg->btg", jnp.tril(jnp.ones((blk, blk), F32)),
                        onehot.astype(F32).reshape(n // blk, blk, N_CLASSES))
    before = jnp.tril(jnp.ones((n // blk, n // blk), F32), -1) @ within[:, -1, :]
    incl = (within + before[:, None, :]).reshape(n, N_CLASSES).astype(jnp.int32)
    count = incl[-1]
    tiles = (count + MOE_TILE - 1) // MOE_TILE
    tile_end = jnp.cumsum(tiles)
    start = (tile_end - tiles) * MOE_TILE
    rank = jnp.sum(incl * onehot, axis=1) - 1
    pos = jnp.sum(onehot * start[None, :], axis=1) + rank
    n_tiles = n // MOE_TILE + N_CLASSES
    tile_ids = jnp.arange(n_tiles, dtype=jnp.int32)
    tile_class = jnp.sum((tile_ids[:, None] >= tile_end[None, :]).astype(jnp.int32), axis=1)
    tile_class = jnp.minimum(tile_class, N_CLASSES - 1)
    tile_hot = (tile_class[:, None] == class_ids[None, :]).astype(jnp.int32)
    done = (tile_ids - jnp.sum(tile_hot * (tile_end - tiles)[None, :], axis=1)) * MOE_TILE
    tile_rows = jnp.clip(jnp.sum(tile_hot * count[None, :], axis=1) - done, 0, MOE_TILE)
    first = jnp.asarray([a for a, _ in PAIRS], jnp.int32)
    second = jnp.asarray([b for _, b in PAIRS], jnp.int32)
    group0 = (class_ids // PAIRS_PER_GROUP) * EXPERTS_PER_GROUP
    tile_ea = jnp.sum(tile_hot * (group0 + first[class_ids % PAIRS_PER_GROUP])[None, :], axis=1)
    tile_eb = jnp.sum(tile_hot * (group0 + second[class_ids % PAIRS_PER_GROUP])[None, :], axis=1)
    pad = tiles * MOE_TILE - count
    pad_end = jnp.cumsum(pad)
    k = jnp.arange(N_CLASSES * MOE_TILE, dtype=jnp.int32)
    seg = jnp.sum((k[:, None] >= pad_end[None, :]).astype(jnp.int32), axis=1)
    seg_hot = (seg[:, None] == class_ids[None, :]).astype(jnp.int32)
    in_class = jnp.sum(seg_hot * (start + count - (pad_end - pad))[None, :], axis=1) + k
    tail = tile_end[-1] * MOE_TILE + k - pad_end[-1]
    free = jnp.where(seg < N_CLASSES, in_class, tail)
    i32 = lambda a: a.astype(jnp.int32)
    return i32(pos), i32(free), i32(tile_ea), i32(tile_eb), i32(tile_rows), i32(tile_end[-1:])


def kernel(x, c, w_ada, b_ada, norm1_w, norm2_w, final_norm_w, w_in, sg_ln_w, sg_ln_b, w_spatial,
           b_spatial, rel_bias, w_out_sg, w_out_att, w_o, w_router_group, w_router_expert,
           w_exp_gate, w_exp_up, w_exp_down):
    batch, seq, d = x.shape
    assert d == D_MODEL and w_ada.shape[0] == 1
    assert seq % PROJ_TM == 0 and seq // MOBA_BLOCK <= EXT_LO
    n = batch * seq
    x2 = x.reshape(n, d)

    assert batch <= BF16_SUBLANES
    c_pad = jnp.zeros((BF16_SUBLANES, d), F32).at[:batch].set(c)
    mod = _adaln(c_pad, w_ada[0], b_ada[0].reshape(1, 6 * d))
    mod = mod[:batch].reshape(batch, 6, d)
    mod = jnp.concatenate([mod, jnp.zeros((batch, 2, d), F32)], axis=1)

    colscale = jnp.ones((1, IN_COLS), F32).at[:, CB_Q * LANES:CB_K * LANES].set(HEAD_DIM ** -0.5 * LOG2E)
    proj = _proj(x2, mod, norm1_w[0].reshape(1, d), w_in[0], colscale, seq)

    y_sg = _sgu(proj, sg_ln_w[0].reshape(1, D_SG), sg_ln_b[0].reshape(1, D_SG),
                w_spatial[0], b_spatial[0].T)

    qt, kx, vt, kmean = _moba_prep(proj, batch, seq)
    bias_tiles = _bias_tiles(rel_bias)
    y_att = _moba(rel_bias, qt, kx, vt, kmean, bias_tiles, batch, seq)

    w_router = jnp.zeros((d, LANES), F32)
    w_router = w_router.at[:, :N_GROUPS].set(w_router_group[0])
    w_router = w_router.at[:, ROUTER_COL0:ROUTER_COL0 + N_EXPERTS].set(w_router_expert[0])
    x1, tok, cls = _merge(x2, y_sg, y_att, proj, mod, norm2_w[0].reshape(1, d),
                          w_out_sg[0].astype(BF16), w_out_att[0].astype(BF16),
                          w_o[0].astype(BF16), w_router, seq)

    tok = tok.reshape(n, TOK_ROWS, LANES)
    pos, free, tile_ea, tile_eb, tile_rows, n_active = _class_sort_plan(cls.reshape(n), n)
    inv = _invert(pos, free)
    src_tok = jnp.where(inv < n, inv, 0)
    y = _moe(src_tok, inv, tile_ea, tile_eb, tile_rows, n_active, tok, w_exp_gate[0].astype(BF16),
             w_exp_up[0].astype(BF16), w_exp_down[0], n)
    out = _final(x1, y.reshape(-1, LANES), mod, final_norm_w.reshape(1, d), seq)
    return out.reshape(batch, seq, d)
```

```python
import functools
import math

import jax
import jax.numpy as jnp
from jax import lax
from jax.experimental import pallas as pl
from jax.experimental.pallas import tpu as pltpu

F32 = jnp.float32
BF16 = jnp.bfloat16

LANES = 128
BF16_SUBLANES = 16
D_MODEL = 2048
D_SG = D_MODEL // 2
SG_GROUPS = 8
SG_CHUNK = 128
ATT_HEADS = 8
HEAD_DIM = 128
D_ATT = ATT_HEADS * HEAD_DIM
MOBA_BLOCK = 256
MOBA_TOPK = 3
REL_BUCKETS = 32
REL_MAX_DIST = 128
N_GROUPS = 4
EXPERTS_PER_GROUP = 4
N_EXPERTS = N_GROUPS * EXPERTS_PER_GROUP
D_EXPERT = 512
EPS = 1e-6
IN_COLS = 2 * D_SG + 3 * D_ATT + 2 * D_MODEL

N_GATE_COLS = 2 * D_MODEL
BLK_GATE_SG = 0
BLK_GATE_ATT = 1
BLK_U = N_GATE_COLS // D_SG
BLK_V = BLK_U + 1
CB_Q = (N_GATE_COLS + 2 * D_SG) // LANES
CB_K = CB_Q + ATT_HEADS
CB_VAL = CB_K + ATT_HEADS

MASK_NEG = -1e9
LOG2E = math.log2(math.e)
ROUTER_COL0 = N_GROUPS
CLASS_LANE = 0
PAIRS_PER_GROUP = EXPERTS_PER_GROUP * (EXPERTS_PER_GROUP - 1) // 2
N_CLASSES = N_GROUPS * PAIRS_PER_GROUP
PAIRS = [(a, b) for a in range(EXPERTS_PER_GROUP) for b in range(a + 1, EXPERTS_PER_GROUP)]
D_ROWS = D_MODEL // LANES
TOK_ROWS = D_ROWS + 8
VMEM_LIMIT = 56 * 1024 * 1024

ADALN_TN = 1024
PROJ_TM = 1024
PROJ_TN = 1024
PROJ_CHUNK = 256
SGU_CHUNKS = 4
MERGE_TM = 256
FINAL_TM = 256


def _cparams(sem):
    return pltpu.CompilerParams(dimension_semantics=sem, vmem_limit_bytes=VMEM_LIMIT)


def _split_bf16(a):
    hi = a.astype(BF16)
    lo = (a - hi.astype(F32)).astype(BF16)
    return hi, lo


def _adaln_kernel(c_ref, w_ref, b_ref, o_ref):
    c = c_ref[...]
    rows = c.shape[0]
    c_hi, c_lo = _split_bf16(c * jax.nn.sigmoid(c))
    w_hi, w_lo = _split_bf16(w_ref[...])
    both = jnp.dot(jnp.concatenate([c_hi, c_lo], axis=0), w_hi, preferred_element_type=F32)
    o_ref[...] = (both[:rows] + both[rows:] + jnp.dot(c_hi, w_lo, preferred_element_type=F32)
                  + b_ref[...])


def _adaln(c_pad, w, b):
    rows, d = c_pad.shape
    cols = w.shape[1]
    tn = ADALN_TN
    return pl.pallas_call(
        _adaln_kernel,
        grid=(cols // tn,),
        in_specs=[pl.BlockSpec((rows, d), lambda j: (0, 0)),
                  pl.BlockSpec((d, tn), lambda j: (0, j)),
                  pl.BlockSpec((1, tn), lambda j: (0, j))],
        out_specs=pl.BlockSpec((rows, tn), lambda j: (0, j)),
        out_shape=jax.ShapeDtypeStruct((rows, cols), F32),
        compiler_params=_cparams(("arbitrary",)),
        name="adaln",
    )(c_pad, w, b)


def _gelu(a):
    return 0.5 * a * (1.0 + lax.erf(a * (1.0 / math.sqrt(2.0))))


_PROJ_FINISH = {
    "gate": lambda acc: acc,
    "sgu": _gelu,
    "q": lambda acc: acc * (HEAD_DIM ** -0.5 * LOG2E),
    "kv": lambda acc: acc,
}
_PROJ_SEGMENTS = (("gate", N_GATE_COLS // PROJ_TN), ("sgu", 2 * D_SG // PROJ_TN),
                  ("q", D_ATT // PROJ_TN), ("kv", 2 * D_ATT // PROJ_TN))


def _proj_kernel(x_ref, mod_ref, nw_ref, w_ref, o_ref, h_scr):
    j = pl.program_id(1)

    @pl.when(j == 0)
    def _():
        finish = _PROJ_FINISH[_PROJ_SEGMENTS[0][0]]
        w = w_ref[...].astype(BF16)
        scale = nw_ref[...] * (1.0 + mod_ref[0, 1:2, :])
        for k in range(x_ref.shape[0] // PROJ_CHUNK):
            rows = slice(k * PROJ_CHUNK, (k + 1) * PROJ_CHUNK)
            x = x_ref[rows, :]
            ms = jnp.mean(x * x, axis=-1, keepdims=True)
            h = (x * lax.rsqrt(ms + EPS) * scale + mod_ref[0, 0:1, :]).astype(BF16)
            h_scr[rows, :] = h
            o_ref[rows, :] = finish(jnp.dot(h, w, preferred_element_type=F32)).astype(BF16)

    start = 0
    for kind, blocks in _PROJ_SEGMENTS:
        @pl.when((j >= max(start, 1)) & (j < start + blocks))
        def _(kind=kind):
            acc = jnp.dot(h_scr[...], w_ref[...].astype(BF16), preferred_element_type=F32)
            o_ref[...] = _PROJ_FINISH[kind](acc).astype(BF16)
        start += blocks


def _proj(x2, mod, norm_w, w_in, seq):
    n, d = x2.shape
    cols = w_in.shape[1]
    tm = min(PROJ_TM, seq)
    tn = PROJ_TN
    tpb = seq // tm
    n_cb = cols // tn
    first = (cols - N_GATE_COLS) // tn
    assert (cols - N_GATE_COLS) % tn == 0
    assert sum(blocks * tn for _, blocks in _PROJ_SEGMENTS) == cols
    return pl.pallas_call(
        _proj_kernel,
        grid=(n // tm, n_cb),
        in_specs=[pl.BlockSpec((tm, d), lambda i, j: (i, 0)),
                  pl.BlockSpec((1, 8, d), lambda i, j: (i // tpb, 0, 0)),
                  pl.BlockSpec((1, d), lambda i, j: (0, 0)),
                  pl.BlockSpec((d, tn), lambda i, j: (0, (j + first) % n_cb))],
        out_specs=pl.BlockSpec((tm, tn), lambda i, j: (i, j)),
        out_shape=jax.ShapeDtypeStruct((n, cols), BF16),
        scratch_shapes=[pltpu.VMEM((tm, d), BF16)],
        compiler_params=_cparams(("parallel", "arbitrary")),
        name="proj",
    )(x2, mod, norm_w, w_in)


def _sgu_kernel(u_ref, v_ref, lnw_ref, lnb_ref, ws_ref, bst_ref, o_ref, *, chunks):
    u = u_ref[...].astype(F32)
    v = v_ref[...].astype(F32)
    mu = jnp.mean(v, axis=-1, keepdims=True)
    vc = v - mu
    var = jnp.mean(vc * vc, axis=-1, keepdims=True)
    vn = (vc * lax.rsqrt(var + EPS) * lnw_ref[...] + lnb_ref[...]).astype(BF16)
    row = lax.broadcasted_iota(jnp.int32, (SG_CHUNK, SG_CHUNK), 0)
    col = lax.broadcasted_iota(jnp.int32, (SG_CHUNK, SG_CHUNK), 1)
    causal = col <= row
    for g in range(SG_GROUPS):
        wm = jnp.where(causal, ws_ref[g], 0.0).astype(BF16)
        bcol = bst_ref[:, g:g + 1]
        gs = slice(g * LANES, (g + 1) * LANES)
        for c in range(chunks):
            rs = slice(c * SG_CHUNK, (c + 1) * SG_CHUNK)
            z = jnp.dot(wm, vn[rs, gs], preferred_element_type=F32) + bcol
            o_ref[rs, gs] = (u[rs, gs] * z).astype(BF16)


def _sgu(proj, ln_w, ln_b, w_s, b_s_t):
    n = proj.shape[0]
    chunks = SGU_CHUNKS
    tm = chunks * SG_CHUNK
    return pl.pallas_call(
        functools.partial(_sgu_kernel, chunks=chunks),
        grid=(n // tm,),
        in_specs=[pl.BlockSpec((tm, D_SG), lambda i: (i, BLK_U)),
                  pl.BlockSpec((tm, D_SG), lambda i: (i, BLK_V)),
                  pl.BlockSpec((1, D_SG), lambda i: (0, 0)),
                  pl.BlockSpec((1, D_SG), lambda i: (0, 0)),
                  pl.BlockSpec((SG_GROUPS, SG_CHUNK, SG_CHUNK), lambda i: (0, 0, 0)),
                  pl.BlockSpec((SG_CHUNK, SG_GROUPS), lambda i: (0, 0))],
        out_specs=pl.BlockSpec((tm, D_SG), lambda i: (i, 0)),
        out_shape=jax.ShapeDtypeStruct((n, D_SG), BF16),
        compiler_params=_cparams(("parallel",)),
        name="sgu",
    )(proj, proj, ln_w, ln_b, w_s, b_s_t)


VT_ROWS = HEAD_DIM + 16
EXT_LO = REL_BUCKETS


def _moba_prep_kernel(q_ref, k_ref, v_ref, qt_ref, kx_ref, vt_ref, km_ref, *, nb):
    lane = lax.broadcasted_iota(jnp.int32, (MOBA_BLOCK, LANES), 1)
    pad = jnp.concatenate([jnp.ones((1, MOBA_BLOCK), F32),
                           jnp.zeros((VT_ROWS - HEAD_DIM - 1, MOBA_BLOCK), F32)], axis=0)
    for j in range(nb):
        rows = slice(j * MOBA_BLOCK, (j + 1) * MOBA_BLOCK)
        kj = k_ref[rows, :]
        onehot = jnp.where((lane == j) | (lane == j + EXT_LO), 1.0, 0.0).astype(BF16)
        kx_ref[rows, :] = jnp.concatenate([kj, onehot], axis=1)
        km_ref[j:j + 1, :] = jnp.mean(kj.astype(F32), axis=0, keepdims=True)
        vt = v_ref[rows, :].astype(F32).T
        vt_ref[j] = jnp.concatenate([vt, pad], axis=0).astype(BF16)
        qt_ref[j] = q_ref[rows, :].astype(F32).T.astype(BF16)


def _moba_prep(proj, batch, seq):
    nb = seq // MOBA_BLOCK
    bh = lambda shape: pl.BlockSpec((None, None) + shape, lambda b, h: (b, h) + (0,) * len(shape))
    return pl.pallas_call(
        functools.partial(_moba_prep_kernel, nb=nb),
        grid=(batch, ATT_HEADS),
        in_specs=[pl.BlockSpec((seq, HEAD_DIM), lambda b, h: (b, CB_Q + h)),
                  pl.BlockSpec((seq, HEAD_DIM), lambda b, h: (b, CB_K + h)),
                  pl.BlockSpec((seq, HEAD_DIM), lambda b, h: (b, CB_VAL + h))],
        out_specs=[bh((nb, HEAD_DIM, MOBA_BLOCK)), bh((seq, 2 * HEAD_DIM)),
                   bh((nb, VT_ROWS, MOBA_BLOCK)), bh((nb, HEAD_DIM))],
        out_shape=[jax.ShapeDtypeStruct((batch, ATT_HEADS, nb, HEAD_DIM, MOBA_BLOCK), BF16),
                   jax.ShapeDtypeStruct((batch, ATT_HEADS, seq, 2 * HEAD_DIM), BF16),
                   jax.ShapeDtypeStruct((batch, ATT_HEADS, nb, VT_ROWS, MOBA_BLOCK), BF16),
                   jax.ShapeDtypeStruct((batch, ATT_HEADS, nb, HEAD_DIM), F32)],
        compiler_params=_cparams(("parallel", "parallel")),
        name="moba_prep",
    )(proj, proj, proj)


def _bias_tiles_kernel(rb_ref, o_ref):
    h = pl.program_id(0)
    kj = lax.broadcasted_iota(jnp.int32, (MOBA_BLOCK, MOBA_BLOCK), 0)
    qi = lax.broadcasted_iota(jnp.int32, (MOBA_BLOCK, MOBA_BLOCK), 1)
    max_exact = REL_BUCKETS // 2
    for t in range(2):
        rel = qi - kj + MOBA_BLOCK * t
        n = jnp.maximum(rel, 0)
        nf = jnp.maximum(n, max_exact).astype(F32)
        large = max_exact + (jnp.log(nf / max_exact) / math.log(REL_MAX_DIST / max_exact)
                             * (REL_BUCKETS - max_exact)).astype(jnp.int32)
        large = jnp.minimum(large, REL_BUCKETS - 1)
        bucket = jnp.where(n < max_exact, n, large)
        bias = jnp.zeros((MOBA_BLOCK, MOBA_BLOCK), F32)
        for r in range(REL_BUCKETS):
            bias = jnp.where(bucket == r, rb_ref[r, h], bias)
        o_ref[t] = jnp.where(rel >= 0, bias * LOG2E, MASK_NEG)


def _bias_tiles(rel_bias):
    return pl.pallas_call(
        _bias_tiles_kernel,
        grid=(ATT_HEADS,),
        in_specs=[pl.BlockSpec(memory_space=pltpu.SMEM)],
        out_specs=pl.BlockSpec((None, 2, MOBA_BLOCK, MOBA_BLOCK), lambda h: (h, 0, 0, 0)),
        out_shape=jax.ShapeDtypeStruct((ATT_HEADS, 2, MOBA_BLOCK, MOBA_BLOCK), F32),
        compiler_params=_cparams(("arbitrary",)),
        name="bias_tiles",
    )(rel_bias)


FAR_BLOCKS = 4
MAX_LOG2_RISE = 64.0
HEADS_PER_STEP = 4


def _moba_kernel(rb_ref, qt_ref, kx_ref, vt_ref, km_ref, bias_ref, o_ref, *, nb):
    i = pl.program_id(2)
    jp = jnp.maximum(i - 1, 0)
    prev0 = pl.multiple_of(jp * MOBA_BLOCK, MOBA_BLOCK)
    own0 = pl.multiple_of(i * MOBA_BLOCK, MOBA_BLOCK)
    no_prev = jnp.where(i > 0, 0.0, MASK_NEG)
    neg_inf = jnp.float32(-jnp.inf)

    heads = range(HEADS_PER_STEP)

    def choose_blocks(hh):
        km_hi, km_lo = _split_bf16(km_ref[hh])
        score = (jnp.dot(km_hi, qt_ref[hh], preferred_element_type=F32)
                 + jnp.dot(km_lo, qt_ref[hh], preferred_element_type=F32))
        bid = lax.broadcasted_iota(jnp.int32, score.shape, 0)
        score = jnp.where(bid < i, score, neg_inf)
        chosen = bid < 0
        for _ in range(MOBA_TOPK):
            mx = jnp.max(score, axis=0, keepdims=True)
            hit = (score == mx) & (mx > neg_inf)
            idx = jnp.min(jnp.where(hit, bid, nb), axis=0, keepdims=True)
            pick = bid == idx
            chosen = chosen | pick
            score = jnp.where(pick, neg_inf, score)
        return chosen

    def with_mask(hh, val):
        hi, lo = _split_bf16(val)
        fill = jnp.zeros((HEAD_DIM - 2 * EXT_LO, MOBA_BLOCK), BF16)
        if nb < EXT_LO:
            gap = jnp.zeros((EXT_LO - nb, MOBA_BLOCK), BF16)
            return jnp.concatenate([qt_ref[hh], hi, gap, lo, gap, fill], axis=0)
        return jnp.concatenate([qt_ref[hh], hi, lo, fill], axis=0)

    chosen = [choose_blocks(hh) for hh in heads]
    bid = lax.broadcasted_iota(jnp.int32, chosen[0].shape, 0)
    far_bias = [rb_ref[REL_BUCKETS - 1, pl.program_id(1) * HEADS_PER_STEP + hh] * LOG2E for hh in heads]
    qx_fars = [with_mask(hh, jnp.where(chosen[hh] & (bid <= i - 2), far_bias[hh], MASK_NEG)) for hh in heads]
    qx_nears = [with_mask(hh, jnp.where((bid == i) | (chosen[hh] & (bid == i - 1)), 0.0, MASK_NEG))
                for hh in heads]

    ss = [jnp.dot(jnp.concatenate([kx_ref[hh, pl.ds(prev0, MOBA_BLOCK), :],
                                   kx_ref[hh, pl.ds(own0, MOBA_BLOCK), :]], axis=0),
                  qx_nears[hh], preferred_element_type=F32)
          + jnp.concatenate([bias_ref[hh, 1] + no_prev, bias_ref[hh, 0]], axis=0) for hh in heads]
    ms = [jnp.max(ss[hh], axis=0, keepdims=True) for hh in heads]
    ps = [jnp.exp2(ss[hh] - ms[hh]).astype(BF16) for hh in heads]
    accs = [jnp.dot(jnp.concatenate([vt_ref[hh, jp], vt_ref[hh, i]], axis=1), ps[hh],
                    preferred_element_type=F32) for hh in heads]

    def far_step(c, carry):
        row0 = pl.multiple_of(c * (FAR_BLOCKS * MOBA_BLOCK), FAR_BLOCKS * MOBA_BLOCK)
        m_olds = [carry[hh][0] for hh in heads]
        accs = [carry[hh][1] for hh in heads]

        def logits(hh):
            return jnp.dot(kx_ref[hh, pl.ds(row0, FAR_BLOCKS * MOBA_BLOCK), :], qx_fars[hh],
                           preferred_element_type=F32)

        def values(hh):
            return jnp.concatenate([vt_ref[hh, c * FAR_BLOCKS + k] for k in range(FAR_BLOCKS)], axis=1)

        ss = [logits(hh) for hh in heads]
        ps = [jnp.exp2(ss[hh] - m_olds[hh]).astype(BF16) for hh in heads]
        m_news = [jnp.maximum(m_olds[hh], jnp.max(ss[hh], axis=0, keepdims=True)) for hh in heads]
        alphas = [jnp.exp2(m_olds[hh] - m_news[hh]) for hh in heads]
        usual = tuple((m_news[hh], alphas[hh] * (accs[hh] + jnp.dot(values(hh), ps[hh],
                                                                    preferred_element_type=F32)))
                      for hh in heads)
        rise = functools.reduce(jnp.maximum, [jnp.max(m_news[hh] - m_olds[hh]) for hh in heads])

        def redo():
            out = []
            for hh in heads:
                p = jnp.exp2(logits(hh) - m_news[hh]).astype(BF16)
                out.append((m_news[hh], alphas[hh] * accs[hh]
                            + jnp.dot(values(hh), p, preferred_element_type=F32)))
            return tuple(out)

        return lax.cond(rise > MAX_LOG2_RISE, redo, lambda: usual)

    n_far = lax.shift_right_logical(jnp.maximum(i - 1, 0) + FAR_BLOCKS - 1, 2)
    final = lax.fori_loop(0, n_far, far_step, tuple((ms[hh], accs[hh]) for hh in heads))
    for hh in heads:
        acc = final[hh][1]
        out = acc[0:HEAD_DIM] / acc[HEAD_DIM:HEAD_DIM + 1]
        o_ref[:, hh * HEAD_DIM:(hh + 1) * HEAD_DIM] = out.T.astype(BF16)


def _moba(rel_bias, qt, kx, vt, kmean, bias_tiles, batch, seq):
    nb = seq // MOBA_BLOCK
    hps = HEADS_PER_STEP
    assert nb % FAR_BLOCKS == 0 and FAR_BLOCKS == 4 and ATT_HEADS % hps == 0
    bh = lambda shape: pl.BlockSpec((None, hps) + shape, lambda b, h, i: (b, h) + (0,) * len(shape),
                                    pipeline_mode=pl.Buffered(1))
    return pl.pallas_call(
        functools.partial(_moba_kernel, nb=nb),
        grid=(batch, ATT_HEADS // hps, nb),
        in_specs=[pl.BlockSpec(memory_space=pltpu.SMEM),
                  pl.BlockSpec((None, hps, None, HEAD_DIM, MOBA_BLOCK), lambda b, h, i: (b, h, i, 0, 0)),
                  bh((seq, 2 * HEAD_DIM)), bh((nb, VT_ROWS, MOBA_BLOCK)), bh((nb, HEAD_DIM)),
                  pl.BlockSpec((hps, 2, MOBA_BLOCK, MOBA_BLOCK), lambda b, h, i: (h, 0, 0, 0))],
        out_specs=pl.BlockSpec((MOBA_BLOCK, hps * HEAD_DIM), lambda b, h, i: (b * nb + i, h)),
        out_shape=jax.ShapeDtypeStruct((batch * seq, D_ATT), BF16),
        compiler_params=_cparams(("parallel", "parallel", "arbitrary")),
        name="moba",
    )(rel_bias, qt, kx, vt, kmean, bias_tiles)


def _merge_kernel(x_ref, ysg_ref, yatt_ref, gsg_ref, gatt_ref, mod_ref, nw_ref,
                  wsg_ref, watt_ref, wo_ref, wr_ref, x1_ref, tok_ref, cls_ref):
    tm = x_ref.shape[0]
    a_sg = jnp.dot(ysg_ref[...], wsg_ref[...], preferred_element_type=F32)
    a_att = jnp.dot(yatt_ref[...], watt_ref[...], preferred_element_type=F32)
    merged = (jax.nn.sigmoid(gsg_ref[...].astype(F32)) * a_sg
              + jax.nn.sigmoid(gatt_ref[...].astype(F32)) * a_att).astype(BF16)
    mixed = jnp.dot(merged, wo_ref[...], preferred_element_type=F32)
    x1 = x_ref[...] + mod_ref[0, 2:3, :] * mixed
    x1_ref[...] = x1
    ms = jnp.mean(x1 * x1, axis=-1, keepdims=True)
    scale = nw_ref[...] * (1.0 + mod_ref[0, 4:5, :])
    h2 = x1 * lax.rsqrt(ms + EPS) * scale + mod_ref[0, 3:4, :]
    for a in range(D_ROWS):
        tok_ref[pl.ds(a, tm, stride=TOK_ROWS), :] = h2[:, a * LANES:(a + 1) * LANES]
    for a in range(D_ROWS + 1, TOK_ROWS):
        tok_ref[pl.ds(a, tm, stride=TOK_ROWS), :] = jnp.zeros((tm, LANES), F32)

    w_hi, w_lo = _split_bf16(wr_ref[...])
    h_hi, h_lo = _split_bf16(h2)
    hi_both = jnp.dot(h_hi, jnp.concatenate([w_hi, w_lo], axis=1), preferred_element_type=F32)
    logits = (hi_both[:, :LANES] + jnp.dot(h_lo, w_hi, preferred_element_type=F32)
              + hi_both[:, LANES:])

    lane = lax.broadcasted_iota(jnp.int32, logits.shape, 1)
    neg_inf = jnp.float32(-jnp.inf)
    in_g = lane < N_GROUPS
    lg = jnp.where(in_g, logits, neg_inf)
    mg = jnp.max(lg, axis=1, keepdims=True)
    eg = jnp.exp(lg - mg)
    g_prob = eg / jnp.sum(eg, axis=1, keepdims=True)
    g_p = jnp.max(g_prob, axis=1, keepdims=True)
    g_idx = jnp.min(jnp.where(g_prob == g_p, lane, LANES), axis=1, keepdims=True)
    lo_col = ROUTER_COL0 + EXPERTS_PER_GROUP * g_idx
    in_e = (lane >= lo_col) & (lane < lo_col + EXPERTS_PER_GROUP)
    le = jnp.where(in_e, logits, neg_inf)
    me = jnp.max(le, axis=1, keepdims=True)
    ee = jnp.exp(le - me)
    e_prob = jnp.where(in_e, ee / jnp.sum(ee, axis=1, keepdims=True), -1.0)
    p1 = jnp.max(e_prob, axis=1, keepdims=True)
    i1 = jnp.min(jnp.where(e_prob == p1, lane, LANES), axis=1, keepdims=True)
    rest = jnp.where(lane == i1, -1.0, e_prob)
    p2 = jnp.max(rest, axis=1, keepdims=True)
    i2 = jnp.min(jnp.where(rest == p2, lane, LANES), axis=1, keepdims=True)
    denom = p1 + p2
    ea = jnp.minimum(i1, i2) - lo_col
    eb = jnp.maximum(i1, i2) - lo_col
    pair = lax.shift_right_logical(ea * (2 * EXPERTS_PER_GROUP - 1 - ea), 1) + eb - ea - 1
    cls = g_idx * PAIRS_PER_GROUP + pair
    routing = (jnp.where(lane == i1, g_p * (p1 / denom), 0.0)
               + jnp.where(lane == i2, g_p * (p2 / denom), 0.0)
               + jnp.where(lane == CLASS_LANE, cls.astype(F32), 0.0))
    tok_ref[pl.ds(D_ROWS, tm, stride=TOK_ROWS), :] = routing
    cls_ref[...] = routing.T[CLASS_LANE:CLASS_LANE + 1, :].astype(jnp.int32)


def _merge(x2, y_sg, y_att, proj, mod, norm2_w, w_sg, w_att, w_o, w_router, seq):
    n, d = x2.shape
    tm = MERGE_TM
    tpb = seq // tm
    resident = lambda shape: pl.BlockSpec(shape, lambda i: (0, 0), pipeline_mode=pl.Buffered(1))
    return pl.pallas_call(
        _merge_kernel,
        grid=(n // tm,),
        in_specs=[pl.BlockSpec((tm, d), lambda i: (i, 0)),
                  pl.BlockSpec((tm, D_SG), lambda i: (i, 0)),
                  pl.BlockSpec((tm, D_ATT), lambda i: (i, 0)),
                  pl.BlockSpec((tm, d), lambda i: (i, BLK_GATE_SG)),
                  pl.BlockSpec((tm, d), lambda i: (i, BLK_GATE_ATT)),
                  pl.BlockSpec((1, 8, d), lambda i: (i // tpb, 0, 0)),
                  pl.BlockSpec((1, d), lambda i: (0, 0)),
                  resident((D_SG, d)), resident((D_ATT, d)), resident((d, d)),
                  resident((d, LANES))],
        out_specs=[pl.BlockSpec((tm, d), lambda i: (i, 0)),
                   pl.BlockSpec((tm * TOK_ROWS, LANES), lambda i: (i, 0)),
                   pl.BlockSpec((None, 1, tm), lambda i: (i, 0, 0))],
        out_shape=[jax.ShapeDtypeStruct((n, d), F32),
                   jax.ShapeDtypeStruct((n * TOK_ROWS, LANES), F32),
                   jax.ShapeDtypeStruct((n // tm, 1, tm), jnp.int32)],
        compiler_params=_cparams(("parallel",)),
        name="merge",
    )(x2, y_sg, y_att, proj, proj, mod, norm2_w, w_sg, w_att, w_o, w_router)


def _invert_kernel(pos_ref, free_ref, inv_ref, *, n_tok, n_free):
    def place(t, carry):
        inv_ref[pos_ref[t]] = t
        return carry

    def place_free(k, carry):
        inv_ref[free_ref[k]] = n_tok + k
        return carry

    lax.fori_loop(0, n_tok, place, 0, unroll=8)
    lax.fori_loop(0, n_free, place_free, 0, unroll=8)


def _invert(pos, free):
    n_tok, n_free = pos.shape[0], free.shape[0]
    return pl.pallas_call(
        functools.partial(_invert_kernel, n_tok=n_tok, n_free=n_free),
        grid_spec=pltpu.PrefetchScalarGridSpec(
            num_scalar_prefetch=2, grid=(1,), in_specs=[],
            out_specs=pl.BlockSpec(memory_space=pltpu.SMEM)),
        out_shape=jax.ShapeDtypeStruct((n_tok + n_free,), jnp.int32),
        compiler_params=_cparams(("arbitrary",)),
        name="invert",
    )(pos, free)


MOE_TILE = 256
ROW_UNROLL_LOG2 = 3
ROW_UNROLL = 1 << ROW_UNROLL_LOG2


def _for_rows(rows, fn):
    groups = lax.shift_right_logical(rows, ROW_UNROLL_LOG2)

    def group(g, carry):
        for u in range(ROW_UNROLL):
            fn(g * ROW_UNROLL + u)
        return carry

    def single(r, carry):
        fn(r)
        return carry

    lax.fori_loop(0, groups, group, 0)
    lax.fori_loop(groups * ROW_UNROLL, rows, single, 0)


def _moe_kernel(src_ref, dst_ref, ea_ref, eb_ref, rows_ref, na_ref, tok_hbm,
                w1a_ref, w3a_ref, w2a_ref, w1b_ref, w3b_ref, w2b_ref, y_hbm,
                gbuf, stage, acc_ref, gsem, ssem):
    i = pl.program_id(0)
    n_act = na_ref[0]
    slot = lax.rem(i, 2)

    def gather_start(tile, sl):
        def one(r, carry):
            row0 = pl.multiple_of(r * TOK_ROWS, 8)
            pltpu.make_async_copy(tok_hbm.at[src_ref[tile * MOE_TILE + r]],
                                  gbuf.at[sl, pl.ds(row0, TOK_ROWS), :], gsem.at[sl]).start()
            return carry
        lax.fori_loop(0, MOE_TILE, one, 0, unroll=8)

    def gather_wait(sl):
        for _ in range(MOE_TILE):
            pltpu.make_async_copy(tok_hbm.at[0], gbuf.at[sl, pl.ds(0, TOK_ROWS), :], gsem.at[sl]).wait()

    def scatter_start(tile, sl):
        def one(r):
            row0 = pl.multiple_of(r * D_ROWS, 8)
            pltpu.make_async_copy(stage.at[sl, pl.ds(row0, D_ROWS), :],
                                  y_hbm.at[dst_ref[tile * MOE_TILE + r]], ssem.at[sl]).start()
        _for_rows(rows_ref[tile], one)

    def scatter_wait(tile, sl):
        def one(r):
            pltpu.make_async_copy(stage.at[sl, pl.ds(0, D_ROWS), :], y_hbm.at[0], ssem.at[sl]).wait()
        _for_rows(rows_ref[tile], one)

    @pl.when(i < n_act)
    def _():
        @pl.when(i == 0)
        def _():
            gather_start(0, 0)

        @pl.when(i + 1 < n_act)
        def _():
            gather_start(i + 1, 1 - slot)

        gather_wait(slot)
        t = jnp.concatenate([gbuf[slot, pl.ds(a, MOE_TILE, stride=TOK_ROWS), :] for a in range(D_ROWS)],
                            axis=1).astype(BF16)
        routing = gbuf[slot, pl.ds(D_ROWS, MOE_TILE, stride=TOK_ROWS), :]
        lane = lax.broadcasted_iota(jnp.int32, routing.shape, 1)
        experts = ((ea_ref[i], w1a_ref, w3a_ref, w2a_ref), (eb_ref[i], w1b_ref, w3b_ref, w2b_ref))
        ups = [(jnp.dot(t, w1_ref[...], preferred_element_type=F32),
                jnp.dot(t, w3_ref[...], preferred_element_type=F32)) for _, w1_ref, w3_ref, _ in experts]
        hmids = []
        for (e, _, _, _), (a, b) in zip(experts, ups):
            ge = jnp.sum(jnp.where(lane == ROUTER_COL0 + e, routing, 0.0), axis=1, keepdims=True)
            hmids.append(((a * jax.nn.sigmoid(a)) * b * ge).astype(BF16))
        acc_ref[...] = (jnp.dot(hmids[0], w2a_ref[...].astype(BF16), preferred_element_type=F32)
                        + jnp.dot(hmids[1], w2b_ref[...].astype(BF16), preferred_element_type=F32))

        @pl.when(i >= 2)
        def _():
            scatter_wait(i - 2, slot)

        for a in range(D_ROWS):
            stage[slot, pl.ds(a, MOE_TILE, stride=D_ROWS), :] = acc_ref[:, a * LANES:(a + 1) * LANES]
        scatter_start(i, slot)

        @pl.when(i == n_act - 1)
        def _():
            scatter_wait(i, slot)

            @pl.when(i >= 1)
            def _():
                scatter_wait(i - 1, 1 - slot)


def _moe(src_tok, dst_row, tile_ea, tile_eb, tile_rows, n_active, tok_blocks, w1, w3, w2, n_tok):
    n_slots = src_tok.shape[0]
    d = D_MODEL
    blk_a = lambda i, src, dst, ea, eb, rows, na: (ea[i], 0, 0)
    blk_b = lambda i, src, dst, ea, eb, rows, na: (eb[i], 0, 0)
    up = lambda blk: pl.BlockSpec((None, d, D_EXPERT), blk)
    down = lambda blk: pl.BlockSpec((None, D_EXPERT, d), blk)
    return pl.pallas_call(
        _moe_kernel,
        grid_spec=pltpu.PrefetchScalarGridSpec(
            num_scalar_prefetch=6, grid=(n_slots // MOE_TILE,),
            in_specs=[pl.BlockSpec(memory_space=pl.ANY),
                      up(blk_a), up(blk_a), down(blk_a), up(blk_b), up(blk_b), down(blk_b)],
            out_specs=pl.BlockSpec(memory_space=pl.ANY),
            scratch_shapes=[pltpu.VMEM((2, MOE_TILE * TOK_ROWS, LANES), F32),
                            pltpu.VMEM((2, MOE_TILE * D_ROWS, LANES), F32),
                            pltpu.VMEM((MOE_TILE, d), F32),
                            pltpu.SemaphoreType.DMA((2,)),
                            pltpu.SemaphoreType.DMA((2,))]),
        out_shape=jax.ShapeDtypeStruct((n_tok, D_ROWS, LANES), F32),
        compiler_params=_cparams(("arbitrary",)),
        name="moe",
    )(src_tok, dst_row, tile_ea, tile_eb, tile_rows, n_active, tok_blocks, w1, w3, w2, w1, w3, w2)


def _final_kernel(x1_ref, y_ref, mod_ref, fw_ref, o_ref):
    tm = x1_ref.shape[0]
    parts = []
    ss = jnp.zeros((tm, 1), F32)
    for a in range(D_ROWS):
        cols = slice(a * LANES, (a + 1) * LANES)
        part = x1_ref[:, cols] + mod_ref[0, 5:6, cols] * y_ref[pl.ds(a, tm, stride=D_ROWS), :]
        ss = ss + jnp.sum(part * part, axis=1, keepdims=True)
        parts.append(part)
    inv = lax.rsqrt(ss * (1.0 / D_MODEL) + EPS)
    for a in range(D_ROWS):
        cols = slice(a * LANES, (a + 1) * LANES)
        o_ref[:, cols] = parts[a] * inv * fw_ref[:, cols]


def _final(x1, y_blocks, mod, final_w, seq):
    n, d = x1.shape
    tm = FINAL_TM
    tpb = seq // tm
    return pl.pallas_call(
        _final_kernel,
        grid=(n // tm,),
        in_specs=[pl.BlockSpec((tm, d), lambda i: (i, 0)),
                  pl.BlockSpec((tm * D_ROWS, LANES), lambda i: (i, 0)),
                  pl.BlockSpec((1, 8, d), lambda i: (i // tpb, 0, 0)),
                  pl.BlockSpec((1, d), lambda i: (0, 0))],
        out_specs=pl.BlockSpec((tm, d), lambda i: (i, 0)),
        out_shape=jax.ShapeDtypeStruct((n, d), F32),
        compiler_params=_cparams(("parallel",)),
        name="final",
    )(x1, y_blocks, mod, final_w)


def _class_sort_plan(cls, n):
    class_ids = jnp.arange(N_CLASSES, dtype=jnp.int32)
    onehot = (cls[:, None] == class_ids[None, :]).astype(jnp.int32)
    blk = LANES
    within = jnp.einsum("ts,bsg->btg", jnp.tril(jnp.ones((blk, blk), F32)),
                        onehot.astype(F32).reshape(n // blk, blk, N_CLASSES))
    before = jnp.tril(jnp.ones((n // blk, n // blk), F32), -1) @ within[:, -1, :]
    incl = (within + before[:, None, :]).reshape(n, N_CLASSES).astype(jnp.int32)
    count = incl[-1]
    tiles = (count + MOE_TILE - 1) // MOE_TILE
    tile_end = jnp.cumsum(tiles)
    start = (tile_end - tiles) * MOE_TILE
    rank = jnp.sum(incl * onehot, axis=1) - 1
    pos = jnp.sum(onehot * start[None, :], axis=1) + rank
    n_tiles = n // MOE_TILE + N_CLASSES
    tile_ids = jnp.arange(n_tiles, dtype=jnp.int32)
    tile_class = jnp.sum((tile_ids[:, None] >= tile_end[None, :]).astype(jnp.int32), axis=1)
    tile_class = jnp.minimum(tile_class, N_CLASSES - 1)
    tile_hot = (tile_class[:, None] == class_ids[None, :]).astype(jnp.int32)
    done = (tile_ids - jnp.sum(tile_hot * (tile_end - tiles)[None, :], axis=1)) * MOE_TILE
    tile_rows = jnp.clip(jnp.sum(tile_hot * count[None, :], axis=1) - done, 0, MOE_TILE)
    first = jnp.asarray([a for a, _ in PAIRS], jnp.int32)
    second = jnp.asarray([b for _, b in PAIRS], jnp.int32)
    group0 = (class_ids // PAIRS_PER_GROUP) * EXPERTS_PER_GROUP
    tile_ea = jnp.sum(tile_hot * (group0 + first[class_ids % PAIRS_PER_GROUP])[None, :], axis=1)
    tile_eb = jnp.sum(tile_hot * (group0 + second[class_ids % PAIRS_PER_GROUP])[None, :], axis=1)
    pad = tiles * MOE_TILE - count
    pad_end = jnp.cumsum(pad)
    k = jnp.arange(N_CLASSES * MOE_TILE, dtype=jnp.int32)
    seg = jnp.sum((k[:, None] >= pad_end[None, :]).astype(jnp.int32), axis=1)
    seg_hot = (seg[:, None] == class_ids[None, :]).astype(jnp.int32)
    in_class = jnp.sum(seg_hot * (start + count - (pad_end - pad))[None, :], axis=1) + k
    tail = tile_end[-1] * MOE_TILE + k - pad_end[-1]
    free = jnp.where(seg < N_CLASSES, in_class, tail)
    i32 = lambda a: a.astype(jnp.int32)
    return i32(pos), i32(free), i32(tile_ea), i32(tile_eb), i32(tile_rows), i32(tile_end[-1:])


def kernel(x, c, w_ada, b_ada, norm1_w, norm2_w, final_norm_w, w_in, sg_ln_w, sg_ln_b, w_spatial,
           b_spatial, rel_bias, w_out_sg, w_out_att, w_o, w_router_group, w_router_expert,
           w_exp_gate, w_exp_up, w_exp_down):
    batch, seq, d = x.shape
    assert d == D_MODEL and w_ada.shape[0] == 1
    assert seq % PROJ_TM == 0 and seq // MOBA_BLOCK <= EXT_LO
    n = batch * seq
    x2 = x.reshape(n, d)

    assert batch <= BF16_SUBLANES
    c_pad = jnp.zeros((BF16_SUBLANES, d), F32).at[:batch].set(c)
    mod = _adaln(c_pad, w_ada[0], b_ada[0].reshape(1, 6 * d))
    mod = mod[:batch].reshape(batch, 6, d)
    mod = jnp.concatenate([mod, jnp.zeros((batch, 2, d), F32)], axis=1)

    proj = _proj(x2, mod, norm1_w[0].reshape(1, d), w_in[0], seq)

    y_sg = _sgu(proj, sg_ln_w[0].reshape(1, D_SG), sg_ln_b[0].reshape(1, D_SG),
                w_spatial[0], b_spatial[0].T)

    qt, kx, vt, kmean = _moba_prep(proj, batch, seq)
    bias_tiles = _bias_tiles(rel_bias)
    y_att = _moba(rel_bias, qt, kx, vt, kmean, bias_tiles, batch, seq)

    w_router = jnp.zeros((d, LANES), F32)
    w_router = w_router.at[:, :N_GROUPS].set(w_router_group[0])
    w_router = w_router.at[:, ROUTER_COL0:ROUTER_COL0 + N_EXPERTS].set(w_router_expert[0])
    x1, tok, cls = _merge(x2, y_sg, y_att, proj, mod, norm2_w[0].reshape(1, d),
                          w_out_sg[0].astype(BF16), w_out_att[0].astype(BF16),
                          w_o[0].astype(BF16), w_router, seq)

    tok = tok.reshape(n, TOK_ROWS, LANES)
    pos, free, tile_ea, tile_eb, tile_rows, n_active = _class_sort_plan(cls.reshape(n), n)
    inv = _invert(pos, free)
    src_tok = jnp.where(inv < n, inv, 0)
    y = _moe(src_tok, inv, tile_ea, tile_eb, tile_rows, n_active, tok, w_exp_gate[0].astype(BF16),
             w_exp_up[0].astype(BF16), w_exp_down[0], n)
    out = _final(x1, y.reshape(-1, LANES), mod, final_norm_w.reshape(1, d), seq)
    return out.reshape(batch, seq, d)
```

```python
import functools
import math

import jax
import jax.numpy as jnp
from jax import lax
from jax.experimental import pallas as pl
from jax.experimental.pallas import tpu as pltpu

F32 = jnp.float32
BF16 = jnp.bfloat16

LANES = 128
BF16_SUBLANES = 16
D_MODEL = 2048
D_SG = D_MODEL // 2
SG_GROUPS = 8
SG_CHUNK = 128
ATT_HEADS = 8
HEAD_DIM = 128
D_ATT = ATT_HEADS * HEAD_DIM
MOBA_BLOCK = 256
MOBA_TOPK = 3
REL_BUCKETS = 32
REL_MAX_DIST = 128
N_GROUPS = 4
EXPERTS_PER_GROUP = 4
N_EXPERTS = N_GROUPS * EXPERTS_PER_GROUP
D_EXPERT = 512
EPS = 1e-6
IN_COLS = 2 * D_SG + 3 * D_ATT + 2 * D_MODEL

N_GATE_COLS = 2 * D_MODEL
BLK_GATE_SG = 0
BLK_GATE_ATT = 1
BLK_U = N_GATE_COLS // D_SG
BLK_V = BLK_U + 1
CB_Q = (N_GATE_COLS + 2 * D_SG) // LANES
CB_K = CB_Q + ATT_HEADS
CB_VAL = CB_K + ATT_HEADS

MASK_NEG = -1e9
LOG2E = math.log2(math.e)
ROUTER_COL0 = N_GROUPS
CLASS_LANE = 0
PAIRS_PER_GROUP = EXPERTS_PER_GROUP * (EXPERTS_PER_GROUP - 1) // 2
N_CLASSES = N_GROUPS * PAIRS_PER_GROUP
PAIRS = [(a, b) for a in range(EXPERTS_PER_GROUP) for b in range(a + 1, EXPERTS_PER_GROUP)]
D_ROWS = D_MODEL // LANES
TOK_ROWS = D_ROWS + 8
VMEM_LIMIT = 56 * 1024 * 1024

ADALN_TN = 1024
PROJ_TM = 1024
PROJ_TN = 1024
PROJ_CHUNK = 256
SGU_CHUNKS = 8
MERGE_TM = 256
FINAL_TM = 512


def _cparams(sem):
    return pltpu.CompilerParams(dimension_semantics=sem, vmem_limit_bytes=VMEM_LIMIT)


def _split_bf16(a):
    hi = a.astype(BF16)
    lo = (a - hi.astype(F32)).astype(BF16)
    return hi, lo


def _adaln_kernel(c_ref, w_ref, b_ref, o_ref):
    c = c_ref[...]
    rows = c.shape[0]
    c_hi, c_lo = _split_bf16(c * jax.nn.sigmoid(c))
    w_hi, w_lo = _split_bf16(w_ref[...])
    both = jnp.dot(jnp.concatenate([c_hi, c_lo], axis=0), w_hi, preferred_element_type=F32)
    o_ref[...] = (both[:rows] + both[rows:] + jnp.dot(c_hi, w_lo, preferred_element_type=F32)
                  + b_ref[...])


def _adaln(c_pad, w, b):
    rows, d = c_pad.shape
    cols = w.shape[1]
    tn = ADALN_TN
    return pl.pallas_call(
        _adaln_kernel,
        grid=(cols // tn,),
        in_specs=[pl.BlockSpec((rows, d), lambda j: (0, 0)),
                  pl.BlockSpec((d, tn), lambda j: (0, j)),
                  pl.BlockSpec((1, tn), lambda j: (0, j))],
        out_specs=pl.BlockSpec((rows, tn), lambda j: (0, j)),
        out_shape=jax.ShapeDtypeStruct((rows, cols), F32),
        compiler_params=_cparams(("arbitrary",)),
        name="adaln",
    )(c_pad, w, b)


def _gelu(a):
    return 0.5 * a * (1.0 + lax.erf(a * (1.0 / math.sqrt(2.0))))


_PROJ_FINISH = {
    "gate": lambda acc: acc,
    "sgu": _gelu,
    "q": lambda acc: acc * (HEAD_DIM ** -0.5 * LOG2E),
    "kv": lambda acc: acc,
}
_PROJ_SEGMENTS = (("gate", N_GATE_COLS // PROJ_TN), ("sgu", 2 * D_SG // PROJ_TN),
                  ("q", D_ATT // PROJ_TN), ("kv", 2 * D_ATT // PROJ_TN))


def _proj_kernel(x_ref, mod_ref, nw_ref, w_ref, o_ref, h_scr):
    j = pl.program_id(1)

    @pl.when(j == 0)
    def _():
        finish = _PROJ_FINISH[_PROJ_SEGMENTS[0][0]]
        w = w_ref[...].astype(BF16)
        scale = nw_ref[...] * (1.0 + mod_ref[0, 1:2, :])
        for k in range(x_ref.shape[0] // PROJ_CHUNK):
            rows = slice(k * PROJ_CHUNK, (k + 1) * PROJ_CHUNK)
            x = x_ref[rows, :]
            ms = jnp.mean(x * x, axis=-1, keepdims=True)
            h = (x * lax.rsqrt(ms + EPS) * scale + mod_ref[0, 0:1, :]).astype(BF16)
            h_scr[rows, :] = h
            o_ref[rows, :] = finish(jnp.dot(h, w, preferred_element_type=F32)).astype(BF16)

    start = 0
    for kind, blocks in _PROJ_SEGMENTS:
        @pl.when((j >= max(start, 1)) & (j < start + blocks))
        def _(kind=kind):
            acc = jnp.dot(h_scr[...], w_ref[...].astype(BF16), preferred_element_type=F32)
            o_ref[...] = _PROJ_FINISH[kind](acc).astype(BF16)
        start += blocks


def _proj(x2, mod, norm_w, w_in, seq):
    n, d = x2.shape
    cols = w_in.shape[1]
    tm = min(PROJ_TM, seq)
    tn = PROJ_TN
    tpb = seq // tm
    n_cb = cols // tn
    first = (cols - N_GATE_COLS) // tn
    assert (cols - N_GATE_COLS) % tn == 0
    assert sum(blocks * tn for _, blocks in _PROJ_SEGMENTS) == cols
    return pl.pallas_call(
        _proj_kernel,
        grid=(n // tm, n_cb),
        in_specs=[pl.BlockSpec((tm, d), lambda i, j: (i, 0)),
                  pl.BlockSpec((1, 8, d), lambda i, j: (i // tpb, 0, 0)),
                  pl.BlockSpec((1, d), lambda i, j: (0, 0)),
                  pl.BlockSpec((d, tn), lambda i, j: (0, (j + first) % n_cb))],
        out_specs=pl.BlockSpec((tm, tn), lambda i, j: (i, j)),
        out_shape=jax.ShapeDtypeStruct((n, cols), BF16),
        scratch_shapes=[pltpu.VMEM((tm, d), BF16)],
        compiler_params=_cparams(("parallel", "arbitrary")),
        name="proj",
    )(x2, mod, norm_w, w_in)


def _sgu_kernel(u_ref, v_ref, lnw_ref, lnb_ref, ws_ref, bst_ref, o_ref, *, chunks):
    u = u_ref[...].astype(F32)
    v = v_ref[...].astype(F32)
    mu = jnp.mean(v, axis=-1, keepdims=True)
    vc = v - mu
    var = jnp.mean(vc * vc, axis=-1, keepdims=True)
    vn = (vc * lax.rsqrt(var + EPS) * lnw_ref[...] + lnb_ref[...]).astype(BF16)
    row = lax.broadcasted_iota(jnp.int32, (SG_CHUNK, SG_CHUNK), 0)
    col = lax.broadcasted_iota(jnp.int32, (SG_CHUNK, SG_CHUNK), 1)
    causal = col <= row
    for g in range(SG_GROUPS):
        wm = jnp.where(causal, ws_ref[g], 0.0).astype(BF16)
        bcol = bst_ref[:, g:g + 1]
        gs = slice(g * LANES, (g + 1) * LANES)
        for c in range(chunks):
            rs = slice(c * SG_CHUNK, (c + 1) * SG_CHUNK)
            z = jnp.dot(wm, vn[rs, gs], preferred_element_type=F32) + bcol
            o_ref[rs, gs] = (u[rs, gs] * z).astype(BF16)


def _sgu(proj, ln_w, ln_b, w_s, b_s_t):
    n = proj.shape[0]
    chunks = SGU_CHUNKS
    tm = chunks * SG_CHUNK
    return pl.pallas_call(
        functools.partial(_sgu_kernel, chunks=chunks),
        grid=(n // tm,),
        in_specs=[pl.BlockSpec((tm, D_SG), lambda i: (i, BLK_U)),
                  pl.BlockSpec((tm, D_SG), lambda i: (i, BLK_V)),
                  pl.BlockSpec((1, D_SG), lambda i: (0, 0)),
                  pl.BlockSpec((1, D_SG), lambda i: (0, 0)),
                  pl.BlockSpec((SG_GROUPS, SG_CHUNK, SG_CHUNK), lambda i: (0, 0, 0)),
                  pl.BlockSpec((SG_CHUNK, SG_GROUPS), lambda i: (0, 0))],
        out_specs=pl.BlockSpec((tm, D_SG), lambda i: (i, 0)),
        out_shape=jax.ShapeDtypeStruct((n, D_SG), BF16),
        compiler_params=_cparams(("parallel",)),
        name="sgu",
    )(proj, proj, ln_w, ln_b, w_s, b_s_t)


VT_ROWS = HEAD_DIM + 16
EXT_LO = REL_BUCKETS


def _moba_prep_kernel(q_ref, k_ref, v_ref, qt_ref, kx_ref, vt_ref, km_ref, *, nb):
    lane = lax.broadcasted_iota(jnp.int32, (MOBA_BLOCK, LANES), 1)
    pad = jnp.concatenate([jnp.ones((1, MOBA_BLOCK), F32),
                           jnp.zeros((VT_ROWS - HEAD_DIM - 1, MOBA_BLOCK), F32)], axis=0)
    for j in range(nb):
        rows = slice(j * MOBA_BLOCK, (j + 1) * MOBA_BLOCK)
        kj = k_ref[rows, :]
        onehot = jnp.where((lane == j) | (lane == j + EXT_LO), 1.0, 0.0).astype(BF16)
        kx_ref[rows, :] = jnp.concatenate([kj, onehot], axis=1)
        km_ref[j:j + 1, :] = jnp.mean(kj.astype(F32), axis=0, keepdims=True)
        vt = v_ref[rows, :].astype(F32).T
        vt_ref[j] = jnp.concatenate([vt, pad], axis=0).astype(BF16)
        qt_ref[j] = q_ref[rows, :].astype(F32).T.astype(BF16)


def _moba_prep(proj, batch, seq):
    nb = seq // MOBA_BLOCK
    bh = lambda shape: pl.BlockSpec((None, None) + shape, lambda b, h: (b, h) + (0,) * len(shape))
    return pl.pallas_call(
        functools.partial(_moba_prep_kernel, nb=nb),
        grid=(batch, ATT_HEADS),
        in_specs=[pl.BlockSpec((seq, HEAD_DIM), lambda b, h: (b, CB_Q + h)),
                  pl.BlockSpec((seq, HEAD_DIM), lambda b, h: (b, CB_K + h)),
                  pl.BlockSpec((seq, HEAD_DIM), lambda b, h: (b, CB_VAL + h))],
        out_specs=[bh((nb, HEAD_DIM, MOBA_BLOCK)), bh((seq, 2 * HEAD_DIM)),
                   bh((nb, VT_ROWS, MOBA_BLOCK)), bh((nb, HEAD_DIM))],
        out_shape=[jax.ShapeDtypeStruct((batch, ATT_HEADS, nb, HEAD_DIM, MOBA_BLOCK), BF16),
                   jax.ShapeDtypeStruct((batch, ATT_HEADS, seq, 2 * HEAD_DIM), BF16),
                   jax.ShapeDtypeStruct((batch, ATT_HEADS, nb, VT_ROWS, MOBA_BLOCK), BF16),
                   jax.ShapeDtypeStruct((batch, ATT_HEADS, nb, HEAD_DIM), F32)],
        compiler_params=_cparams(("parallel", "parallel")),
        name="moba_prep",
    )(proj, proj, proj)


def _bias_tiles_kernel(rb_ref, o_ref):
    h = pl.program_id(0)
    kj = lax.broadcasted_iota(jnp.int32, (MOBA_BLOCK, MOBA_BLOCK), 0)
    qi = lax.broadcasted_iota(jnp.int32, (MOBA_BLOCK, MOBA_BLOCK), 1)
    max_exact = REL_BUCKETS // 2
    for t in range(2):
        rel = qi - kj + MOBA_BLOCK * t
        n = jnp.maximum(rel, 0)
        nf = jnp.maximum(n, max_exact).astype(F32)
        large = max_exact + (jnp.log(nf / max_exact) / math.log(REL_MAX_DIST / max_exact)
                             * (REL_BUCKETS - max_exact)).astype(jnp.int32)
        large = jnp.minimum(large, REL_BUCKETS - 1)
        bucket = jnp.where(n < max_exact, n, large)
        bias = jnp.zeros((MOBA_BLOCK, MOBA_BLOCK), F32)
        for r in range(REL_BUCKETS):
            bias = jnp.where(bucket == r, rb_ref[r, h], bias)
        o_ref[t] = jnp.where(rel >= 0, bias * LOG2E, MASK_NEG)


def _bias_tiles(rel_bias):
    return pl.pallas_call(
        _bias_tiles_kernel,
        grid=(ATT_HEADS,),
        in_specs=[pl.BlockSpec(memory_space=pltpu.SMEM)],
        out_specs=pl.BlockSpec((None, 2, MOBA_BLOCK, MOBA_BLOCK), lambda h: (h, 0, 0, 0)),
        out_shape=jax.ShapeDtypeStruct((ATT_HEADS, 2, MOBA_BLOCK, MOBA_BLOCK), F32),
        compiler_params=_cparams(("arbitrary",)),
        name="bias_tiles",
    )(rel_bias)


FAR_BLOCKS = 4
MAX_LOG2_RISE = 64.0
HEADS_PER_STEP = 4


def _moba_kernel(rb_ref, qt_ref, kx_ref, vt_ref, km_ref, bias_ref, o_ref, *, nb):
    i = pl.program_id(2)
    jp = jnp.maximum(i - 1, 0)
    prev0 = pl.multiple_of(jp * MOBA_BLOCK, MOBA_BLOCK)
    own0 = pl.multiple_of(i * MOBA_BLOCK, MOBA_BLOCK)
    no_prev = jnp.where(i > 0, 0.0, MASK_NEG)
    neg_inf = jnp.float32(-jnp.inf)

    heads = range(HEADS_PER_STEP)

    def choose_blocks(hh):
        km_hi, km_lo = _split_bf16(km_ref[hh])
        score = (jnp.dot(km_hi, qt_ref[hh], preferred_element_type=F32)
                 + jnp.dot(km_lo, qt_ref[hh], preferred_element_type=F32))
        bid = lax.broadcasted_iota(jnp.int32, score.shape, 0)
        score = jnp.where(bid < i, score, neg_inf)
        chosen = bid < 0
        for _ in range(MOBA_TOPK):
            mx = jnp.max(score, axis=0, keepdims=True)
            hit = (score == mx) & (mx > neg_inf)
            idx = jnp.min(jnp.where(hit, bid, nb), axis=0, keepdims=True)
            pick = bid == idx
            chosen = chosen | pick
            score = jnp.where(pick, neg_inf, score)
        return chosen

    def with_mask(hh, val):
        hi, lo = _split_bf16(val)
        fill = jnp.zeros((HEAD_DIM - 2 * EXT_LO, MOBA_BLOCK), BF16)
        if nb < EXT_LO:
            gap = jnp.zeros((EXT_LO - nb, MOBA_BLOCK), BF16)
            return jnp.concatenate([qt_ref[hh], hi, gap, lo, gap, fill], axis=0)
        return jnp.concatenate([qt_ref[hh], hi, lo, fill], axis=0)

    chosen = [choose_blocks(hh) for hh in heads]
    bid = lax.broadcasted_iota(jnp.int32, chosen[0].shape, 0)
    far_bias = [rb_ref[REL_BUCKETS - 1, pl.program_id(1) * HEADS_PER_STEP + hh] * LOG2E for hh in heads]
    qx_fars = [with_mask(hh, jnp.where(chosen[hh] & (bid <= i - 2), far_bias[hh], MASK_NEG)) for hh in heads]
    qx_nears = [with_mask(hh, jnp.where((bid == i) | (chosen[hh] & (bid == i - 1)), 0.0, MASK_NEG))
                for hh in heads]

    ss = [jnp.dot(jnp.concatenate([kx_ref[hh, pl.ds(prev0, MOBA_BLOCK), :],
                                   kx_ref[hh, pl.ds(own0, MOBA_BLOCK), :]], axis=0),
                  qx_nears[hh], preferred_element_type=F32)
          + jnp.concatenate([bias_ref[hh, 1] + no_prev, bias_ref[hh, 0]], axis=0) for hh in heads]
    ms = [jnp.max(ss[hh], axis=0, keepdims=True) for hh in heads]
    ps = [jnp.exp2(ss[hh] - ms[hh]).astype(BF16) for hh in heads]
    accs = [jnp.dot(jnp.concatenate([vt_ref[hh, jp], vt_ref[hh, i]], axis=1), ps[hh],
                    preferred_element_type=F32) for hh in heads]

    def far_step(c, carry):
        row0 = pl.multiple_of(c * (FAR_BLOCKS * MOBA_BLOCK), FAR_BLOCKS * MOBA_BLOCK)
        m_olds = [carry[hh][0] for hh in heads]
        accs = [carry[hh][1] for hh in heads]

        def logits(hh):
            return jnp.dot(kx_ref[hh, pl.ds(row0, FAR_BLOCKS * MOBA_BLOCK), :], qx_fars[hh],
                           preferred_element_type=F32)

        def values(hh):
            return jnp.concatenate([vt_ref[hh, c * FAR_BLOCKS + k] for k in range(FAR_BLOCKS)], axis=1)

        ss = [logits(hh) for hh in heads]
        ps = [jnp.exp2(ss[hh] - m_olds[hh]).astype(BF16) for hh in heads]
        m_news = [jnp.maximum(m_olds[hh], jnp.max(ss[hh], axis=0, keepdims=True)) for hh in heads]
        alphas = [jnp.exp2(m_olds[hh] - m_news[hh]) for hh in heads]
        usual = tuple((m_news[hh], alphas[hh] * (accs[hh] + jnp.dot(values(hh), ps[hh],
                                                                    preferred_element_type=F32)))
                      for hh in heads)
        rise = functools.reduce(jnp.maximum, [jnp.max(m_news[hh] - m_olds[hh]) for hh in heads])

        def redo():
            out = []
            for hh in heads:
                p = jnp.exp2(logits(hh) - m_news[hh]).astype(BF16)
                out.append((m_news[hh], alphas[hh] * accs[hh]
                            + jnp.dot(values(hh), p, preferred_element_type=F32)))
            return tuple(out)

        return lax.cond(rise > MAX_LOG2_RISE, redo, lambda: usual)

    n_far = lax.shift_right_logical(jnp.maximum(i - 1, 0) + FAR_BLOCKS - 1, 2)
    final = lax.fori_loop(0, n_far, far_step, tuple((ms[hh], accs[hh]) for hh in heads))
    for hh in heads:
        acc = final[hh][1]
        out = acc[0:HEAD_DIM] / acc[HEAD_DIM:HEAD_DIM + 1]
        o_ref[:, hh * HEAD_DIM:(hh + 1) * HEAD_DIM] = out.T.astype(BF16)


def _moba(rel_bias, qt, kx, vt, kmean, bias_tiles, batch, seq):
    nb = seq // MOBA_BLOCK
    hps = HEADS_PER_STEP
    assert nb % FAR_BLOCKS == 0 and FAR_BLOCKS == 4 and ATT_HEADS % hps == 0
    bh = lambda shape: pl.BlockSpec((None, hps) + shape, lambda b, h, i: (b, h) + (0,) * len(shape),
                                    pipeline_mode=pl.Buffered(1))
    return pl.pallas_call(
        functools.partial(_moba_kernel, nb=nb),
        grid=(batch, ATT_HEADS // hps, nb),
        in_specs=[pl.BlockSpec(memory_space=pltpu.SMEM),
                  pl.BlockSpec((None, hps, None, HEAD_DIM, MOBA_BLOCK), lambda b, h, i: (b, h, i, 0, 0)),
                  bh((seq, 2 * HEAD_DIM)), bh((nb, VT_ROWS, MOBA_BLOCK)), bh((nb, HEAD_DIM)),
                  pl.BlockSpec((hps, 2, MOBA_BLOCK, MOBA_BLOCK), lambda b, h, i: (h, 0, 0, 0))],
        out_specs=pl.BlockSpec((MOBA_BLOCK, hps * HEAD_DIM), lambda b, h, i: (b * nb + i, h)),
        out_shape=jax.ShapeDtypeStruct((batch * seq, D_ATT), BF16),
        compiler_params=_cparams(("parallel", "parallel", "arbitrary")),
        name="moba",
    )(rel_bias, qt, kx, vt, kmean, bias_tiles)


def _merge_kernel(x_ref, ysg_ref, yatt_ref, gsg_ref, gatt_ref, mod_ref, nw_ref,
                  wsg_ref, watt_ref, wo_ref, wr_ref, x1_ref, tok_ref, cls_ref):
    tm = x_ref.shape[0]
    a_sg = jnp.dot(ysg_ref[...], wsg_ref[...], preferred_element_type=F32)
    a_att = jnp.dot(yatt_ref[...], watt_ref[...], preferred_element_type=F32)
    merged = (jax.nn.sigmoid(gsg_ref[...].astype(F32)) * a_sg
              + jax.nn.sigmoid(gatt_ref[...].astype(F32)) * a_att).astype(BF16)
    mixed = jnp.dot(merged, wo_ref[...], preferred_element_type=F32)
    x1 = x_ref[...] + mod_ref[0, 2:3, :] * mixed
    x1_ref[...] = x1
    ms = jnp.mean(x1 * x1, axis=-1, keepdims=True)
    scale = nw_ref[...] * (1.0 + mod_ref[0, 4:5, :])
    h2 = x1 * lax.rsqrt(ms + EPS) * scale + mod_ref[0, 3:4, :]
    for a in range(D_ROWS):
        tok_ref[pl.ds(a, tm, stride=TOK_ROWS), :] = h2[:, a * LANES:(a + 1) * LANES]
    for a in range(D_ROWS + 1, TOK_ROWS):
        tok_ref[pl.ds(a, tm, stride=TOK_ROWS), :] = jnp.zeros((tm, LANES), F32)

    w_hi, w_lo = _split_bf16(wr_ref[...])
    h_hi, h_lo = _split_bf16(h2)
    hi_both = jnp.dot(h_hi, jnp.concatenate([w_hi, w_lo], axis=1), preferred_element_type=F32)
    logits = (hi_both[:, :LANES] + jnp.dot(h_lo, w_hi, preferred_element_type=F32)
              + hi_both[:, LANES:])

    lane = lax.broadcasted_iota(jnp.int32, logits.shape, 1)
    neg_inf = jnp.float32(-jnp.inf)
    in_g = lane < N_GROUPS
    lg = jnp.where(in_g, logits, neg_inf)
    mg = jnp.max(lg, axis=1, keepdims=True)
    eg = jnp.exp(lg - mg)
    g_prob = eg / jnp.sum(eg, axis=1, keepdims=True)
    g_p = jnp.max(g_prob, axis=1, keepdims=True)
    g_idx = jnp.min(jnp.where(g_prob == g_p, lane, LANES), axis=1, keepdims=True)
    lo_col = ROUTER_COL0 + EXPERTS_PER_GROUP * g_idx
    in_e = (lane >= lo_col) & (lane < lo_col + EXPERTS_PER_GROUP)
    le = jnp.where(in_e, logits, neg_inf)
    me = jnp.max(le, axis=1, keepdims=True)
    ee = jnp.exp(le - me)
    e_prob = jnp.where(in_e, ee / jnp.sum(ee, axis=1, keepdims=True), -1.0)
    p1 = jnp.max(e_prob, axis=1, keepdims=True)
    i1 = jnp.min(jnp.where(e_prob == p1, lane, LANES), axis=1, keepdims=True)
    rest = jnp.where(lane == i1, -1.0, e_prob)
    p2 = jnp.max(rest, axis=1, keepdims=True)
    i2 = jnp.min(jnp.where(rest == p2, lane, LANES), axis=1, keepdims=True)
    denom = p1 + p2
    ea = jnp.minimum(i1, i2) - lo_col
    eb = jnp.maximum(i1, i2) - lo_col
    pair = lax.shift_right_logical(ea * (2 * EXPERTS_PER_GROUP - 1 - ea), 1) + eb - ea - 1
    cls = g_idx * PAIRS_PER_GROUP + pair
    routing = (jnp.where(lane == i1, g_p * (p1 / denom), 0.0)
               + jnp.where(lane == i2, g_p * (p2 / denom), 0.0)
               + jnp.where(lane == CLASS_LANE, cls.astype(F32), 0.0))
    tok_ref[pl.ds(D_ROWS, tm, stride=TOK_ROWS), :] = routing
    cls_ref[...] = routing.T[CLASS_LANE:CLASS_LANE + 1, :].astype(jnp.int32)


def _merge(x2, y_sg, y_att, proj, mod, norm2_w, w_sg, w_att, w_o, w_router, seq):
    n, d = x2.shape
    tm = MERGE_TM
    tpb = seq // tm
    resident = lambda shape: pl.BlockSpec(shape, lambda i: (0, 0), pipeline_mode=pl.Buffered(1))
    return pl.pallas_call(
        _merge_kernel,
        grid=(n // tm,),
        in_specs=[pl.BlockSpec((tm, d), lambda i: (i, 0)),
                  pl.BlockSpec((tm, D_SG), lambda i: (i, 0)),
                  pl.BlockSpec((tm, D_ATT), lambda i: (i, 0)),
                  pl.BlockSpec((tm, d), lambda i: (i, BLK_GATE_SG)),
                  pl.BlockSpec((tm, d), lambda i: (i, BLK_GATE_ATT)),
                  pl.BlockSpec((1, 8, d), lambda i: (i // tpb, 0, 0)),
                  pl.BlockSpec((1, d), lambda i: (0, 0)),
                  resident((D_SG, d)), resident((D_ATT, d)), resident((d, d)),
                  resident((d, LANES))],
        out_specs=[pl.BlockSpec((tm, d), lambda i: (i, 0)),
                   pl.BlockSpec((tm * TOK_ROWS, LANES), lambda i: (i, 0)),
                   pl.BlockSpec((None, 1, tm), lambda i: (i, 0, 0))],
        out_shape=[jax.ShapeDtypeStruct((n, d), F32),
                   jax.ShapeDtypeStruct((n * TOK_ROWS, LANES), F32),
                   jax.ShapeDtypeStruct((n // tm, 1, tm), jnp.int32)],
        compiler_params=_cparams(("parallel",)),
        name="merge",
    )(x2, y_sg, y_att, proj, proj, mod, norm2_w, w_sg, w_att, w_o, w_router)


def _invert_kernel(pos_ref, free_ref, inv_ref, *, n_tok, n_free):
    def place(t, carry):
        inv_ref[pos_ref[t]] = t
        return carry

    def place_free(k, carry):
        inv_ref[free_ref[k]] = n_tok + k
        return carry

    lax.fori_loop(0, n_tok, place, 0, unroll=8)
    lax.fori_loop(0, n_free, place_free, 0, unroll=8)


def _invert(pos, free):
    n_tok, n_free = pos.shape[0], free.shape[0]
    return pl.pallas_call(
        functools.partial(_invert_kernel, n_tok=n_tok, n_free=n_free),
        grid_spec=pltpu.PrefetchScalarGridSpec(
            num_scalar_prefetch=2, grid=(1,), in_specs=[],
            out_specs=pl.BlockSpec(memory_space=pltpu.SMEM)),
        out_shape=jax.ShapeDtypeStruct((n_tok + n_free,), jnp.int32),
        compiler_params=_cparams(("arbitrary",)),
        name="invert",
    )(pos, free)


MOE_TILE = 256
ROW_UNROLL_LOG2 = 3
ROW_UNROLL = 1 << ROW_UNROLL_LOG2


def _for_rows(rows, fn):
    groups = lax.shift_right_logical(rows, ROW_UNROLL_LOG2)

    def group(g, carry):
        for u in range(ROW_UNROLL):
            fn(g * ROW_UNROLL + u)
        return carry

    def single(r, carry):
        fn(r)
        return carry

    lax.fori_loop(0, groups, group, 0)
    lax.fori_loop(groups * ROW_UNROLL, rows, single, 0)


def _moe_kernel(src_ref, dst_ref, ea_ref, eb_ref, rows_ref, na_ref, tok_hbm,
                w1a_ref, w3a_ref, w2a_ref, w1b_ref, w3b_ref, w2b_ref, y_hbm,
                gbuf, stage, acc_ref, gsem, ssem):
    i = pl.program_id(0)
    n_act = na_ref[0]
    slot = lax.rem(i, 2)

    def gather_start(tile, sl):
        def one(r, carry):
            row0 = pl.multiple_of(r * TOK_ROWS, 8)
            pltpu.make_async_copy(tok_hbm.at[src_ref[tile * MOE_TILE + r]],
                                  gbuf.at[sl, pl.ds(row0, TOK_ROWS), :], gsem.at[sl]).start()
            return carry
        lax.fori_loop(0, MOE_TILE, one, 0, unroll=8)

    def gather_wait(sl):
        for _ in range(MOE_TILE):
            pltpu.make_async_copy(tok_hbm.at[0], gbuf.at[sl, pl.ds(0, TOK_ROWS), :], gsem.at[sl]).wait()

    def scatter_start(tile, sl):
        def one(r):
            row0 = pl.multiple_of(r * D_ROWS, 8)
            pltpu.make_async_copy(stage.at[sl, pl.ds(row0, D_ROWS), :],
                                  y_hbm.at[dst_ref[tile * MOE_TILE + r]], ssem.at[sl]).start()
        _for_rows(rows_ref[tile], one)

    def scatter_wait(tile, sl):
        def one(r):
            pltpu.make_async_copy(stage.at[sl, pl.ds(0, D_ROWS), :], y_hbm.at[0], ssem.at[sl]).wait()
        _for_rows(rows_ref[tile], one)

    @pl.when(i < n_act)
    def _():
        @pl.when(i == 0)
        def _():
            gather_start(0, 0)

        @pl.when(i + 1 < n_act)
        def _():
            gather_start(i + 1, 1 - slot)

        gather_wait(slot)
        t = jnp.concatenate([gbuf[slot, pl.ds(a, MOE_TILE, stride=TOK_ROWS), :] for a in range(D_ROWS)],
                            axis=1).astype(BF16)
        routing = gbuf[slot, pl.ds(D_ROWS, MOE_TILE, stride=TOK_ROWS), :]
        lane = lax.broadcasted_iota(jnp.int32, routing.shape, 1)
        experts = ((ea_ref[i], w1a_ref, w3a_ref, w2a_ref), (eb_ref[i], w1b_ref, w3b_ref, w2b_ref))
        ups = [(jnp.dot(t, w1_ref[...], preferred_element_type=F32),
                jnp.dot(t, w3_ref[...], preferred_element_type=F32)) for _, w1_ref, w3_ref, _ in experts]
        hmids = []
        for (e, _, _, _), (a, b) in zip(experts, ups):
            ge = jnp.sum(jnp.where(lane == ROUTER_COL0 + e, routing, 0.0), axis=1, keepdims=True)
            hmids.append(((a * jax.nn.sigmoid(a)) * b * ge).astype(BF16))
        acc_ref[...] = (jnp.dot(hmids[0], w2a_ref[...].astype(BF16), preferred_element_type=F32)
                        + jnp.dot(hmids[1], w2b_ref[...].astype(BF16), preferred_element_type=F32))

        @pl.when(i >= 2)
        def _():
            scatter_wait(i - 2, slot)

        for a in range(D_ROWS):
            stage[slot, pl.ds(a, MOE_TILE, stride=D_ROWS), :] = acc_ref[:, a * LANES:(a + 1) * LANES]
        scatter_start(i, slot)

        @pl.when(i == n_act - 1)
        def _():
            scatter_wait(i, slot)

            @pl.when(i >= 1)
            def _():
                scatter_wait(i - 1, 1 - slot)


def _moe(src_tok, dst_row, tile_ea, tile_eb, tile_rows, n_active, tok_blocks, w1, w3, w2, n_tok):
    n_slots = src_tok.shape[0]
    d = D_MODEL
    blk_a = lambda i, src, dst, ea, eb, rows, na: (ea[i], 0, 0)
    blk_b = lambda i, src, dst, ea, eb, rows, na: (eb[i], 0, 0)
    up = lambda blk: pl.BlockSpec((None, d, D_EXPERT), blk)
    down = lambda blk: pl.BlockSpec((None, D_EXPERT, d), blk)
    return pl.pallas_call(
        _moe_kernel,
        grid_spec=pltpu.PrefetchScalarGridSpec(
            num_scalar_prefetch=6, grid=(n_slots // MOE_TILE,),
            in_specs=[pl.BlockSpec(memory_space=pl.ANY),
                      up(blk_a), up(blk_a), down(blk_a), up(blk_b), up(blk_b), down(blk_b)],
            out_specs=pl.BlockSpec(memory_space=pl.ANY),
            scratch_shapes=[pltpu.VMEM((2, MOE_TILE * TOK_ROWS, LANES), F32),
                            pltpu.VMEM((2, MOE_TILE * D_ROWS, LANES), F32),
                            pltpu.VMEM((MOE_TILE, d), F32),
                            pltpu.SemaphoreType.DMA((2,)),
                            pltpu.SemaphoreType.DMA((2,))]),
        out_shape=jax.ShapeDtypeStruct((n_tok, D_ROWS, LANES), F32),
        compiler_params=_cparams(("arbitrary",)),
        name="moe",
    )(src_tok, dst_row, tile_ea, tile_eb, tile_rows, n_active, tok_blocks, w1, w3, w2, w1, w3, w2)


def _final_kernel(x1_ref, y_ref, mod_ref, fw_ref, o_ref):
    tm = x1_ref.shape[0]
    parts = []
    ss = jnp.zeros((tm, 1), F32)
    for a in range(D_ROWS):
        cols = slice(a * LANES, (a + 1) * LANES)
        part = x1_ref[:, cols] + mod_ref[0, 5:6, cols] * y_ref[pl.ds(a, tm, stride=D_ROWS), :]
        ss = ss + jnp.sum(part * part, axis=1, keepdims=True)
        parts.append(part)
    inv = lax.rsqrt(ss * (1.0 / D_MODEL) + EPS)
    for a in range(D_ROWS):
        cols = slice(a * LANES, (a + 1) * LANES)
        o_ref[:, cols] = parts[a] * inv * fw_ref[:, cols]


def _final(x1, y_blocks, mod, final_w, seq):
    n, d = x1.shape
    tm = FINAL_TM
    tpb = seq // tm
    return pl.pallas_call(
        _final_kernel,
        grid=(n // tm,),
        in_specs=[pl.BlockSpec((tm, d), lambda i: (i, 0)),
                  pl.BlockSpec((tm * D_ROWS, LANES), lambda i: (i, 0)),
                  pl.BlockSpec((1, 8, d), lambda i: (i // tpb, 0, 0)),
                  pl.BlockSpec((1, d), lambda i: (0, 0))],
        out_specs=pl.BlockSpec((tm, d), lambda i: (i, 0)),
        out_shape=jax.ShapeDtypeStruct((n, d), F32),
        compiler_params=_cparams(("parallel",)),
        name="final",
    )(x1, y_blocks, mod, final_w)


def _class_sort_plan(cls, n):
    class_ids = jnp.arange(N_CLASSES, dtype=jnp.int32)
    onehot = (cls[:, None] == class_ids[None, :]).astype(jnp.int32)
    blk = LANES
    within = jnp.einsum("ts,bsg->btg", jnp.tril(jnp.ones((blk, blk), F32)),
                        onehot.astype(F32).reshape(n // blk, blk, N_CLASSES))
    before = jnp.tril(jnp.ones((n // blk, n // blk), F32), -1) @ within[:, -1, :]
    incl = (within + before[:, None, :]).reshape(n, N_CLASSES).astype(jnp.int32)
    count = incl[-1]
    tiles = (count + MOE_TILE - 1) // MOE_TILE
    tile_end = jnp.cumsum(tiles)
    start = (tile_end - tiles) * MOE_TILE
    rank = jnp.sum(incl * onehot, axis=1) - 1
    pos = jnp.sum(onehot * start[None, :], axis=1) + rank
    n_tiles = n // MOE_TILE + N_CLASSES
    tile_ids = jnp.arange(n_tiles, dtype=jnp.int32)
    tile_class = jnp.sum((tile_ids[:, None] >= tile_end[None, :]).astype(jnp.int32), axis=1)
    tile_class = jnp.minimum(tile_class, N_CLASSES - 1)
    tile_hot = (tile_class[:, None] == class_ids[None, :]).astype(jnp.int32)
    done = (tile_ids - jnp.sum(tile_hot * (tile_end - tiles)[None, :], axis=1)) * MOE_TILE
    tile_rows = jnp.clip(jnp.sum(tile_hot * count[None, :], axis=1) - done, 0, MOE_TILE)
    first = jnp.asarray([a for a, _ in PAIRS], jnp.int32)
    second = jnp.asarray([b for _, b in PAIRS], jnp.int32)
    group0 = (class_ids // PAIRS_PER_GROUP) * EXPERTS_PER_GROUP
    tile_ea = jnp.sum(tile_hot * (group0 + first[class_ids % PAIRS_PER_GROUP])[None, :], axis=1)
    tile_eb = jnp.sum(tile_hot * (group0 + second[class_ids % PAIRS_PER_GROUP])[None, :], axis=1)
    pad = tiles * MOE_TILE - count
    pad_end = jnp.cumsum(pad)
    k = jnp.arange(N_CLASSES * MOE_TILE, dtype=jnp.int32)
    seg = jnp.sum((k[:, None] >= pad_end[None, :]).astype(jnp.int32), axis=1)
    seg_hot = (seg[:, None] == class_ids[None, :]).astype(jnp.int32)
    in_class = jnp.sum(seg_hot * (start + count - (pad_end - pad))[None, :], axis=1) + k
    tail = tile_end[-1] * MOE_TILE + k - pad_end[-1]
    free = jnp.where(seg < N_CLASSES, in_class, tail)
    i32 = lambda a: a.astype(jnp.int32)
    return i32(pos), i32(free), i32(tile_ea), i32(tile_eb), i32(tile_rows), i32(tile_end[-1:])


def kernel(x, c, w_ada, b_ada, norm1_w, norm2_w, final_norm_w, w_in, sg_ln_w, sg_ln_b, w_spatial,
           b_spatial, rel_bias, w_out_sg, w_out_att, w_o, w_router_group, w_router_expert,
           w_exp_gate, w_exp_up, w_exp_down):
    batch, seq, d = x.shape
    assert d == D_MODEL and w_ada.shape[0] == 1
    assert seq % PROJ_TM == 0 and seq // MOBA_BLOCK <= EXT_LO
    n = batch * seq
    x2 = x.reshape(n, d)

    assert batch <= BF16_SUBLANES
    c_pad = jnp.zeros((BF16_SUBLANES, d), F32).at[:batch].set(c)
    mod = _adaln(c_pad, w_ada[0], b_ada[0].reshape(1, 6 * d))
    mod = mod[:batch].reshape(batch, 6, d)
    mod = jnp.concatenate([mod, jnp.zeros((batch, 2, d), F32)], axis=1)

    proj = _proj(x2, mod, norm1_w[0].reshape(1, d), w_in[0], seq)

    y_sg = _sgu(proj, sg_ln_w[0].reshape(1, D_SG), sg_ln_b[0].reshape(1, D_SG),
                w_spatial[0], b_spatial[0].T)

    qt, kx, vt, kmean = _moba_prep(proj, batch, seq)
    bias_tiles = _bias_tiles(rel_bias)
    y_att = _moba(rel_bias, qt, kx, vt, kmean, bias_tiles, batch, seq)

    w_router = jnp.zeros((d, LANES), F32)
    w_router = w_router.at[:, :N_GROUPS].set(w_router_group[0])
    w_router = w_router.at[:, ROUTER_COL0:ROUTER_COL0 + N_EXPERTS].set(w_router_expert[0])
    x1, tok, cls = _merge(x2, y_sg, y_att, proj, mod, norm2_w[0].reshape(1, d),
                          w_out_sg[0].astype(BF16), w_out_att[0].astype(BF16),
                          w_o[0].astype(BF16), w_router, seq)

    tok = tok.reshape(n, TOK_ROWS, LANES)
    pos, free, tile_ea, tile_eb, tile_rows, n_active = _class_sort_plan(cls.reshape(n), n)
    inv = _invert(pos, free)
    src_tok = jnp.where(inv < n, inv, 0)
    y = _moe(src_tok, inv, tile_ea, tile_eb, tile_rows, n_active, tok, w_exp_gate[0].astype(BF16),
             w_exp_up[0].astype(BF16), w_exp_down[0], n)
    out = _final(x1, y.reshape(-1, LANES), mod, final_norm_w.reshape(1, d), seq)
    return out.reshape(batch, seq, d)
```

```python
import functools
import math

import jax
import jax.numpy as jnp
from jax import lax
from jax.experimental import pallas as pl
from jax.experimental.pallas import tpu as pltpu

F32 = jnp.float32
BF16 = jnp.bfloat16

LANES = 128
BF16_SUBLANES = 16
D_MODEL = 2048
D_SG = D_MODEL // 2
SG_GROUPS = 8
SG_CHUNK = 128
ATT_HEADS = 8
HEAD_DIM = 128
D_ATT = ATT_HEADS * HEAD_DIM
MOBA_BLOCK = 256
MOBA_TOPK = 3
REL_BUCKETS = 32
REL_MAX_DIST = 128
N_GROUPS = 4
EXPERTS_PER_GROUP = 4
N_EXPERTS = N_GROUPS * EXPERTS_PER_GROUP
D_EXPERT = 512
EPS = 1e-6
IN_COLS = 2 * D_SG + 3 * D_ATT + 2 * D_MODEL

N_GATE_COLS = 2 * D_MODEL
BLK_GATE_SG = 0
BLK_GATE_ATT = 1
BLK_U = N_GATE_COLS // D_SG
BLK_V = BLK_U + 1
CB_Q = (N_GATE_COLS + 2 * D_SG) // LANES
CB_K = CB_Q + ATT_HEADS
CB_VAL = CB_K + ATT_HEADS

MASK_NEG = -1e9
LOG2E = math.log2(math.e)
ROUTER_COL0 = N_GROUPS
CLASS_LANE = 0
PAIRS_PER_GROUP = EXPERTS_PER_GROUP * (EXPERTS_PER_GROUP - 1) // 2
N_CLASSES = N_GROUPS * PAIRS_PER_GROUP
PAIRS = [(a, b) for a in range(EXPERTS_PER_GROUP) for b in range(a + 1, EXPERTS_PER_GROUP)]
D_ROWS = D_MODEL // LANES
TOK_ROWS = D_ROWS + 8
VMEM_LIMIT = 56 * 1024 * 1024

ADALN_TN = 1024
PROJ_TM = 1024
PROJ_TN = 1024
PROJ_CHUNK = 256
SGU_CHUNKS = 8
MERGE_TM = 256
FINAL_TM = 512


def _cparams(sem):
    return pltpu.CompilerParams(dimension_semantics=sem, vmem_limit_bytes=VMEM_LIMIT)


def _split_bf16(a):
    hi = a.astype(BF16)
    lo = (a - hi.astype(F32)).astype(BF16)
    return hi, lo


def _adaln_kernel(c_ref, w_ref, b_ref, o_ref):
    c = c_ref[...]
    rows = c.shape[0]
    c_hi, c_lo = _split_bf16(c * jax.nn.sigmoid(c))
    w_hi, w_lo = _split_bf16(w_ref[...])
    both = jnp.dot(jnp.concatenate([c_hi, c_lo], axis=0), w_hi, preferred_element_type=F32)
    o_ref[...] = (both[:rows] + both[rows:] + jnp.dot(c_hi, w_lo, preferred_element_type=F32)
                  + b_ref[...])


def _adaln(c_pad, w, b):
    rows, d = c_pad.shape
    cols = w.shape[1]
    tn = ADALN_TN
    return pl.pallas_call(
        _adaln_kernel,
        grid=(cols // tn,),
        in_specs=[pl.BlockSpec((rows, d), lambda j: (0, 0)),
                  pl.BlockSpec((d, tn), lambda j: (0, j)),
                  pl.BlockSpec((1, tn), lambda j: (0, j))],
        out_specs=pl.BlockSpec((rows, tn), lambda j: (0, j)),
        out_shape=jax.ShapeDtypeStruct((rows, cols), F32),
        compiler_params=_cparams(("arbitrary",)),
        name="adaln",
    )(c_pad, w, b)


def _gelu(a):
    return 0.5 * a * (1.0 + lax.erf(a * (1.0 / math.sqrt(2.0))))


_PROJ_FINISH = {
    "gate": lambda acc: acc,
    "sgu": _gelu,
    "q": lambda acc: acc * (HEAD_DIM ** -0.5 * LOG2E),
    "kv": lambda acc: acc,
}
_PROJ_SEGMENTS = (("gate", N_GATE_COLS // PROJ_TN), ("sgu", 2 * D_SG // PROJ_TN),
                  ("q", D_ATT // PROJ_TN), ("kv", 2 * D_ATT // PROJ_TN))


def _proj_kernel(x_ref, mod_ref, nw_ref, w_ref, o_ref, h_scr):
    j = pl.program_id(1)

    @pl.when(j == 0)
    def _():
        finish = _PROJ_FINISH[_PROJ_SEGMENTS[0][0]]
        w = w_ref[...].astype(BF16)
        scale = nw_ref[...] * (1.0 + mod_ref[0, 1:2, :])
        for k in range(x_ref.shape[0] // PROJ_CHUNK):
            rows = slice(k * PROJ_CHUNK, (k + 1) * PROJ_CHUNK)
            x = x_ref[rows, :]
            ms = jnp.mean(x * x, axis=-1, keepdims=True)
            h = (x * lax.rsqrt(ms + EPS) * scale + mod_ref[0, 0:1, :]).astype(BF16)
            h_scr[rows, :] = h
            o_ref[rows, :] = finish(jnp.dot(h, w, preferred_element_type=F32)).astype(BF16)

    start = 0
    for kind, blocks in _PROJ_SEGMENTS:
        @pl.when((j >= max(start, 1)) & (j < start + blocks))
        def _(kind=kind):
            acc = jnp.dot(h_scr[...], w_ref[...].astype(BF16), preferred_element_type=F32)
            o_ref[...] = _PROJ_FINISH[kind](acc).astype(BF16)
        start += blocks


def _proj(x2, mod, norm_w, w_in, seq):
    n, d = x2.shape
    cols = w_in.shape[1]
    tm = min(PROJ_TM, seq)
    tn = PROJ_TN
    tpb = seq // tm
    n_cb = cols // tn
    first = (cols - N_GATE_COLS) // tn
    assert (cols - N_GATE_COLS) % tn == 0
    assert sum(blocks * tn for _, blocks in _PROJ_SEGMENTS) == cols
    return pl.pallas_call(
        _proj_kernel,
        grid=(n // tm, n_cb),
        in_specs=[pl.BlockSpec((tm, d), lambda i, j: (i, 0)),
                  pl.BlockSpec((1, 8, d), lambda i, j: (i // tpb, 0, 0)),
                  pl.BlockSpec((1, d), lambda i, j: (0, 0)),
                  pl.BlockSpec((d, tn), lambda i, j: (0, (j + first) % n_cb))],
        out_specs=pl.BlockSpec((tm, tn), lambda i, j: (i, j)),
        out_shape=jax.ShapeDtypeStruct((n, cols), BF16),
        scratch_shapes=[pltpu.VMEM((tm, d), BF16)],
        compiler_params=_cparams(("parallel", "arbitrary")),
        name="proj",
    )(x2, mod, norm_w, w_in)


def _sgu_kernel(u_ref, v_ref, lnw_ref, lnb_ref, ws_ref, bst_ref, o_ref, *, chunks):
    u = u_ref[...].astype(F32)
    v = v_ref[...].astype(F32)
    mu = jnp.mean(v, axis=-1, keepdims=True)
    vc = v - mu
    var = jnp.mean(vc * vc, axis=-1, keepdims=True)
    vn = (vc * lax.rsqrt(var + EPS) * lnw_ref[...] + lnb_ref[...]).astype(BF16)
    row = lax.broadcasted_iota(jnp.int32, (SG_CHUNK, SG_CHUNK), 0)
    col = lax.broadcasted_iota(jnp.int32, (SG_CHUNK, SG_CHUNK), 1)
    causal = col <= row
    for g in range(SG_GROUPS):
        wm = jnp.where(causal, ws_ref[g], 0.0).astype(BF16)
        bcol = bst_ref[:, g:g + 1]
        gs = slice(g * LANES, (g + 1) * LANES)
        for c in range(chunks):
            rs = slice(c * SG_CHUNK, (c + 1) * SG_CHUNK)
            z = jnp.dot(wm, vn[rs, gs], preferred_element_type=F32) + bcol
            o_ref[rs, gs] = (u[rs, gs] * z).astype(BF16)


def _sgu(proj, ln_w, ln_b, w_s, b_s_t):
    n = proj.shape[0]
    chunks = SGU_CHUNKS
    tm = chunks * SG_CHUNK
    return pl.pallas_call(
        functools.partial(_sgu_kernel, chunks=chunks),
        grid=(n // tm,),
        in_specs=[pl.BlockSpec((tm, D_SG), lambda i: (i, BLK_U)),
                  pl.BlockSpec((tm, D_SG), lambda i: (i, BLK_V)),
                  pl.BlockSpec((1, D_SG), lambda i: (0, 0)),
                  pl.BlockSpec((1, D_SG), lambda i: (0, 0)),
                  pl.BlockSpec((SG_GROUPS, SG_CHUNK, SG_CHUNK), lambda i: (0, 0, 0)),
                  pl.BlockSpec((SG_CHUNK, SG_GROUPS), lambda i: (0, 0))],
        out_specs=pl.BlockSpec((tm, D_SG), lambda i: (i, 0)),
        out_shape=jax.ShapeDtypeStruct((n, D_SG), BF16),
        compiler_params=_cparams(("parallel",)),
        name="sgu",
    )(proj, proj, ln_w, ln_b, w_s, b_s_t)


VT_ROWS = HEAD_DIM + 16
EXT_LO = REL_BUCKETS


def _moba_prep_kernel(q_ref, k_ref, v_ref, qt_ref, kx_ref, vt_ref, km_ref, *, nb):
    lane = lax.broadcasted_iota(jnp.int32, (MOBA_BLOCK, LANES), 1)
    pad = jnp.concatenate([jnp.ones((1, MOBA_BLOCK), F32),
                           jnp.zeros((VT_ROWS - HEAD_DIM - 1, MOBA_BLOCK), F32)], axis=0)
    for j in range(nb):
        rows = slice(j * MOBA_BLOCK, (j + 1) * MOBA_BLOCK)
        kj = k_ref[rows, :]
        onehot = jnp.where((lane == j) | (lane == j + EXT_LO), 1.0, 0.0).astype(BF16)
        kx_ref[rows, :] = jnp.concatenate([kj, onehot], axis=1)
        km_ref[j:j + 1, :] = jnp.mean(kj.astype(F32), axis=0, keepdims=True)
        vt = v_ref[rows, :].astype(F32).T
        vt_ref[j] = jnp.concatenate([vt, pad], axis=0).astype(BF16)
        qt_ref[j] = q_ref[rows, :].astype(F32).T.astype(BF16)


def _moba_prep(proj, batch, seq):
    nb = seq // MOBA_BLOCK
    bh = lambda shape: pl.BlockSpec((None, None) + shape, lambda b, h: (b, h) + (0,) * len(shape))
    return pl.pallas_call(
        functools.partial(_moba_prep_kernel, nb=nb),
        grid=(batch, ATT_HEADS),
        in_specs=[pl.BlockSpec((seq, HEAD_DIM), lambda b, h: (b, CB_Q + h)),
                  pl.BlockSpec((seq, HEAD_DIM), lambda b, h: (b, CB_K + h)),
                  pl.BlockSpec((seq, HEAD_DIM), lambda b, h: (b, CB_VAL + h))],
        out_specs=[bh((nb, HEAD_DIM, MOBA_BLOCK)), bh((seq, 2 * HEAD_DIM)),
                   bh((nb, VT_ROWS, MOBA_BLOCK)), bh((nb, HEAD_DIM))],
        out_shape=[jax.ShapeDtypeStruct((batch, ATT_HEADS, nb, HEAD_DIM, MOBA_BLOCK), BF16),
                   jax.ShapeDtypeStruct((batch, ATT_HEADS, seq, 2 * HEAD_DIM), BF16),
                   jax.ShapeDtypeStruct((batch, ATT_HEADS, nb, VT_ROWS, MOBA_BLOCK), BF16),
                   jax.ShapeDtypeStruct((batch, ATT_HEADS, nb, HEAD_DIM), F32)],
        compiler_params=_cparams(("parallel", "parallel")),
        name="moba_prep",
    )(proj, proj, proj)


def _bias_tiles_kernel(rb_ref, o_ref):
    h = pl.program_id(0)
    kj = lax.broadcasted_iota(jnp.int32, (MOBA_BLOCK, MOBA_BLOCK), 0)
    qi = lax.broadcasted_iota(jnp.int32, (MOBA_BLOCK, MOBA_BLOCK), 1)
    max_exact = REL_BUCKETS // 2
    for t in range(2):
        rel = qi - kj + MOBA_BLOCK * t
        n = jnp.maximum(rel, 0)
        nf = jnp.maximum(n, max_exact).astype(F32)
        large = max_exact + (jnp.log(nf / max_exact) / math.log(REL_MAX_DIST / max_exact)
                             * (REL_BUCKETS - max_exact)).astype(jnp.int32)
        large = jnp.minimum(large, REL_BUCKETS - 1)
        bucket = jnp.where(n < max_exact, n, large)
        bias = jnp.zeros((MOBA_BLOCK, MOBA_BLOCK), F32)
        for r in range(REL_BUCKETS):
            bias = jnp.where(bucket == r, rb_ref[r, h], bias)
        o_ref[t] = jnp.where(rel >= 0, bias * LOG2E, MASK_NEG)


def _bias_tiles(rel_bias):
    return pl.pallas_call(
        _bias_tiles_kernel,
        grid=(ATT_HEADS,),
        in_specs=[pl.BlockSpec(memory_space=pltpu.SMEM)],
        out_specs=pl.BlockSpec((None, 2, MOBA_BLOCK, MOBA_BLOCK), lambda h: (h, 0, 0, 0)),
        out_shape=jax.ShapeDtypeStruct((ATT_HEADS, 2, MOBA_BLOCK, MOBA_BLOCK), F32),
        compiler_params=_cparams(("arbitrary",)),
        name="bias_tiles",
    )(rel_bias)


FAR_BLOCKS = 4
MAX_LOG2_RISE = 64.0
HEADS_PER_STEP = 4


def _moba_kernel(rb_ref, qt_ref, kx_ref, vt_ref, km_ref, bias_ref, o_ref, *, nb):
    i = pl.program_id(2)
    jp = jnp.maximum(i - 1, 0)
    prev0 = pl.multiple_of(jp * MOBA_BLOCK, MOBA_BLOCK)
    own0 = pl.multiple_of(i * MOBA_BLOCK, MOBA_BLOCK)
    no_prev = jnp.where(i > 0, 0.0, MASK_NEG)
    neg_inf = jnp.float32(-jnp.inf)

    heads = range(HEADS_PER_STEP)

    def choose_blocks(hh):
        km_hi, km_lo = _split_bf16(km_ref[hh])
        score = (jnp.dot(km_hi, qt_ref[hh], preferred_element_type=F32)
                 + jnp.dot(km_lo, qt_ref[hh], preferred_element_type=F32))
        bid = lax.broadcasted_iota(jnp.int32, score.shape, 0)
        score = jnp.where(bid < i, score, neg_inf)
        chosen = bid < 0
        for _ in range(MOBA_TOPK):
            mx = jnp.max(score, axis=0, keepdims=True)
            hit = (score == mx) & (mx > neg_inf)
            idx = jnp.min(jnp.where(hit, bid, nb), axis=0, keepdims=True)
            pick = bid == idx
            chosen = chosen | pick
            score = jnp.where(pick, neg_inf, score)
        return chosen

    def with_mask(hh, val):
        hi, lo = _split_bf16(val)
        fill = jnp.zeros((HEAD_DIM - 2 * EXT_LO, MOBA_BLOCK), BF16)
        if nb < EXT_LO:
            gap = jnp.zeros((EXT_LO - nb, MOBA_BLOCK), BF16)
            return jnp.concatenate([qt_ref[hh], hi, gap, lo, gap, fill], axis=0)
        return jnp.concatenate([qt_ref[hh], hi, lo, fill], axis=0)

    chosen = [choose_blocks(hh) for hh in heads]
    bid = lax.broadcasted_iota(jnp.int32, chosen[0].shape, 0)
    far_bias = [rb_ref[REL_BUCKETS - 1, pl.program_id(1) * HEADS_PER_STEP + hh] * LOG2E for hh in heads]
    qx_fars = [with_mask(hh, jnp.where(chosen[hh] & (bid <= i - 2), far_bias[hh], MASK_NEG)) for hh in heads]
    qx_nears = [with_mask(hh, jnp.where((bid == i) | (chosen[hh] & (bid == i - 1)), 0.0, MASK_NEG))
                for hh in heads]

    ss = [jnp.dot(jnp.concatenate([kx_ref[hh, pl.ds(prev0, MOBA_BLOCK), :],
                                   kx_ref[hh, pl.ds(own0, MOBA_BLOCK), :]], axis=0),
                  qx_nears[hh], preferred_element_type=F32)
          + jnp.concatenate([bias_ref[hh, 1] + no_prev, bias_ref[hh, 0]], axis=0) for hh in heads]
    ms = [jnp.max(ss[hh], axis=0, keepdims=True) for hh in heads]
    ps = [jnp.exp2(ss[hh] - ms[hh]).astype(BF16) for hh in heads]
    accs = [jnp.dot(jnp.concatenate([vt_ref[hh, jp], vt_ref[hh, i]], axis=1), ps[hh],
                    preferred_element_type=F32) for hh in heads]

    def far_step(c, carry):
        row0 = pl.multiple_of(c * (FAR_BLOCKS * MOBA_BLOCK), FAR_BLOCKS * MOBA_BLOCK)
        m_olds = [carry[hh][0] for hh in heads]
        accs = [carry[hh][1] for hh in heads]

        def logits(hh):
            return jnp.dot(kx_ref[hh, pl.ds(row0, FAR_BLOCKS * MOBA_BLOCK), :], qx_fars[hh],
                           preferred_element_type=F32)

        def values(hh):
            return jnp.concatenate([vt_ref[hh, c * FAR_BLOCKS + k] for k in range(FAR_BLOCKS)], axis=1)

        ss = [logits(hh) for hh in heads]
        ps = [jnp.exp2(ss[hh] - m_olds[hh]).astype(BF16) for hh in heads]
        m_news = [jnp.maximum(m_olds[hh], jnp.max(ss[hh], axis=0, keepdims=True)) for hh in heads]
        alphas = [jnp.exp2(m_olds[hh] - m_news[hh]) for hh in heads]
        usual = tuple((m_news[hh], alphas[hh] * (accs[hh] + jnp.dot(values(hh), ps[hh],
                                                                    preferred_element_type=F32)))
                      for hh in heads)
        rise = functools.reduce(jnp.maximum, [jnp.max(m_news[hh] - m_olds[hh]) for hh in heads])

        def redo():
            out = []
            for hh in heads:
                p = jnp.exp2(logits(hh) - m_news[hh]).astype(BF16)
                out.append((m_news[hh], alphas[hh] * accs[hh]
                            + jnp.dot(values(hh), p, preferred_element_type=F32)))
            return tuple(out)

        return lax.cond(rise > MAX_LOG2_RISE, redo, lambda: usual)

    n_far = lax.shift_right_logical(jnp.maximum(i - 1, 0) + FAR_BLOCKS - 1, 2)
    final = lax.fori_loop(0, n_far, far_step, tuple((ms[hh], accs[hh]) for hh in heads))
    for hh in heads:
        acc = final[hh][1]
        out = acc[0:HEAD_DIM] / acc[HEAD_DIM:HEAD_DIM + 1]
        o_ref[:, hh * HEAD_DIM:(hh + 1) * HEAD_DIM] = out.T.astype(BF16)


def _moba(rel_bias, qt, kx, vt, kmean, bias_tiles, batch, seq):
    nb = seq // MOBA_BLOCK
    hps = HEADS_PER_STEP
    assert nb % FAR_BLOCKS == 0 and FAR_BLOCKS == 4 and ATT_HEADS % hps == 0
    bh = lambda shape: pl.BlockSpec((None, hps) + shape, lambda b, h, i: (b, h) + (0,) * len(shape),
                                    pipeline_mode=pl.Buffered(1))
    return pl.pallas_call(
        functools.partial(_moba_kernel, nb=nb),
        grid=(batch, ATT_HEADS // hps, nb),
        in_specs=[pl.BlockSpec(memory_space=pltpu.SMEM),
                  pl.BlockSpec((None, hps, None, HEAD_DIM, MOBA_BLOCK), lambda b, h, i: (b, h, i, 0, 0)),
                  bh((seq, 2 * HEAD_DIM)), bh((nb, VT_ROWS, MOBA_BLOCK)), bh((nb, HEAD_DIM)),
                  pl.BlockSpec((hps, 2, MOBA_BLOCK, MOBA_BLOCK), lambda b, h, i: (h, 0, 0, 0))],
        out_specs=pl.BlockSpec((MOBA_BLOCK, hps * HEAD_DIM), lambda b, h, i: (b * nb + i, h)),
        out_shape=jax.ShapeDtypeStruct((batch * seq, D_ATT), BF16),
        compiler_params=_cparams(("parallel", "parallel", "arbitrary")),
        name="moba",
    )(rel_bias, qt, kx, vt, kmean, bias_tiles)


def _merge_kernel(x_ref, ysg_ref, yatt_ref, gsg_ref, gatt_ref, mod_ref, nw_ref,
                  wsg_ref, watt_ref, wo_ref, wr_ref, x1_ref, tok_ref, cls_ref):
    tm = x_ref.shape[0]
    a_sg = jnp.dot(ysg_ref[...], wsg_ref[...], preferred_element_type=F32)
    a_att = jnp.dot(yatt_ref[...], watt_ref[...], preferred_element_type=F32)
    merged = (jax.nn.sigmoid(gsg_ref[...].astype(F32)) * a_sg
              + jax.nn.sigmoid(gatt_ref[...].astype(F32)) * a_att).astype(BF16)
    mixed = jnp.dot(merged, wo_ref[...], preferred_element_type=F32)
    x1 = x_ref[...] + mod_ref[0, 2:3, :] * mixed
    x1_ref[...] = x1
    ms = jnp.mean(x1 * x1, axis=-1, keepdims=True)
    scale = nw_ref[...] * (1.0 + mod_ref[0, 4:5, :])
    h2 = x1 * lax.rsqrt(ms + EPS) * scale + mod_ref[0, 3:4, :]
    for a in range(D_ROWS):
        tok_ref[pl.ds(a, tm, stride=TOK_ROWS), :] = h2[:, a * LANES:(a + 1) * LANES]
    for a in range(D_ROWS + 1, TOK_ROWS):
        tok_ref[pl.ds(a, tm, stride=TOK_ROWS), :] = jnp.zeros((tm, LANES), F32)

    w_hi, w_lo = _split_bf16(wr_ref[...])
    h_hi, h_lo = _split_bf16(h2)
    hi_both = jnp.dot(h_hi, jnp.concatenate([w_hi, w_lo], axis=1), preferred_element_type=F32)
    logits = (hi_both[:, :LANES] + jnp.dot(h_lo, w_hi, preferred_element_type=F32)
              + hi_both[:, LANES:])

    lane = lax.broadcasted_iota(jnp.int32, logits.shape, 1)
    neg_inf = jnp.float32(-jnp.inf)
    in_g = lane < N_GROUPS
    lg = jnp.where(in_g, logits, neg_inf)
    mg = jnp.max(lg, axis=1, keepdims=True)
    g_idx = jnp.min(jnp.where(lg == mg, lane, LANES), axis=1, keepdims=True)
    g_p = 1.0 / jnp.sum(jnp.exp(lg - mg), axis=1, keepdims=True)
    lo_col = ROUTER_COL0 + EXPERTS_PER_GROUP * g_idx
    in_e = (lane >= lo_col) & (lane < lo_col + EXPERTS_PER_GROUP)
    le = jnp.where(in_e, logits, neg_inf)
    me = jnp.max(le, axis=1, keepdims=True)
    i1 = jnp.min(jnp.where(le == me, lane, LANES), axis=1, keepdims=True)
    se = jnp.sum(jnp.exp(le - me), axis=1, keepdims=True)
    rest = jnp.where(lane == i1, neg_inf, le)
    m2 = jnp.max(rest, axis=1, keepdims=True)
    i2 = jnp.min(jnp.where(rest == m2, lane, LANES), axis=1, keepdims=True)
    p1 = 1.0 / se
    p2 = jnp.exp(m2 - me) / se
    denom = p1 + p2
    ea = jnp.minimum(i1, i2) - lo_col
    eb = jnp.maximum(i1, i2) - lo_col
    pair = lax.shift_right_logical(ea * (2 * EXPERTS_PER_GROUP - 1 - ea), 1) + eb - ea - 1
    cls = g_idx * PAIRS_PER_GROUP + pair
    routing = (jnp.where(lane == i1, g_p * (p1 / denom), 0.0)
               + jnp.where(lane == i2, g_p * (p2 / denom), 0.0)
               + jnp.where(lane == CLASS_LANE, cls.astype(F32), 0.0))
    tok_ref[pl.ds(D_ROWS, tm, stride=TOK_ROWS), :] = routing
    cls_ref[...] = routing.T[CLASS_LANE:CLASS_LANE + 1, :].astype(jnp.int32)


def _merge(x2, y_sg, y_att, proj, mod, norm2_w, w_sg, w_att, w_o, w_router, seq):
    n, d = x2.shape
    tm = MERGE_TM
    tpb = seq // tm
    resident = lambda shape: pl.BlockSpec(shape, lambda i: (0, 0), pipeline_mode=pl.Buffered(1))
    return pl.pallas_call(
        _merge_kernel,
        grid=(n // tm,),
        in_specs=[pl.BlockSpec((tm, d), lambda i: (i, 0)),
                  pl.BlockSpec((tm, D_SG), lambda i: (i, 0)),
                  pl.BlockSpec((tm, D_ATT), lambda i: (i, 0)),
                  pl.BlockSpec((tm, d), lambda i: (i, BLK_GATE_SG)),
                  pl.BlockSpec((tm, d), lambda i: (i, BLK_GATE_ATT)),
                  pl.BlockSpec((1, 8, d), lambda i: (i // tpb, 0, 0)),
                  pl.BlockSpec((1, d), lambda i: (0, 0)),
                  resident((D_SG, d)), resident((D_ATT, d)), resident((d, d)),
                  resident((d, LANES))],
        out_specs=[pl.BlockSpec((tm, d), lambda i: (i, 0)),
                   pl.BlockSpec((tm * TOK_ROWS, LANES), lambda i: (i, 0)),
                   pl.BlockSpec((None, 1, tm), lambda i: (i, 0, 0))],
        out_shape=[jax.ShapeDtypeStruct((n, d), F32),
                   jax.ShapeDtypeStruct((n * TOK_ROWS, LANES), F32),
                   jax.ShapeDtypeStruct((n // tm, 1, tm), jnp.int32)],
        compiler_params=_cparams(("parallel",)),
        name="merge",
    )(x2, y_sg, y_att, proj, proj, mod, norm2_w, w_sg, w_att, w_o, w_router)


def _invert_kernel(pos_ref, free_ref, inv_ref, *, n_tok, n_free):
    def place(t, carry):
        inv_ref[pos_ref[t]] = t
        return carry

    def place_free(k, carry):
        inv_ref[free_ref[k]] = n_tok + k
        return carry

    lax.fori_loop(0, n_tok, place, 0, unroll=8)
    lax.fori_loop(0, n_free, place_free, 0, unroll=8)


def _invert(pos, free):
    n_tok, n_free = pos.shape[0], free.shape[0]
    return pl.pallas_call(
        functools.partial(_invert_kernel, n_tok=n_tok, n_free=n_free),
        grid_spec=pltpu.PrefetchScalarGridSpec(
            num_scalar_prefetch=2, grid=(1,), in_specs=[],
            out_specs=pl.BlockSpec(memory_space=pltpu.SMEM)),
        out_shape=jax.ShapeDtypeStruct((n_tok + n_free,), jnp.int32),
        compiler_params=_cparams(("arbitrary",)),
        name="invert",
    )(pos, free)


MOE_TILE = 256
ROW_UNROLL_LOG2 = 3
ROW_UNROLL = 1 << ROW_UNROLL_LOG2


def _for_rows(rows, fn):
    groups = lax.shift_right_logical(rows, ROW_UNROLL_LOG2)

    def group(g, carry):
        for u in range(ROW_UNROLL):
            fn(g * ROW_UNROLL + u)
        return carry

    def single(r, carry):
        fn(r)
        return carry

    lax.fori_loop(0, groups, group, 0)
    lax.fori_loop(groups * ROW_UNROLL, rows, single, 0)


def _moe_kernel(src_ref, dst_ref, ea_ref, eb_ref, rows_ref, na_ref, tok_hbm,
                w1a_ref, w3a_ref, w2a_ref, w1b_ref, w3b_ref, w2b_ref, y_hbm,
                gbuf, stage, acc_ref, gsem, ssem):
    i = pl.program_id(0)
    n_act = na_ref[0]
    slot = lax.rem(i, 2)

    def gather_start(tile, sl):
        def one(r, carry):
            row0 = pl.multiple_of(r * TOK_ROWS, 8)
            pltpu.make_async_copy(tok_hbm.at[src_ref[tile * MOE_TILE + r]],
                                  gbuf.at[sl, pl.ds(row0, TOK_ROWS), :], gsem.at[sl]).start()
            return carry
        lax.fori_loop(0, MOE_TILE, one, 0, unroll=8)

    def gather_wait(sl):
        for _ in range(MOE_TILE):
            pltpu.make_async_copy(tok_hbm.at[0], gbuf.at[sl, pl.ds(0, TOK_ROWS), :], gsem.at[sl]).wait()

    def scatter_start(tile, sl):
        def one(r):
            row0 = pl.multiple_of(r * D_ROWS, 8)
            pltpu.make_async_copy(stage.at[sl, pl.ds(row0, D_ROWS), :],
                                  y_hbm.at[dst_ref[tile * MOE_TILE + r]], ssem.at[sl]).start()
        _for_rows(rows_ref[tile], one)

    def scatter_wait(tile, sl):
        def one(r):
            pltpu.make_async_copy(stage.at[sl, pl.ds(0, D_ROWS), :], y_hbm.at[0], ssem.at[sl]).wait()
        _for_rows(rows_ref[tile], one)

    @pl.when(i < n_act)
    def _():
        @pl.when(i == 0)
        def _():
            gather_start(0, 0)

        @pl.when(i + 1 < n_act)
        def _():
            gather_start(i + 1, 1 - slot)

        gather_wait(slot)
        t = jnp.concatenate([gbuf[slot, pl.ds(a, MOE_TILE, stride=TOK_ROWS), :] for a in range(D_ROWS)],
                            axis=1).astype(BF16)
        routing = gbuf[slot, pl.ds(D_ROWS, MOE_TILE, stride=TOK_ROWS), :]
        lane = lax.broadcasted_iota(jnp.int32, routing.shape, 1)
        experts = ((ea_ref[i], w1a_ref, w3a_ref, w2a_ref), (eb_ref[i], w1b_ref, w3b_ref, w2b_ref))
        ups = [(jnp.dot(t, w1_ref[...], preferred_element_type=F32),
                jnp.dot(t, w3_ref[...], preferred_element_type=F32)) for _, w1_ref, w3_ref, _ in experts]
        hmids = []
        for (e, _, _, _), (a, b) in zip(experts, ups):
            ge = jnp.sum(jnp.where(lane == ROUTER_COL0 + e, routing, 0.0), axis=1, keepdims=True)
            hmids.append(((a * jax.nn.sigmoid(a)) * b * ge).astype(BF16))
        acc_ref[...] = (jnp.dot(hmids[0], w2a_ref[...].astype(BF16), preferred_element_type=F32)
                        + jnp.dot(hmids[1], w2b_ref[...].astype(BF16), preferred_element_type=F32))

        @pl.when(i >= 2)
        def _():
            scatter_wait(i - 2, slot)

        for a in range(D_ROWS):
            stage[slot, pl.ds(a, MOE_TILE, stride=D_ROWS), :] = acc_ref[:, a * LANES:(a + 1) * LANES]
        scatter_start(i, slot)

        @pl.when(i == n_act - 1)
        def _():
            scatter_wait(i, slot)

            @pl.when(i >= 1)
            def _():
                scatter_wait(i - 1, 1 - slot)


def _moe(src_tok, dst_row, tile_ea, tile_eb, tile_rows, n_active, tok_blocks, w1, w3, w2, n_tok):
    n_slots = src_tok.shape[0]
    d = D_MODEL
    blk_a = lambda i, src, dst, ea, eb, rows, na: (ea[i], 0, 0)
    blk_b = lambda i, src, dst, ea, eb, rows, na: (eb[i], 0, 0)
    up = lambda blk: pl.BlockSpec((None, d, D_EXPERT), blk)
    down = lambda blk: pl.BlockSpec((None, D_EXPERT, d), blk)
    return pl.pallas_call(
        _moe_kernel,
        grid_spec=pltpu.PrefetchScalarGridSpec(
            num_scalar_prefetch=6, grid=(n_slots // MOE_TILE,),
            in_specs=[pl.BlockSpec(memory_space=pl.ANY),
                      up(blk_a), up(blk_a), down(blk_a), up(blk_b), up(blk_b), down(blk_b)],
            out_specs=pl.BlockSpec(memory_space=pl.ANY),
            scratch_shapes=[pltpu.VMEM((2, MOE_TILE * TOK_ROWS, LANES), F32),
                            pltpu.VMEM((2, MOE_TILE * D_ROWS, LANES), F32),
                            pltpu.VMEM((MOE_TILE, d), F32),
                            pltpu.SemaphoreType.DMA((2,)),
                            pltpu.SemaphoreType.DMA((2,))]),
        out_shape=jax.ShapeDtypeStruct((n_tok, D_ROWS, LANES), F32),
        compiler_params=_cparams(("arbitrary",)),
        name="moe",
    )(src_tok, dst_row, tile_ea, tile_eb, tile_rows, n_active, tok_blocks, w1, w3, w2, w1, w3, w2)


def _final_kernel(x1_ref, y_ref, mod_ref, fw_ref, o_ref):
    tm = x1_ref.shape[0]
    parts = []
    ss = jnp.zeros((tm, 1), F32)
    for a in range(D_ROWS):
        cols = slice(a * LANES, (a + 1) * LANES)
        part = x1_ref[:, cols] + mod_ref[0, 5:6, cols] * y_ref[pl.ds(a, tm, stride=D_ROWS), :]
        ss = ss + jnp.sum(part * part, axis=1, keepdims=True)
        parts.append(part)
    inv = lax.rsqrt(ss * (1.0 / D_MODEL) + EPS)
    for a in range(D_ROWS):
        cols = slice(a * LANES, (a + 1) * LANES)
        o_ref[:, cols] = parts[a] * inv * fw_ref[:, cols]


def _final(x1, y_blocks, mod, final_w, seq):
    n, d = x1.shape
    tm = FINAL_TM
    tpb = seq // tm
    return pl.pallas_call(
        _final_kernel,
        grid=(n // tm,),
        in_specs=[pl.BlockSpec((tm, d), lambda i: (i, 0)),
                  pl.BlockSpec((tm * D_ROWS, LANES), lambda i: (i, 0)),
                  pl.BlockSpec((1, 8, d), lambda i: (i // tpb, 0, 0)),
                  pl.BlockSpec((1, d), lambda i: (0, 0))],
        out_specs=pl.BlockSpec((tm, d), lambda i: (i, 0)),
        out_shape=jax.ShapeDtypeStruct((n, d), F32),
        compiler_params=_cparams(("parallel",)),
        name="final",
    )(x1, y_blocks, mod, final_w)


def _class_sort_plan(cls, n):
    class_ids = jnp.arange(N_CLASSES, dtype=jnp.int32)
    onehot = (cls[:, None] == class_ids[None, :]).astype(jnp.int32)
    blk = LANES
    within = jnp.einsum("ts,bsg->btg", jnp.tril(jnp.ones((blk, blk), F32)),
                        onehot.astype(F32).reshape(n // blk, blk, N_CLASSES))
    before = jnp.tril(jnp.ones((n // blk, n // blk), F32), -1) @ within[:, -1, :]
    incl = (within + before[:, None, :]).reshape(n, N_CLASSES).astype(jnp.int32)
    count = incl[-1]
    tiles = (count + MOE_TILE - 1) // MOE_TILE
    tile_end = jnp.cumsum(tiles)
    start = (tile_end - tiles) * MOE_TILE
    rank = jnp.sum(incl * onehot, axis=1) - 1
    pos = jnp.sum(onehot * start[None, :], axis=1) + rank
    n_tiles = n // MOE_TILE + N_CLASSES
    tile_ids = jnp.arange(n_tiles, dtype=jnp.int32)
    tile_class = jnp.sum((tile_ids[:, None] >= tile_end[None, :]).astype(jnp.int32), axis=1)
    tile_class = jnp.minimum(tile_class, N_CLASSES - 1)
    tile_hot = (tile_class[:, None] == class_ids[None, :]).astype(jnp.int32)
    done = (tile_ids - jnp.sum(tile_hot * (tile_end - tiles)[None, :], axis=1)) * MOE_TILE
    tile_rows = jnp.clip(jnp.sum(tile_hot * count[None, :], axis=1) - done, 0, MOE_TILE)
    first = jnp.asarray([a for a, _ in PAIRS], jnp.int32)
    second = jnp.asarray([b for _, b in PAIRS], jnp.int32)
    group0 = (class_ids // PAIRS_PER_GROUP) * EXPERTS_PER_GROUP
    tile_ea = jnp.sum(tile_hot * (group0 + first[class_ids % PAIRS_PER_GROUP])[None, :], axis=1)
    tile_eb = jnp.sum(tile_hot * (group0 + second[class_ids % PAIRS_PER_GROUP])[None, :], axis=1)
    pad = tiles * MOE_TILE - count
    pad_end = jnp.cumsum(pad)
    k = jnp.arange(N_CLASSES * MOE_TILE, dtype=jnp.int32)
    seg = jnp.sum((k[:, None] >= pad_end[None, :]).astype(jnp.int32), axis=1)
    seg_hot = (seg[:, None] == class_ids[None, :]).astype(jnp.int32)
    in_class = jnp.sum(seg_hot * (start + count - (pad_end - pad))[None, :], axis=1) + k
    tail = tile_end[-1] * MOE_TILE + k - pad_end[-1]
    free = jnp.where(seg < N_CLASSES, in_class, tail)
    i32 = lambda a: a.astype(jnp.int32)
    return i32(pos), i32(free), i32(tile_ea), i32(tile_eb), i32(tile_rows), i32(tile_end[-1:])


def kernel(x, c, w_ada, b_ada, norm1_w, norm2_w, final_norm_w, w_in, sg_ln_w, sg_ln_b, w_spatial,
           b_spatial, rel_bias, w_out_sg, w_out_att, w_o, w_router_group, w_router_expert,
           w_exp_gate, w_exp_up, w_exp_down):
    batch, seq, d = x.shape
    assert d == D_MODEL and w_ada.shape[0] == 1
    assert seq % PROJ_TM == 0 and seq // MOBA_BLOCK <= EXT_LO
    n = batch * seq
    x2 = x.reshape(n, d)

    assert batch <= BF16_SUBLANES
    c_pad = jnp.zeros((BF16_SUBLANES, d), F32).at[:batch].set(c)
    mod = _adaln(c_pad, w_ada[0], b_ada[0].reshape(1, 6 * d))
    mod = mod[:batch].reshape(batch, 6, d)
    mod = jnp.concatenate([mod, jnp.zeros((batch, 2, d), F32)], axis=1)

    proj = _proj(x2, mod, norm1_w[0].reshape(1, d), w_in[0], seq)

    y_sg = _sgu(proj, sg_ln_w[0].reshape(1, D_SG), sg_ln_b[0].reshape(1, D_SG),
                w_spatial[0], b_spatial[0].T)

    qt, kx, vt, kmean = _moba_prep(proj, batch, seq)
    bias_tiles = _bias_tiles(rel_bias)
    y_att = _moba(rel_bias, qt, kx, vt, kmean, bias_tiles, batch, seq)

    w_router = jnp.zeros((d, LANES), F32)
    w_router = w_router.at[:, :N_GROUPS].set(w_router_group[0])
    w_router = w_router.at[:, ROUTER_COL0:ROUTER_COL0 + N_EXPERTS].set(w_router_expert[0])
    x1, tok, cls = _merge(x2, y_sg, y_att, proj, mod, norm2_w[0].reshape(1, d),
                          w_out_sg[0].astype(BF16), w_out_att[0].astype(BF16),
                          w_o[0].astype(BF16), w_router, seq)

    tok = tok.reshape(n, TOK_ROWS, LANES)
    pos, free, tile_ea, tile_eb, tile_rows, n_active = _class_sort_plan(cls.reshape(n), n)
    inv = _invert(pos, free)
    src_tok = jnp.where(inv < n, inv, 0)
    y = _moe(src_tok, inv, tile_ea, tile_eb, tile_rows, n_active, tok, w_exp_gate[0].astype(BF16),
             w_exp_up[0].astype(BF16), w_exp_down[0], n)
    out = _final(x1, y.reshape(-1, LANES), mod, final_norm_w.reshape(1, d), seq)
    return out.reshape(batch, seq, d)
```

```python
import functools
import math

import jax
import jax.numpy as jnp
from jax import lax
from jax.experimental import pallas as pl
from jax.experimental.pallas import tpu as pltpu

F32 = jnp.float32
BF16 = jnp.bfloat16

LANES = 128
BF16_SUBLANES = 16
D_MODEL = 2048
D_SG = D_MODEL // 2
SG_GROUPS = 8
SG_CHUNK = 128
ATT_HEADS = 8
HEAD_DIM = 128
D_ATT = ATT_HEADS * HEAD_DIM
MOBA_BLOCK = 256
MOBA_TOPK = 3
REL_BUCKETS = 32
REL_MAX_DIST = 128
N_GROUPS = 4
EXPERTS_PER_GROUP = 4
N_EXPERTS = N_GROUPS * EXPERTS_PER_GROUP
D_EXPERT = 512
EPS = 1e-6
IN_COLS = 2 * D_SG + 3 * D_ATT + 2 * D_MODEL

N_GATE_COLS = 2 * D_MODEL
BLK_GATE_SG = 0
BLK_GATE_ATT = 1
BLK_U = N_GATE_COLS // D_SG
BLK_V = BLK_U + 1
CB_Q = (N_GATE_COLS + 2 * D_SG) // LANES
CB_K = CB_Q + ATT_HEADS
CB_VAL = CB_K + ATT_HEADS

MASK_NEG = -1e9
LOG2E = math.log2(math.e)
ROUTER_COL0 = N_GROUPS
CLASS_LANE = 0
PAIRS_PER_GROUP = EXPERTS_PER_GROUP * (EXPERTS_PER_GROUP - 1) // 2
N_CLASSES = N_GROUPS * PAIRS_PER_GROUP
PAIRS = [(a, b) for a in range(EXPERTS_PER_GROUP) for b in range(a + 1, EXPERTS_PER_GROUP)]
D_ROWS = D_MODEL // LANES
TOK_ROWS = D_ROWS + 8
VMEM_LIMIT = 56 * 1024 * 1024

ADALN_TN = 1024
PROJ_TM = 1024
PROJ_TN = 1024
PROJ_CHUNK = 256
SGU_CHUNKS = 8
MERGE_TM = 256
FINAL_TM = 512


def _cparams(sem):
    return pltpu.CompilerParams(dimension_semantics=sem, vmem_limit_bytes=VMEM_LIMIT)


def _split_bf16(a):
    hi = a.astype(BF16)
    lo = (a - hi.astype(F32)).astype(BF16)
    return hi, lo


def _adaln_kernel(c_ref, w_ref, b_ref, o_ref):
    c = c_ref[...]
    rows = c.shape[0]
    c_hi, c_lo = _split_bf16(c * jax.nn.sigmoid(c))
    w_hi, w_lo = _split_bf16(w_ref[...])
    both = jnp.dot(jnp.concatenate([c_hi, c_lo], axis=0), w_hi, preferred_element_type=F32)
    o_ref[...] = (both[:rows] + both[rows:] + jnp.dot(c_hi, w_lo, preferred_element_type=F32)
                  + b_ref[...])


def _adaln(c_pad, w, b):
    rows, d = c_pad.shape
    cols = w.shape[1]
    tn = ADALN_TN
    return pl.pallas_call(
        _adaln_kernel,
        grid=(cols // tn,),
        in_specs=[pl.BlockSpec((rows, d), lambda j: (0, 0)),
                  pl.BlockSpec((d, tn), lambda j: (0, j)),
                  pl.BlockSpec((1, tn), lambda j: (0, j))],
        out_specs=pl.BlockSpec((rows, tn), lambda j: (0, j)),
        out_shape=jax.ShapeDtypeStruct((rows, cols), F32),
        compiler_params=_cparams(("arbitrary",)),
        name="adaln",
    )(c_pad, w, b)


def _gelu(a):
    return 0.5 * a * (1.0 + lax.erf(a * (1.0 / math.sqrt(2.0))))


_PROJ_FINISH = {
    "gate": lambda acc: acc,
    "sgu": _gelu,
    "q": lambda acc: acc * (HEAD_DIM ** -0.5 * LOG2E),
    "kv": lambda acc: acc,
}
_PROJ_SEGMENTS = (("gate", N_GATE_COLS // PROJ_TN), ("sgu", 2 * D_SG // PROJ_TN),
                  ("q", D_ATT // PROJ_TN), ("kv", 2 * D_ATT // PROJ_TN))


def _proj_kernel(x_ref, mod_ref, nw_ref, w_ref, o_ref, h_scr):
    j = pl.program_id(1)

    @pl.when(j == 0)
    def _():
        finish = _PROJ_FINISH[_PROJ_SEGMENTS[0][0]]
        w = w_ref[...].astype(BF16)
        scale = nw_ref[...] * (1.0 + mod_ref[0, 1:2, :])
        for k in range(x_ref.shape[0] // PROJ_CHUNK):
            rows = slice(k * PROJ_CHUNK, (k + 1) * PROJ_CHUNK)
            x = x_ref[rows, :]
            ms = jnp.mean(x * x, axis=-1, keepdims=True)
            h = (x * lax.rsqrt(ms + EPS) * scale + mod_ref[0, 0:1, :]).astype(BF16)
            h_scr[rows, :] = h
            o_ref[rows, :] = finish(jnp.dot(h, w, preferred_element_type=F32)).astype(BF16)

    start = 0
    for kind, blocks in _PROJ_SEGMENTS:
        @pl.when((j >= max(start, 1)) & (j < start + blocks))
        def _(kind=kind):
            acc = jnp.dot(h_scr[...], w_ref[...].astype(BF16), preferred_element_type=F32)
            o_ref[...] = _PROJ_FINISH[kind](acc).astype(BF16)
        start += blocks


def _proj(x2, mod, norm_w, w_in, seq):
    n, d = x2.shape
    cols = w_in.shape[1]
    tm = min(PROJ_TM, seq)
    tn = PROJ_TN
    tpb = seq // tm
    n_cb = cols // tn
    first = (cols - N_GATE_COLS) // tn
    assert (cols - N_GATE_COLS) % tn == 0
    assert sum(blocks * tn for _, blocks in _PROJ_SEGMENTS) == cols
    return pl.pallas_call(
        _proj_kernel,
        grid=(n // tm, n_cb),
        in_specs=[pl.BlockSpec((tm, d), lambda i, j: (i, 0)),
                  pl.BlockSpec((1, 8, d), lambda i, j: (i // tpb, 0, 0)),
                  pl.BlockSpec((1, d), lambda i, j: (0, 0)),
                  pl.BlockSpec((d, tn), lambda i, j: (0, (j + first) % n_cb))],
        out_specs=pl.BlockSpec((tm, tn), lambda i, j: (i, j)),
        out_shape=jax.ShapeDtypeStruct((n, cols), BF16),
        scratch_shapes=[pltpu.VMEM((tm, d), BF16)],
        compiler_params=_cparams(("parallel", "arbitrary")),
        name="proj",
    )(x2, mod, norm_w, w_in)


def _sgu_kernel(u_ref, v_ref, lnw_ref, lnb_ref, ws_ref, bst_ref, o_ref, *, chunks):
    u = u_ref[...].astype(F32)
    v = v_ref[...].astype(F32)
    mu = jnp.mean(v, axis=-1, keepdims=True)
    vc = v - mu
    var = jnp.mean(vc * vc, axis=-1, keepdims=True)
    vn = (vc * lax.rsqrt(var + EPS) * lnw_ref[...] + lnb_ref[...]).astype(BF16)
    row = lax.broadcasted_iota(jnp.int32, (SG_CHUNK, SG_CHUNK), 0)
    col = lax.broadcasted_iota(jnp.int32, (SG_CHUNK, SG_CHUNK), 1)
    causal = col <= row
    for g in range(SG_GROUPS):
        wm = jnp.where(causal, ws_ref[g], 0.0).astype(BF16)
        bcol = bst_ref[:, g:g + 1]
        gs = slice(g * LANES, (g + 1) * LANES)
        for c in range(chunks):
            rs = slice(c * SG_CHUNK, (c + 1) * SG_CHUNK)
            z = jnp.dot(wm, vn[rs, gs], preferred_element_type=F32) + bcol
            o_ref[rs, gs] = (u[rs, gs] * z).astype(BF16)


def _sgu(proj, ln_w, ln_b, w_s, b_s_t):
    n = proj.shape[0]
    chunks = SGU_CHUNKS
    tm = chunks * SG_CHUNK
    return pl.pallas_call(
        functools.partial(_sgu_kernel, chunks=chunks),
        grid=(n // tm,),
        in_specs=[pl.BlockSpec((tm, D_SG), lambda i: (i, BLK_U)),
                  pl.BlockSpec((tm, D_SG), lambda i: (i, BLK_V)),
                  pl.BlockSpec((1, D_SG), lambda i: (0, 0)),
                  pl.BlockSpec((1, D_SG), lambda i: (0, 0)),
                  pl.BlockSpec((SG_GROUPS, SG_CHUNK, SG_CHUNK), lambda i: (0, 0, 0)),
                  pl.BlockSpec((SG_CHUNK, SG_GROUPS), lambda i: (0, 0))],
        out_specs=pl.BlockSpec((tm, D_SG), lambda i: (i, 0)),
        out_shape=jax.ShapeDtypeStruct((n, D_SG), BF16),
        compiler_params=_cparams(("parallel",)),
        name="sgu",
    )(proj, proj, ln_w, ln_b, w_s, b_s_t)


VT_ROWS = HEAD_DIM + 16
EXT_LO = REL_BUCKETS


def _moba_prep_kernel(q_ref, k_ref, v_ref, qt_ref, kx_ref, vt_ref, km_ref, *, nb):
    lane = lax.broadcasted_iota(jnp.int32, (MOBA_BLOCK, LANES), 1)
    pad = jnp.concatenate([jnp.ones((1, MOBA_BLOCK), F32),
                           jnp.zeros((VT_ROWS - HEAD_DIM - 1, MOBA_BLOCK), F32)], axis=0)
    for j in range(nb):
        rows = slice(j * MOBA_BLOCK, (j + 1) * MOBA_BLOCK)
        kj = k_ref[rows, :]
        onehot = jnp.where((lane == j) | (lane == j + EXT_LO), 1.0, 0.0).astype(BF16)
        kx_ref[rows, :] = jnp.concatenate([kj, onehot], axis=1)
        km_ref[j:j + 1, :] = jnp.mean(kj.astype(F32), axis=0, keepdims=True)
        vt = v_ref[rows, :].astype(F32).T
        vt_ref[j] = jnp.concatenate([vt, pad], axis=0).astype(BF16)
        qt_ref[j] = q_ref[rows, :].astype(F32).T.astype(BF16)


def _moba_prep(proj, batch, seq):
    nb = seq // MOBA_BLOCK
    bh = lambda shape: pl.BlockSpec((None, None) + shape, lambda b, h: (b, h) + (0,) * len(shape))
    return pl.pallas_call(
        functools.partial(_moba_prep_kernel, nb=nb),
        grid=(batch, ATT_HEADS),
        in_specs=[pl.BlockSpec((seq, HEAD_DIM), lambda b, h: (b, CB_Q + h)),
                  pl.BlockSpec((seq, HEAD_DIM), lambda b, h: (b, CB_K + h)),
                  pl.BlockSpec((seq, HEAD_DIM), lambda b, h: (b, CB_VAL + h))],
        out_specs=[bh((nb, HEAD_DIM, MOBA_BLOCK)), bh((seq, 2 * HEAD_DIM)),
                   bh((nb, VT_ROWS, MOBA_BLOCK)), bh((nb, HEAD_DIM))],
        out_shape=[jax.ShapeDtypeStruct((batch, ATT_HEADS, nb, HEAD_DIM, MOBA_BLOCK), BF16),
                   jax.ShapeDtypeStruct((batch, ATT_HEADS, seq, 2 * HEAD_DIM), BF16),
                   jax.ShapeDtypeStruct((batch, ATT_HEADS, nb, VT_ROWS, MOBA_BLOCK), BF16),
                   jax.ShapeDtypeStruct((batch, ATT_HEADS, nb, HEAD_DIM), F32)],
        compiler_params=_cparams(("parallel", "parallel")),
        name="moba_prep",
    )(proj, proj, proj)


def _bias_tiles_kernel(rb_ref, o_ref):
    h = pl.program_id(0)
    kj = lax.broadcasted_iota(jnp.int32, (MOBA_BLOCK, MOBA_BLOCK), 0)
    qi = lax.broadcasted_iota(jnp.int32, (MOBA_BLOCK, MOBA_BLOCK), 1)
    max_exact = REL_BUCKETS // 2
    for t in range(2):
        rel = qi - kj + MOBA_BLOCK * t
        n = jnp.maximum(rel, 0)
        nf = jnp.maximum(n, max_exact).astype(F32)
        large = max_exact + (jnp.log(nf / max_exact) / math.log(REL_MAX_DIST / max_exact)
                             * (REL_BUCKETS - max_exact)).astype(jnp.int32)
        large = jnp.minimum(large, REL_BUCKETS - 1)
        bucket = jnp.where(n < max_exact, n, large)
        bias = jnp.zeros((MOBA_BLOCK, MOBA_BLOCK), F32)
        for r in range(REL_BUCKETS):
            bias = jnp.where(bucket == r, rb_ref[r, h], bias)
        o_ref[t] = jnp.where(rel >= 0, bias * LOG2E, MASK_NEG)


def _bias_tiles(rel_bias):
    return pl.pallas_call(
        _bias_tiles_kernel,
        grid=(ATT_HEADS,),
        in_specs=[pl.BlockSpec(memory_space=pltpu.SMEM)],
        out_specs=pl.BlockSpec((None, 2, MOBA_BLOCK, MOBA_BLOCK), lambda h: (h, 0, 0, 0)),
        out_shape=jax.ShapeDtypeStruct((ATT_HEADS, 2, MOBA_BLOCK, MOBA_BLOCK), F32),
        compiler_params=_cparams(("arbitrary",)),
        name="bias_tiles",
    )(rel_bias)


FAR_BLOCKS = 4
MAX_LOG2_RISE = 64.0
HEADS_PER_STEP = 4


def _moba_kernel(rb_ref, qt_ref, kx_ref, vt_ref, km_ref, bias_ref, o_ref, *, nb):
    i = pl.program_id(2)
    jp = jnp.maximum(i - 1, 0)
    prev0 = pl.multiple_of(jp * MOBA_BLOCK, MOBA_BLOCK)
    own0 = pl.multiple_of(i * MOBA_BLOCK, MOBA_BLOCK)
    no_prev = jnp.where(i > 0, 0.0, MASK_NEG)
    neg_inf = jnp.float32(-jnp.inf)

    heads = range(HEADS_PER_STEP)

    def choose_blocks(hh):
        km_hi, km_lo = _split_bf16(km_ref[hh])
        score = (jnp.dot(km_hi, qt_ref[hh], preferred_element_type=F32)
                 + jnp.dot(km_lo, qt_ref[hh], preferred_element_type=F32))
        bid = lax.broadcasted_iota(jnp.int32, score.shape, 0)
        score = jnp.where(bid < i, score, neg_inf)
        chosen = bid < 0
        for _ in range(MOBA_TOPK):
            mx = jnp.max(score, axis=0, keepdims=True)
            hit = (score == mx) & (mx > neg_inf)
            idx = jnp.min(jnp.where(hit, bid, nb), axis=0, keepdims=True)
            pick = bid == idx
            chosen = chosen | pick
            score = jnp.where(pick, neg_inf, score)
        return chosen

    def with_mask(hh, val):
        hi, lo = _split_bf16(val)
        fill = jnp.zeros((HEAD_DIM - 2 * EXT_LO, MOBA_BLOCK), BF16)
        if nb < EXT_LO:
            gap = jnp.zeros((EXT_LO - nb, MOBA_BLOCK), BF16)
            return jnp.concatenate([qt_ref[hh], hi, gap, lo, gap, fill], axis=0)
        return jnp.concatenate([qt_ref[hh], hi, lo, fill], axis=0)

    chosen = [choose_blocks(hh) for hh in heads]
    bid = lax.broadcasted_iota(jnp.int32, chosen[0].shape, 0)
    far_bias = [rb_ref[REL_BUCKETS - 1, pl.program_id(1) * HEADS_PER_STEP + hh] * LOG2E for hh in heads]
    qx_fars = [with_mask(hh, jnp.where(chosen[hh] & (bid <= i - 2), far_bias[hh], MASK_NEG)) for hh in heads]
    qx_nears = [with_mask(hh, jnp.where((bid == i) | (chosen[hh] & (bid == i - 1)), 0.0, MASK_NEG))
                for hh in heads]

    ss = [jnp.dot(jnp.concatenate([kx_ref[hh, pl.ds(prev0, MOBA_BLOCK), :],
                                   kx_ref[hh, pl.ds(own0, MOBA_BLOCK), :]], axis=0),
                  qx_nears[hh], preferred_element_type=F32)
          + jnp.concatenate([bias_ref[hh, 1] + no_prev, bias_ref[hh, 0]], axis=0) for hh in heads]
    ms = [jnp.max(ss[hh], axis=0, keepdims=True) for hh in heads]
    ps = [jnp.exp2(ss[hh] - ms[hh]).astype(BF16) for hh in heads]
    accs = [jnp.dot(jnp.concatenate([vt_ref[hh, jp], vt_ref[hh, i]], axis=1), ps[hh],
                    preferred_element_type=F32) for hh in heads]

    def far_step(c, carry):
        row0 = pl.multiple_of(c * (FAR_BLOCKS * MOBA_BLOCK), FAR_BLOCKS * MOBA_BLOCK)
        m_olds = [carry[hh][0] for hh in heads]
        accs = [carry[hh][1] for hh in heads]

        def logits(hh):
            return jnp.dot(kx_ref[hh, pl.ds(row0, FAR_BLOCKS * MOBA_BLOCK), :], qx_fars[hh],
                           preferred_element_type=F32)

        def values(hh):
            return jnp.concatenate([vt_ref[hh, c * FAR_BLOCKS + k] for k in range(FAR_BLOCKS)], axis=1)

        ss = [logits(hh) for hh in heads]
        ps = [jnp.exp2(ss[hh] - m_olds[hh]).astype(BF16) for hh in heads]
        m_news = [jnp.maximum(m_olds[hh], jnp.max(ss[hh], axis=0, keepdims=True)) for hh in heads]
        alphas = [jnp.exp2(m_olds[hh] - m_news[hh]) for hh in heads]
        usual = tuple((m_news[hh], alphas[hh] * (accs[hh] + jnp.dot(values(hh), ps[hh],
                                                                    preferred_element_type=F32)))
                      for hh in heads)
        rise = functools.reduce(jnp.maximum, [jnp.max(m_news[hh] - m_olds[hh]) for hh in heads])

        def redo():
            out = []
            for hh in heads:
                p = jnp.exp2(logits(hh) - m_news[hh]).astype(BF16)
                out.append((m_news[hh], alphas[hh] * accs[hh]
                            + jnp.dot(values(hh), p, preferred_element_type=F32)))
            return tuple(out)

        return lax.cond(rise > MAX_LOG2_RISE, redo, lambda: usual)

    n_far = lax.shift_right_logical(jnp.maximum(i - 1, 0) + FAR_BLOCKS - 1, 2)
    final = lax.fori_loop(0, n_far, far_step, tuple((ms[hh], accs[hh]) for hh in heads))
    for hh in heads:
        acc = final[hh][1]
        out = acc[0:HEAD_DIM] / acc[HEAD_DIM:HEAD_DIM + 1]
        o_ref[:, hh * HEAD_DIM:(hh + 1) * HEAD_DIM] = out.T.astype(BF16)


def _moba(rel_bias, qt, kx, vt, kmean, bias_tiles, batch, seq):
    nb = seq // MOBA_BLOCK
    hps = HEADS_PER_STEP
    assert nb % FAR_BLOCKS == 0 and FAR_BLOCKS == 4 and ATT_HEADS % hps == 0
    bh = lambda shape: pl.BlockSpec((None, hps) + shape, lambda b, h, i: (b, h) + (0,) * len(shape),
                                    pipeline_mode=pl.Buffered(1))
    return pl.pallas_call(
        functools.partial(_moba_kernel, nb=nb),
        grid=(batch, ATT_HEADS // hps, nb),
        in_specs=[pl.BlockSpec(memory_space=pltpu.SMEM),
                  pl.BlockSpec((None, hps, None, HEAD_DIM, MOBA_BLOCK), lambda b, h, i: (b, h, i, 0, 0)),
                  bh((seq, 2 * HEAD_DIM)), bh((nb, VT_ROWS, MOBA_BLOCK)), bh((nb, HEAD_DIM)),
                  pl.BlockSpec((hps, 2, MOBA_BLOCK, MOBA_BLOCK), lambda b, h, i: (h, 0, 0, 0))],
        out_specs=pl.BlockSpec((MOBA_BLOCK, hps * HEAD_DIM), lambda b, h, i: (b * nb + i, h)),
        out_shape=jax.ShapeDtypeStruct((batch * seq, D_ATT), BF16),
        compiler_params=_cparams(("parallel", "parallel", "arbitrary")),
        name="moba",
    )(rel_bias, qt, kx, vt, kmean, bias_tiles)


def _merge_kernel(x_ref, ysg_ref, yatt_ref, gsg_ref, gatt_ref, mod_ref, nw_ref,
                  wsg_ref, watt_ref, wo_ref, wr_ref, x1_ref, tok_ref, cls_ref):
    tm = x_ref.shape[0]
    a_sg = jnp.dot(ysg_ref[...], wsg_ref[...], preferred_element_type=F32)
    a_att = jnp.dot(yatt_ref[...], watt_ref[...], preferred_element_type=F32)
    merged = (jax.nn.sigmoid(gsg_ref[...].astype(F32)) * a_sg
              + jax.nn.sigmoid(gatt_ref[...].astype(F32)) * a_att).astype(BF16)
    mixed = jnp.dot(merged, wo_ref[...], preferred_element_type=F32)
    x1 = x_ref[...] + mod_ref[0, 2:3, :] * mixed
    x1_ref[...] = x1
    ms = jnp.mean(x1 * x1, axis=-1, keepdims=True)
    scale = nw_ref[...] * (1.0 + mod_ref[0, 4:5, :])
    h2 = x1 * lax.rsqrt(ms + EPS) * scale + mod_ref[0, 3:4, :]
    for a in range(D_ROWS):
        tok_ref[pl.ds(a, tm, stride=TOK_ROWS), :] = h2[:, a * LANES:(a + 1) * LANES]
    for a in range(D_ROWS + 1, TOK_ROWS):
        tok_ref[pl.ds(a, tm, stride=TOK_ROWS), :] = jnp.zeros((tm, LANES), F32)

    w_hi, w_lo = _split_bf16(wr_ref[...])
    h_hi, h_lo = _split_bf16(h2)
    hi_both = jnp.dot(h_hi, jnp.concatenate([w_hi, w_lo], axis=1), preferred_element_type=F32)
    logits = (hi_both[:, :LANES] + jnp.dot(h_lo, w_hi, preferred_element_type=F32)
              + hi_both[:, LANES:])

    lane = lax.broadcasted_iota(jnp.int32, logits.shape, 1)
    neg_inf = jnp.float32(-jnp.inf)
    in_g = lane < N_GROUPS
    lg = jnp.where(in_g, logits, neg_inf)
    mg = jnp.max(lg, axis=1, keepdims=True)
    g_idx = jnp.min(jnp.where(lg == mg, lane, LANES), axis=1, keepdims=True)
    g_p = 1.0 / jnp.sum(jnp.exp(lg - mg), axis=1, keepdims=True)
    lo_col = ROUTER_COL0 + EXPERTS_PER_GROUP * g_idx
    in_e = (lane >= lo_col) & (lane < lo_col + EXPERTS_PER_GROUP)
    le = jnp.where(in_e, logits, neg_inf)
    me = jnp.max(le, axis=1, keepdims=True)
    i1 = jnp.min(jnp.where(le == me, lane, LANES), axis=1, keepdims=True)
    se = jnp.sum(jnp.exp(le - me), axis=1, keepdims=True)
    rest = jnp.where(lane == i1, neg_inf, le)
    m2 = jnp.max(rest, axis=1, keepdims=True)
    i2 = jnp.min(jnp.where(rest == m2, lane, LANES), axis=1, keepdims=True)
    p1 = 1.0 / se
    p2 = jnp.exp(m2 - me) / se
    denom = p1 + p2
    ea = jnp.minimum(i1, i2) - lo_col
    eb = jnp.maximum(i1, i2) - lo_col
    pair = lax.shift_right_logical(ea * (2 * EXPERTS_PER_GROUP - 1 - ea), 1) + eb - ea - 1
    cls = g_idx * PAIRS_PER_GROUP + pair
    routing = (jnp.where(lane == i1, g_p * (p1 / denom), 0.0)
               + jnp.where(lane == i2, g_p * (p2 / denom), 0.0)
               + jnp.where(lane == CLASS_LANE, cls.astype(F32), 0.0))
    tok_ref[pl.ds(D_ROWS, tm, stride=TOK_ROWS), :] = routing
    cls_ref[...] = routing.T[CLASS_LANE:CLASS_LANE + 1, :].astype(jnp.int32)


def _merge(x2, y_sg, y_att, proj, mod, norm2_w, w_sg, w_att, w_o, w_router, seq):
    n, d = x2.shape
    tm = MERGE_TM
    tpb = seq // tm
    resident = lambda shape: pl.BlockSpec(shape, lambda i: (0, 0), pipeline_mode=pl.Buffered(1))
    return pl.pallas_call(
        _merge_kernel,
        grid=(n // tm,),
        in_specs=[pl.BlockSpec((tm, d), lambda i: (i, 0)),
                  pl.BlockSpec((tm, D_SG), lambda i: (i, 0)),
                  pl.BlockSpec((tm, D_ATT), lambda i: (i, 0)),
                  pl.BlockSpec((tm, d), lambda i: (i, BLK_GATE_SG)),
                  pl.BlockSpec((tm, d), lambda i: (i, BLK_GATE_ATT)),
                  pl.BlockSpec((1, 8, d), lambda i: (i // tpb, 0, 0)),
                  pl.BlockSpec((1, d), lambda i: (0, 0)),
                  resident((D_SG, d)), resident((D_ATT, d)), resident((d, d)),
                  resident((d, LANES))],
        out_specs=[pl.BlockSpec((tm, d), lambda i: (i, 0)),
                   pl.BlockSpec((tm * TOK_ROWS, LANES), lambda i: (i, 0)),
                   pl.BlockSpec((None, 1, tm), lambda i: (i, 0, 0))],
        out_shape=[jax.ShapeDtypeStruct((n, d), F32),
                   jax.ShapeDtypeStruct((n * TOK_ROWS, LANES), F32),
                   jax.ShapeDtypeStruct((n // tm, 1, tm), jnp.int32)],
        compiler_params=_cparams(("parallel",)),
        name="merge",
    )(x2, y_sg, y_att, proj, proj, mod, norm2_w, w_sg, w_att, w_o, w_router)


def _invert_kernel(pos_ref, free_ref, inv_ref, *, n_tok, n_free):
    def place(t, carry):
        inv_ref[pos_ref[t]] = t
        return carry

    def place_free(k, carry):
        inv_ref[free_ref[k]] = n_tok + k
        return carry

    lax.fori_loop(0, n_tok, place, 0, unroll=8)
    lax.fori_loop(0, n_free, place_free, 0, unroll=8)


def _invert(pos, free):
    n_tok, n_free = pos.shape[0], free.shape[0]
    return pl.pallas_call(
        functools.partial(_invert_kernel, n_tok=n_tok, n_free=n_free),
        grid_spec=pltpu.PrefetchScalarGridSpec(
            num_scalar_prefetch=2, grid=(1,), in_specs=[],
            out_specs=pl.BlockSpec(memory_space=pltpu.SMEM)),
        out_shape=jax.ShapeDtypeStruct((n_tok + n_free,), jnp.int32),
        compiler_params=_cparams(("arbitrary",)),
        name="invert",
    )(pos, free)


MOE_TILE = 256
GATHER_DMA_PRIORITY = 1
ROW_UNROLL_LOG2 = 3
ROW_UNROLL = 1 << ROW_UNROLL_LOG2


def _for_rows(rows, fn):
    groups = lax.shift_right_logical(rows, ROW_UNROLL_LOG2)

    def group(g, carry):
        for u in range(ROW_UNROLL):
            fn(g * ROW_UNROLL + u)
        return carry

    def single(r, carry):
        fn(r)
        return carry

    lax.fori_loop(0, groups, group, 0)
    lax.fori_loop(groups * ROW_UNROLL, rows, single, 0)


def _moe_kernel(src_ref, dst_ref, ea_ref, eb_ref, rows_ref, na_ref, tok_hbm,
                w1a_ref, w3a_ref, w2a_ref, w1b_ref, w3b_ref, w2b_ref, y_hbm,
                gbuf, stage, acc_ref, gsem, ssem):
    i = pl.program_id(0)
    n_act = na_ref[0]
    slot = lax.rem(i, 2)

    def gather_start(tile, sl):
        def one(r, carry):
            row0 = pl.multiple_of(r * TOK_ROWS, 8)
            pltpu.make_async_copy(tok_hbm.at[src_ref[tile * MOE_TILE + r]],
                                  gbuf.at[sl, pl.ds(row0, TOK_ROWS), :], gsem.at[sl]
                                  ).start(priority=GATHER_DMA_PRIORITY)
            return carry
        lax.fori_loop(0, MOE_TILE, one, 0, unroll=8)

    def gather_wait(sl):
        for _ in range(MOE_TILE):
            pltpu.make_async_copy(tok_hbm.at[0], gbuf.at[sl, pl.ds(0, TOK_ROWS), :], gsem.at[sl]).wait()

    def scatter_start(tile, sl):
        def one(r):
            row0 = pl.multiple_of(r * D_ROWS, 8)
            pltpu.make_async_copy(stage.at[sl, pl.ds(row0, D_ROWS), :],
                                  y_hbm.at[dst_ref[tile * MOE_TILE + r]], ssem.at[sl]).start()
        _for_rows(rows_ref[tile], one)

    def scatter_wait(tile, sl):
        def one(r):
            pltpu.make_async_copy(stage.at[sl, pl.ds(0, D_ROWS), :], y_hbm.at[0], ssem.at[sl]).wait()
        _for_rows(rows_ref[tile], one)

    @pl.when(i < n_act)
    def _():
        @pl.when(i == 0)
        def _():
            gather_start(0, 0)

        @pl.when(i + 1 < n_act)
        def _():
            gather_start(i + 1, 1 - slot)

        gather_wait(slot)
        t = jnp.concatenate([gbuf[slot, pl.ds(a, MOE_TILE, stride=TOK_ROWS), :] for a in range(D_ROWS)],
                            axis=1).astype(BF16)
        routing = gbuf[slot, pl.ds(D_ROWS, MOE_TILE, stride=TOK_ROWS), :]
        lane = lax.broadcasted_iota(jnp.int32, routing.shape, 1)
        experts = ((ea_ref[i], w1a_ref, w3a_ref, w2a_ref), (eb_ref[i], w1b_ref, w3b_ref, w2b_ref))
        ups = [(jnp.dot(t, w1_ref[...], preferred_element_type=F32),
                jnp.dot(t, w3_ref[...], preferred_element_type=F32)) for _, w1_ref, w3_ref, _ in experts]
        hmids = []
        for (e, _, _, _), (a, b) in zip(experts, ups):
            ge = jnp.sum(jnp.where(lane == ROUTER_COL0 + e, routing, 0.0), axis=1, keepdims=True)
            hmids.append(((a * jax.nn.sigmoid(a)) * b * ge).astype(BF16))
        acc_ref[...] = (jnp.dot(hmids[0], w2a_ref[...].astype(BF16), preferred_element_type=F32)
                        + jnp.dot(hmids[1], w2b_ref[...].astype(BF16), preferred_element_type=F32))

        @pl.when(i >= 2)
        def _():
            scatter_wait(i - 2, slot)

        for a in range(D_ROWS):
            stage[slot, pl.ds(a, MOE_TILE, stride=D_ROWS), :] = acc_ref[:, a * LANES:(a + 1) * LANES]
        scatter_start(i, slot)

        @pl.when(i == n_act - 1)
        def _():
            scatter_wait(i, slot)

            @pl.when(i >= 1)
            def _():
                scatter_wait(i - 1, 1 - slot)


def _moe(src_tok, dst_row, tile_ea, tile_eb, tile_rows, n_active, tok_blocks, w1, w3, w2, n_tok):
    n_slots = src_tok.shape[0]
    d = D_MODEL
    blk_a = lambda i, src, dst, ea, eb, rows, na: (ea[i], 0, 0)
    blk_b = lambda i, src, dst, ea, eb, rows, na: (eb[i], 0, 0)
    up = lambda blk: pl.BlockSpec((None, d, D_EXPERT), blk)
    down = lambda blk: pl.BlockSpec((None, D_EXPERT, d), blk)
    return pl.pallas_call(
        _moe_kernel,
        grid_spec=pltpu.PrefetchScalarGridSpec(
            num_scalar_prefetch=6, grid=(n_slots // MOE_TILE,),
            in_specs=[pl.BlockSpec(memory_space=pl.ANY),
                      up(blk_a), up(blk_a), down(blk_a), up(blk_b), up(blk_b), down(blk_b)],
            out_specs=pl.BlockSpec(memory_space=pl.ANY),
            scratch_shapes=[pltpu.VMEM((2, MOE_TILE * TOK_ROWS, LANES), F32),
                            pltpu.VMEM((2, MOE_TILE * D_ROWS, LANES), F32),
                            pltpu.VMEM((MOE_TILE, d), F32),
                            pltpu.SemaphoreType.DMA((2,)),
                            pltpu.SemaphoreType.DMA((2,))]),
        out_shape=jax.ShapeDtypeStruct((n_tok, D_ROWS, LANES), F32),
        compiler_params=_cparams(("arbitrary",)),
        name="moe",
    )(src_tok, dst_row, tile_ea, tile_eb, tile_rows, n_active, tok_blocks, w1, w3, w2, w1, w3, w2)


def _final_kernel(x1_ref, y_ref, mod_ref, fw_ref, o_ref):
    tm = x1_ref.shape[0]
    parts = []
    ss = jnp.zeros((tm, 1), F32)
    for a in range(D_ROWS):
        cols = slice(a * LANES, (a + 1) * LANES)
        part = x1_ref[:, cols] + mod_ref[0, 5:6, cols] * y_ref[pl.ds(a, tm, stride=D_ROWS), :]
        ss = ss + jnp.sum(part * part, axis=1, keepdims=True)
        parts.append(part)
    inv = lax.rsqrt(ss * (1.0 / D_MODEL) + EPS)
    for a in range(D_ROWS):
        cols = slice(a * LANES, (a + 1) * LANES)
        o_ref[:, cols] = parts[a] * inv * fw_ref[:, cols]


def _final(x1, y_blocks, mod, final_w, seq):
    n, d = x1.shape
    tm = FINAL_TM
    tpb = seq // tm
    return pl.pallas_call(
        _final_kernel,
        grid=(n // tm,),
        in_specs=[pl.BlockSpec((tm, d), lambda i: (i, 0)),
                  pl.BlockSpec((tm * D_ROWS, LANES), lambda i: (i, 0)),
                  pl.BlockSpec((1, 8, d), lambda i: (i // tpb, 0, 0)),
                  pl.BlockSpec((1, d), lambda i: (0, 0))],
        out_specs=pl.BlockSpec((tm, d), lambda i: (i, 0)),
        out_shape=jax.ShapeDtypeStruct((n, d), F32),
        compiler_params=_cparams(("parallel",)),
        name="final",
    )(x1, y_blocks, mod, final_w)


def _class_sort_plan(cls, n):
    class_ids = jnp.arange(N_CLASSES, dtype=jnp.int32)
    onehot = (cls[:, None] == class_ids[None, :]).astype(jnp.int32)
    blk = LANES
    within = jnp.einsum("ts,bsg->btg", jnp.tril(jnp.ones((blk, blk), F32)),
                        onehot.astype(F32).reshape(n // blk, blk, N_CLASSES))
    before = jnp.tril(jnp.ones((n // blk, n // blk), F32), -1) @ within[:, -1, :]
    incl = (within + before[:, None, :]).reshape(n, N_CLASSES).astype(jnp.int32)
    count = incl[-1]
    tiles = (count + MOE_TILE - 1) // MOE_TILE
    tile_end = jnp.cumsum(tiles)
    start = (tile_end - tiles) * MOE_TILE
    rank = jnp.sum(incl * onehot, axis=1) - 1
    pos = jnp.sum(onehot * start[None, :], axis=1) + rank
    n_tiles = n // MOE_TILE + N_CLASSES
    tile_ids = jnp.arange(n_tiles, dtype=jnp.int32)
    tile_class = jnp.sum((tile_ids[:, None] >= tile_end[None, :]).astype(jnp.int32), axis=1)
    tile_class = jnp.minimum(tile_class, N_CLASSES - 1)
    tile_hot = (tile_class[:, None] == class_ids[None, :]).astype(jnp.int32)
    done = (tile_ids - jnp.sum(tile_hot * (tile_end - tiles)[None, :], axis=1)) * MOE_TILE
    tile_rows = jnp.clip(jnp.sum(tile_hot * count[None, :], axis=1) - done, 0, MOE_TILE)
    first = jnp.asarray([a for a, _ in PAIRS], jnp.int32)
    second = jnp.asarray([b for _, b in PAIRS], jnp.int32)
    group0 = (class_ids // PAIRS_PER_GROUP) * EXPERTS_PER_GROUP
    tile_ea = jnp.sum(tile_hot * (group0 + first[class_ids % PAIRS_PER_GROUP])[None, :], axis=1)
    tile_eb = jnp.sum(tile_hot * (group0 + second[class_ids % PAIRS_PER_GROUP])[None, :], axis=1)
    pad = tiles * MOE_TILE - count
    pad_end = jnp.cumsum(pad)
    k = jnp.arange(N_CLASSES * MOE_TILE, dtype=jnp.int32)
    seg = jnp.sum((k[:, None] >= pad_end[None, :]).astype(jnp.int32), axis=1)
    seg_hot = (seg[:, None] == class_ids[None, :]).astype(jnp.int32)
    in_class = jnp.sum(seg_hot * (start + count - (pad_end - pad))[None, :], axis=1) + k
    tail = tile_end[-1] * MOE_TILE + k - pad_end[-1]
    free = jnp.where(seg < N_CLASSES, in_class, tail)
    i32 = lambda a: a.astype(jnp.int32)
    return i32(pos), i32(free), i32(tile_ea), i32(tile_eb), i32(tile_rows), i32(tile_end[-1:])


def kernel(x, c, w_ada, b_ada, norm1_w, norm2_w, final_norm_w, w_in, sg_ln_w, sg_ln_b, w_spatial,
           b_spatial, rel_bias, w_out_sg, w_out_att, w_o, w_router_group, w_router_expert,
           w_exp_gate, w_exp_up, w_exp_down):
    batch, seq, d = x.shape
    assert d == D_MODEL and w_ada.shape[0] == 1
    assert seq % PROJ_TM == 0 and seq // MOBA_BLOCK <= EXT_LO
    n = batch * seq
    x2 = x.reshape(n, d)

    assert batch <= BF16_SUBLANES
    c_pad = jnp.zeros((BF16_SUBLANES, d), F32).at[:batch].set(c)
    mod = _adaln(c_pad, w_ada[0], b_ada[0].reshape(1, 6 * d))
    mod = mod[:batch].reshape(batch, 6, d)
    mod = jnp.concatenate([mod, jnp.zeros((batch, 2, d), F32)], axis=1)

    proj = _proj(x2, mod, norm1_w[0].reshape(1, d), w_in[0], seq)

    y_sg = _sgu(proj, sg_ln_w[0].reshape(1, D_SG), sg_ln_b[0].reshape(1, D_SG),
                w_spatial[0], b_spatial[0].T)

    qt, kx, vt, kmean = _moba_prep(proj, batch, seq)
    bias_tiles = _bias_tiles(rel_bias)
    y_att = _moba(rel_bias, qt, kx, vt, kmean, bias_tiles, batch, seq)

    w_router = jnp.zeros((d, LANES), F32)
    w_router = w_router.at[:, :N_GROUPS].set(w_router_group[0])
    w_router = w_router.at[:, ROUTER_COL0:ROUTER_COL0 + N_EXPERTS].set(w_router_expert[0])
    x1, tok, cls = _merge(x2, y_sg, y_att, proj, mod, norm2_w[0].reshape(1, d),
                          w_out_sg[0].astype(BF16), w_out_att[0].astype(BF16),
                          w_o[0].astype(BF16), w_router, seq)

    tok = tok.reshape(n, TOK_ROWS, LANES)
    pos, free, tile_ea, tile_eb, tile_rows, n_active = _class_sort_plan(cls.reshape(n), n)
    inv = _invert(pos, free)
    src_tok = jnp.where(inv < n, inv, 0)
    y = _moe(src_tok, inv, tile_ea, tile_eb, tile_rows, n_active, tok, w_exp_gate[0].astype(BF16),
             w_exp_up[0].astype(BF16), w_exp_down[0], n)
    out = _final(x1, y.reshape(-1, LANES), mod, final_norm_w.reshape(1, d), seq)
    return out.reshape(batch, seq, d)
```
